```python
import math
import jax, jax.numpy as jnp
from jax import lax
import numpy as np

D_MODEL = 2048
BATCH = 16
SEQ = 256
DEPTH = 2
DEC_BATCH = 2
DEC_SEQ = 4096
PAST_LEN = 256

GRID_W = 64
HEAD_DIM = 128
ROPE_THETA = 10000.0
A_HEADS = 4
A_QK_DIM = 2 * HEAD_DIM
A_V_DIM = 2 * HEAD_DIM
B_Q_HEADS = 8
B_KV_HEADS = 2
C_Q_HEADS = 8
C_KV_HEADS = 2
WINDOW = 128
Q_BLOCK = 128
BRANCH_WIDTH = 1024
N_BRANCHES = 3
IN_COLS = 3 * A_HEADS * A_QK_DIM + (B_Q_HEADS + 2 * B_KV_HEADS) * HEAD_DIM + (C_Q_HEADS + 2 * C_KV_HEADS) * HEAD_DIM
N_EXPERTS = 16
N_GROUPS = 4
EXPERTS_PER_GROUP = N_EXPERTS // N_GROUPS
TOP_K = 2
D_EXPERT = 512
N_MOD = 6
ALPHA = (2 * DEPTH) ** 0.25
BETA = (8 * DEPTH) ** -0.25
EPS = 1e-6

kernel_name = "hybrid_diffusion_prefix_trunk_step"


def norm_plain(x):
    xf = x.astype(jnp.float32)
    mu = jnp.mean(xf, -1, keepdims=True)
    var = jnp.mean(jnp.square(xf - mu), -1, keepdims=True)
    return ((xf - mu) * lax.rsqrt(var + EPS)).astype(x.dtype)


def layer_norm(x, gain, bias):
    xf = x.astype(jnp.float32)
    mu = jnp.mean(xf, -1, keepdims=True)
    var = jnp.mean(jnp.square(xf - mu), -1, keepdims=True)
    y = (xf - mu) * lax.rsqrt(var + EPS) * gain.astype(jnp.float32) + bias.astype(jnp.float32)
    return y.astype(x.dtype)


def rms_norm(x, gain):
    xf = x.astype(jnp.float32)
    y = xf * lax.rsqrt(jnp.mean(jnp.square(xf), -1, keepdims=True) + EPS)
    return (y * gain.astype(jnp.float32)).astype(x.dtype)


def modulation(cond, w_mod, b_mod):
    m = jnp.dot(jax.nn.silu(cond), w_mod) + b_mod
    return jnp.split(m[:, None, :], N_MOD, axis=-1)


def modulate(x, shift, scale):
    return norm_plain(x) * (1 + scale) + shift


def axial_rope_tables(n_tokens, dtype):
    rows = n_tokens // GRID_W
    row = jnp.repeat(jnp.arange(rows), GRID_W).astype(jnp.float32)
    col = jnp.tile(jnp.arange(GRID_W), rows).astype(jnp.float32)
    quarter = HEAD_DIM // 4
    inv_freq = ROPE_THETA ** (-jnp.arange(quarter, dtype=jnp.float32) / quarter)
    ang = jnp.stack([row[:, None] * inv_freq, col[:, None] * inv_freq], axis=0)
    return jnp.cos(ang).astype(dtype), jnp.sin(ang).astype(dtype)


def apply_axial_rope(x, cos, sin):
    shp = x.shape
    xs = x.reshape(shp[0], shp[1], -1, 2, 2, HEAD_DIM // 4)
    c = jnp.transpose(cos, (1, 0, 2))[None, :, None]
    s = jnp.transpose(sin, (1, 0, 2))[None, :, None]
    x1, x2 = xs[..., 0, :], xs[..., 1, :]
    out = jnp.stack([x1 * c - x2 * s, x1 * s + x2 * c], axis=-2)
    return out.reshape(shp)


def project_heads(p):
    b, l = p.shape[:2]
    widths = (A_HEADS * A_QK_DIM, A_HEADS * A_QK_DIM, A_HEADS * A_V_DIM,
              B_Q_HEADS * HEAD_DIM, B_KV_HEADS * HEAD_DIM, B_KV_HEADS * HEAD_DIM,
              C_Q_HEADS * HEAD_DIM, C_KV_HEADS * HEAD_DIM, C_KV_HEADS * HEAD_DIM)
    offsets, acc = [], 0
    for w in widths[:-1]:
        acc += w
        offsets.append(acc)
    aq, ak, av, bq, bk, bv, cq, ck, cv = jnp.split(p, offsets, axis=-1)
    return (aq.reshape(b, l, A_HEADS, 2, HEAD_DIM), ak.reshape(b, l, A_HEADS, 2, HEAD_DIM),
            av.reshape(b, l, A_HEADS, A_V_DIM),
            bq.reshape(b, l, B_KV_HEADS, B_Q_HEADS // B_KV_HEADS, HEAD_DIM),
            bk.reshape(b, l, B_KV_HEADS, HEAD_DIM), bv.reshape(b, l, B_KV_HEADS, HEAD_DIM),
            cq.reshape(b, l, C_KV_HEADS, C_Q_HEADS // C_KV_HEADS, HEAD_DIM),
            ck.reshape(b, l, C_KV_HEADS, HEAD_DIM), cv.reshape(b, l, C_KV_HEADS, HEAD_DIM))


def _query_blocks(q):
    b, lq = q.shape[:2]
    return jnp.moveaxis(q.reshape((b, lq // Q_BLOCK, Q_BLOCK) + q.shape[2:]), 1, 0)


def _merge_blocks(o):
    o = jnp.moveaxis(o, 0, 1)
    return o.reshape((o.shape[0], -1) + o.shape[3:])


def diff_lambda(lam_p, lam_init):
    lp = lam_p.astype(jnp.float32)
    return jnp.exp(jnp.sum(lp[0] * lp[1])) - jnp.exp(jnp.sum(lp[2] * lp[3])) + lam_init


def diff_attention(q, k, v, lam):
    scale = HEAD_DIM ** -0.5

    def block(qb):
        s = jnp.einsum("bqhmd,bkhmd->bhmqk", qb, k).astype(jnp.float32) * scale
        p = jax.nn.softmax(s, axis=-1)
        w = p[:, :, 0] - lam * p[:, :, 1]
        return jnp.einsum("bhqk,bkhe->bqhe", w.astype(v.dtype), v)

    return _merge_blocks(lax.map(block, _query_blocks(q)))


def gqa_attention(q, k, v, sink=None):
    scale = HEAD_DIM ** -0.5

    def block(qb):
        s = jnp.einsum("bqhgd,bkhd->bhgqk", qb, k).astype(jnp.float32) * scale
        if sink is not None:
            sk = jnp.broadcast_to(sink.astype(jnp.float32)[None, :, :, None, None], s.shape[:-1] + (1,))
            p = jax.nn.softmax(jnp.concatenate([sk, s], axis=-1), axis=-1)[..., 1:]
        else:
            p = jax.nn.softmax(s, axis=-1)
        return jnp.einsum("bhgqk,bkhd->bqhgd", p.astype(v.dtype), v)

    return _merge_blocks(lax.map(block, _query_blocks(q)))


def window_attention(q, k, v, k_ctx, v_ctx, sink):
    b, t = q.shape[:2]
    nblk = t // Q_BLOCK
    band = Q_BLOCK + 2 * WINDOW
    scale = HEAD_DIM ** -0.5
    pad = ((0, 0), (WINDOW, WINDOW), (0, 0), (0, 0))
    kp, vp = jnp.pad(k, pad), jnp.pad(v, pad)
    idx = jnp.arange(nblk)[:, None] * Q_BLOCK + jnp.arange(band)[None, :]
    kb, vb = kp[:, idx], vp[:, idx]
    qb = q.reshape((b, nblk, Q_BLOCK) + q.shape[2:])
    q_pos = jnp.arange(nblk)[:, None] * Q_BLOCK + jnp.arange(Q_BLOCK)[None, :]
    k_pos = idx - WINDOW
    valid = ((jnp.abs(q_pos[:, :, None] - k_pos[:, None, :]) <= WINDOW)
             & (k_pos[:, None, :] >= 0) & (k_pos[:, None, :] < t))
    s_band = jnp.einsum("bnqhgd,bnkhd->bnhgqk", qb, kb).astype(jnp.float32) * scale
    s_band = jnp.where(valid[None, :, None, None], s_band, -jnp.inf)
    s_ctx = jnp.einsum("bnqhgd,bkhd->bnhgqk", qb, k_ctx).astype(jnp.float32) * scale
    sk = jnp.broadcast_to(sink.astype(jnp.float32)[None, None, :, :, None, None], s_ctx.shape[:-1] + (1,))
    p = jax.nn.softmax(jnp.concatenate([sk, s_ctx, s_band], axis=-1), axis=-1)
    n_ctx = k_ctx.shape[1]
    p_ctx = p[..., 1:1 + n_ctx].astype(v.dtype)
    p_band = p[..., 1 + n_ctx:].astype(v.dtype)
    o = (jnp.einsum("bnhgqk,bkhd->bnqhgd", p_ctx, v_ctx)
         + jnp.einsum("bnhgqk,bnkhd->bnqhgd", p_band, vb))
    return o.reshape(q.shape)


def merge_branches(h, a_o, b_o, c_o, w_gate, w_branch, w_o):
    b, l = h.shape[:2]
    g_a, g_b, g_c = jnp.split(jax.nn.sigmoid(h @ w_gate), N_BRANCHES, axis=-1)
    y = (g_a * (a_o.reshape(b, l, BRANCH_WIDTH) @ w_branch[0])
         + g_b * (b_o.reshape(b, l, BRANCH_WIDTH) @ w_branch[1])
         + g_c * (c_o.reshape(b, l, BRANCH_WIDTH) @ w_branch[2]))
    return y @ w_o


def grouped_moe(h, w_router, w_e_gate, w_e_up, w_e_down):
    shp = h.shape
    x = h.reshape(-1, D_MODEL)
    probs = jax.nn.softmax((x @ w_router).astype(jnp.float32), axis=-1)
    per_group = probs.reshape(-1, N_GROUPS, EXPERTS_PER_GROUP)
    group_score = jnp.sum(lax.top_k(per_group, TOP_K)[0], axis=-1)
    best_group = jnp.argmax(group_score, axis=-1)
    in_group = (jnp.arange(N_EXPERTS) // EXPERTS_PER_GROUP)[None, :] == best_group[:, None]
    top_w, top_i = lax.top_k(jnp.where(in_group, probs, -1.0), TOP_K)
    top_w = top_w / jnp.sum(top_w, axis=-1, keepdims=True)
    combine = jnp.sum(jax.nn.one_hot(top_i, N_EXPERTS, dtype=jnp.float32) * top_w[..., None], axis=1)
    hid = jax.nn.silu(jnp.einsum("td,edf->tef", x, w_e_gate)) * jnp.einsum("td,edf->tef", x, w_e_up)
    hid = hid * combine.astype(x.dtype)[..., None]
    return jnp.einsum("tef,efd->td", hid, w_e_down).reshape(shp)


def setup_inputs(seed: int = 0) -> dict:
    key = jax.random.key(seed)
    ks = jax.random.split(key, 26)
    n = lambda k, shp, s: jax.random.normal(k, shp, jnp.float32) * s
    return {
        "x_prompt": n(ks[0], (BATCH, SEQ, D_MODEL), 1.0),
        "x_sample": n(ks[1], (DEC_BATCH, DEC_SEQ, D_MODEL), 1.0),
        "cache_kv_a": n(ks[2], (DEC_BATCH, DEPTH, 2, PAST_LEN, A_HEADS, A_QK_DIM), 1.0),
        "cache_kv_b": n(ks[3], (DEC_BATCH, DEPTH, 2, PAST_LEN, B_KV_HEADS, HEAD_DIM), 1.0),
        "cache_kv_c": n(ks[4], (DEC_BATCH, DEPTH, 2, PAST_LEN, C_KV_HEADS, HEAD_DIM), 1.0),
        "c": n(ks[5], (DEC_BATCH, D_MODEL), 1.0),
        "c_ctx": n(ks[6], (D_MODEL,), 1.0),
        "w_in": n(ks[7], (DEPTH, D_MODEL, IN_COLS), D_MODEL ** -0.5),
        "w_gate": n(ks[8], (DEPTH, D_MODEL, N_BRANCHES * D_MODEL), D_MODEL ** -0.5),
        "w_branch": n(ks[9], (DEPTH, N_BRANCHES, BRANCH_WIDTH, D_MODEL), BRANCH_WIDTH ** -0.5),
        "w_o": n(ks[10], (DEPTH, D_MODEL, D_MODEL), BETA * D_MODEL ** -0.5),
        "w_mod": n(ks[11], (DEPTH, D_MODEL, N_MOD * D_MODEL), D_MODEL ** -0.5),
        "b_mod": n(ks[12], (DEPTH, N_MOD * D_MODEL), 0.01),
        "ln_gain": 1.0 + n(ks[13], (DEPTH, 2, D_MODEL), 0.01),
        "ln_bias": n(ks[14], (DEPTH, 2, D_MODEL), 0.01),
        "diff_lam": n(ks[15], (DEPTH, 4, HEAD_DIM), 0.1),
        "diff_subln": 1.0 + n(ks[16], (DEPTH, A_V_DIM), 0.01),
        "qk_gain": 1.0 + n(ks[17], (DEPTH, 2, HEAD_DIM), 0.01),
        "sink": n(ks[18], (DEPTH, C_Q_HEADS), 0.5),
        "w_router": n(ks[19], (D_MODEL, N_EXPERTS), D_MODEL ** -0.5),
        "w_e_gate": n(ks[20], (DEPTH, N_EXPERTS, D_MODEL, D_EXPERT), D_MODEL ** -0.5),
        "w_e_up": n(ks[21], (DEPTH, N_EXPERTS, D_MODEL, D_EXPERT), D_MODEL ** -0.5),
        "w_e_down": n(ks[22], (DEPTH, N_EXPERTS, D_EXPERT, D_MODEL), BETA * D_EXPERT ** -0.5),
    }


def reference(x_prompt, x_sample, cache_kv_a, cache_kv_b, cache_kv_c, c, c_ctx, w_in, w_gate, w_branch,
              w_o, w_mod, b_mod, ln_gain, ln_bias, diff_lam, diff_subln, qk_gain, sink, w_router,
              w_e_gate, w_e_up, w_e_down):
    y_p = x_prompt
    y_s = x_sample
    cos, sin = axial_rope_tables(x_sample.shape[1], x_sample.dtype)
    db, past = cache_kv_a.shape[0], cache_kv_a.shape[3]
    new_a, new_b, new_c = [], [], []
    for l in range(DEPTH):
        lam_init = 0.8 - 0.6 * math.exp(-0.3 * l)
        lam = diff_lambda(diff_lam[l], lam_init)
        sink_l = sink[l].reshape(C_KV_HEADS, C_Q_HEADS // C_KV_HEADS)

        sh1, sc1, g1, sh2, sc2, g2 = modulation(c_ctx[None], w_mod[l], b_mod[l])
        h = modulate(y_p, sh1, sc1)
        aq, ak, av, bq, bk, bv, cq, ck, cv = project_heads(h @ w_in[l])
        bq, bk = rms_norm(bq, qk_gain[l, 0]), rms_norm(bk, qk_gain[l, 1])
        a_o = rms_norm(diff_attention(aq, ak, av, lam), diff_subln[l]) * (1 - lam_init)
        b_o = gqa_attention(bq, bk, bv)
        c_o = gqa_attention(cq, ck, cv, sink_l)
        y_p = layer_norm(ALPHA * y_p + g1 * merge_branches(h, a_o, b_o, c_o, w_gate[l], w_branch[l], w_o[l]),
                         ln_gain[l, 0], ln_bias[l, 0])
        h2 = modulate(y_p, sh2, sc2)
        y_p = layer_norm(ALPHA * y_p + g2 * grouped_moe(h2, w_router, w_e_gate[l], w_e_up[l], w_e_down[l]),
                         ln_gain[l, 1], ln_bias[l, 1])
        new_a.append(jnp.stack([ak.reshape(ak.shape[:3] + (A_QK_DIM,)), av], axis=1))
        new_b.append(jnp.stack([bk, bv], axis=1))
        new_c.append(jnp.stack([ck, cv], axis=1))

        sh1, sc1, g1, sh2, sc2, g2 = modulation(c, w_mod[l], b_mod[l])
        h = modulate(y_s, sh1, sc1)
        aq, ak, av, bq, bk, bv, cq, ck, cv = project_heads(h @ w_in[l])
        bq, bk = rms_norm(bq, qk_gain[l, 0]), rms_norm(bk, qk_gain[l, 1])
        aq, ak = apply_axial_rope(aq, cos, sin), apply_axial_rope(ak, cos, sin)
        bq, bk = apply_axial_rope(bq, cos, sin), apply_axial_rope(bk, cos, sin)
        cq, ck = apply_axial_rope(cq, cos, sin), apply_axial_rope(ck, cos, sin)
        ctx_ak = cache_kv_a[:, l, 0].reshape(db, past, A_HEADS, 2, HEAD_DIM)
        a_o = diff_attention(aq, jnp.concatenate([ctx_ak, ak], axis=1),
                             jnp.concatenate([cache_kv_a[:, l, 1], av], axis=1), lam)
        a_o = rms_norm(a_o, diff_subln[l]) * (1 - lam_init)
        b_o = gqa_attention(bq, jnp.concatenate([cache_kv_b[:, l, 0], bk], axis=1),
                            jnp.concatenate([cache_kv_b[:, l, 1], bv], axis=1))
        c_o = window_attention(cq, ck, cv, cache_kv_c[:, l, 0], cache_kv_c[:, l, 1], sink_l)
        y_s = layer_norm(ALPHA * y_s + g1 * merge_branches(h, a_o, b_o, c_o, w_gate[l], w_branch[l], w_o[l]),
                         ln_gain[l, 0], ln_bias[l, 0])
        h2 = modulate(y_s, sh2, sc2)
        y_s = layer_norm(ALPHA * y_s + g2 * grouped_moe(h2, w_router, w_e_gate[l], w_e_up[l], w_e_down[l]),
                         ln_gain[l, 1], ln_bias[l, 1])

    new_kv_a = jnp.stack(new_a, axis=1)
    new_kv_b = jnp.stack(new_b, axis=1)
    new_kv_c = jnp.stack(new_c, axis=1)
    return (y_p, y_s, new_kv_a, new_kv_b, new_kv_c)
```

```python
import functools
import math

import jax
import jax.numpy as jnp
from jax import lax
from jax.experimental import pallas as pl
from jax.experimental.pallas import tpu as pltpu

F32 = jnp.float32
BF16 = jnp.bfloat16

D_MODEL = 2048
HEAD_DIM = 128
GRID_W = 64
ROPE_THETA = 10000.0
WINDOW = 128
N_EXPERTS = 16
N_GROUPS = 4
EXPERTS_PER_GROUP = N_EXPERTS // N_GROUPS
D_EXPERT = 512
N_MOD = 6
DEPTH = 2
ALPHA = (2 * DEPTH) ** 0.25
EPS = 1e-6
IN_COLS = 6144
BRANCH_WIDTH = 1024
SCALE = HEAD_DIM ** -0.5
NEG_BIG = -1e30

COL_AQ, COL_AK, COL_AV = 0, 1024, 2048
COL_BQ, COL_BK, COL_BV = 3072, 4096, 4352
COL_CQ, COL_CK, COL_CV = 4608, 5632, 5888

LANES = 128
VMEM_LIMIT = 56 * 1024 * 1024

TM = 1024
TN_PROJ = 512
TM_SMALL = 512


def _cparams(sem):
    return pltpu.CompilerParams(dimension_semantics=sem, vmem_limit_bytes=VMEM_LIMIT)


def _dot(a, b):
    return jnp.dot(a, b, preferred_element_type=F32)


def _dot_nt(a, b):
    return lax.dot_general(a, b, (((1,), (1,)), ((), ())), preferred_element_type=F32)


def _norm_rows(x):
    mu = jnp.mean(x, axis=-1, keepdims=True)
    xc = x - mu
    var = jnp.mean(xc * xc, axis=-1, keepdims=True)
    return xc * lax.rsqrt(var + EPS)


def _rms(v, gain):
    ms = jnp.mean(v * v, axis=-1, keepdims=True)
    return v * lax.rsqrt(ms + EPS) * gain


def _rope(v, cos, sin_signed):
    lane = lax.broadcasted_iota(jnp.int32, v.shape, 1)
    first_half = (lane % 64) < 32
    partner = jnp.where(first_half, pltpu.roll(v, 96, 1), pltpu.roll(v, 32, 1))
    return v * cos + partner * sin_signed


N_COND = 3
MOD_ROWS = 8
TN_MOD = 1024


def _mod_kernel(cond_ref, w_ref, b_ref, o_ref):
    w = w_ref[...]
    row_idx = lax.broadcasted_iota(jnp.int32, (MOD_ROWS, TN_MOD), 0)
    out = jnp.zeros((MOD_ROWS, TN_MOD), F32)
    for r in range(N_COND):
        c = cond_ref[:, r:r + 1]
        s = c / (1.0 + jnp.exp(-c))
        m = jnp.sum(w * s, axis=0, keepdims=True) + b_ref[...]
        out = jnp.where(row_idx == r, m, out)
    o_ref[...] = out


def _modulation(cond_t, w_mod, b_mod):
    n = N_MOD * D_MODEL
    return pl.pallas_call(
        _mod_kernel,
        out_shape=jax.ShapeDtypeStruct((DEPTH, MOD_ROWS, n), F32),
        grid=(DEPTH, n // TN_MOD),
        in_specs=[
            pl.BlockSpec((D_MODEL, MOD_ROWS), lambda l, j: (0, 0)),
            pl.BlockSpec((None, D_MODEL, TN_MOD), lambda l, j: (l, 0, j)),
            pl.BlockSpec((None, 1, TN_MOD), lambda l, j: (l, 0, j)),
        ],
        out_specs=pl.BlockSpec((None, MOD_ROWS, TN_MOD), lambda l, j: (l, 0, j)),
        compiler_params=_cparams(("parallel", "parallel")),
        name="modulation",
    )(cond_t, w_mod, b_mod.reshape(DEPTH, 1, n))


def _modulate_to_scratch(x_ref, sh_ref, sc_ref, h_scr):
    h = _norm_rows(x_ref[...]) * (1.0 + sc_ref[...]) + sh_ref[...]
    h_scr[...] = h.astype(BF16)


def _inproj_kernel(x_ref, sh_ref, sc_ref, w_ref, qk_gain_ref, cos_ref, sin_ref, o_ref, h_scr, *, rope):
    j = pl.program_id(1)
    gain_q = qk_gain_ref[0:1, :]
    gain_k = qk_gain_ref[1:2, :]

    @pl.when(j == 0)
    def _():
        _modulate_to_scratch(x_ref, sh_ref, sc_ref, h_scr)

    acc = _dot(h_scr[...], w_ref[...].astype(BF16))
    n_chunks = TN_PROJ // LANES

    def chunk(c):
        return acc[:, c * LANES:(c + 1) * LANES]

    def store(c, v):
        o_ref[:, c * LANES:(c + 1) * LANES] = v.astype(o_ref.dtype)

    def rp(v):
        return _rope(v, cos_ref[...], sin_ref[...]) if rope else v

    @pl.when((j <= 3) | (j == 9) | (j == 10))
    def _():
        for c in range(n_chunks):
            store(c, rp(chunk(c)))

    @pl.when((j == 4) | (j == 5))
    def _():
        o_ref[...] = acc.astype(o_ref.dtype)

    @pl.when((j == 6) | (j == 7))
    def _():
        for c in range(n_chunks):
            store(c, rp(_rms(chunk(c), gain_q)))

    @pl.when(j == 8)
    def _():
        for c in range(2):
            store(c, rp(_rms(chunk(c), gain_k)))
        for c in range(2, n_chunks):
            store(c, chunk(c))

    @pl.when(j == 11)
    def _():
        for c in range(2):
            store(c, rp(chunk(c)))
        for c in range(2, n_chunks):
            store(c, chunk(c))


def _gate_kernel(x_ref, sh_ref, sc_ref, w_ref, o_ref, h_scr):
    @pl.when(pl.program_id(1) == 0)
    def _():
        _modulate_to_scratch(x_ref, sh_ref, sc_ref, h_scr)

    acc = _dot(h_scr[...], w_ref[...].astype(BF16))
    o_ref[...] = (1.0 / (1.0 + jnp.exp(-acc))).astype(o_ref.dtype)


def _mod_spec(which, row_of_tile):
    return pl.BlockSpec((None, 1, D_MODEL), lambda i, *_: (row_of_tile(i), 0, which))


def _in_projection(x, mod_l, w_in_l, qk_gain_l, cos_t, sin_t, *, row_of_tile, rope, out_dtype):
    t = x.shape[0]
    tiles_per_seq = cos_t.shape[0] // TM
    return pl.pallas_call(
        functools.partial(_inproj_kernel, rope=rope),
        out_shape=jax.ShapeDtypeStruct((t, IN_COLS), out_dtype),
        grid=(t // TM, IN_COLS // TN_PROJ),
        in_specs=[
            pl.BlockSpec((TM, D_MODEL), lambda i, j: (i, 0)),
            _mod_spec(0, row_of_tile),
            _mod_spec(1, row_of_tile),
            pl.BlockSpec((D_MODEL, TN_PROJ), lambda i, j: (0, j)),
            pl.BlockSpec((2, HEAD_DIM), lambda i, j: (0, 0)),
            pl.BlockSpec((TM, HEAD_DIM), lambda i, j: (i % tiles_per_seq, 0)),
            pl.BlockSpec((TM, HEAD_DIM), lambda i, j: (i % tiles_per_seq, 0)),
        ],
        out_specs=pl.BlockSpec((TM, TN_PROJ), lambda i, j: (i, j)),
        scratch_shapes=[pltpu.VMEM((TM, D_MODEL), BF16)],
        compiler_params=_cparams(("parallel", "arbitrary")),
        name="in_projection",
    )(x, mod_l, mod_l, w_in_l, qk_gain_l, cos_t, sin_t)


def _branch_gates(x, mod_l, w_gate_l, *, row_of_tile):
    t = x.shape[0]
    n = w_gate_l.shape[1]
    return pl.pallas_call(
        _gate_kernel,
        out_shape=jax.ShapeDtypeStruct((t, n), BF16),
        grid=(t // TM, n // TN_PROJ),
        in_specs=[
            pl.BlockSpec((TM, D_MODEL), lambda i, j: (i, 0)),
            _mod_spec(0, row_of_tile),
            _mod_spec(1, row_of_tile),
            pl.BlockSpec((D_MODEL, TN_PROJ), lambda i, j: (0, j)),
        ],
        out_specs=pl.BlockSpec((TM, TN_PROJ), lambda i, j: (i, j)),
        scratch_shapes=[pltpu.VMEM((TM, D_MODEL), BF16)],
        compiler_params=_cparams(("parallel", "arbitrary")),
        name="branch_gates",
    )(x, mod_l, mod_l, w_gate_l)


def _diff_lambda(lam_ref, lam_init):
    lp = lam_ref[...]
    t1 = jnp.sum(lp[0:1] * lp[1:2], axis=-1, keepdims=True)
    t2 = jnp.sum(lp[2:3] * lp[3:4], axis=-1, keepdims=True)
    return jnp.exp(t1) - jnp.exp(t2) + lam_init


def _softmax_rows(s, sink=None):
    m = jnp.max(s, axis=-1, keepdims=True)
    if sink is not None:
        m = jnp.maximum(m, sink)
    e = jnp.exp(s - m)
    den = jnp.sum(e, axis=-1, keepdims=True)
    if sink is not None:
        den = den + jnp.exp(sink - m)
    return e * (1.0 / den)


def _attn_prompt_kernel(p_ref, lam_ref, subln_ref, sink_ref, o_ref, *, lam_init):
    lam = _diff_lambda(lam_ref, lam_init)

    def blk(c0, w):
        return p_ref[:, c0:c0 + w].astype(BF16)

    for h in range(4):
        probs = []
        for m in range(2):
            q = blk(COL_AQ + h * 256 + m * HEAD_DIM, HEAD_DIM)
            k = blk(COL_AK + h * 256 + m * HEAD_DIM, HEAD_DIM)
            probs.append(_softmax_rows(_dot_nt(q, k) * SCALE))
        w = probs[0] - lam * probs[1]
        o = _dot(w.astype(BF16), blk(COL_AV + h * 256, 256))
        o = _rms(o, subln_ref[...]) * (1.0 - lam_init)
        o_ref[:, h * 256:(h + 1) * 256] = o.astype(o_ref.dtype)

    for mixer, (cq, ck, cv) in enumerate(((COL_BQ, COL_BK, COL_BV), (COL_CQ, COL_CK, COL_CV))):
        for kvh in range(2):
            k = blk(ck + kvh * HEAD_DIM, HEAD_DIM)
            v = blk(cv + kvh * HEAD_DIM, HEAD_DIM)
            for g in range(4):
                hq = kvh * 4 + g
                q = blk(cq + hq * HEAD_DIM, HEAD_DIM)
                sink = sink_ref[hq] if mixer == 1 else None
                p = _softmax_rows(_dot_nt(q, k) * SCALE, sink)
                o = _dot(p.astype(BF16), v)
                c0 = BRANCH_WIDTH * (1 + mixer) + hq * HEAD_DIM
                o_ref[:, c0:c0 + HEAD_DIM] = o.astype(o_ref.dtype)


def _attention_prompt(p, diff_lam_l, subln_l, sink_l, *, lam_init, seq):
    t = p.shape[0]
    return pl.pallas_call(
        functools.partial(_attn_prompt_kernel, lam_init=lam_init),
        out_shape=jax.ShapeDtypeStruct((t, 3 * BRANCH_WIDTH), BF16),
        grid=(t // seq,),
        in_specs=[
            pl.BlockSpec((seq, IN_COLS), lambda b: (b, 0)),
            pl.BlockSpec((4, HEAD_DIM), lambda b: (0, 0)),
            pl.BlockSpec((1, 256), lambda b: (0, 0)),
            pl.BlockSpec(memory_space=pltpu.SMEM),
        ],
        out_specs=pl.BlockSpec((seq, 3 * BRANCH_WIDTH), lambda b: (b, 0)),
        compiler_params=_cparams(("parallel",)),
        name="attention_prompt",
    )(p, diff_lam_l, subln_l, sink_l)


TQ = 512
TK = 512


def _flash_update(idx, s, v, m_scr, l_scr, acc_scr):
    m_prev = m_scr[idx]
    m_new = jnp.maximum(m_prev, jnp.max(s, axis=-1, keepdims=True))
    alpha = jnp.exp(m_prev - m_new)
    p = jnp.exp(s - m_new)
    l_scr[idx] = alpha * l_scr[idx] + jnp.sum(p, axis=-1, keepdims=True)
    acc_scr[idx] = alpha * acc_scr[idx] + _dot(p.astype(BF16), v)
    m_scr[idx] = m_new


def _flash_init(m_scr, l_scr, acc_scr):
    m_scr[...] = jnp.full(m_scr.shape, NEG_BIG, F32)
    l_scr[...] = jnp.zeros(l_scr.shape, F32)
    acc_scr[...] = jnp.zeros(acc_scr.shape, F32)


def _diff_sample_kernel(q_ref, k_ref, v_ref, kc_ref, vc_ref, lam_ref, subln_ref, o_ref,
                        m_scr, l_scr, acc_scr, *, lam_init):
    kk = pl.program_id(2)

    @pl.when(kk == 0)
    def _():
        _flash_init(m_scr, l_scr, acc_scr)

    def process(kb_ref, vb_ref):
        for h in range(4):
            v = vb_ref[:, h * 256:(h + 1) * 256].astype(BF16)
            for m in range(2):
                c0 = h * 256 + m * HEAD_DIM
                k = kb_ref[:, c0:c0 + HEAD_DIM].astype(BF16)
                s = _dot_nt(q_ref[:, c0:c0 + HEAD_DIM], k) * SCALE
                _flash_update(h * 2 + m, s, v, m_scr, l_scr, acc_scr)

    @pl.when(kk == 0)
    def _():
        process(kc_ref, vc_ref)

    @pl.when(kk > 0)
    def _():
        process(k_ref, v_ref)

    @pl.when(kk == pl.num_programs(2) - 1)
    def _():
        lam = _diff_lambda(lam_ref, lam_init)
        for h in range(4):
            o1 = acc_scr[2 * h] * (1.0 / l_scr[2 * h])
            o2 = acc_scr[2 * h + 1] * (1.0 / l_scr[2 * h + 1])
            o = _rms(o1 - lam * o2, subln_ref[...]) * (1.0 - lam_init)
            o_ref[:, h * 256:(h + 1) * 256] = o.astype(o_ref.dtype)


def _gqa_sample_kernel(q_ref, k_ref, v_ref, kc_ref, vc_ref, o_ref, m_scr, l_scr, acc_scr):
    kk = pl.program_id(2)

    @pl.when(kk == 0)
    def _():
        _flash_init(m_scr, l_scr, acc_scr)

    def process(kb_ref, vb_ref):
        for kvh in range(2):
            k = kb_ref[:, kvh * HEAD_DIM:(kvh + 1) * HEAD_DIM].astype(BF16)
            v = vb_ref[:, kvh * HEAD_DIM:(kvh + 1) * HEAD_DIM].astype(BF16)
            for g in range(4):
                hq = kvh * 4 + g
                s = _dot_nt(q_ref[:, hq * HEAD_DIM:(hq + 1) * HEAD_DIM], k) * SCALE
                _flash_update(hq, s, v, m_scr, l_scr, acc_scr)

    @pl.when(kk == 0)
    def _():
        process(kc_ref, vc_ref)

    @pl.when(kk > 0)
    def _():
        process(k_ref, v_ref)

    @pl.when(kk == pl.num_programs(2) - 1)
    def _():
        for hq in range(8):
            o = acc_scr[hq] * (1.0 / l_scr[hq])
            o_ref[:, hq * HEAD_DIM:(hq + 1) * HEAD_DIM] = o.astype(o_ref.dtype)


def _flash_sample(kernel, qkv, cache, l, *, q_col, k_col, v_col, kv_width, n_state, acc_width, extra,
                  extra_specs, name, dec_seq):
    t = qkv.shape[0]
    nb = t // dec_seq
    nq = dec_seq // TQ
    nk = dec_seq // TK
    past = cache.shape[3]
    q_blk, k_blk, v_blk = q_col // BRANCH_WIDTH, k_col // kv_width, v_col // kv_width
    kv_row = lambda b, qi, kk: b * nk + jnp.maximum(kk - 1, 0)
    return pl.pallas_call(
        kernel,
        out_shape=jax.ShapeDtypeStruct((t, BRANCH_WIDTH), BF16),
        grid=(nb, nq, nk + 1),
        in_specs=[
            pl.BlockSpec((TQ, BRANCH_WIDTH), lambda b, qi, kk: (b * nq + qi, q_blk)),
            pl.BlockSpec((TK, kv_width), lambda b, qi, kk: (kv_row(b, qi, kk), k_blk)),
            pl.BlockSpec((TK, kv_width), lambda b, qi, kk: (kv_row(b, qi, kk), v_blk)),
            pl.BlockSpec((None, None, None, past, kv_width), lambda b, qi, kk: (b, l, 0, 0, 0)),
            pl.BlockSpec((None, None, None, past, kv_width), lambda b, qi, kk: (b, l, 1, 0, 0)),
        ] + extra_specs,
        out_specs=pl.BlockSpec((TQ, BRANCH_WIDTH), lambda b, qi, kk: (b * nq + qi, 0)),
        scratch_shapes=[
            pltpu.VMEM((n_state, TQ, 1), F32),
            pltpu.VMEM((n_state, TQ, 1), F32),
            pltpu.VMEM((n_state, TQ, acc_width), F32),
        ],
        compiler_params=_cparams(("parallel", "parallel", "arbitrary")),
        name=name,
    )(qkv, qkv, qkv, cache, cache, *extra)


def _window_sample_kernel(q0_ref, q1_ref, kp_ref, kc_ref, kn_ref, vp_ref, vc_ref, vn_ref, kctx_ref, vctx_ref,
                          sink_ref, o_ref, *, dec_seq):
    qi = pl.program_id(1)
    q_start = qi * TQ
    half = TQ // 2
    q_pos = q_start + lax.broadcasted_iota(jnp.int32, (TQ, 1), 0)
    segs = ((kp_ref, vp_ref, q_start - half, half), (kc_ref, vc_ref, q_start, TQ),
            (kn_ref, vn_ref, q_start + TQ, half))
    valid = []
    for _, _, start, n in segs:
        k_pos = start + lax.broadcasted_iota(jnp.int32, (TQ, n), 1)
        valid.append((jnp.abs(q_pos - k_pos) <= WINDOW) & (k_pos >= 0) & (k_pos < dec_seq))
    for kvh in range(2):
        q_ref = q0_ref if kvh == 0 else q1_ref
        lo, hi = kvh * HEAD_DIM, (kvh + 1) * HEAD_DIM
        kctx = kctx_ref[:, lo:hi].astype(BF16)
        vctx = vctx_ref[:, lo:hi].astype(BF16)
        for g in range(4):
            hq = kvh * 4 + g
            q = q_ref[:, g * HEAD_DIM:(g + 1) * HEAD_DIM]
            sink = sink_ref[hq]
            scores = [_dot_nt(q, kctx) * SCALE]
            for (k_ref, _, _, _), ok in zip(segs, valid):
                scores.append(jnp.where(ok, _dot_nt(q, k_ref[:, lo:hi]) * SCALE, NEG_BIG))
            m = jnp.maximum(scores[0].max(axis=-1, keepdims=True), sink)
            for s in scores[1:]:
                m = jnp.maximum(m, s.max(axis=-1, keepdims=True))
            es = [jnp.exp(s - m) for s in scores]
            den = jnp.exp(sink - m)
            for e in es:
                den = den + jnp.sum(e, axis=-1, keepdims=True)
            o = _dot(es[0].astype(BF16), vctx)
            for e, (_, v_ref, _, _) in zip(es[1:], segs):
                o = o + _dot(e.astype(BF16), v_ref[:, lo:hi])
            o = o * (1.0 / den)
            o_ref[:, hq * HEAD_DIM:(hq + 1) * HEAD_DIM] = o.astype(o_ref.dtype)


def _window_sample(qkv, cache, sink_l, l, *, dec_seq):
    t = qkv.shape[0]
    nb = t // dec_seq
    nq = dec_seq // TQ
    half = TQ // 2
    n_half = dec_seq // half
    past = cache.shape[3]
    kvw = 2 * HEAD_DIM
    q_blk = COL_CQ // 512
    k_blk, v_blk = COL_CK // kvw, COL_CV // kvw
    prev_row = lambda b, qi: b * n_half + jnp.maximum(2 * qi - 1, 0)
    next_row = lambda b, qi: b * n_half + jnp.minimum(2 * qi + 2, n_half - 1)
    return pl.pallas_call(
        functools.partial(_window_sample_kernel, dec_seq=dec_seq),
        out_shape=jax.ShapeDtypeStruct((t, BRANCH_WIDTH), BF16),
        grid=(nb, nq),
        in_specs=[
            pl.BlockSpec((TQ, 512), lambda b, qi: (b * nq + qi, q_blk)),
            pl.BlockSpec((TQ, 512), lambda b, qi: (b * nq + qi, q_blk + 1)),
            pl.BlockSpec((half, kvw), lambda b, qi: (prev_row(b, qi), k_blk)),
            pl.BlockSpec((TQ, kvw), lambda b, qi: (b * nq + qi, k_blk)),
            pl.BlockSpec((half, kvw), lambda b, qi: (next_row(b, qi), k_blk)),
            pl.BlockSpec((half, kvw), lambda b, qi: (prev_row(b, qi), v_blk)),
            pl.BlockSpec((TQ, kvw), lambda b, qi: (b * nq + qi, v_blk)),
            pl.BlockSpec((half, kvw), lambda b, qi: (next_row(b, qi), v_blk)),
            pl.BlockSpec((None, None, None, past, kvw), lambda b, qi: (b, l, 0, 0, 0)),
            pl.BlockSpec((None, None, None, past, kvw), lambda b, qi: (b, l, 1, 0, 0)),
            pl.BlockSpec(memory_space=pltpu.SMEM),
        ],
        out_specs=pl.BlockSpec((TQ, BRANCH_WIDTH), lambda b, qi: (b * nq + qi, 0)),
        compiler_params=_cparams(("parallel", "parallel")),
        name="window_attention",
    )(qkv, qkv, qkv, qkv, qkv, qkv, qkv, qkv, cache, cache, sink_l)


def _merge_kernel(oa_ref, ob_ref, oc_ref, g_ref, w_ref, y_ref, acc_scr):
    r = pl.program_id(2)

    def contrib(o_ref):
        return g_ref[...].astype(F32) * _dot(o_ref[...], w_ref[...].astype(BF16))

    @pl.when(r == 0)
    def _():
        acc_scr[...] = contrib(oa_ref)

    @pl.when(r == 1)
    def _():
        acc_scr[...] += contrib(ob_ref)

    @pl.when(r == 2)
    def _():
        y_ref[...] = (acc_scr[...] + contrib(oc_ref)).astype(y_ref.dtype)


def _merge_branches(o_arrays, o_blocks, gates, w_branch_l):
    t = gates.shape[0]
    tn = 1024
    nn = D_MODEL // tn
    o_specs = [pl.BlockSpec((TM, BRANCH_WIDTH), functools.partial(lambda i, n, r, blk: (i, blk), blk=blk))
               for blk in o_blocks]
    return pl.pallas_call(
        _merge_kernel,
        out_shape=jax.ShapeDtypeStruct((t, D_MODEL), BF16),
        grid=(t // TM, nn, 3),
        in_specs=o_specs + [
            pl.BlockSpec((TM, tn), lambda i, n, r: (i, r * nn + n)),
            pl.BlockSpec((None, BRANCH_WIDTH, tn), lambda i, n, r: (r, 0, n)),
        ],
        out_specs=pl.BlockSpec((TM, tn), lambda i, n, r: (i, n)),
        scratch_shapes=[pltpu.VMEM((TM, tn), F32)],
        compiler_params=_cparams(("parallel", "parallel", "arbitrary")),
        name="merge_branches",
    )(*o_arrays, gates, w_branch_l)


def _layer_norm_rows(v, gain, bias):
    return _norm_rows(v) * gain + bias


def _outproj_kernel(y_ref, w_ref, x_ref, g_ref, gain_ref, bias_ref, o_ref, z_scr):
    n = pl.program_id(1)
    n_blocks, _, tn = z_scr.shape
    z_scr[n] = _dot(y_ref[...], w_ref[...].astype(BF16))

    @pl.when(n == n_blocks - 1)
    def _():
        for b in range(n_blocks):
            cols = slice(b * tn, (b + 1) * tn)
            o_ref[:, cols] = ALPHA * x_ref[:, cols] + g_ref[:, cols] * z_scr[b]
        o_ref[...] = _layer_norm_rows(o_ref[...], gain_ref[...], bias_ref[...])


def _out_projection(y, w_o_l, x, mod_l, ln_gain_l, ln_bias_l, *, row_of_tile):
    t = x.shape[0]
    tn = 512
    return pl.pallas_call(
        _outproj_kernel,
        out_shape=jax.ShapeDtypeStruct((t, D_MODEL), F32),
        grid=(t // TM_SMALL, D_MODEL // tn),
        in_specs=[
            pl.BlockSpec((TM_SMALL, D_MODEL), lambda i, n: (i, 0)),
            pl.BlockSpec((D_MODEL, tn), lambda i, n: (0, n)),
            pl.BlockSpec((TM_SMALL, D_MODEL), lambda i, n: (i, 0)),
            _mod_spec(2, row_of_tile),
            pl.BlockSpec((1, D_MODEL), lambda i, n: (0, 0)),
            pl.BlockSpec((1, D_MODEL), lambda i, n: (0, 0)),
        ],
        out_specs=pl.BlockSpec((TM_SMALL, D_MODEL), lambda i, n: (i, 0)),
        scratch_shapes=[pltpu.VMEM((D_MODEL // tn, TM_SMALL, tn), F32)],
        compiler_params=_cparams(("parallel", "arbitrary")),
        name="out_projection",
    )(y, w_o_l, x, mod_l, ln_gain_l, ln_bias_l)


def _route(p):
    rows = [p[e:e + 1, :] for e in range(N_EXPERTS)]
    best_score, best_group = None, None
    for g in range(N_GROUPS):
        members = rows[g * EXPERTS_PER_GROUP:(g + 1) * EXPERTS_PER_GROUP]
        score = None
        for a in range(EXPERTS_PER_GROUP):
            for b in range(a + 1, EXPERTS_PER_GROUP):
                pair = members[a] + members[b]
                score = pair if score is None else jnp.maximum(score, pair)
        if g == 0:
            best_score, best_group = score, jnp.zeros(score.shape, F32)
        else:
            better = score > best_score
            best_group = jnp.where(better, float(g), best_group)
            best_score = jnp.where(better, score, best_score)
    e_idx = lax.broadcasted_iota(jnp.int32, p.shape, 0).astype(F32)
    g_idx = jnp.floor(e_idx * (1.0 / EXPERTS_PER_GROUP))
    masked = jnp.where(g_idx == best_group, p, -1.0)
    w1 = jnp.max(masked, axis=0, keepdims=True)
    i1 = jnp.min(jnp.where(masked == w1, e_idx, float(N_EXPERTS)), axis=0, keepdims=True)
    masked2 = jnp.where(e_idx == i1, -2.0, masked)
    w2 = jnp.max(masked2, axis=0, keepdims=True)
    i2 = jnp.min(jnp.where(masked2 == w2, e_idx, float(N_EXPERTS)), axis=0, keepdims=True)
    tot = w1 + w2
    return jnp.where(e_idx == i1, w1 / tot, 0.0) + jnp.where(e_idx == i2, w2 / tot, 0.0)


def _moe_prep_kernel(x_ref, sh_ref, sc_ref, wr_ref, h_ref, comb_ref):
    h = _norm_rows(x_ref[...]) * (1.0 + sc_ref[...]) + sh_ref[...]
    hb = h.astype(BF16)
    h_ref[...] = hb
    logits = _dot_nt(wr_ref[...].astype(BF16), hb)
    m = jnp.max(logits, axis=0, keepdims=True)
    e = jnp.exp(logits - m)
    probs = e / jnp.sum(e, axis=0, keepdims=True)
    comb_t = _route(probs)
    pad = jnp.zeros((LANES - N_EXPERTS, comb_t.shape[1]), F32)
    comb_ref[...] = jnp.concatenate([comb_t, pad], axis=0).T


def _moe_prep(x, mod_l, w_router_t, *, row_of_tile):
    t = x.shape[0]
    return pl.pallas_call(
        _moe_prep_kernel,
        out_shape=(jax.ShapeDtypeStruct((t, D_MODEL), BF16), jax.ShapeDtypeStruct((t, LANES), F32)),
        grid=(t // TM_SMALL,),
        in_specs=[
            pl.BlockSpec((TM_SMALL, D_MODEL), lambda i: (i, 0)),
            _mod_spec(3, row_of_tile),
            _mod_spec(4, row_of_tile),
            pl.BlockSpec((N_EXPERTS, D_MODEL), lambda i: (0, 0)),
        ],
        out_specs=(pl.BlockSpec((TM_SMALL, D_MODEL), lambda i: (i, 0)),
                   pl.BlockSpec((TM_SMALL, LANES), lambda i: (i, 0))),
        compiler_params=_cparams(("parallel",)),
        name="moe_prep",
    )(x, mod_l, mod_l, w_router_t)


F_SPLIT = 2
TF = D_EXPERT // F_SPLIT


def _moe_dense_kernel(h_ref, comb_ref, wg_ref, wu_ref, wd_ref, o_ref):
    e = pl.program_id(1)
    f = pl.program_id(2)

    @pl.when((e == 0) & (f == 0))
    def _():
        o_ref[...] = jnp.zeros(o_ref.shape, F32)

    h = h_ref[...]
    lane = lax.broadcasted_iota(jnp.int32, comb_ref.shape, 1)
    cw = jnp.sum(jnp.where(lane == e, comb_ref[...], 0.0), axis=-1, keepdims=True)
    gate = _dot(h, wg_ref[...].astype(BF16))
    up = _dot(h, wu_ref[...].astype(BF16))
    hid = gate / (1.0 + jnp.exp(-gate)) * up * cw
    o_ref[...] += _dot(hid.astype(BF16), wd_ref[...].astype(BF16))


def _moe_dense(h, comb, wg_l, wu_l, wd_l):
    t = h.shape[0]
    return pl.pallas_call(
        _moe_dense_kernel,
        out_shape=jax.ShapeDtypeStruct((t, D_MODEL), F32),
        grid=(t // TM, N_EXPERTS, F_SPLIT),
        in_specs=[
            pl.BlockSpec((TM, D_MODEL), lambda i, e, f: (i, 0)),
            pl.BlockSpec((TM, LANES), lambda i, e, f: (i, 0)),
            pl.BlockSpec((None, D_MODEL, TF), lambda i, e, f: (e, 0, f)),
            pl.BlockSpec((None, D_MODEL, TF), lambda i, e, f: (e, 0, f)),
            pl.BlockSpec((None, TF, D_MODEL), lambda i, e, f: (e, f, 0)),
        ],
        out_specs=pl.BlockSpec((TM, D_MODEL), lambda i, e, f: (i, 0)),
        compiler_params=_cparams(("parallel", "arbitrary", "arbitrary")),
        name="moe_experts",
    )(h, comb, wg_l, wu_l, wd_l)


def _residual_ln_kernel(x_ref, z_ref, g_ref, gain_ref, bias_ref, o_ref):
    v = ALPHA * x_ref[...] + g_ref[...] * z_ref[...]
    o_ref[...] = _layer_norm_rows(v, gain_ref[...], bias_ref[...])


def _residual_ln(x, z, mod_l, ln_gain_l, ln_bias_l, *, which, row_of_tile):
    t = x.shape[0]
    row = pl.BlockSpec((TM_SMALL, D_MODEL), lambda i: (i, 0))
    vec = pl.BlockSpec((1, D_MODEL), lambda i: (0, 0))
    return pl.pallas_call(
        _residual_ln_kernel,
        out_shape=jax.ShapeDtypeStruct((t, D_MODEL), F32),
        grid=(t // TM_SMALL,),
        in_specs=[row, row, _mod_spec(which, row_of_tile), vec, vec],
        out_specs=row,
        compiler_params=_cparams(("parallel",)),
        name="residual_layer_norm",
    )(x, z, mod_l, ln_gain_l, ln_bias_l)


def _rope_tables(n_tokens):
    rows = n_tokens // GRID_W
    row = jnp.repeat(jnp.arange(rows), GRID_W).astype(F32)
    col = jnp.tile(jnp.arange(GRID_W), rows).astype(F32)
    quarter = HEAD_DIM // 4
    inv_freq = ROPE_THETA ** (-jnp.arange(quarter, dtype=F32) / quarter)
    ang_r, ang_c = row[:, None] * inv_freq, col[:, None] * inv_freq
    cos = jnp.concatenate([jnp.cos(ang_r), jnp.cos(ang_r), jnp.cos(ang_c), jnp.cos(ang_c)], axis=-1)
    sin = jnp.concatenate([-jnp.sin(ang_r), jnp.sin(ang_r), -jnp.sin(ang_c), jnp.sin(ang_c)], axis=-1)
    return cos, sin


def _row_of_tile(first_row, tokens_per_row):
    def for_tile(tile):
        return lambda i: first_row + (i * tile) // tokens_per_row
    return for_tile


def _mixer_and_ffn(x, attn_arrays, attn_blocks, gates, mod_l, rows, l, w_branch, w_o, ln_gain, ln_bias,
                   w_router_t, w_e_gate, w_e_up, w_e_down):
    y = _merge_branches(attn_arrays, attn_blocks, gates, w_branch[l])
    x = _out_projection(y, w_o[l], x, mod_l, ln_gain[l, 0:1], ln_bias[l, 0:1], row_of_tile=rows(TM_SMALL))
    h2, comb = _moe_prep(x, mod_l, w_router_t, row_of_tile=rows(TM_SMALL))
    z = _moe_dense(h2, comb, w_e_gate[l], w_e_up[l], w_e_down[l])
    return _residual_ln(x, z, mod_l, ln_gain[l, 1:2], ln_bias[l, 1:2], which=5, row_of_tile=rows(TM_SMALL))


def kernel(x_prompt, x_sample, cache_kv_a, cache_kv_b, cache_kv_c, c, c_ctx, w_in, w_gate, w_branch, w_o,
           w_mod, b_mod, ln_gain, ln_bias, diff_lam, diff_subln, qk_gain, sink, w_router, w_e_gate, w_e_up,
           w_e_down):
    batch, seq, _ = x_prompt.shape
    dec_batch, dec_seq, _ = x_sample.shape
    past = cache_kv_a.shape[3]
    t_p, t_s = batch * seq, dec_batch * dec_seq

    cond = jnp.concatenate([c_ctx[None], c, jnp.zeros((MOD_ROWS - 1 - dec_batch, D_MODEL), F32)], axis=0)
    mod = _modulation(cond.T, w_mod, b_mod)
    cos_t, sin_t = _rope_tables(dec_seq)
    w_router_t = w_router.T
    cache_a = cache_kv_a.reshape(dec_batch, DEPTH, 2, past, 4 * 256)
    cache_b = cache_kv_b.reshape(dec_batch, DEPTH, 2, past, 2 * HEAD_DIM)
    cache_c = cache_kv_c.reshape(dec_batch, DEPTH, 2, past, 2 * HEAD_DIM)
    rows_p = _row_of_tile(0, t_p)
    rows_s = _row_of_tile(1, dec_seq)

    y_p = x_prompt.reshape(t_p, D_MODEL)
    y_s = x_sample.reshape(t_s, D_MODEL)
    new_a, new_b, new_c = [], [], []
    for l in range(DEPTH):
        lam_init = 0.8 - 0.6 * math.exp(-0.3 * l)
        mod_l = mod[l].reshape(MOD_ROWS, 1, N_MOD * D_MODEL)
        subln_l = diff_subln[l].reshape(1, 256)
        shared = (w_branch, w_o, ln_gain, ln_bias, w_router_t, w_e_gate, w_e_up, w_e_down)

        p = _in_projection(y_p, mod_l, w_in[l], qk_gain[l], cos_t, sin_t, row_of_tile=rows_p(TM),
                           rope=False, out_dtype=F32)
        gates = _branch_gates(y_p, mod_l, w_gate[l], row_of_tile=rows_p(TM))
        attn = _attention_prompt(p, diff_lam[l], subln_l, sink[l], lam_init=lam_init, seq=seq)
        y_p = _mixer_and_ffn(y_p, (attn, attn, attn), (0, 1, 2), gates, mod_l, rows_p, l, *shared)
        p5 = p.reshape(batch, seq, IN_COLS)
        new_a.append(jnp.stack([p5[..., COL_AK:COL_AV], p5[..., COL_AV:COL_BQ]], axis=1))
        new_b.append(jnp.stack([p5[..., COL_BK:COL_BV], p5[..., COL_BV:COL_CQ]], axis=1))
        new_c.append(jnp.stack([p5[..., COL_CK:COL_CV], p5[..., COL_CV:]], axis=1))

        qkv = _in_projection(y_s, mod_l, w_in[l], qk_gain[l], cos_t, sin_t, row_of_tile=rows_s(TM),
                             rope=True, out_dtype=BF16)
        gates = _branch_gates(y_s, mod_l, w_gate[l], row_of_tile=rows_s(TM))
        vec = lambda shape: pl.BlockSpec(shape, lambda b, qi, kk: (0, 0))
        a_o = _flash_sample(functools.partial(_diff_sample_kernel, lam_init=lam_init), qkv, cache_a, l,
                            q_col=COL_AQ, k_col=COL_AK, v_col=COL_AV, kv_width=1024, n_state=8,
                            acc_width=256, extra=(diff_lam[l], subln_l),
                            extra_specs=[vec((4, HEAD_DIM)), vec((1, 256))], name="diff_attention",
                            dec_seq=dec_seq)
        b_o = _flash_sample(_gqa_sample_kernel, qkv, cache_b, l, q_col=COL_BQ, k_col=COL_BK, v_col=COL_BV,
                            kv_width=256, n_state=8, acc_width=HEAD_DIM, extra=(), extra_specs=[],
                            name="gqa_attention", dec_seq=dec_seq)
        c_o = _window_sample(qkv, cache_c, sink[l], l, dec_seq=dec_seq)
        y_s = _mixer_and_ffn(y_s, (a_o, b_o, c_o), (0, 0, 0), gates, mod_l, rows_s, l, *shared)

    new_kv_a = jnp.stack(new_a, axis=1).reshape(batch, DEPTH, 2, seq, 4, 256)
    new_kv_b = jnp.stack(new_b, axis=1).reshape(batch, DEPTH, 2, seq, 2, HEAD_DIM)
    new_kv_c = jnp.stack(new_c, axis=1).reshape(batch, DEPTH, 2, seq, 2, HEAD_DIM)
    return (y_p.reshape(batch, seq, D_MODEL), y_s.reshape(dec_batch, dec_seq, D_MODEL),
            new_kv_a, new_kv_b, new_kv_c)
```

```python
import functools
import math

import jax
import jax.numpy as jnp
from jax import lax
from jax.experimental import pallas as pl
from jax.experimental.pallas import tpu as pltpu

F32 = jnp.float32
BF16 = jnp.bfloat16

D_MODEL = 2048
HEAD_DIM = 128
GRID_W = 64
ROPE_THETA = 10000.0
WINDOW = 128
N_EXPERTS = 16
N_GROUPS = 4
EXPERTS_PER_GROUP = N_EXPERTS // N_GROUPS
D_EXPERT = 512
N_MOD = 6
DEPTH = 2
ALPHA = (2 * DEPTH) ** 0.25
EPS = 1e-6
IN_COLS = 6144
BRANCH_WIDTH = 1024
SCALE = HEAD_DIM ** -0.5
LOG2E = math.log2(math.e)
NEG_BIG = -1e30

COL_AQ, COL_AK, COL_AV = 0, 1024, 2048
COL_BQ, COL_BK, COL_BV = 3072, 4096, 4352
COL_CQ, COL_CK, COL_CV = 4608, 5632, 5888

LANES = 128
VMEM_LIMIT = 56 * 1024 * 1024

TM = 1024
TN_PROJ = 512
TM_SMALL = 512


def _cparams(sem):
    return pltpu.CompilerParams(dimension_semantics=sem, vmem_limit_bytes=VMEM_LIMIT)


def _dot(a, b):
    return jnp.dot(a, b, preferred_element_type=F32)


def _dot_nt(a, b):
    return lax.dot_general(a, b, (((1,), (1,)), ((), ())), preferred_element_type=F32)


def _norm_rows(x):
    mu = jnp.mean(x, axis=-1, keepdims=True)
    xc = x - mu
    var = jnp.mean(xc * xc, axis=-1, keepdims=True)
    return xc * lax.rsqrt(var + EPS)


def _rms(v, gain):
    ms = jnp.mean(v * v, axis=-1, keepdims=True)
    return v * lax.rsqrt(ms + EPS) * gain


def _rope(v, cos, sin_signed):
    lane = lax.broadcasted_iota(jnp.int32, v.shape, 1)
    first_half = (lane % 64) < 32
    partner = jnp.where(first_half, pltpu.roll(v, 96, 1), pltpu.roll(v, 32, 1))
    return v * cos + partner * sin_signed


N_COND = 3
MOD_ROWS = 8
TN_MOD = 1024


def _mod_kernel(cond_ref, w_ref, b_ref, o_ref):
    w = w_ref[...]
    row_idx = lax.broadcasted_iota(jnp.int32, (MOD_ROWS, TN_MOD), 0)
    out = jnp.zeros((MOD_ROWS, TN_MOD), F32)
    for r in range(N_COND):
        c = cond_ref[:, r:r + 1]
        s = c / (1.0 + jnp.exp(-c))
        m = jnp.sum(w * s, axis=0, keepdims=True) + b_ref[...]
        out = jnp.where(row_idx == r, m, out)
    o_ref[...] = out


def _modulation(cond_t, w_mod, b_mod):
    n = N_MOD * D_MODEL
    return pl.pallas_call(
        _mod_kernel,
        out_shape=jax.ShapeDtypeStruct((DEPTH, MOD_ROWS, n), F32),
        grid=(DEPTH, n // TN_MOD),
        in_specs=[
            pl.BlockSpec((D_MODEL, MOD_ROWS), lambda l, j: (0, 0)),
            pl.BlockSpec((None, D_MODEL, TN_MOD), lambda l, j: (l, 0, j)),
            pl.BlockSpec((None, 1, TN_MOD), lambda l, j: (l, 0, j)),
        ],
        out_specs=pl.BlockSpec((None, MOD_ROWS, TN_MOD), lambda l, j: (l, 0, j)),
        compiler_params=_cparams(("parallel", "parallel")),
        name="modulation",
    )(cond_t, w_mod, b_mod.reshape(DEPTH, 1, n))


def _modulate_to_scratch(x_ref, sh_ref, sc_ref, h_scr):
    h = _norm_rows(x_ref[...]) * (1.0 + sc_ref[...]) + sh_ref[...]
    h_scr[...] = h.astype(BF16)


def _inproj_kernel(x_ref, sh_ref, sc_ref, w_ref, qk_gain_ref, cos_ref, sin_ref, o_ref, h_scr, *, rope,
                   q_scale):
    j = pl.program_id(1)
    gain_q = qk_gain_ref[0:1, :]
    gain_k = qk_gain_ref[1:2, :]

    @pl.when(j == 0)
    def _():
        _modulate_to_scratch(x_ref, sh_ref, sc_ref, h_scr)

    acc = _dot(h_scr[...], w_ref[...].astype(BF16))
    n_chunks = TN_PROJ // LANES

    def chunk(c):
        return acc[:, c * LANES:(c + 1) * LANES]

    def store(c, v):
        o_ref[:, c * LANES:(c + 1) * LANES] = v.astype(o_ref.dtype)

    def rp(v):
        return _rope(v, cos_ref[...], sin_ref[...]) if rope else v

    def qs(v):
        return v * q_scale if q_scale != 1.0 else v

    @pl.when((j <= 1) | (j == 9) | (j == 10))
    def _():
        for c in range(n_chunks):
            store(c, qs(rp(chunk(c))))

    @pl.when((j == 2) | (j == 3))
    def _():
        for c in range(n_chunks):
            store(c, rp(chunk(c)))

    @pl.when((j == 4) | (j == 5))
    def _():
        o_ref[...] = acc.astype(o_ref.dtype)

    @pl.when((j == 6) | (j == 7))
    def _():
        for c in range(n_chunks):
            store(c, qs(rp(_rms(chunk(c), gain_q))))

    @pl.when(j == 8)
    def _():
        for c in range(2):
            store(c, rp(_rms(chunk(c), gain_k)))
        for c in range(2, n_chunks):
            store(c, chunk(c))

    @pl.when(j == 11)
    def _():
        for c in range(2):
            store(c, rp(chunk(c)))
        for c in range(2, n_chunks):
            store(c, chunk(c))


def _gate_kernel(x_ref, sh_ref, sc_ref, w_ref, o_ref, h_scr):
    @pl.when(pl.program_id(1) == 0)
    def _():
        _modulate_to_scratch(x_ref, sh_ref, sc_ref, h_scr)

    acc = _dot(h_scr[...], w_ref[...].astype(BF16))
    o_ref[...] = (1.0 / (1.0 + jnp.exp(-acc))).astype(o_ref.dtype)


def _mod_spec(which, row_of_tile):
    return pl.BlockSpec((None, 1, D_MODEL), lambda i, *_: (row_of_tile(i), 0, which))


def _in_projection(x, mod_l, w_in, l, qk_gain_l, cos_t, sin_t, *, row_of_tile, rope, q_scale, out_dtype):
    t = x.shape[0]
    tiles_per_seq = cos_t.shape[0] // TM
    return pl.pallas_call(
        functools.partial(_inproj_kernel, rope=rope, q_scale=q_scale),
        out_shape=jax.ShapeDtypeStruct((t, IN_COLS), out_dtype),
        grid=(t // TM, IN_COLS // TN_PROJ),
        in_specs=[
            pl.BlockSpec((TM, D_MODEL), lambda i, j: (i, 0)),
            _mod_spec(0, row_of_tile),
            _mod_spec(1, row_of_tile),
            pl.BlockSpec((None, D_MODEL, TN_PROJ), lambda i, j: (l, 0, j)),
            pl.BlockSpec((2, HEAD_DIM), lambda i, j: (0, 0)),
            pl.BlockSpec((TM, HEAD_DIM), lambda i, j: (i % tiles_per_seq, 0)),
            pl.BlockSpec((TM, HEAD_DIM), lambda i, j: (i % tiles_per_seq, 0)),
        ],
        out_specs=pl.BlockSpec((TM, TN_PROJ), lambda i, j: (i, j)),
        scratch_shapes=[pltpu.VMEM((TM, D_MODEL), BF16)],
        compiler_params=_cparams(("parallel", "arbitrary")),
        name="in_projection",
    )(x, mod_l, mod_l, w_in, qk_gain_l, cos_t, sin_t)


def _branch_gates(x, mod_l, w_gate, l, *, row_of_tile):
    t = x.shape[0]
    n = w_gate.shape[2]
    return pl.pallas_call(
        _gate_kernel,
        out_shape=jax.ShapeDtypeStruct((t, n), BF16),
        grid=(t // TM, n // TN_PROJ),
        in_specs=[
            pl.BlockSpec((TM, D_MODEL), lambda i, j: (i, 0)),
            _mod_spec(0, row_of_tile),
            _mod_spec(1, row_of_tile),
            pl.BlockSpec((None, D_MODEL, TN_PROJ), lambda i, j: (l, 0, j)),
        ],
        out_specs=pl.BlockSpec((TM, TN_PROJ), lambda i, j: (i, j)),
        scratch_shapes=[pltpu.VMEM((TM, D_MODEL), BF16)],
        compiler_params=_cparams(("parallel", "arbitrary")),
        name="branch_gates",
    )(x, mod_l, mod_l, w_gate)


def _diff_lambda(lam_ref, lam_init):
    lp = lam_ref[...]
    t1 = jnp.sum(lp[0:1] * lp[1:2], axis=-1, keepdims=True)
    t2 = jnp.sum(lp[2:3] * lp[3:4], axis=-1, keepdims=True)
    return jnp.exp(t1) - jnp.exp(t2) + lam_init


def _softmax_rows(s, sink=None):
    m = jnp.max(s, axis=-1, keepdims=True)
    if sink is not None:
        m = jnp.maximum(m, sink)
    e = jnp.exp(s - m)
    den = jnp.sum(e, axis=-1, keepdims=True)
    if sink is not None:
        den = den + jnp.exp(sink - m)
    return e * (1.0 / den)


def _attn_prompt_kernel(p_ref, lam_ref, subln_ref, sink_ref, o_ref, *, lam_init):
    lam = _diff_lambda(lam_ref, lam_init)

    def blk(c0, w):
        return p_ref[:, c0:c0 + w].astype(BF16)

    for h in range(4):
        probs = []
        for m in range(2):
            q = blk(COL_AQ + h * 256 + m * HEAD_DIM, HEAD_DIM)
            k = blk(COL_AK + h * 256 + m * HEAD_DIM, HEAD_DIM)
            probs.append(_softmax_rows(_dot_nt(q, k) * SCALE))
        w = probs[0] - lam * probs[1]
        o = _dot(w.astype(BF16), blk(COL_AV + h * 256, 256))
        o = _rms(o, subln_ref[...]) * (1.0 - lam_init)
        o_ref[:, h * 256:(h + 1) * 256] = o.astype(o_ref.dtype)

    for mixer, (cq, ck, cv) in enumerate(((COL_BQ, COL_BK, COL_BV), (COL_CQ, COL_CK, COL_CV))):
        for kvh in range(2):
            k = blk(ck + kvh * HEAD_DIM, HEAD_DIM)
            v = blk(cv + kvh * HEAD_DIM, HEAD_DIM)
            for g in range(4):
                hq = kvh * 4 + g
                q = blk(cq + hq * HEAD_DIM, HEAD_DIM)
                sink = sink_ref[hq] if mixer == 1 else None
                p = _softmax_rows(_dot_nt(q, k) * SCALE, sink)
                o = _dot(p.astype(BF16), v)
                c0 = BRANCH_WIDTH * (1 + mixer) + hq * HEAD_DIM
                o_ref[:, c0:c0 + HEAD_DIM] = o.astype(o_ref.dtype)


def _attention_prompt(p, diff_lam_l, subln_l, sink_l, *, lam_init, seq):
    t = p.shape[0]
    return pl.pallas_call(
        functools.partial(_attn_prompt_kernel, lam_init=lam_init),
        out_shape=jax.ShapeDtypeStruct((t, 3 * BRANCH_WIDTH), BF16),
        grid=(t // seq,),
        in_specs=[
            pl.BlockSpec((seq, IN_COLS), lambda b: (b, 0)),
            pl.BlockSpec((4, HEAD_DIM), lambda b: (0, 0)),
            pl.BlockSpec((1, 256), lambda b: (0, 0)),
            pl.BlockSpec(memory_space=pltpu.SMEM),
        ],
        out_specs=pl.BlockSpec((seq, 3 * BRANCH_WIDTH), lambda b: (b, 0)),
        compiler_params=_cparams(("parallel",)),
        name="attention_prompt",
    )(p, diff_lam_l, subln_l, sink_l)


TQ = 512
TK = 1024


def _tile_lanes(v, n):
    return jnp.concatenate([v] * n, axis=-1) if n > 1 else v


def _online_softmax(idx, s, m_scr):
    m_prev = m_scr[idx]
    m_new = jnp.maximum(m_prev, jnp.max(s, axis=-1, keepdims=True))
    m_scr[idx] = m_new
    alpha = jnp.exp2(m_prev - m_new)
    p = jnp.exp2(s - _tile_lanes(m_new, s.shape[1] // LANES))
    return p, alpha


def _diff_sample_kernel(q_ref, k_ref, v_ref, kc_ref, vc_ref, lam_ref, subln_ref, o_ref,
                        m_scr, l_scr, acc_scr, *, lam_init):
    kk = pl.program_id(2)

    @pl.when(kk == 0)
    def _():
        m_scr[...] = jnp.full(m_scr.shape, NEG_BIG, F32)
        l_scr[...] = jnp.zeros(l_scr.shape, F32)
        acc_scr[...] = jnp.zeros(acc_scr.shape, F32)

    def process(kb_ref, vb_ref):
        for h in range(4):
            v = vb_ref[:, h * 256:(h + 1) * 256].astype(BF16)
            for m in range(2):
                c0 = h * 256 + m * HEAD_DIM
                idx = h * 2 + m
                k = kb_ref[:, c0:c0 + HEAD_DIM].astype(BF16)
                p, alpha = _online_softmax(idx, _dot_nt(q_ref[:, c0:c0 + HEAD_DIM], k), m_scr)
                part = p[:, 0:LANES]
                for c in range(1, p.shape[1] // LANES):
                    part = part + p[:, c * LANES:(c + 1) * LANES]
                l_scr[idx] = alpha * l_scr[idx] + part
                acc_scr[idx] = _tile_lanes(alpha, 2) * acc_scr[idx] + _dot(p.astype(BF16), v)

    @pl.when(kk == 0)
    def _():
        process(kc_ref, vc_ref)

    @pl.when(kk > 0)
    def _():
        process(k_ref, v_ref)

    @pl.when(kk == pl.num_programs(2) - 1)
    def _():
        lam = _diff_lambda(lam_ref, lam_init)
        for h in range(4):
            l1 = jnp.sum(l_scr[2 * h], axis=-1, keepdims=True)
            l2 = jnp.sum(l_scr[2 * h + 1], axis=-1, keepdims=True)
            o1 = acc_scr[2 * h] * (1.0 / l1)
            o2 = acc_scr[2 * h + 1] * (1.0 / l2)
            o = _rms(o1 - lam * o2, subln_ref[...]) * (1.0 - lam_init)
            o_ref[:, h * 256:(h + 1) * 256] = o.astype(o_ref.dtype)


def _gqa_sample_kernel(q_ref, k_ref, v_ref, kc_ref, vc_ref, o_ref, m_scr, acc_scr):
    kk = pl.program_id(2)

    @pl.when(kk == 0)
    def _():
        m_scr[...] = jnp.full(m_scr.shape, NEG_BIG, F32)
        acc_scr[...] = jnp.zeros(acc_scr.shape, F32)

    def process(kb_ref, vb_ref):
        for kvh in range(2):
            k = kb_ref[:, kvh * HEAD_DIM:(kvh + 1) * HEAD_DIM].astype(BF16)
            v = vb_ref[:, kvh * HEAD_DIM:(kvh + 1) * HEAD_DIM].astype(BF16)
            v_ones = jnp.concatenate([v, jnp.ones_like(v)], axis=-1)
            for g in range(4):
                hq = kvh * 4 + g
                s = _dot_nt(q_ref[:, hq * HEAD_DIM:(hq + 1) * HEAD_DIM], k)
                p, alpha = _online_softmax(hq, s, m_scr)
                acc_scr[hq] = _tile_lanes(alpha, 2) * acc_scr[hq] + _dot(p.astype(BF16), v_ones)

    @pl.when(kk == 0)
    def _():
        process(kc_ref, vc_ref)

    @pl.when(kk > 0)
    def _():
        process(k_ref, v_ref)

    @pl.when(kk == pl.num_programs(2) - 1)
    def _():
        for hq in range(8):
            o = acc_scr[hq, :, 0:HEAD_DIM] / acc_scr[hq, :, HEAD_DIM:2 * HEAD_DIM]
            o_ref[:, hq * HEAD_DIM:(hq + 1) * HEAD_DIM] = o.astype(o_ref.dtype)


def _flash_sample(kernel, qkv, cache, l, *, q_col, k_col, v_col, kv_width, n_state, row_sum_scratch, extra,
                  extra_specs, name, dec_seq):
    t = qkv.shape[0]
    nb = t // dec_seq
    nq = dec_seq // TQ
    nk = dec_seq // TK
    past = cache.shape[3]
    q_blk, k_blk, v_blk = q_col // BRANCH_WIDTH, k_col // kv_width, v_col // kv_width
    kv_row = lambda b, qi, kk: b * nk + jnp.maximum(kk - 1, 0)
    return pl.pallas_call(
        kernel,
        out_shape=jax.ShapeDtypeStruct((t, BRANCH_WIDTH), BF16),
        grid=(nb, nq, nk + 1),
        in_specs=[
            pl.BlockSpec((TQ, BRANCH_WIDTH), lambda b, qi, kk: (b * nq + qi, q_blk)),
            pl.BlockSpec((TK, kv_width), lambda b, qi, kk: (kv_row(b, qi, kk), k_blk)),
            pl.BlockSpec((TK, kv_width), lambda b, qi, kk: (kv_row(b, qi, kk), v_blk)),
            pl.BlockSpec((None, None, None, past, kv_width), lambda b, qi, kk: (b, l, 0, 0, 0)),
            pl.BlockSpec((None, None, None, past, kv_width), lambda b, qi, kk: (b, l, 1, 0, 0)),
        ] + extra_specs,
        out_specs=pl.BlockSpec((TQ, BRANCH_WIDTH), lambda b, qi, kk: (b * nq + qi, 0)),
        scratch_shapes=[pltpu.VMEM((n_state, TQ, LANES), F32)] * (2 if row_sum_scratch else 1)
        + [pltpu.VMEM((n_state, TQ, 2 * LANES), F32)],
        compiler_params=_cparams(("parallel", "parallel", "arbitrary")),
        name=name,
    )(qkv, qkv, qkv, cache, cache, *extra)


def _window_sample_kernel(q0_ref, q1_ref, kp_ref, kc_ref, kn_ref, vp_ref, vc_ref, vn_ref, kctx_ref, vctx_ref,
                          sink_ref, o_ref, *, dec_seq):
    qi = pl.program_id(1)
    q_start = qi * TQ
    half = TQ // 2
    q_pos = q_start + lax.broadcasted_iota(jnp.int32, (TQ, 1), 0)
    segs = ((kp_ref, vp_ref, q_start - half, half), (kc_ref, vc_ref, q_start, TQ),
            (kn_ref, vn_ref, q_start + TQ, half))
    valid = []
    for _, _, start, n in segs:
        k_pos = start + lax.broadcasted_iota(jnp.int32, (TQ, n), 1)
        valid.append((jnp.abs(q_pos - k_pos) <= WINDOW) & (k_pos >= 0) & (k_pos < dec_seq))
    for kvh in range(2):
        q_ref = q0_ref if kvh == 0 else q1_ref
        lo, hi = kvh * HEAD_DIM, (kvh + 1) * HEAD_DIM
        kctx = kctx_ref[:, lo:hi].astype(BF16)
        vctx = vctx_ref[:, lo:hi].astype(BF16)
        for g in range(4):
            hq = kvh * 4 + g
            q = q_ref[:, g * HEAD_DIM:(g + 1) * HEAD_DIM]
            sink = sink_ref[hq] * LOG2E
            scores = [_dot_nt(q, kctx)]
            for (k_ref, _, _, _), ok in zip(segs, valid):
                scores.append(jnp.where(ok, _dot_nt(q, k_ref[:, lo:hi]), NEG_BIG))
            m = jnp.maximum(scores[0].max(axis=-1, keepdims=True), sink)
            for s in scores[1:]:
                m = jnp.maximum(m, s.max(axis=-1, keepdims=True))
            es = [jnp.exp2(s - m) for s in scores]
            den = jnp.exp2(sink - m)
            for e in es:
                den = den + jnp.sum(e, axis=-1, keepdims=True)
            o = _dot(es[0].astype(BF16), vctx)
            for e, (_, v_ref, _, _) in zip(es[1:], segs):
                o = o + _dot(e.astype(BF16), v_ref[:, lo:hi])
            o = o * (1.0 / den)
            o_ref[:, hq * HEAD_DIM:(hq + 1) * HEAD_DIM] = o.astype(o_ref.dtype)


def _window_sample(qkv, cache, sink_l, l, *, dec_seq):
    t = qkv.shape[0]
    nb = t // dec_seq
    nq = dec_seq // TQ
    half = TQ // 2
    n_half = dec_seq // half
    past = cache.shape[3]
    kvw = 2 * HEAD_DIM
    q_blk = COL_CQ // 512
    k_blk, v_blk = COL_CK // kvw, COL_CV // kvw
    prev_row = lambda b, qi: b * n_half + jnp.maximum(2 * qi - 1, 0)
    next_row = lambda b, qi: b * n_half + jnp.minimum(2 * qi + 2, n_half - 1)
    return pl.pallas_call(
        functools.partial(_window_sample_kernel, dec_seq=dec_seq),
        out_shape=jax.ShapeDtypeStruct((t, BRANCH_WIDTH), BF16),
        grid=(nb, nq),
        in_specs=[
            pl.BlockSpec((TQ, 512), lambda b, qi: (b * nq + qi, q_blk)),
            pl.BlockSpec((TQ, 512), lambda b, qi: (b * nq + qi, q_blk + 1)),
            pl.BlockSpec((half, kvw), lambda b, qi: (prev_row(b, qi), k_blk)),
            pl.BlockSpec((TQ, kvw), lambda b, qi: (b * nq + qi, k_blk)),
            pl.BlockSpec((half, kvw), lambda b, qi: (next_row(b, qi), k_blk)),
            pl.BlockSpec((half, kvw), lambda b, qi: (prev_row(b, qi), v_blk)),
            pl.BlockSpec((TQ, kvw), lambda b, qi: (b * nq + qi, v_blk)),
            pl.BlockSpec((half, kvw), lambda b, qi: (next_row(b, qi), v_blk)),
            pl.BlockSpec((None, None, None, past, kvw), lambda b, qi: (b, l, 0, 0, 0)),
            pl.BlockSpec((None, None, None, past, kvw), lambda b, qi: (b, l, 1, 0, 0)),
            pl.BlockSpec(memory_space=pltpu.SMEM),
        ],
        out_specs=pl.BlockSpec((TQ, BRANCH_WIDTH), lambda b, qi: (b * nq + qi, 0)),
        compiler_params=_cparams(("parallel", "parallel")),
        name="window_attention",
    )(qkv, qkv, qkv, qkv, qkv, qkv, qkv, qkv, cache, cache, sink_l)


def _merge_kernel(oa_ref, ob_ref, oc_ref, g_ref, w_ref, y_ref, acc_scr):
    r = pl.program_id(2)

    def contrib(o_ref):
        return g_ref[...].astype(F32) * _dot(o_ref[...], w_ref[...].astype(BF16))

    @pl.when(r == 0)
    def _():
        acc_scr[...] = contrib(oa_ref)

    @pl.when(r == 1)
    def _():
        acc_scr[...] += contrib(ob_ref)

    @pl.when(r == 2)
    def _():
        y_ref[...] = (acc_scr[...] + contrib(oc_ref)).astype(y_ref.dtype)


def _merge_branches(o_arrays, o_blocks, gates, w_branch, l):
    t = gates.shape[0]
    tn = 1024
    nn = D_MODEL // tn
    o_specs = [pl.BlockSpec((TM, BRANCH_WIDTH), functools.partial(lambda i, n, r, blk: (i, blk), blk=blk))
               for blk in o_blocks]
    return pl.pallas_call(
        _merge_kernel,
        out_shape=jax.ShapeDtypeStruct((t, D_MODEL), BF16),
        grid=(t // TM, nn, 3),
        in_specs=o_specs + [
            pl.BlockSpec((TM, tn), lambda i, n, r: (i, r * nn + n)),
            pl.BlockSpec((None, None, BRANCH_WIDTH, tn), lambda i, n, r: (l, r, 0, n)),
        ],
        out_specs=pl.BlockSpec((TM, tn), lambda i, n, r: (i, n)),
        scratch_shapes=[pltpu.VMEM((TM, tn), F32)],
        compiler_params=_cparams(("parallel", "parallel", "arbitrary")),
        name="merge_branches",
    )(*o_arrays, gates, w_branch)


def _layer_norm_rows(v, gain, bias):
    return _norm_rows(v) * gain + bias


def _outproj_kernel(y_ref, w_ref, x_ref, g_ref, gain_ref, bias_ref, o_ref, z_scr):
    n = pl.program_id(1)
    n_blocks, _, tn = z_scr.shape
    z_scr[n] = _dot(y_ref[...], w_ref[...].astype(BF16))

    @pl.when(n == n_blocks - 1)
    def _():
        for b in range(n_blocks):
            cols = slice(b * tn, (b + 1) * tn)
            o_ref[:, cols] = ALPHA * x_ref[:, cols] + g_ref[:, cols] * z_scr[b]
        o_ref[...] = _layer_norm_rows(o_ref[...], gain_ref[...], bias_ref[...])


def _out_projection(y, w_o, l, x, mod_l, ln_gain_l, ln_bias_l, *, row_of_tile):
    t = x.shape[0]
    tn = 512
    return pl.pallas_call(
        _outproj_kernel,
        out_shape=jax.ShapeDtypeStruct((t, D_MODEL), F32),
        grid=(t // TM_SMALL, D_MODEL // tn),
        in_specs=[
            pl.BlockSpec((TM_SMALL, D_MODEL), lambda i, n: (i, 0)),
            pl.BlockSpec((None, D_MODEL, tn), lambda i, n: (l, 0, n)),
            pl.BlockSpec((TM_SMALL, D_MODEL), lambda i, n: (i, 0)),
            _mod_spec(2, row_of_tile),
            pl.BlockSpec((1, D_MODEL), lambda i, n: (0, 0)),
            pl.BlockSpec((1, D_MODEL), lambda i, n: (0, 0)),
        ],
        out_specs=pl.BlockSpec((TM_SMALL, D_MODEL), lambda i, n: (i, 0)),
        scratch_shapes=[pltpu.VMEM((D_MODEL // tn, TM_SMALL, tn), F32)],
        compiler_params=_cparams(("parallel", "arbitrary")),
        name="out_projection",
    )(y, w_o, x, mod_l, ln_gain_l, ln_bias_l)


def _route(p):
    rows = [p[e:e + 1, :] for e in range(N_EXPERTS)]
    best_score, best_group = None, None
    for g in range(N_GROUPS):
        members = rows[g * EXPERTS_PER_GROUP:(g + 1) * EXPERTS_PER_GROUP]
        score = None
        for a in range(EXPERTS_PER_GROUP):
            for b in range(a + 1, EXPERTS_PER_GROUP):
                pair = members[a] + members[b]
                score = pair if score is None else jnp.maximum(score, pair)
        if g == 0:
            best_score, best_group = score, jnp.zeros(score.shape, F32)
        else:
            better = score > best_score
            best_group = jnp.where(better, float(g), best_group)
            best_score = jnp.where(better, score, best_score)
    e_idx = lax.broadcasted_iota(jnp.int32, p.shape, 0).astype(F32)
    g_idx = jnp.floor(e_idx * (1.0 / EXPERTS_PER_GROUP))
    masked = jnp.where(g_idx == best_group, p, -1.0)
    w1 = jnp.max(masked, axis=0, keepdims=True)
    i1 = jnp.min(jnp.where(masked == w1, e_idx, float(N_EXPERTS)), axis=0, keepdims=True)
    masked2 = jnp.where(e_idx == i1, -2.0, masked)
    w2 = jnp.max(masked2, axis=0, keepdims=True)
    i2 = jnp.min(jnp.where(masked2 == w2, e_idx, float(N_EXPERTS)), axis=0, keepdims=True)
    tot = w1 + w2
    return jnp.where(e_idx == i1, w1 / tot, 0.0) + jnp.where(e_idx == i2, w2 / tot, 0.0)


def _moe_prep_kernel(x_ref, sh_ref, sc_ref, wr_ref, h_ref, comb_ref):
    h = _norm_rows(x_ref[...]) * (1.0 + sc_ref[...]) + sh_ref[...]
    hb = h.astype(BF16)
    h_ref[...] = hb
    logits = _dot_nt(wr_ref[...].astype(BF16), hb)
    m = jnp.max(logits, axis=0, keepdims=True)
    e = jnp.exp(logits - m)
    probs = e / jnp.sum(e, axis=0, keepdims=True)
    comb_t = _route(probs)
    pad = jnp.zeros((LANES - N_EXPERTS, comb_t.shape[1]), F32)
    comb_ref[...] = jnp.concatenate([comb_t, pad], axis=0).T


def _moe_prep(x, mod_l, w_router_t, *, row_of_tile):
    t = x.shape[0]
    return pl.pallas_call(
        _moe_prep_kernel,
        out_shape=(jax.ShapeDtypeStruct((t, D_MODEL), BF16), jax.ShapeDtypeStruct((t, LANES), F32)),
        grid=(t // TM_SMALL,),
        in_specs=[
            pl.BlockSpec((TM_SMALL, D_MODEL), lambda i: (i, 0)),
            _mod_spec(3, row_of_tile),
            _mod_spec(4, row_of_tile),
            pl.BlockSpec((N_EXPERTS, D_MODEL), lambda i: (0, 0)),
        ],
        out_specs=(pl.BlockSpec((TM_SMALL, D_MODEL), lambda i: (i, 0)),
                   pl.BlockSpec((TM_SMALL, LANES), lambda i: (i, 0))),
        compiler_params=_cparams(("parallel",)),
        name="moe_prep",
    )(x, mod_l, mod_l, w_router_t)


F_SPLIT = 2
TF = D_EXPERT // F_SPLIT


def _moe_dense_kernel(h_ref, comb_ref, wg_ref, wu_ref, wd_ref, o_ref):
    e = pl.program_id(1)
    f = pl.program_id(2)

    @pl.when((e == 0) & (f == 0))
    def _():
        o_ref[...] = jnp.zeros(o_ref.shape, F32)

    h = h_ref[...]
    lane = lax.broadcasted_iota(jnp.int32, comb_ref.shape, 1)
    cw = jnp.sum(jnp.where(lane == e, comb_ref[...], 0.0), axis=-1, keepdims=True)
    gate = _dot(h, wg_ref[...].astype(BF16))
    up = _dot(h, wu_ref[...].astype(BF16))
    hid = gate / (1.0 + jnp.exp(-gate)) * up * cw
    o_ref[...] += _dot(hid.astype(BF16), wd_ref[...].astype(BF16))


def _moe_dense(h, comb, wg, wu, wd, l):
    t = h.shape[0]
    return pl.pallas_call(
        _moe_dense_kernel,
        out_shape=jax.ShapeDtypeStruct((t, D_MODEL), F32),
        grid=(t // TM, N_EXPERTS, F_SPLIT),
        in_specs=[
            pl.BlockSpec((TM, D_MODEL), lambda i, e, f: (i, 0)),
            pl.BlockSpec((TM, LANES), lambda i, e, f: (i, 0)),
            pl.BlockSpec((None, None, D_MODEL, TF), lambda i, e, f: (l, e, 0, f)),
            pl.BlockSpec((None, None, D_MODEL, TF), lambda i, e, f: (l, e, 0, f)),
            pl.BlockSpec((None, None, TF, D_MODEL), lambda i, e, f: (l, e, f, 0)),
        ],
        out_specs=pl.BlockSpec((TM, D_MODEL), lambda i, e, f: (i, 0)),
        compiler_params=_cparams(("parallel", "arbitrary", "arbitrary")),
        name="moe_experts",
    )(h, comb, wg, wu, wd)


def _residual_ln_kernel(x_ref, z_ref, g_ref, gain_ref, bias_ref, o_ref):
    v = ALPHA * x_ref[...] + g_ref[...] * z_ref[...]
    o_ref[...] = _layer_norm_rows(v, gain_ref[...], bias_ref[...])


def _residual_ln(x, z, mod_l, ln_gain_l, ln_bias_l, *, which, row_of_tile):
    t = x.shape[0]
    row = pl.BlockSpec((TM_SMALL, D_MODEL), lambda i: (i, 0))
    vec = pl.BlockSpec((1, D_MODEL), lambda i: (0, 0))
    return pl.pallas_call(
        _residual_ln_kernel,
        out_shape=jax.ShapeDtypeStruct((t, D_MODEL), F32),
        grid=(t // TM_SMALL,),
        in_specs=[row, row, _mod_spec(which, row_of_tile), vec, vec],
        out_specs=row,
        compiler_params=_cparams(("parallel",)),
        name="residual_layer_norm",
    )(x, z, mod_l, ln_gain_l, ln_bias_l)


def _rope_tables(n_tokens):
    rows = n_tokens // GRID_W
    row = jnp.repeat(jnp.arange(rows), GRID_W).astype(F32)
    col = jnp.tile(jnp.arange(GRID_W), rows).astype(F32)
    quarter = HEAD_DIM // 4
    inv_freq = ROPE_THETA ** (-jnp.arange(quarter, dtype=F32) / quarter)
    ang_r, ang_c = row[:, None] * inv_freq, col[:, None] * inv_freq
    cos = jnp.concatenate([jnp.cos(ang_r), jnp.cos(ang_r), jnp.cos(ang_c), jnp.cos(ang_c)], axis=-1)
    sin = jnp.concatenate([-jnp.sin(ang_r), jnp.sin(ang_r), -jnp.sin(ang_c), jnp.sin(ang_c)], axis=-1)
    return cos, sin


def _row_of_tile(first_row, tokens_per_row):
    def for_tile(tile):
        return lambda i: first_row + (i * tile) // tokens_per_row
    return for_tile


def _mixer_and_ffn(x, attn_arrays, attn_blocks, gates, mod_l, rows, l, w_branch, w_o, ln_gain, ln_bias,
                   w_router_t, w_e_gate, w_e_up, w_e_down):
    y = _merge_branches(attn_arrays, attn_blocks, gates, w_branch, l)
    x = _out_projection(y, w_o, l, x, mod_l, ln_gain[l, 0:1], ln_bias[l, 0:1], row_of_tile=rows(TM_SMALL))
    h2, comb = _moe_prep(x, mod_l, w_router_t, row_of_tile=rows(TM_SMALL))
    z = _moe_dense(h2, comb, w_e_gate, w_e_up, w_e_down, l)
    return _residual_ln(x, z, mod_l, ln_gain[l, 1:2], ln_bias[l, 1:2], which=5, row_of_tile=rows(TM_SMALL))


def kernel(x_prompt, x_sample, cache_kv_a, cache_kv_b, cache_kv_c, c, c_ctx, w_in, w_gate, w_branch, w_o,
           w_mod, b_mod, ln_gain, ln_bias, diff_lam, diff_subln, qk_gain, sink, w_router, w_e_gate, w_e_up,
           w_e_down):
    batch, seq, _ = x_prompt.shape
    dec_batch, dec_seq, _ = x_sample.shape
    past = cache_kv_a.shape[3]
    t_p, t_s = batch * seq, dec_batch * dec_seq

    cond = jnp.concatenate([c_ctx[None], c, jnp.zeros((MOD_ROWS - 1 - dec_batch, D_MODEL), F32)], axis=0)
    mod = _modulation(cond.T, w_mod, b_mod)
    cos_t, sin_t = _rope_tables(dec_seq)
    w_router_t = w_router.T
    cache_a = cache_kv_a.reshape(dec_batch, DEPTH, 2, past, 4 * 256)
    cache_b = cache_kv_b.reshape(dec_batch, DEPTH, 2, past, 2 * HEAD_DIM)
    cache_c = cache_kv_c.reshape(dec_batch, DEPTH, 2, past, 2 * HEAD_DIM)
    rows_p = _row_of_tile(0, t_p)
    rows_s = _row_of_tile(1, dec_seq)

    y_p = x_prompt.reshape(t_p, D_MODEL)
    y_s = x_sample.reshape(t_s, D_MODEL)
    new_a, new_b, new_c = [], [], []
    for l in range(DEPTH):
        lam_init = 0.8 - 0.6 * math.exp(-0.3 * l)
        mod_l = mod[l].reshape(MOD_ROWS, 1, N_MOD * D_MODEL)
        subln_l = diff_subln[l].reshape(1, 256)
        shared = (w_branch, w_o, ln_gain, ln_bias, w_router_t, w_e_gate, w_e_up, w_e_down)

        p = _in_projection(y_p, mod_l, w_in, l, qk_gain[l], cos_t, sin_t, row_of_tile=rows_p(TM),
                           rope=False, q_scale=1.0, out_dtype=F32)
        gates = _branch_gates(y_p, mod_l, w_gate, l, row_of_tile=rows_p(TM))
        attn = _attention_prompt(p, diff_lam[l], subln_l, sink[l], lam_init=lam_init, seq=seq)
        y_p = _mixer_and_ffn(y_p, (attn, attn, attn), (0, 1, 2), gates, mod_l, rows_p, l, *shared)
        p5 = p.reshape(batch, seq, IN_COLS)
        new_a.append(jnp.stack([p5[..., COL_AK:COL_AV], p5[..., COL_AV:COL_BQ]], axis=1))
        new_b.append(jnp.stack([p5[..., COL_BK:COL_BV], p5[..., COL_BV:COL_CQ]], axis=1))
        new_c.append(jnp.stack([p5[..., COL_CK:COL_CV], p5[..., COL_CV:]], axis=1))

        qkv = _in_projection(y_s, mod_l, w_in, l, qk_gain[l], cos_t, sin_t, row_of_tile=rows_s(TM),
                             rope=True, q_scale=SCALE * LOG2E, out_dtype=BF16)
        gates = _branch_gates(y_s, mod_l, w_gate, l, row_of_tile=rows_s(TM))
        vec = lambda shape: pl.BlockSpec(shape, lambda b, qi, kk: (0, 0))
        a_o = _flash_sample(functools.partial(_diff_sample_kernel, lam_init=lam_init), qkv, cache_a, l,
                            q_col=COL_AQ, k_col=COL_AK, v_col=COL_AV, kv_width=1024, n_state=8,
                            row_sum_scratch=True, extra=(diff_lam[l], subln_l),
                            extra_specs=[vec((4, HEAD_DIM)), vec((1, 256))], name="diff_attention",
                            dec_seq=dec_seq)
        b_o = _flash_sample(_gqa_sample_kernel, qkv, cache_b, l, q_col=COL_BQ, k_col=COL_BK, v_col=COL_BV,
                            kv_width=256, n_state=8, row_sum_scratch=False, extra=(), extra_specs=[],
                            name="gqa_attention", dec_seq=dec_seq)
        c_o = _window_sample(qkv, cache_c, sink[l], l, dec_seq=dec_seq)
        y_s = _mixer_and_ffn(y_s, (a_o, b_o, c_o), (0, 0, 0), gates, mod_l, rows_s, l, *shared)

    new_kv_a = jnp.stack(new_a, axis=1).reshape(batch, DEPTH, 2, seq, 4, 256)
    new_kv_b = jnp.stack(new_b, axis=1).reshape(batch, DEPTH, 2, seq, 2, HEAD_DIM)
    new_kv_c = jnp.stack(new_c, axis=1).reshape(batch, DEPTH, 2, seq, 2, HEAD_DIM)
    return (y_p.reshape(batch, seq, D_MODEL), y_s.reshape(dec_batch, dec_seq, D_MODEL),
            new_kv_a, new_kv_b, new_kv_c)
```

```python
import functools
import math

import jax
import jax.numpy as jnp
from jax import lax
from jax.experimental import pallas as pl
from jax.experimental.pallas import tpu as pltpu

F32 = jnp.float32
BF16 = jnp.bfloat16

D_MODEL = 2048
HEAD_DIM = 128
GRID_W = 64
ROPE_THETA = 10000.0
WINDOW = 128
N_EXPERTS = 16
N_GROUPS = 4
EXPERTS_PER_GROUP = N_EXPERTS // N_GROUPS
D_EXPERT = 512
N_MOD = 6
DEPTH = 2
ALPHA = (2 * DEPTH) ** 0.25
EPS = 1e-6
IN_COLS = 6144
BRANCH_WIDTH = 1024
SCALE = HEAD_DIM ** -0.5
LOG2E = math.log2(math.e)
NEG_BIG = -1e30

COL_AQ, COL_AK, COL_AV = 0, 1024, 2048
COL_BQ, COL_BK, COL_BV = 3072, 4096, 4352
COL_CQ, COL_CK, COL_CV = 4608, 5632, 5888

LANES = 128
VMEM_LIMIT = 56 * 1024 * 1024

TM = 1024
TN_PROJ = 512
TM_SMALL = 512


def _cparams(sem):
    return pltpu.CompilerParams(dimension_semantics=sem, vmem_limit_bytes=VMEM_LIMIT)


def _dot(a, b):
    return jnp.dot(a, b, preferred_element_type=F32)


def _dot_nt(a, b):
    return lax.dot_general(a, b, (((1,), (1,)), ((), ())), preferred_element_type=F32)


def _norm_rows(x):
    mu = jnp.mean(x, axis=-1, keepdims=True)
    xc = x - mu
    var = jnp.mean(xc * xc, axis=-1, keepdims=True)
    return xc * lax.rsqrt(var + EPS)


def _rms(v, gain):
    ms = jnp.mean(v * v, axis=-1, keepdims=True)
    return v * lax.rsqrt(ms + EPS) * gain


def _rope(v, cos, sin_signed):
    lane = lax.broadcasted_iota(jnp.int32, v.shape, 1)
    first_half = (lane % 64) < 32
    partner = jnp.where(first_half, pltpu.roll(v, 96, 1), pltpu.roll(v, 32, 1))
    return v * cos + partner * sin_signed


N_COND = 3
MOD_ROWS = 8
TN_MOD = 1024


def _mod_kernel(cond_ref, w_ref, b_ref, o_ref):
    w = w_ref[...]
    row_idx = lax.broadcasted_iota(jnp.int32, (MOD_ROWS, TN_MOD), 0)
    out = jnp.zeros((MOD_ROWS, TN_MOD), F32)
    for r in range(N_COND):
        c = cond_ref[:, r:r + 1]
        s = c / (1.0 + jnp.exp(-c))
        m = jnp.sum(w * s, axis=0, keepdims=True) + b_ref[...]
        out = jnp.where(row_idx == r, m, out)
    o_ref[...] = out


def _modulation(cond_t, w_mod, b_mod):
    n = N_MOD * D_MODEL
    return pl.pallas_call(
        _mod_kernel,
        out_shape=jax.ShapeDtypeStruct((DEPTH, MOD_ROWS, n), F32),
        grid=(DEPTH, n // TN_MOD),
        in_specs=[
            pl.BlockSpec((D_MODEL, MOD_ROWS), lambda l, j: (0, 0)),
            pl.BlockSpec((None, D_MODEL, TN_MOD), lambda l, j: (l, 0, j)),
            pl.BlockSpec((None, 1, TN_MOD), lambda l, j: (l, 0, j)),
        ],
        out_specs=pl.BlockSpec((None, MOD_ROWS, TN_MOD), lambda l, j: (l, 0, j)),
        compiler_params=_cparams(("parallel", "parallel")),
        name="modulation",
    )(cond_t, w_mod, b_mod.reshape(DEPTH, 1, n))


def _modulate_to_scratch(x_ref, sh_ref, sc_ref, h_scr):
    h = _norm_rows(x_ref[...]) * (1.0 + sc_ref[...]) + sh_ref[...]
    h_scr[...] = h.astype(BF16)


def _inproj_kernel(x_ref, sh_ref, sc_ref, w_ref, qk_gain_ref, cos_ref, sin_ref, o_ref, h_scr, *, rope,
                   q_scale):
    j = pl.program_id(1)
    gain_q = qk_gain_ref[0:1, :]
    gain_k = qk_gain_ref[1:2, :]

    @pl.when(j == 0)
    def _():
        _modulate_to_scratch(x_ref, sh_ref, sc_ref, h_scr)

    acc = _dot(h_scr[...], w_ref[...].astype(BF16))
    n_chunks = TN_PROJ // LANES

    def chunk(c):
        return acc[:, c * LANES:(c + 1) * LANES]

    def store(c, v):
        o_ref[:, c * LANES:(c + 1) * LANES] = v.astype(o_ref.dtype)

    def rp(v):
        return _rope(v, cos_ref[...], sin_ref[...]) if rope else v

    def qs(v):
        return v * q_scale if q_scale != 1.0 else v

    @pl.when((j <= 1) | (j == 9) | (j == 10))
    def _():
        for c in range(n_chunks):
            store(c, qs(rp(chunk(c))))

    @pl.when((j == 2) | (j == 3))
    def _():
        for c in range(n_chunks):
            store(c, rp(chunk(c)))

    @pl.when((j == 4) | (j == 5))
    def _():
        o_ref[...] = acc.astype(o_ref.dtype)

    @pl.when((j == 6) | (j == 7))
    def _():
        for c in range(n_chunks):
            store(c, qs(rp(_rms(chunk(c), gain_q))))

    @pl.when(j == 8)
    def _():
        for c in range(2):
            store(c, rp(_rms(chunk(c), gain_k)))
        for c in range(2, n_chunks):
            store(c, chunk(c))

    @pl.when(j == 11)
    def _():
        for c in range(2):
            store(c, rp(chunk(c)))
        for c in range(2, n_chunks):
            store(c, chunk(c))


def _gate_kernel(x_ref, sh_ref, sc_ref, w_ref, o_ref, h_scr):
    @pl.when(pl.program_id(1) == 0)
    def _():
        _modulate_to_scratch(x_ref, sh_ref, sc_ref, h_scr)

    acc = _dot(h_scr[...], w_ref[...].astype(BF16))
    o_ref[...] = (1.0 / (1.0 + jnp.exp(-acc))).astype(o_ref.dtype)


def _mod_spec(which, row_of_tile):
    return pl.BlockSpec((None, 1, D_MODEL), lambda i, *_: (row_of_tile(i), 0, which))


def _in_projection(x, mod_l, w_in, l, qk_gain_l, cos_t, sin_t, *, row_of_tile, rope, q_scale, out_dtype):
    t = x.shape[0]
    tiles_per_seq = cos_t.shape[0] // TM
    return pl.pallas_call(
        functools.partial(_inproj_kernel, rope=rope, q_scale=q_scale),
        out_shape=jax.ShapeDtypeStruct((t, IN_COLS), out_dtype),
        grid=(t // TM, IN_COLS // TN_PROJ),
        in_specs=[
            pl.BlockSpec((TM, D_MODEL), lambda i, j: (i, 0)),
            _mod_spec(0, row_of_tile),
            _mod_spec(1, row_of_tile),
            pl.BlockSpec((None, D_MODEL, TN_PROJ), lambda i, j: (l, 0, j)),
            pl.BlockSpec((2, HEAD_DIM), lambda i, j: (0, 0)),
            pl.BlockSpec((TM, HEAD_DIM), lambda i, j: (i % tiles_per_seq, 0)),
            pl.BlockSpec((TM, HEAD_DIM), lambda i, j: (i % tiles_per_seq, 0)),
        ],
        out_specs=pl.BlockSpec((TM, TN_PROJ), lambda i, j: (i, j)),
        scratch_shapes=[pltpu.VMEM((TM, D_MODEL), BF16)],
        compiler_params=_cparams(("parallel", "arbitrary")),
        name="in_projection",
    )(x, mod_l, mod_l, w_in, qk_gain_l, cos_t, sin_t)


def _branch_gates(x, mod_l, w_gate, l, *, row_of_tile):
    t = x.shape[0]
    n = w_gate.shape[2]
    return pl.pallas_call(
        _gate_kernel,
        out_shape=jax.ShapeDtypeStruct((t, n), BF16),
        grid=(t // TM, n // TN_PROJ),
        in_specs=[
            pl.BlockSpec((TM, D_MODEL), lambda i, j: (i, 0)),
            _mod_spec(0, row_of_tile),
            _mod_spec(1, row_of_tile),
            pl.BlockSpec((None, D_MODEL, TN_PROJ), lambda i, j: (l, 0, j)),
        ],
        out_specs=pl.BlockSpec((TM, TN_PROJ), lambda i, j: (i, j)),
        scratch_shapes=[pltpu.VMEM((TM, D_MODEL), BF16)],
        compiler_params=_cparams(("parallel", "arbitrary")),
        name="branch_gates",
    )(x, mod_l, mod_l, w_gate)


def _diff_lambda(lam_ref, lam_init):
    lp = lam_ref[...]
    t1 = jnp.sum(lp[0:1] * lp[1:2], axis=-1, keepdims=True)
    t2 = jnp.sum(lp[2:3] * lp[3:4], axis=-1, keepdims=True)
    return jnp.exp(t1) - jnp.exp(t2) + lam_init


def _softmax_rows(s, sink=None):
    m = jnp.max(s, axis=-1, keepdims=True)
    if sink is not None:
        m = jnp.maximum(m, sink)
    e = jnp.exp(s - m)
    den = jnp.sum(e, axis=-1, keepdims=True)
    if sink is not None:
        den = den + jnp.exp(sink - m)
    return e * (1.0 / den)


def _attn_prompt_kernel(p_ref, lam_ref, subln_ref, sink_ref, o_ref, *, lam_init):
    lam = _diff_lambda(lam_ref, lam_init)

    def blk(c0, w):
        return p_ref[:, c0:c0 + w].astype(BF16)

    for h in range(4):
        probs = []
        for m in range(2):
            q = blk(COL_AQ + h * 256 + m * HEAD_DIM, HEAD_DIM)
            k = blk(COL_AK + h * 256 + m * HEAD_DIM, HEAD_DIM)
            probs.append(_softmax_rows(_dot_nt(q, k) * SCALE))
        w = probs[0] - lam * probs[1]
        o = _dot(w.astype(BF16), blk(COL_AV + h * 256, 256))
        o = _rms(o, subln_ref[...]) * (1.0 - lam_init)
        o_ref[:, h * 256:(h + 1) * 256] = o.astype(o_ref.dtype)

    for mixer, (cq, ck, cv) in enumerate(((COL_BQ, COL_BK, COL_BV), (COL_CQ, COL_CK, COL_CV))):
        for kvh in range(2):
            k = blk(ck + kvh * HEAD_DIM, HEAD_DIM)
            v = blk(cv + kvh * HEAD_DIM, HEAD_DIM)
            for g in range(4):
                hq = kvh * 4 + g
                q = blk(cq + hq * HEAD_DIM, HEAD_DIM)
                sink = sink_ref[hq] if mixer == 1 else None
                p = _softmax_rows(_dot_nt(q, k) * SCALE, sink)
                o = _dot(p.astype(BF16), v)
                c0 = BRANCH_WIDTH * (1 + mixer) + hq * HEAD_DIM
                o_ref[:, c0:c0 + HEAD_DIM] = o.astype(o_ref.dtype)


def _attention_prompt(p, diff_lam_l, subln_l, sink_l, *, lam_init, seq):
    t = p.shape[0]
    return pl.pallas_call(
        functools.partial(_attn_prompt_kernel, lam_init=lam_init),
        out_shape=jax.ShapeDtypeStruct((t, 3 * BRANCH_WIDTH), BF16),
        grid=(t // seq,),
        in_specs=[
            pl.BlockSpec((seq, IN_COLS), lambda b: (b, 0)),
            pl.BlockSpec((4, HEAD_DIM), lambda b: (0, 0)),
            pl.BlockSpec((1, 256), lambda b: (0, 0)),
            pl.BlockSpec(memory_space=pltpu.SMEM),
        ],
        out_specs=pl.BlockSpec((seq, 3 * BRANCH_WIDTH), lambda b: (b, 0)),
        compiler_params=_cparams(("parallel",)),
        name="attention_prompt",
    )(p, diff_lam_l, subln_l, sink_l)


TQ = 512
TK = 1024


def _tile_lanes(v, n):
    return jnp.concatenate([v] * n, axis=-1) if n > 1 else v


def _online_softmax(idx, s, m_scr):
    m_prev = m_scr[idx]
    m_new = jnp.maximum(m_prev, jnp.max(s, axis=-1, keepdims=True))
    m_scr[idx] = m_new
    alpha = jnp.exp2(m_prev - m_new)
    p = jnp.exp2(s - _tile_lanes(m_new, s.shape[1] // LANES))
    return p, alpha


def _diff_sample_kernel(q_ref, k_ref, v_ref, kc_ref, vc_ref, lam_ref, subln_ref, o_ref,
                        m_scr, l_scr, acc_scr, *, lam_init):
    kk = pl.program_id(2)

    @pl.when(kk == 0)
    def _():
        m_scr[...] = jnp.full(m_scr.shape, NEG_BIG, F32)
        l_scr[...] = jnp.zeros(l_scr.shape, F32)
        acc_scr[...] = jnp.zeros(acc_scr.shape, F32)

    def process(kb_ref, vb_ref):
        for h in range(4):
            v = vb_ref[:, h * 256:(h + 1) * 256].astype(BF16)
            for m in range(2):
                c0 = h * 256 + m * HEAD_DIM
                idx = h * 2 + m
                k = kb_ref[:, c0:c0 + HEAD_DIM].astype(BF16)
                p, alpha = _online_softmax(idx, _dot_nt(q_ref[:, c0:c0 + HEAD_DIM], k), m_scr)
                part = p[:, 0:LANES]
                for c in range(1, p.shape[1] // LANES):
                    part = part + p[:, c * LANES:(c + 1) * LANES]
                l_scr[idx] = alpha * l_scr[idx] + part
                acc_scr[idx] = _tile_lanes(alpha, 2) * acc_scr[idx] + _dot(p.astype(BF16), v)

    @pl.when(kk == 0)
    def _():
        process(kc_ref, vc_ref)

    @pl.when(kk > 0)
    def _():
        process(k_ref, v_ref)

    @pl.when(kk == pl.num_programs(2) - 1)
    def _():
        lam = _diff_lambda(lam_ref, lam_init)
        for h in range(4):
            l1 = jnp.sum(l_scr[2 * h], axis=-1, keepdims=True)
            l2 = jnp.sum(l_scr[2 * h + 1], axis=-1, keepdims=True)
            o1 = acc_scr[2 * h] * (1.0 / l1)
            o2 = acc_scr[2 * h + 1] * (1.0 / l2)
            o = _rms(o1 - lam * o2, subln_ref[...]) * (1.0 - lam_init)
            o_ref[:, h * 256:(h + 1) * 256] = o.astype(o_ref.dtype)


def _gqa_sample_kernel(q_ref, k_ref, v_ref, kc_ref, vc_ref, o_ref, m_scr, acc_scr):
    kk = pl.program_id(2)

    @pl.when(kk == 0)
    def _():
        m_scr[...] = jnp.full(m_scr.shape, NEG_BIG, F32)
        acc_scr[...] = jnp.zeros(acc_scr.shape, F32)

    def process(kb_ref, vb_ref):
        for kvh in range(2):
            k = kb_ref[:, kvh * HEAD_DIM:(kvh + 1) * HEAD_DIM].astype(BF16)
            v = vb_ref[:, kvh * HEAD_DIM:(kvh + 1) * HEAD_DIM].astype(BF16)
            v_ones = jnp.concatenate([v, jnp.ones_like(v)], axis=-1)
            for g in range(4):
                hq = kvh * 4 + g
                s = _dot_nt(q_ref[:, hq * HEAD_DIM:(hq + 1) * HEAD_DIM], k)
                p, alpha = _online_softmax(hq, s, m_scr)
                acc_scr[hq] = _tile_lanes(alpha, 2) * acc_scr[hq] + _dot(p.astype(BF16), v_ones)

    @pl.when(kk == 0)
    def _():
        process(kc_ref, vc_ref)

    @pl.when(kk > 0)
    def _():
        process(k_ref, v_ref)

    @pl.when(kk == pl.num_programs(2) - 1)
    def _():
        for hq in range(8):
            o = acc_scr[hq, :, 0:HEAD_DIM] / acc_scr[hq, :, HEAD_DIM:2 * HEAD_DIM]
            o_ref[:, hq * HEAD_DIM:(hq + 1) * HEAD_DIM] = o.astype(o_ref.dtype)


def _flash_sample(kernel, qkv, cache, l, *, q_col, k_col, v_col, kv_width, n_state, row_sum_scratch, extra,
                  extra_specs, name, dec_seq):
    t = qkv.shape[0]
    nb = t // dec_seq
    nq = dec_seq // TQ
    nk = dec_seq // TK
    past = cache.shape[3]
    q_blk, k_blk, v_blk = q_col // BRANCH_WIDTH, k_col // kv_width, v_col // kv_width
    kv_row = lambda b, qi, kk: b * nk + jnp.maximum(kk - 1, 0)
    return pl.pallas_call(
        kernel,
        out_shape=jax.ShapeDtypeStruct((t, BRANCH_WIDTH), BF16),
        grid=(nb, nq, nk + 1),
        in_specs=[
            pl.BlockSpec((TQ, BRANCH_WIDTH), lambda b, qi, kk: (b * nq + qi, q_blk)),
            pl.BlockSpec((TK, kv_width), lambda b, qi, kk: (kv_row(b, qi, kk), k_blk)),
            pl.BlockSpec((TK, kv_width), lambda b, qi, kk: (kv_row(b, qi, kk), v_blk)),
            pl.BlockSpec((None, None, None, past, kv_width), lambda b, qi, kk: (b, l, 0, 0, 0)),
            pl.BlockSpec((None, None, None, past, kv_width), lambda b, qi, kk: (b, l, 1, 0, 0)),
        ] + extra_specs,
        out_specs=pl.BlockSpec((TQ, BRANCH_WIDTH), lambda b, qi, kk: (b * nq + qi, 0)),
        scratch_shapes=[pltpu.VMEM((n_state, TQ, LANES), F32)] * (2 if row_sum_scratch else 1)
        + [pltpu.VMEM((n_state, TQ, 2 * LANES), F32)],
        compiler_params=_cparams(("parallel", "parallel", "arbitrary")),
        name=name,
    )(qkv, qkv, qkv, cache, cache, *extra)


def _window_sample_kernel(q0_ref, q1_ref, kp_ref, kc_ref, kn_ref, vp_ref, vc_ref, vn_ref, kctx_ref, vctx_ref,
                          sink_ref, o_ref, *, dec_seq):
    qi = pl.program_id(1)
    q_start = qi * TQ
    half = TQ // 2
    q_pos = q_start + lax.broadcasted_iota(jnp.int32, (TQ, 1), 0)
    segs = ((kp_ref, vp_ref, q_start - half, half), (kc_ref, vc_ref, q_start, TQ),
            (kn_ref, vn_ref, q_start + TQ, half))
    valid = []
    for _, _, start, n in segs:
        k_pos = start + lax.broadcasted_iota(jnp.int32, (TQ, n), 1)
        valid.append((jnp.abs(q_pos - k_pos) <= WINDOW) & (k_pos >= 0) & (k_pos < dec_seq))
    for kvh in range(2):
        q_ref = q0_ref if kvh == 0 else q1_ref
        lo, hi = kvh * HEAD_DIM, (kvh + 1) * HEAD_DIM
        kctx = kctx_ref[:, lo:hi].astype(BF16)
        vctx = vctx_ref[:, lo:hi].astype(BF16)
        for g in range(4):
            hq = kvh * 4 + g
            q = q_ref[:, g * HEAD_DIM:(g + 1) * HEAD_DIM]
            sink = sink_ref[hq] * LOG2E
            scores = [_dot_nt(q, kctx)]
            for (k_ref, _, _, _), ok in zip(segs, valid):
                scores.append(jnp.where(ok, _dot_nt(q, k_ref[:, lo:hi]), NEG_BIG))
            m = jnp.maximum(scores[0].max(axis=-1, keepdims=True), sink)
            for s in scores[1:]:
                m = jnp.maximum(m, s.max(axis=-1, keepdims=True))
            es = [jnp.exp2(s - m) for s in scores]
            den = jnp.exp2(sink - m)
            for e in es:
                den = den + jnp.sum(e, axis=-1, keepdims=True)
            o = _dot(es[0].astype(BF16), vctx)
            for e, (_, v_ref, _, _) in zip(es[1:], segs):
                o = o + _dot(e.astype(BF16), v_ref[:, lo:hi])
            o = o * (1.0 / den)
            o_ref[:, hq * HEAD_DIM:(hq + 1) * HEAD_DIM] = o.astype(o_ref.dtype)


def _window_sample(qkv, cache, sink_l, l, *, dec_seq):
    t = qkv.shape[0]
    nb = t // dec_seq
    nq = dec_seq // TQ
    half = TQ // 2
    n_half = dec_seq // half
    past = cache.shape[3]
    kvw = 2 * HEAD_DIM
    q_blk = COL_CQ // 512
    k_blk, v_blk = COL_CK // kvw, COL_CV // kvw
    prev_row = lambda b, qi: b * n_half + jnp.maximum(2 * qi - 1, 0)
    next_row = lambda b, qi: b * n_half + jnp.minimum(2 * qi + 2, n_half - 1)
    return pl.pallas_call(
        functools.partial(_window_sample_kernel, dec_seq=dec_seq),
        out_shape=jax.ShapeDtypeStruct((t, BRANCH_WIDTH), BF16),
        grid=(nb, nq),
        in_specs=[
            pl.BlockSpec((TQ, 512), lambda b, qi: (b * nq + qi, q_blk)),
            pl.BlockSpec((TQ, 512), lambda b, qi: (b * nq + qi, q_blk + 1)),
            pl.BlockSpec((half, kvw), lambda b, qi: (prev_row(b, qi), k_blk)),
            pl.BlockSpec((TQ, kvw), lambda b, qi: (b * nq + qi, k_blk)),
            pl.BlockSpec((half, kvw), lambda b, qi: (next_row(b, qi), k_blk)),
            pl.BlockSpec((half, kvw), lambda b, qi: (prev_row(b, qi), v_blk)),
            pl.BlockSpec((TQ, kvw), lambda b, qi: (b * nq + qi, v_blk)),
            pl.BlockSpec((half, kvw), lambda b, qi: (next_row(b, qi), v_blk)),
            pl.BlockSpec((None, None, None, past, kvw), lambda b, qi: (b, l, 0, 0, 0)),
            pl.BlockSpec((None, None, None, past, kvw), lambda b, qi: (b, l, 1, 0, 0)),
            pl.BlockSpec(memory_space=pltpu.SMEM),
        ],
        out_specs=pl.BlockSpec((TQ, BRANCH_WIDTH), lambda b, qi: (b * nq + qi, 0)),
        compiler_params=_cparams(("parallel", "parallel")),
        name="window_attention",
    )(qkv, qkv, qkv, qkv, qkv, qkv, qkv, qkv, cache, cache, sink_l)


def _merge_kernel(oa_ref, ob_ref, oc_ref, g_ref, w_ref, y_ref, acc_scr):
    r = pl.program_id(2)

    def contrib(o_ref):
        return g_ref[...].astype(F32) * _dot(o_ref[...], w_ref[...].astype(BF16))

    @pl.when(r == 0)
    def _():
        acc_scr[...] = contrib(oa_ref)

    @pl.when(r == 1)
    def _():
        acc_scr[...] += contrib(ob_ref)

    @pl.when(r == 2)
    def _():
        y_ref[...] = (acc_scr[...] + contrib(oc_ref)).astype(y_ref.dtype)


def _merge_branches(o_arrays, o_blocks, gates, w_branch, l):
    t = gates.shape[0]
    tn = 1024
    nn = D_MODEL // tn
    o_specs = [pl.BlockSpec((TM, BRANCH_WIDTH), functools.partial(lambda i, n, r, blk: (i, blk), blk=blk))
               for blk in o_blocks]
    return pl.pallas_call(
        _merge_kernel,
        out_shape=jax.ShapeDtypeStruct((t, D_MODEL), BF16),
        grid=(t // TM, nn, 3),
        in_specs=o_specs + [
            pl.BlockSpec((TM, tn), lambda i, n, r: (i, r * nn + n)),
            pl.BlockSpec((None, None, BRANCH_WIDTH, tn), lambda i, n, r: (l, r, 0, n)),
        ],
        out_specs=pl.BlockSpec((TM, tn), lambda i, n, r: (i, n)),
        scratch_shapes=[pltpu.VMEM((TM, tn), F32)],
        compiler_params=_cparams(("parallel", "parallel", "arbitrary")),
        name="merge_branches",
    )(*o_arrays, gates, w_branch)


def _layer_norm_rows(v, gain, bias):
    return _norm_rows(v) * gain + bias


def _outproj_kernel(y_ref, w_ref, x_ref, g_ref, gain_ref, bias_ref, o_ref, z_scr):
    n = pl.program_id(1)
    n_blocks, _, tn = z_scr.shape
    z_scr[n] = _dot(y_ref[...], w_ref[...].astype(BF16))

    @pl.when(n == n_blocks - 1)
    def _():
        for b in range(n_blocks):
            cols = slice(b * tn, (b + 1) * tn)
            o_ref[:, cols] = ALPHA * x_ref[:, cols] + g_ref[:, cols] * z_scr[b]
        o_ref[...] = _layer_norm_rows(o_ref[...], gain_ref[...], bias_ref[...])


def _out_projection(y, w_o, l, x, mod_l, ln_gain_l, ln_bias_l, *, row_of_tile):
    t = x.shape[0]
    tn = 512
    return pl.pallas_call(
        _outproj_kernel,
        out_shape=jax.ShapeDtypeStruct((t, D_MODEL), F32),
        grid=(t // TM_SMALL, D_MODEL // tn),
        in_specs=[
            pl.BlockSpec((TM_SMALL, D_MODEL), lambda i, n: (i, 0)),
            pl.BlockSpec((None, D_MODEL, tn), lambda i, n: (l, 0, n)),
            pl.BlockSpec((TM_SMALL, D_MODEL), lambda i, n: (i, 0)),
            _mod_spec(2, row_of_tile),
            pl.BlockSpec((1, D_MODEL), lambda i, n: (0, 0)),
            pl.BlockSpec((1, D_MODEL), lambda i, n: (0, 0)),
        ],
        out_specs=pl.BlockSpec((TM_SMALL, D_MODEL), lambda i, n: (i, 0)),
        scratch_shapes=[pltpu.VMEM((D_MODEL // tn, TM_SMALL, tn), F32)],
        compiler_params=_cparams(("parallel", "arbitrary")),
        name="out_projection",
    )(y, w_o, x, mod_l, ln_gain_l, ln_bias_l)


def _route(p):
    rows = [p[e:e + 1, :] for e in range(N_EXPERTS)]
    best_score, best_group = None, None
    for g in range(N_GROUPS):
        members = rows[g * EXPERTS_PER_GROUP:(g + 1) * EXPERTS_PER_GROUP]
        score = None
        for a in range(EXPERTS_PER_GROUP):
            for b in range(a + 1, EXPERTS_PER_GROUP):
                pair = members[a] + members[b]
                score = pair if score is None else jnp.maximum(score, pair)
        if g == 0:
            best_score, best_group = score, jnp.zeros(score.shape, F32)
        else:
            better = score > best_score
            best_group = jnp.where(better, float(g), best_group)
            best_score = jnp.where(better, score, best_score)
    e_idx = lax.broadcasted_iota(jnp.int32, p.shape, 0).astype(F32)
    g_idx = jnp.floor(e_idx * (1.0 / EXPERTS_PER_GROUP))
    masked = jnp.where(g_idx == best_group, p, -1.0)
    w1 = jnp.max(masked, axis=0, keepdims=True)
    i1 = jnp.min(jnp.where(masked == w1, e_idx, float(N_EXPERTS)), axis=0, keepdims=True)
    masked2 = jnp.where(e_idx == i1, -2.0, masked)
    w2 = jnp.max(masked2, axis=0, keepdims=True)
    i2 = jnp.min(jnp.where(masked2 == w2, e_idx, float(N_EXPERTS)), axis=0, keepdims=True)
    tot = w1 + w2
    return e_idx, i1, i2, w1 / tot, w2 / tot


ROUTE_ROWS = 8


def _moe_route_kernel(x_ref, sh_ref, sc_ref, wr_ref, h_ref, rec_ref, rec_t_ref, cnt_ref, carry_scr):
    @pl.when(pl.program_id(0) == 0)
    def _():
        carry_scr[...] = jnp.zeros(carry_scr.shape, F32)

    h = _norm_rows(x_ref[...]) * (1.0 + sc_ref[...]) + sh_ref[...]
    h_ref[...] = h
    logits = _dot_nt(wr_ref[...].astype(BF16), h.astype(BF16))
    m = jnp.max(logits, axis=0, keepdims=True)
    e = jnp.exp(logits - m)
    probs = e / jnp.sum(e, axis=0, keepdims=True)
    e_idx, i1, i2, w1, w2 = _route(probs)
    tm = probs.shape[1]
    oh1 = (e_idx == i1).astype(F32)
    oh2 = (e_idx == i2).astype(F32)
    oh = oh1 + oh2
    earlier = (lax.broadcasted_iota(jnp.int32, (tm, tm), 0) < lax.broadcasted_iota(jnp.int32, (tm, tm), 1))
    rank = carry_scr[:, 0:1] + _dot(oh.astype(BF16), earlier.astype(BF16))
    r1 = jnp.sum(oh1 * rank, axis=0, keepdims=True)
    r2 = jnp.sum(oh2 * rank, axis=0, keepdims=True)
    carry_scr[...] = carry_scr[...] + jnp.sum(oh, axis=1, keepdims=True)
    cnt_ref[...] = carry_scr[...]
    row = lax.broadcasted_iota(jnp.int32, (ROUTE_ROWS, tm), 0)
    rec = jnp.zeros((ROUTE_ROWS, tm), F32)
    for k, v in enumerate((i1, i2, r1, r2, w1, w2)):
        rec = jnp.where(row == k, v, rec)
    rec_ref[...] = rec
    pad = jnp.zeros((LANES - ROUTE_ROWS, tm), F32)
    rec_t_ref[...] = jnp.concatenate([rec, pad], axis=0).T


def _moe_route(x, mod_l, w_router_t, *, row_of_tile):
    t = x.shape[0]
    tm = TM_SMALL
    return pl.pallas_call(
        _moe_route_kernel,
        out_shape=(jax.ShapeDtypeStruct((t, D_MODEL), F32), jax.ShapeDtypeStruct((ROUTE_ROWS, t), F32),
                   jax.ShapeDtypeStruct((t, LANES), F32), jax.ShapeDtypeStruct((N_EXPERTS, LANES), F32)),
        grid=(t // tm,),
        in_specs=[
            pl.BlockSpec((tm, D_MODEL), lambda i: (i, 0)),
            _mod_spec(3, row_of_tile),
            _mod_spec(4, row_of_tile),
            pl.BlockSpec((N_EXPERTS, D_MODEL), lambda i: (0, 0)),
        ],
        out_specs=(pl.BlockSpec((tm, D_MODEL), lambda i: (i, 0)),
                   pl.BlockSpec((ROUTE_ROWS, tm), lambda i: (0, i)),
                   pl.BlockSpec((tm, LANES), lambda i: (i, 0)),
                   pl.BlockSpec((N_EXPERTS, LANES), lambda i: (0, 0))),
        scratch_shapes=[pltpu.VMEM((N_EXPERTS, LANES), F32)],
        compiler_params=_cparams(("arbitrary",)),
        name="moe_route",
    )(x, mod_l, mod_l, w_router_t)


ROW_DMA_UNROLL = 8


def _row_copy(src_hbm, src_row, dst_buf, dst_row, sem):
    return pltpu.make_async_copy(src_hbm.at[pl.ds(src_row, 1)], dst_buf.at[pl.ds(dst_row, 1)], sem)


def _moe_expert_kernel(tile_expert_ref, n_tiles_ref, src_ref, h_hbm, wg_ref, wu_ref, wd_ref, ys_ref,
                       x_buf, sems, *, te):
    del tile_expert_ref
    j = pl.program_id(0)
    n_valid = n_tiles_ref[0]
    slot = j % 2

    def start_gather(tile, s):
        def body(r, carry):
            _row_copy(h_hbm, src_ref[tile * te + r], x_buf.at[s], r, sems.at[s]).start()
            return carry
        lax.fori_loop(0, te, body, 0, unroll=ROW_DMA_UNROLL)

    def wait_gather(s):
        def body(r, carry):
            _row_copy(h_hbm, 0, x_buf.at[s], r, sems.at[s]).wait()
            return carry
        lax.fori_loop(0, te, body, 0, unroll=ROW_DMA_UNROLL)

    @pl.when((j == 0) & (n_valid > 0))
    def _():
        start_gather(0, 0)

    @pl.when(j + 1 < n_valid)
    def _():
        start_gather(j + 1, 1 - slot)

    @pl.when(j < n_valid)
    def _():
        wait_gather(slot)
        x = x_buf[slot].astype(BF16)
        gate = _dot(x, wg_ref[...].astype(BF16))
        up = _dot(x, wu_ref[...].astype(BF16))
        hid = gate / (1.0 + jnp.exp(-gate)) * up
        ys_ref[...] = _dot(hid.astype(BF16), wd_ref[...].astype(BF16))

    @pl.when(j >= n_valid)
    def _():
        ys_ref[...] = jnp.zeros(ys_ref.shape, F32)


def _moe_experts(h, tile_expert, n_tiles, src, wg, wu, wd, l, *, te):
    n_rows = src.shape[0]
    w_in_spec = pl.BlockSpec((None, None, D_MODEL, D_EXPERT), lambda j, te_ref, *_: (l, te_ref[j], 0, 0))
    w_out_spec = pl.BlockSpec((None, None, D_EXPERT, D_MODEL), lambda j, te_ref, *_: (l, te_ref[j], 0, 0))
    return pl.pallas_call(
        functools.partial(_moe_expert_kernel, te=te),
        out_shape=jax.ShapeDtypeStruct((n_rows, D_MODEL), F32),
        grid_spec=pltpu.PrefetchScalarGridSpec(
            num_scalar_prefetch=3,
            grid=(n_rows // te,),
            in_specs=[pl.BlockSpec(memory_space=pltpu.HBM), w_in_spec, w_in_spec, w_out_spec],
            out_specs=pl.BlockSpec((te, D_MODEL), lambda j, *_: (j, 0)),
            scratch_shapes=[pltpu.VMEM((2, te, D_MODEL), F32), pltpu.SemaphoreType.DMA((2,))],
        ),
        compiler_params=_cparams(("arbitrary",)),
        name="moe_experts",
    )(tile_expert, n_tiles, src, h, wg, wu, wd)


TM_COMBINE = 256


def _moe_combine_kernel(dest_ref, ys_hbm, x_ref, rec_t_ref, g_ref, gain_ref, bias_ref, o_ref, y_buf, sems,
                        *, n_tokens):
    i = pl.program_id(0)
    tm = x_ref.shape[0]
    slot = i % 2

    def start_gather(tile, s):
        def body(r, carry):
            for k in range(2):
                row = dest_ref[k * n_tokens + tile * tm + r]
                _row_copy(ys_hbm, row, y_buf.at[s, k], r, sems.at[s]).start()
            return carry
        lax.fori_loop(0, tm, body, 0, unroll=ROW_DMA_UNROLL)

    def wait_gather(s):
        def body(r, carry):
            for k in range(2):
                _row_copy(ys_hbm, 0, y_buf.at[s, k], r, sems.at[s]).wait()
            return carry
        lax.fori_loop(0, tm, body, 0, unroll=ROW_DMA_UNROLL)

    @pl.when(i == 0)
    def _():
        start_gather(0, 0)

    @pl.when(i + 1 < pl.num_programs(0))
    def _():
        start_gather(i + 1, 1 - slot)

    wait_gather(slot)
    z = rec_t_ref[:, 4:5] * y_buf[slot, 0] + rec_t_ref[:, 5:6] * y_buf[slot, 1]
    v = ALPHA * x_ref[...] + g_ref[...] * z
    o_ref[...] = _layer_norm_rows(v, gain_ref[...], bias_ref[...])


def _moe_combine(dest, ys, x, rec_t, mod_l, ln_gain_l, ln_bias_l, *, row_of_tile):
    t = x.shape[0]
    tm = TM_COMBINE
    vec = pl.BlockSpec((1, D_MODEL), lambda i, *_: (0, 0))
    return pl.pallas_call(
        functools.partial(_moe_combine_kernel, n_tokens=t),
        out_shape=jax.ShapeDtypeStruct((t, D_MODEL), F32),
        grid_spec=pltpu.PrefetchScalarGridSpec(
            num_scalar_prefetch=1,
            grid=(t // tm,),
            in_specs=[
                pl.BlockSpec(memory_space=pltpu.HBM),
                pl.BlockSpec((tm, D_MODEL), lambda i, *_: (i, 0)),
                pl.BlockSpec((tm, LANES), lambda i, *_: (i, 0)),
                _mod_spec(5, row_of_tile),
                vec, vec,
            ],
            out_specs=pl.BlockSpec((tm, D_MODEL), lambda i, *_: (i, 0)),
            scratch_shapes=[pltpu.VMEM((2, 2, tm, D_MODEL), F32), pltpu.SemaphoreType.DMA((2,))],
        ),
        compiler_params=_cparams(("arbitrary",)),
        name="moe_combine",
    )(dest, ys, x, rec_t, mod_l, ln_gain_l, ln_bias_l)


def _dispatch_tables(rec, cnt, *, te):
    t = rec.shape[1]
    n_rows = 2 * t + N_EXPERTS * te
    e12 = rec[0:2].astype(jnp.int32)
    r12 = rec[2:4].astype(jnp.int32)
    counts = cnt[:, 0].astype(jnp.int32)
    padded = (counts + te - 1) // te * te
    ends = jnp.cumsum(padded)
    offsets = ends - padded
    dest = (offsets[e12] + r12).reshape(2 * t)
    tokens = jnp.tile(jnp.arange(t, dtype=jnp.int32), 2)
    src = jnp.zeros((n_rows,), jnp.int32).at[dest].set(tokens, unique_indices=True)
    tile_start = jnp.arange(n_rows // te, dtype=jnp.int32) * te
    tile_expert = jnp.minimum(jnp.searchsorted(ends, tile_start, side="right"), N_EXPERTS - 1).astype(jnp.int32)
    n_tiles = (ends[-1:] // te).astype(jnp.int32)
    return dest, src, tile_expert, n_tiles


def _rope_tables(n_tokens):
    rows = n_tokens // GRID_W
    row = jnp.repeat(jnp.arange(rows), GRID_W).astype(F32)
    col = jnp.tile(jnp.arange(GRID_W), rows).astype(F32)
    quarter = HEAD_DIM // 4
    inv_freq = ROPE_THETA ** (-jnp.arange(quarter, dtype=F32) / quarter)
    ang_r, ang_c = row[:, None] * inv_freq, col[:, None] * inv_freq
    cos = jnp.concatenate([jnp.cos(ang_r), jnp.cos(ang_r), jnp.cos(ang_c), jnp.cos(ang_c)], axis=-1)
    sin = jnp.concatenate([-jnp.sin(ang_r), jnp.sin(ang_r), -jnp.sin(ang_c), jnp.sin(ang_c)], axis=-1)
    return cos, sin


def _row_of_tile(first_row, tokens_per_row):
    def for_tile(tile):
        return lambda i: first_row + (i * tile) // tokens_per_row
    return for_tile


def _mixer_and_ffn(x, attn_arrays, attn_blocks, gates, mod_l, rows, l, expert_tile, w_branch, w_o, ln_gain,
                   ln_bias, w_router_t, w_e_gate, w_e_up, w_e_down):
    y = _merge_branches(attn_arrays, attn_blocks, gates, w_branch, l)
    x = _out_projection(y, w_o, l, x, mod_l, ln_gain[l, 0:1], ln_bias[l, 0:1], row_of_tile=rows(TM_SMALL))
    h2, rec, rec_t, cnt = _moe_route(x, mod_l, w_router_t, row_of_tile=rows(TM_SMALL))
    dest, src, tile_expert, n_tiles = _dispatch_tables(rec, cnt, te=expert_tile)
    ys = _moe_experts(h2, tile_expert, n_tiles, src, w_e_gate, w_e_up, w_e_down, l, te=expert_tile)
    return _moe_combine(dest, ys, x, rec_t, mod_l, ln_gain[l, 1:2], ln_bias[l, 1:2],
                        row_of_tile=rows(TM_COMBINE))


def kernel(x_prompt, x_sample, cache_kv_a, cache_kv_b, cache_kv_c, c, c_ctx, w_in, w_gate, w_branch, w_o,
           w_mod, b_mod, ln_gain, ln_bias, diff_lam, diff_subln, qk_gain, sink, w_router, w_e_gate, w_e_up,
           w_e_down):
    batch, seq, _ = x_prompt.shape
    dec_batch, dec_seq, _ = x_sample.shape
    past = cache_kv_a.shape[3]
    t_p, t_s = batch * seq, dec_batch * dec_seq

    cond = jnp.concatenate([c_ctx[None], c, jnp.zeros((MOD_ROWS - 1 - dec_batch, D_MODEL), F32)], axis=0)
    mod = _modulation(cond.T, w_mod, b_mod)
    cos_t, sin_t = _rope_tables(dec_seq)
    w_router_t = w_router.T
    cache_a = cache_kv_a.reshape(dec_batch, DEPTH, 2, past, 4 * 256)
    cache_b = cache_kv_b.reshape(dec_batch, DEPTH, 2, past, 2 * HEAD_DIM)
    cache_c = cache_kv_c.reshape(dec_batch, DEPTH, 2, past, 2 * HEAD_DIM)
    rows_p = _row_of_tile(0, t_p)
    rows_s = _row_of_tile(1, dec_seq)

    y_p = x_prompt.reshape(t_p, D_MODEL)
    y_s = x_sample.reshape(t_s, D_MODEL)
    new_a, new_b, new_c = [], [], []
    for l in range(DEPTH):
        lam_init = 0.8 - 0.6 * math.exp(-0.3 * l)
        mod_l = mod[l].reshape(MOD_ROWS, 1, N_MOD * D_MODEL)
        subln_l = diff_subln[l].reshape(1, 256)
        shared = (w_branch, w_o, ln_gain, ln_bias, w_router_t, w_e_gate, w_e_up, w_e_down)

        p = _in_projection(y_p, mod_l, w_in, l, qk_gain[l], cos_t, sin_t, row_of_tile=rows_p(TM),
                           rope=False, q_scale=1.0, out_dtype=F32)
        gates = _branch_gates(y_p, mod_l, w_gate, l, row_of_tile=rows_p(TM))
        attn = _attention_prompt(p, diff_lam[l], subln_l, sink[l], lam_init=lam_init, seq=seq)
        y_p = _mixer_and_ffn(y_p, (attn, attn, attn), (0, 1, 2), gates, mod_l, rows_p, l, 256, *shared)
        p5 = p.reshape(batch, seq, IN_COLS)
        new_a.append(jnp.stack([p5[..., COL_AK:COL_AV], p5[..., COL_AV:COL_BQ]], axis=1))
        new_b.append(jnp.stack([p5[..., COL_BK:COL_BV], p5[..., COL_BV:COL_CQ]], axis=1))
        new_c.append(jnp.stack([p5[..., COL_CK:COL_CV], p5[..., COL_CV:]], axis=1))

        qkv = _in_projection(y_s, mod_l, w_in, l, qk_gain[l], cos_t, sin_t, row_of_tile=rows_s(TM),
                             rope=True, q_scale=SCALE * LOG2E, out_dtype=BF16)
        gates = _branch_gates(y_s, mod_l, w_gate, l, row_of_tile=rows_s(TM))
        vec = lambda shape: pl.BlockSpec(shape, lambda b, qi, kk: (0, 0))
        a_o = _flash_sample(functools.partial(_diff_sample_kernel, lam_init=lam_init), qkv, cache_a, l,
                            q_col=COL_AQ, k_col=COL_AK, v_col=COL_AV, kv_width=1024, n_state=8,
                            row_sum_scratch=True, extra=(diff_lam[l], subln_l),
                            extra_specs=[vec((4, HEAD_DIM)), vec((1, 256))], name="diff_attention",
                            dec_seq=dec_seq)
        b_o = _flash_sample(_gqa_sample_kernel, qkv, cache_b, l, q_col=COL_BQ, k_col=COL_BK, v_col=COL_BV,
                            kv_width=256, n_state=8, row_sum_scratch=False, extra=(), extra_specs=[],
                            name="gqa_attention", dec_seq=dec_seq)
        c_o = _window_sample(qkv, cache_c, sink[l], l, dec_seq=dec_seq)
        y_s = _mixer_and_ffn(y_s, (a_o, b_o, c_o), (0, 0, 0), gates, mod_l, rows_s, l, 512, *shared)

    new_kv_a = jnp.stack(new_a, axis=1).reshape(batch, DEPTH, 2, seq, 4, 256)
    new_kv_b = jnp.stack(new_b, axis=1).reshape(batch, DEPTH, 2, seq, 2, HEAD_DIM)
    new_kv_c = jnp.stack(new_c, axis=1).reshape(batch, DEPTH, 2, seq, 2, HEAD_DIM)
    return (y_p.reshape(batch, seq, D_MODEL), y_s.reshape(dec_batch, dec_seq, D_MODEL),
            new_kv_a, new_kv_b, new_kv_c)
```

```python
import functools
import math

import jax
import jax.numpy as jnp
from jax import lax
from jax.experimental import pallas as pl
from jax.experimental.pallas import tpu as pltpu

F32 = jnp.float32
BF16 = jnp.bfloat16

D_MODEL = 2048
HEAD_DIM = 128
GRID_W = 64
ROPE_THETA = 10000.0
WINDOW = 128
N_EXPERTS = 16
N_GROUPS = 4
EXPERTS_PER_GROUP = N_EXPERTS // N_GROUPS
D_EXPERT = 512
N_MOD = 6
DEPTH = 2
ALPHA = (2 * DEPTH) ** 0.25
EPS = 1e-6
IN_COLS = 6144
BRANCH_WIDTH = 1024
SCALE = HEAD_DIM ** -0.5
LOG2E = math.log2(math.e)
NEG_BIG = -1e30

COL_AQ, COL_AK, COL_AV = 0, 1024, 2048
COL_BQ, COL_BK, COL_BV = 3072, 4096, 4352
COL_CQ, COL_CK, COL_CV = 4608, 5632, 5888

LANES = 128
VMEM_LIMIT = 56 * 1024 * 1024

TM = 1024
TN_PROJ = 512
TN_GATE = 1024
TM_SMALL = 512


def _cparams(sem):
    return pltpu.CompilerParams(dimension_semantics=sem, vmem_limit_bytes=VMEM_LIMIT)


def _dot(a, b):
    return jnp.dot(a, b, preferred_element_type=F32)


def _dot_nt(a, b):
    return lax.dot_general(a, b, (((1,), (1,)), ((), ())), preferred_element_type=F32)


def _norm_rows(x):
    mu = jnp.mean(x, axis=-1, keepdims=True)
    xc = x - mu
    var = jnp.mean(xc * xc, axis=-1, keepdims=True)
    return xc * lax.rsqrt(var + EPS)


def _rms(v, gain):
    ms = jnp.mean(v * v, axis=-1, keepdims=True)
    return v * lax.rsqrt(ms + EPS) * gain


def _rope(v, cos, sin_signed):
    lane = lax.broadcasted_iota(jnp.int32, v.shape, 1)
    first_half = (lane % 64) < 32
    partner = jnp.where(first_half, pltpu.roll(v, 96, 1), pltpu.roll(v, 32, 1))
    return v * cos + partner * sin_signed


N_COND = 3
MOD_ROWS = 8
TN_MOD = 1024


def _mod_kernel(cond_ref, w_ref, b_ref, o_ref):
    w = w_ref[...]
    row_idx = lax.broadcasted_iota(jnp.int32, (MOD_ROWS, TN_MOD), 0)
    out = jnp.zeros((MOD_ROWS, TN_MOD), F32)
    for r in range(N_COND):
        c = cond_ref[:, r:r + 1]
        s = c / (1.0 + jnp.exp(-c))
        m = jnp.sum(w * s, axis=0, keepdims=True) + b_ref[...]
        out = jnp.where(row_idx == r, m, out)
    o_ref[...] = out


def _modulation(cond_t, w_mod, b_mod):
    n = N_MOD * D_MODEL
    return pl.pallas_call(
        _mod_kernel,
        out_shape=jax.ShapeDtypeStruct((DEPTH, MOD_ROWS, n), F32),
        grid=(DEPTH, n // TN_MOD),
        in_specs=[
            pl.BlockSpec((D_MODEL, MOD_ROWS), lambda l, j: (0, 0)),
            pl.BlockSpec((None, D_MODEL, TN_MOD), lambda l, j: (l, 0, j)),
            pl.BlockSpec((None, 1, TN_MOD), lambda l, j: (l, 0, j)),
        ],
        out_specs=pl.BlockSpec((None, MOD_ROWS, TN_MOD), lambda l, j: (l, 0, j)),
        compiler_params=_cparams(("parallel", "parallel")),
        name="modulation",
    )(cond_t, w_mod, b_mod.reshape(DEPTH, 1, n))


def _modulate_to_scratch(x_ref, sh_ref, sc_ref, h_scr):
    h = _norm_rows(x_ref[...]) * (1.0 + sc_ref[...]) + sh_ref[...]
    h_scr[...] = h.astype(BF16)


def _inproj_kernel(x_ref, sh_ref, sc_ref, w_ref, qk_gain_ref, cos_ref, sin_ref, o_ref, h_scr, *, rope,
                   q_scale):
    j = pl.program_id(1)
    gain_q = qk_gain_ref[0:1, :]
    gain_k = qk_gain_ref[1:2, :]

    @pl.when(j == 0)
    def _():
        _modulate_to_scratch(x_ref, sh_ref, sc_ref, h_scr)

    acc = _dot(h_scr[...], w_ref[...].astype(BF16))
    n_chunks = TN_PROJ // LANES

    def chunk(c):
        return acc[:, c * LANES:(c + 1) * LANES]

    def store(c, v):
        o_ref[:, c * LANES:(c + 1) * LANES] = v.astype(o_ref.dtype)

    def rp(v):
        return _rope(v, cos_ref[...], sin_ref[...]) if rope else v

    def qs(v):
        return v * q_scale if q_scale != 1.0 else v

    @pl.when((j <= 1) | (j == 9) | (j == 10))
    def _():
        for c in range(n_chunks):
            store(c, qs(rp(chunk(c))))

    @pl.when((j == 2) | (j == 3))
    def _():
        for c in range(n_chunks):
            store(c, rp(chunk(c)))

    @pl.when((j == 4) | (j == 5))
    def _():
        o_ref[...] = acc.astype(o_ref.dtype)

    @pl.when((j == 6) | (j == 7))
    def _():
        for c in range(n_chunks):
            store(c, qs(rp(_rms(chunk(c), gain_q))))

    @pl.when(j == 8)
    def _():
        for c in range(2):
            store(c, rp(_rms(chunk(c), gain_k)))
        for c in range(2, n_chunks):
            store(c, chunk(c))

    @pl.when(j == 11)
    def _():
        for c in range(2):
            store(c, rp(chunk(c)))
        for c in range(2, n_chunks):
            store(c, chunk(c))


def _gate_kernel(x_ref, sh_ref, sc_ref, w_ref, o_ref, h_scr):
    @pl.when(pl.program_id(1) == 0)
    def _():
        _modulate_to_scratch(x_ref, sh_ref, sc_ref, h_scr)

    acc = _dot(h_scr[...], w_ref[...].astype(BF16))
    o_ref[...] = (1.0 / (1.0 + jnp.exp(-acc))).astype(o_ref.dtype)


def _mod_spec(which, row_of_tile):
    return pl.BlockSpec((None, 1, D_MODEL), lambda i, *_: (row_of_tile(i), 0, which))


def _in_projection(x, mod_l, w_in, l, qk_gain_l, cos_t, sin_t, *, row_of_tile, rope, q_scale, out_dtype):
    t = x.shape[0]
    tiles_per_seq = cos_t.shape[0] // TM
    return pl.pallas_call(
        functools.partial(_inproj_kernel, rope=rope, q_scale=q_scale),
        out_shape=jax.ShapeDtypeStruct((t, IN_COLS), out_dtype),
        grid=(t // TM, IN_COLS // TN_PROJ),
        in_specs=[
            pl.BlockSpec((TM, D_MODEL), lambda i, j: (i, 0)),
            _mod_spec(0, row_of_tile),
            _mod_spec(1, row_of_tile),
            pl.BlockSpec((None, D_MODEL, TN_PROJ), lambda i, j: (l, 0, j)),
            pl.BlockSpec((2, HEAD_DIM), lambda i, j: (0, 0)),
            pl.BlockSpec((TM, HEAD_DIM), lambda i, j: (i % tiles_per_seq, 0)),
            pl.BlockSpec((TM, HEAD_DIM), lambda i, j: (i % tiles_per_seq, 0)),
        ],
        out_specs=pl.BlockSpec((TM, TN_PROJ), lambda i, j: (i, j)),
        scratch_shapes=[pltpu.VMEM((TM, D_MODEL), BF16)],
        compiler_params=_cparams(("parallel", "arbitrary")),
        name="in_projection",
    )(x, mod_l, mod_l, w_in, qk_gain_l, cos_t, sin_t)


def _branch_gates(x, mod_l, w_gate, l, *, row_of_tile):
    t = x.shape[0]
    n = w_gate.shape[2]
    return pl.pallas_call(
        _gate_kernel,
        out_shape=jax.ShapeDtypeStruct((t, n), BF16),
        grid=(t // TM, n // TN_GATE),
        in_specs=[
            pl.BlockSpec((TM, D_MODEL), lambda i, j: (i, 0)),
            _mod_spec(0, row_of_tile),
            _mod_spec(1, row_of_tile),
            pl.BlockSpec((None, D_MODEL, TN_GATE), lambda i, j: (l, 0, j)),
        ],
        out_specs=pl.BlockSpec((TM, TN_GATE), lambda i, j: (i, j)),
        scratch_shapes=[pltpu.VMEM((TM, D_MODEL), BF16)],
        compiler_params=_cparams(("parallel", "arbitrary")),
        name="branch_gates",
    )(x, mod_l, mod_l, w_gate)


def _diff_lambda(lam_ref, lam_init):
    lp = lam_ref[...]
    t1 = jnp.sum(lp[0:1] * lp[1:2], axis=-1, keepdims=True)
    t2 = jnp.sum(lp[2:3] * lp[3:4], axis=-1, keepdims=True)
    return jnp.exp(t1) - jnp.exp(t2) + lam_init


def _softmax_rows(s, sink=None):
    m = jnp.max(s, axis=-1, keepdims=True)
    if sink is not None:
        m = jnp.maximum(m, sink)
    e = jnp.exp(s - m)
    den = jnp.sum(e, axis=-1, keepdims=True)
    if sink is not None:
        den = den + jnp.exp(sink - m)
    return e * (1.0 / den)


def _attn_prompt_kernel(p_ref, lam_ref, subln_ref, sink_ref, o_ref, *, lam_init):
    lam = _diff_lambda(lam_ref, lam_init)

    def blk(c0, w):
        return p_ref[:, c0:c0 + w].astype(BF16)

    for h in range(4):
        probs = []
        for m in range(2):
            q = blk(COL_AQ + h * 256 + m * HEAD_DIM, HEAD_DIM)
            k = blk(COL_AK + h * 256 + m * HEAD_DIM, HEAD_DIM)
            probs.append(_softmax_rows(_dot_nt(q, k) * SCALE))
        w = probs[0] - lam * probs[1]
        o = _dot(w.astype(BF16), blk(COL_AV + h * 256, 256))
        o = _rms(o, subln_ref[...]) * (1.0 - lam_init)
        o_ref[:, h * 256:(h + 1) * 256] = o.astype(o_ref.dtype)

    for mixer, (cq, ck, cv) in enumerate(((COL_BQ, COL_BK, COL_BV), (COL_CQ, COL_CK, COL_CV))):
        for kvh in range(2):
            k = blk(ck + kvh * HEAD_DIM, HEAD_DIM)
            v = blk(cv + kvh * HEAD_DIM, HEAD_DIM)
            for g in range(4):
                hq = kvh * 4 + g
                q = blk(cq + hq * HEAD_DIM, HEAD_DIM)
                sink = sink_ref[hq] if mixer == 1 else None
                p = _softmax_rows(_dot_nt(q, k) * SCALE, sink)
                o = _dot(p.astype(BF16), v)
                c0 = BRANCH_WIDTH * (1 + mixer) + hq * HEAD_DIM
                o_ref[:, c0:c0 + HEAD_DIM] = o.astype(o_ref.dtype)


def _attention_prompt(p, diff_lam_l, subln_l, sink_l, *, lam_init, seq):
    t = p.shape[0]
    return pl.pallas_call(
        functools.partial(_attn_prompt_kernel, lam_init=lam_init),
        out_shape=jax.ShapeDtypeStruct((t, 3 * BRANCH_WIDTH), BF16),
        grid=(t // seq,),
        in_specs=[
            pl.BlockSpec((seq, IN_COLS), lambda b: (b, 0)),
            pl.BlockSpec((4, HEAD_DIM), lambda b: (0, 0)),
            pl.BlockSpec((1, 256), lambda b: (0, 0)),
            pl.BlockSpec(memory_space=pltpu.SMEM),
        ],
        out_specs=pl.BlockSpec((seq, 3 * BRANCH_WIDTH), lambda b: (b, 0)),
        compiler_params=_cparams(("parallel",)),
        name="attention_prompt",
    )(p, diff_lam_l, subln_l, sink_l)


TQ = 512
TK = 1024


def _tile_lanes(v, n):
    return jnp.concatenate([v] * n, axis=-1) if n > 1 else v


def _online_softmax(idx, s, m_scr):
    m_prev = m_scr[idx]
    m_new = jnp.maximum(m_prev, jnp.max(s, axis=-1, keepdims=True))
    m_scr[idx] = m_new
    alpha = jnp.exp2(m_prev - m_new)
    p = jnp.exp2(s - _tile_lanes(m_new, s.shape[1] // LANES))
    return p, alpha


def _diff_sample_kernel(q_ref, k_ref, v_ref, kc_ref, vc_ref, lam_ref, subln_ref, o_ref,
                        m_scr, l_scr, acc_scr, *, lam_init):
    kk = pl.program_id(2)

    @pl.when(kk == 0)
    def _():
        m_scr[...] = jnp.full(m_scr.shape, NEG_BIG, F32)
        l_scr[...] = jnp.zeros(l_scr.shape, F32)
        acc_scr[...] = jnp.zeros(acc_scr.shape, F32)

    def process(kb_ref, vb_ref):
        for h in range(4):
            v = vb_ref[:, h * 256:(h + 1) * 256].astype(BF16)
            for m in range(2):
                c0 = h * 256 + m * HEAD_DIM
                idx = h * 2 + m
                k = kb_ref[:, c0:c0 + HEAD_DIM].astype(BF16)
                p, alpha = _online_softmax(idx, _dot_nt(q_ref[:, c0:c0 + HEAD_DIM], k), m_scr)
                part = p[:, 0:LANES]
                for c in range(1, p.shape[1] // LANES):
                    part = part + p[:, c * LANES:(c + 1) * LANES]
                l_scr[idx] = alpha * l_scr[idx] + part
                acc_scr[idx] = _tile_lanes(alpha, 2) * acc_scr[idx] + _dot(p.astype(BF16), v)

    @pl.when(kk == 0)
    def _():
        process(kc_ref, vc_ref)

    @pl.when(kk > 0)
    def _():
        process(k_ref, v_ref)

    @pl.when(kk == pl.num_programs(2) - 1)
    def _():
        lam = _diff_lambda(lam_ref, lam_init)
        for h in range(4):
            l1 = jnp.sum(l_scr[2 * h], axis=-1, keepdims=True)
            l2 = jnp.sum(l_scr[2 * h + 1], axis=-1, keepdims=True)
            o1 = acc_scr[2 * h] * (1.0 / l1)
            o2 = acc_scr[2 * h + 1] * (1.0 / l2)
            o = _rms(o1 - lam * o2, subln_ref[...]) * (1.0 - lam_init)
            o_ref[:, h * 256:(h + 1) * 256] = o.astype(o_ref.dtype)


def _gqa_sample_kernel(q_ref, k_ref, v_ref, kc_ref, vc_ref, o_ref, m_scr, acc_scr):
    kk = pl.program_id(2)

    @pl.when(kk == 0)
    def _():
        m_scr[...] = jnp.full(m_scr.shape, NEG_BIG, F32)
        acc_scr[...] = jnp.zeros(acc_scr.shape, F32)

    def process(kb_ref, vb_ref):
        for kvh in range(2):
            k = kb_ref[:, kvh * HEAD_DIM:(kvh + 1) * HEAD_DIM].astype(BF16)
            v = vb_ref[:, kvh * HEAD_DIM:(kvh + 1) * HEAD_DIM].astype(BF16)
            v_ones = jnp.concatenate([v, jnp.ones_like(v)], axis=-1)
            for g in range(4):
                hq = kvh * 4 + g
                s = _dot_nt(q_ref[:, hq * HEAD_DIM:(hq + 1) * HEAD_DIM], k)
                p, alpha = _online_softmax(hq, s, m_scr)
                acc_scr[hq] = _tile_lanes(alpha, 2) * acc_scr[hq] + _dot(p.astype(BF16), v_ones)

    @pl.when(kk == 0)
    def _():
        process(kc_ref, vc_ref)

    @pl.when(kk > 0)
    def _():
        process(k_ref, v_ref)

    @pl.when(kk == pl.num_programs(2) - 1)
    def _():
        for hq in range(8):
            o = acc_scr[hq, :, 0:HEAD_DIM] / acc_scr[hq, :, HEAD_DIM:2 * HEAD_DIM]
            o_ref[:, hq * HEAD_DIM:(hq + 1) * HEAD_DIM] = o.astype(o_ref.dtype)


def _flash_sample(kernel, qkv, cache, l, *, q_col, k_col, v_col, kv_width, n_state, row_sum_scratch, extra,
                  extra_specs, name, dec_seq):
    t = qkv.shape[0]
    nb = t // dec_seq
    nq = dec_seq // TQ
    nk = dec_seq // TK
    past = cache.shape[3]
    q_blk, k_blk, v_blk = q_col // BRANCH_WIDTH, k_col // kv_width, v_col // kv_width
    kv_row = lambda b, qi, kk: b * nk + jnp.maximum(kk - 1, 0)
    return pl.pallas_call(
        kernel,
        out_shape=jax.ShapeDtypeStruct((t, BRANCH_WIDTH), BF16),
        grid=(nb, nq, nk + 1),
        in_specs=[
            pl.BlockSpec((TQ, BRANCH_WIDTH), lambda b, qi, kk: (b * nq + qi, q_blk)),
            pl.BlockSpec((TK, kv_width), lambda b, qi, kk: (kv_row(b, qi, kk), k_blk)),
            pl.BlockSpec((TK, kv_width), lambda b, qi, kk: (kv_row(b, qi, kk), v_blk)),
            pl.BlockSpec((None, None, None, past, kv_width), lambda b, qi, kk: (b, l, 0, 0, 0)),
            pl.BlockSpec((None, None, None, past, kv_width), lambda b, qi, kk: (b, l, 1, 0, 0)),
        ] + extra_specs,
        out_specs=pl.BlockSpec((TQ, BRANCH_WIDTH), lambda b, qi, kk: (b * nq + qi, 0)),
        scratch_shapes=[pltpu.VMEM((n_state, TQ, LANES), F32)] * (2 if row_sum_scratch else 1)
        + [pltpu.VMEM((n_state, TQ, 2 * LANES), F32)],
        compiler_params=_cparams(("parallel", "parallel", "arbitrary")),
        name=name,
    )(qkv, qkv, qkv, cache, cache, *extra)


def _window_sample_kernel(q0_ref, q1_ref, kp_ref, kc_ref, kn_ref, vp_ref, vc_ref, vn_ref, kctx_ref, vctx_ref,
                          sink_ref, o_ref, *, dec_seq):
    qi = pl.program_id(1)
    q_start = qi * TQ
    half = TQ // 2
    q_pos = q_start + lax.broadcasted_iota(jnp.int32, (TQ, 1), 0)
    segs = ((kp_ref, vp_ref, q_start - half, half), (kc_ref, vc_ref, q_start, TQ),
            (kn_ref, vn_ref, q_start + TQ, half))
    valid = []
    for _, _, start, n in segs:
        k_pos = start + lax.broadcasted_iota(jnp.int32, (TQ, n), 1)
        valid.append((jnp.abs(q_pos - k_pos) <= WINDOW) & (k_pos >= 0) & (k_pos < dec_seq))
    for kvh in range(2):
        q_ref = q0_ref if kvh == 0 else q1_ref
        lo, hi = kvh * HEAD_DIM, (kvh + 1) * HEAD_DIM
        kctx = kctx_ref[:, lo:hi].astype(BF16)
        vctx = vctx_ref[:, lo:hi].astype(BF16)
        for g in range(4):
            hq = kvh * 4 + g
            q = q_ref[:, g * HEAD_DIM:(g + 1) * HEAD_DIM]
            sink = sink_ref[hq] * LOG2E
            scores = [_dot_nt(q, kctx)]
            for (k_ref, _, _, _), ok in zip(segs, valid):
                scores.append(jnp.where(ok, _dot_nt(q, k_ref[:, lo:hi]), NEG_BIG))
            m = jnp.maximum(scores[0].max(axis=-1, keepdims=True), sink)
            for s in scores[1:]:
                m = jnp.maximum(m, s.max(axis=-1, keepdims=True))
            es = [jnp.exp2(s - m) for s in scores]
            den = jnp.exp2(sink - m)
            for e in es:
                den = den + jnp.sum(e, axis=-1, keepdims=True)
            o = _dot(es[0].astype(BF16), vctx)
            for e, (_, v_ref, _, _) in zip(es[1:], segs):
                o = o + _dot(e.astype(BF16), v_ref[:, lo:hi])
            o = o * (1.0 / den)
            o_ref[:, hq * HEAD_DIM:(hq + 1) * HEAD_DIM] = o.astype(o_ref.dtype)


def _window_sample(qkv, cache, sink_l, l, *, dec_seq):
    t = qkv.shape[0]
    nb = t // dec_seq
    nq = dec_seq // TQ
    half = TQ // 2
    n_half = dec_seq // half
    past = cache.shape[3]
    kvw = 2 * HEAD_DIM
    q_blk = COL_CQ // 512
    k_blk, v_blk = COL_CK // kvw, COL_CV // kvw
    prev_row = lambda b, qi: b * n_half + jnp.maximum(2 * qi - 1, 0)
    next_row = lambda b, qi: b * n_half + jnp.minimum(2 * qi + 2, n_half - 1)
    return pl.pallas_call(
        functools.partial(_window_sample_kernel, dec_seq=dec_seq),
        out_shape=jax.ShapeDtypeStruct((t, BRANCH_WIDTH), BF16),
        grid=(nb, nq),
        in_specs=[
            pl.BlockSpec((TQ, 512), lambda b, qi: (b * nq + qi, q_blk)),
            pl.BlockSpec((TQ, 512), lambda b, qi: (b * nq + qi, q_blk + 1)),
            pl.BlockSpec((half, kvw), lambda b, qi: (prev_row(b, qi), k_blk)),
            pl.BlockSpec((TQ, kvw), lambda b, qi: (b * nq + qi, k_blk)),
            pl.BlockSpec((half, kvw), lambda b, qi: (next_row(b, qi), k_blk)),
            pl.BlockSpec((half, kvw), lambda b, qi: (prev_row(b, qi), v_blk)),
            pl.BlockSpec((TQ, kvw), lambda b, qi: (b * nq + qi, v_blk)),
            pl.BlockSpec((half, kvw), lambda b, qi: (next_row(b, qi), v_blk)),
            pl.BlockSpec((None, None, None, past, kvw), lambda b, qi: (b, l, 0, 0, 0)),
            pl.BlockSpec((None, None, None, past, kvw), lambda b, qi: (b, l, 1, 0, 0)),
            pl.BlockSpec(memory_space=pltpu.SMEM),
        ],
        out_specs=pl.BlockSpec((TQ, BRANCH_WIDTH), lambda b, qi: (b * nq + qi, 0)),
        compiler_params=_cparams(("parallel", "parallel")),
        name="window_attention",
    )(qkv, qkv, qkv, qkv, qkv, qkv, qkv, qkv, cache, cache, sink_l)


def _merge_kernel(oa_ref, ob_ref, oc_ref, g_ref, w_ref, y_ref, acc_scr):
    r = pl.program_id(2)

    def contrib(o_ref):
        return g_ref[...].astype(F32) * _dot(o_ref[...], w_ref[...].astype(BF16))

    @pl.when(r == 0)
    def _():
        acc_scr[...] = contrib(oa_ref)

    @pl.when(r == 1)
    def _():
        acc_scr[...] += contrib(ob_ref)

    @pl.when(r == 2)
    def _():
        y_ref[...] = (acc_scr[...] + contrib(oc_ref)).astype(y_ref.dtype)


def _merge_branches(o_arrays, o_blocks, gates, w_branch, l):
    t = gates.shape[0]
    tn = 1024
    nn = D_MODEL // tn
    o_specs = [pl.BlockSpec((TM, BRANCH_WIDTH), functools.partial(lambda i, n, r, blk: (i, blk), blk=blk))
               for blk in o_blocks]
    return pl.pallas_call(
        _merge_kernel,
        out_shape=jax.ShapeDtypeStruct((t, D_MODEL), BF16),
        grid=(t // TM, nn, 3),
        in_specs=o_specs + [
            pl.BlockSpec((TM, tn), lambda i, n, r: (i, r * nn + n)),
            pl.BlockSpec((None, None, BRANCH_WIDTH, tn), lambda i, n, r: (l, r, 0, n)),
        ],
        out_specs=pl.BlockSpec((TM, tn), lambda i, n, r: (i, n)),
        scratch_shapes=[pltpu.VMEM((TM, tn), F32)],
        compiler_params=_cparams(("parallel", "parallel", "arbitrary")),
        name="merge_branches",
    )(*o_arrays, gates, w_branch)


def _layer_norm_rows(v, gain, bias):
    return _norm_rows(v) * gain + bias


def _outproj_kernel(y_ref, w_ref, x_ref, g_ref, gain_ref, bias_ref, o_ref, z_scr):
    n = pl.program_id(1)
    n_blocks, _, tn = z_scr.shape
    z_scr[n] = _dot(y_ref[...], w_ref[...].astype(BF16))

    @pl.when(n == n_blocks - 1)
    def _():
        for b in range(n_blocks):
            cols = slice(b * tn, (b + 1) * tn)
            o_ref[:, cols] = ALPHA * x_ref[:, cols] + g_ref[:, cols] * z_scr[b]
        o_ref[...] = _layer_norm_rows(o_ref[...], gain_ref[...], bias_ref[...])


def _out_projection(y, w_o, l, x, mod_l, ln_gain_l, ln_bias_l, *, row_of_tile):
    t = x.shape[0]
    tn = 512
    return pl.pallas_call(
        _outproj_kernel,
        out_shape=jax.ShapeDtypeStruct((t, D_MODEL), F32),
        grid=(t // TM_SMALL, D_MODEL // tn),
        in_specs=[
            pl.BlockSpec((TM_SMALL, D_MODEL), lambda i, n: (i, 0)),
            pl.BlockSpec((None, D_MODEL, tn), lambda i, n: (l, 0, n)),
            pl.BlockSpec((TM_SMALL, D_MODEL), lambda i, n: (i, 0)),
            _mod_spec(2, row_of_tile),
            pl.BlockSpec((1, D_MODEL), lambda i, n: (0, 0)),
            pl.BlockSpec((1, D_MODEL), lambda i, n: (0, 0)),
        ],
        out_specs=pl.BlockSpec((TM_SMALL, D_MODEL), lambda i, n: (i, 0)),
        scratch_shapes=[pltpu.VMEM((D_MODEL // tn, TM_SMALL, tn), F32)],
        compiler_params=_cparams(("parallel", "arbitrary")),
        name="out_projection",
    )(y, w_o, x, mod_l, ln_gain_l, ln_bias_l)


def _route(p):
    rows = [p[e:e + 1, :] for e in range(N_EXPERTS)]
    best_score, best_group = None, None
    for g in range(N_GROUPS):
        members = rows[g * EXPERTS_PER_GROUP:(g + 1) * EXPERTS_PER_GROUP]
        score = None
        for a in range(EXPERTS_PER_GROUP):
            for b in range(a + 1, EXPERTS_PER_GROUP):
                pair = members[a] + members[b]
                score = pair if score is None else jnp.maximum(score, pair)
        if g == 0:
            best_score, best_group = score, jnp.zeros(score.shape, F32)
        else:
            better = score > best_score
            best_group = jnp.where(better, float(g), best_group)
            best_score = jnp.where(better, score, best_score)
    e_idx = lax.broadcasted_iota(jnp.int32, p.shape, 0).astype(F32)
    g_idx = jnp.floor(e_idx * (1.0 / EXPERTS_PER_GROUP))
    masked = jnp.where(g_idx == best_group, p, -1.0)
    w1 = jnp.max(masked, axis=0, keepdims=True)
    i1 = jnp.min(jnp.where(masked == w1, e_idx, float(N_EXPERTS)), axis=0, keepdims=True)
    masked2 = jnp.where(e_idx == i1, -2.0, masked)
    w2 = jnp.max(masked2, axis=0, keepdims=True)
    i2 = jnp.min(jnp.where(masked2 == w2, e_idx, float(N_EXPERTS)), axis=0, keepdims=True)
    tot = w1 + w2
    return e_idx, i1, i2, w1 / tot, w2 / tot


ROUTE_ROWS = 8


def _moe_route_kernel(x_ref, sh_ref, sc_ref, wr_ref, h_ref, rec_ref, rec_t_ref, cnt_ref, carry_scr):
    @pl.when(pl.program_id(0) == 0)
    def _():
        carry_scr[...] = jnp.zeros(carry_scr.shape, F32)

    h = _norm_rows(x_ref[...]) * (1.0 + sc_ref[...]) + sh_ref[...]
    h_ref[...] = h
    logits = _dot_nt(wr_ref[...].astype(BF16), h.astype(BF16))
    m = jnp.max(logits, axis=0, keepdims=True)
    e = jnp.exp(logits - m)
    probs = e / jnp.sum(e, axis=0, keepdims=True)
    e_idx, i1, i2, w1, w2 = _route(probs)
    tm = probs.shape[1]
    oh1 = (e_idx == i1).astype(F32)
    oh2 = (e_idx == i2).astype(F32)
    oh = oh1 + oh2
    earlier = (lax.broadcasted_iota(jnp.int32, (tm, tm), 0) < lax.broadcasted_iota(jnp.int32, (tm, tm), 1))
    rank = carry_scr[:, 0:1] + _dot(oh.astype(BF16), earlier.astype(BF16))
    r1 = jnp.sum(oh1 * rank, axis=0, keepdims=True)
    r2 = jnp.sum(oh2 * rank, axis=0, keepdims=True)
    carry_scr[...] = carry_scr[...] + jnp.sum(oh, axis=1, keepdims=True)
    cnt_ref[...] = carry_scr[...]
    row = lax.broadcasted_iota(jnp.int32, (ROUTE_ROWS, tm), 0)
    rec = jnp.zeros((ROUTE_ROWS, tm), F32)
    for k, v in enumerate((i1, i2, r1, r2, w1, w2)):
        rec = jnp.where(row == k, v, rec)
    rec_ref[...] = rec
    pad = jnp.zeros((LANES - ROUTE_ROWS, tm), F32)
    rec_t_ref[...] = jnp.concatenate([rec, pad], axis=0).T


def _moe_route(x, mod_l, w_router_t, *, row_of_tile):
    t = x.shape[0]
    tm = TM_SMALL
    return pl.pallas_call(
        _moe_route_kernel,
        out_shape=(jax.ShapeDtypeStruct((t, D_MODEL), F32), jax.ShapeDtypeStruct((ROUTE_ROWS, t), F32),
                   jax.ShapeDtypeStruct((t, LANES), F32), jax.ShapeDtypeStruct((N_EXPERTS, LANES), F32)),
        grid=(t // tm,),
        in_specs=[
            pl.BlockSpec((tm, D_MODEL), lambda i: (i, 0)),
            _mod_spec(3, row_of_tile),
            _mod_spec(4, row_of_tile),
            pl.BlockSpec((N_EXPERTS, D_MODEL), lambda i: (0, 0)),
        ],
        out_specs=(pl.BlockSpec((tm, D_MODEL), lambda i: (i, 0)),
                   pl.BlockSpec((ROUTE_ROWS, tm), lambda i: (0, i)),
                   pl.BlockSpec((tm, LANES), lambda i: (i, 0)),
                   pl.BlockSpec((N_EXPERTS, LANES), lambda i: (0, 0))),
        scratch_shapes=[pltpu.VMEM((N_EXPERTS, LANES), F32)],
        compiler_params=_cparams(("arbitrary",)),
        name="moe_route",
    )(x, mod_l, mod_l, w_router_t)


ROW_DMA_UNROLL = 8


def _row_copy(src_hbm, src_row, dst_buf, dst_row, sem):
    return pltpu.make_async_copy(src_hbm.at[pl.ds(src_row, 1)], dst_buf.at[pl.ds(dst_row, 1)], sem)


def _moe_dispatch_kernel(n_tiles_ref, src_ref, h_hbm, xs_hbm, zeros_buf, sems, zero_sem, *, te):
    j = pl.program_id(0)
    n_valid = n_tiles_ref[0]

    @pl.when(j == 0)
    def _():
        zeros_buf[...] = jnp.zeros(zeros_buf.shape, F32)

    @pl.when(j >= n_valid)
    def _():
        fill = pltpu.make_async_copy(zeros_buf, xs_hbm.at[pl.ds(j * te, te)], zero_sem)
        fill.start()
        fill.wait()

    def copies(tile, s, act):
        def body(r, carry):
            row = tile * te + r
            getattr(_row_copy(h_hbm, src_ref[row], xs_hbm, row, sems.at[s]), act)()
            return carry
        lax.fori_loop(0, te, body, 0, unroll=ROW_DMA_UNROLL)

    @pl.when(j < n_valid)
    def _():
        copies(j, j % 2, "start")

    @pl.when((j >= 1) & (j - 1 < n_valid))
    def _():
        copies(j - 1, (j - 1) % 2, "wait")

    @pl.when((j == pl.num_programs(0) - 1) & (j < n_valid))
    def _():
        copies(j, j % 2, "wait")


def _moe_dispatch(h, n_tiles, src, *, te):
    n_rows = src.shape[0]
    return pl.pallas_call(
        functools.partial(_moe_dispatch_kernel, te=te),
        out_shape=jax.ShapeDtypeStruct((n_rows, D_MODEL), F32),
        grid_spec=pltpu.PrefetchScalarGridSpec(
            num_scalar_prefetch=2,
            grid=(n_rows // te,),
            in_specs=[pl.BlockSpec(memory_space=pltpu.HBM)],
            out_specs=pl.BlockSpec(memory_space=pltpu.HBM),
            scratch_shapes=[pltpu.VMEM((te, D_MODEL), F32), pltpu.SemaphoreType.DMA((2,)),
                            pltpu.SemaphoreType.DMA],
        ),
        compiler_params=_cparams(("arbitrary",)),
        name="moe_dispatch",
    )(n_tiles, src, h)


def _moe_expert_kernel(tile_expert_ref, n_tiles_ref, xs_ref, wg_ref, wu_ref, wd_ref, ys_ref):
    del tile_expert_ref
    j = pl.program_id(0)

    @pl.when(j < n_tiles_ref[0])
    def _():
        x = xs_ref[...].astype(BF16)
        gate = _dot(x, wg_ref[...].astype(BF16))
        up = _dot(x, wu_ref[...].astype(BF16))
        hid = gate / (1.0 + jnp.exp(-gate)) * up
        ys_ref[...] = _dot(hid.astype(BF16), wd_ref[...].astype(BF16))

    @pl.when(j >= n_tiles_ref[0])
    def _():
        ys_ref[...] = jnp.zeros(ys_ref.shape, F32)


def _moe_experts(xs, tile_expert, n_tiles, wg, wu, wd, l, *, te):
    n_rows = xs.shape[0]
    w_in_spec = pl.BlockSpec((None, None, D_MODEL, D_EXPERT), lambda j, te_ref, nt_ref: (l, te_ref[j], 0, 0))
    w_out_spec = pl.BlockSpec((None, None, D_EXPERT, D_MODEL), lambda j, te_ref, nt_ref: (l, te_ref[j], 0, 0))
    xs_spec = pl.BlockSpec((te, D_MODEL), lambda j, te_ref, nt_ref: (j, 0))
    return pl.pallas_call(
        _moe_expert_kernel,
        out_shape=jax.ShapeDtypeStruct((n_rows, D_MODEL), F32),
        grid_spec=pltpu.PrefetchScalarGridSpec(
            num_scalar_prefetch=2,
            grid=(n_rows // te,),
            in_specs=[xs_spec, w_in_spec, w_in_spec, w_out_spec],
            out_specs=pl.BlockSpec((te, D_MODEL), lambda j, *_: (j, 0)),
        ),
        compiler_params=_cparams(("arbitrary",)),
        name="moe_experts",
    )(tile_expert, n_tiles, xs, wg, wu, wd)


TM_COMBINE = 256


def _moe_combine_kernel(dest_ref, ys_hbm, x_ref, rec_t_ref, g_ref, gain_ref, bias_ref, o_ref, y_buf, sems,
                        *, n_tokens):
    i = pl.program_id(0)
    tm = x_ref.shape[0]
    slot = i % 2

    def start_gather(tile, s):
        def body(r, carry):
            for k in range(2):
                row = dest_ref[k * n_tokens + tile * tm + r]
                _row_copy(ys_hbm, row, y_buf.at[s, k], r, sems.at[s]).start()
            return carry
        lax.fori_loop(0, tm, body, 0, unroll=ROW_DMA_UNROLL)

    def wait_gather(s):
        def body(r, carry):
            for k in range(2):
                _row_copy(ys_hbm, 0, y_buf.at[s, k], r, sems.at[s]).wait()
            return carry
        lax.fori_loop(0, tm, body, 0, unroll=ROW_DMA_UNROLL)

    @pl.when(i == 0)
    def _():
        start_gather(0, 0)

    @pl.when(i + 1 < pl.num_programs(0))
    def _():
        start_gather(i + 1, 1 - slot)

    wait_gather(slot)
    z = rec_t_ref[:, 4:5] * y_buf[slot, 0] + rec_t_ref[:, 5:6] * y_buf[slot, 1]
    v = ALPHA * x_ref[...] + g_ref[...] * z
    o_ref[...] = _layer_norm_rows(v, gain_ref[...], bias_ref[...])


def _moe_combine(dest, ys, x, rec_t, mod_l, ln_gain_l, ln_bias_l, *, row_of_tile):
    t = x.shape[0]
    tm = TM_COMBINE
    vec = pl.BlockSpec((1, D_MODEL), lambda i, *_: (0, 0))
    return pl.pallas_call(
        functools.partial(_moe_combine_kernel, n_tokens=t),
        out_shape=jax.ShapeDtypeStruct((t, D_MODEL), F32),
        grid_spec=pltpu.PrefetchScalarGridSpec(
            num_scalar_prefetch=1,
            grid=(t // tm,),
            in_specs=[
                pl.BlockSpec(memory_space=pltpu.HBM),
                pl.BlockSpec((tm, D_MODEL), lambda i, *_: (i, 0)),
                pl.BlockSpec((tm, LANES), lambda i, *_: (i, 0)),
                _mod_spec(5, row_of_tile),
                vec, vec,
            ],
            out_specs=pl.BlockSpec((tm, D_MODEL), lambda i, *_: (i, 0)),
            scratch_shapes=[pltpu.VMEM((2, 2, tm, D_MODEL), F32), pltpu.SemaphoreType.DMA((2,))],
        ),
        compiler_params=_cparams(("arbitrary",)),
        name="moe_combine",
    )(dest, ys, x, rec_t, mod_l, ln_gain_l, ln_bias_l)


def _dispatch_tables(rec, cnt, *, te):
    t = rec.shape[1]
    n_rows = 2 * t + N_EXPERTS * te
    e12 = rec[0:2].astype(jnp.int32)
    r12 = rec[2:4].astype(jnp.int32)
    counts = cnt[:, 0].astype(jnp.int32)
    padded = (counts + te - 1) // te * te
    ends = jnp.cumsum(padded)
    offsets = ends - padded
    expert_ids = jnp.arange(N_EXPERTS, dtype=jnp.int32)[:, None, None]
    dest = (jnp.sum(jnp.where(e12[None] == expert_ids, offsets[:, None, None], 0), axis=0) + r12).reshape(2 * t)
    tokens = jnp.tile(jnp.arange(t, dtype=jnp.int32), 2)
    src = jnp.zeros((n_rows,), jnp.int32).at[dest].set(tokens, unique_indices=True)
    tile_start = jnp.arange(n_rows // te, dtype=jnp.int32) * te
    tile_expert = jnp.minimum(jnp.searchsorted(ends, tile_start, side="right"), N_EXPERTS - 1).astype(jnp.int32)
    n_tiles = (ends[-1:] // te).astype(jnp.int32)
    return dest, src, tile_expert, n_tiles


def _rope_tables(n_tokens):
    rows = n_tokens // GRID_W
    row = jnp.repeat(jnp.arange(rows), GRID_W).astype(F32)
    col = jnp.tile(jnp.arange(GRID_W), rows).astype(F32)
    quarter = HEAD_DIM // 4
    inv_freq = ROPE_THETA ** (-jnp.arange(quarter, dtype=F32) / quarter)
    ang_r, ang_c = row[:, None] * inv_freq, col[:, None] * inv_freq
    cos = jnp.concatenate([jnp.cos(ang_r), jnp.cos(ang_r), jnp.cos(ang_c), jnp.cos(ang_c)], axis=-1)
    sin = jnp.concatenate([-jnp.sin(ang_r), jnp.sin(ang_r), -jnp.sin(ang_c), jnp.sin(ang_c)], axis=-1)
    return cos, sin


def _row_of_tile(first_row, tokens_per_row):
    def for_tile(tile):
        return lambda i: first_row + (i * tile) // tokens_per_row
    return for_tile


def _mixer_and_ffn(x, attn_arrays, attn_blocks, gates, mod_l, rows, l, expert_tile, w_branch, w_o, ln_gain,
                   ln_bias, w_router_t, w_e_gate, w_e_up, w_e_down):
    y = _merge_branches(attn_arrays, attn_blocks, gates, w_branch, l)
    x = _out_projection(y, w_o, l, x, mod_l, ln_gain[l, 0:1], ln_bias[l, 0:1], row_of_tile=rows(TM_SMALL))
    h2, rec, rec_t, cnt = _moe_route(x, mod_l, w_router_t, row_of_tile=rows(TM_SMALL))
    dest, src, tile_expert, n_tiles = _dispatch_tables(rec, cnt, te=expert_tile)
    xs = _moe_dispatch(h2, n_tiles, src, te=expert_tile)
    ys = _moe_experts(xs, tile_expert, n_tiles, w_e_gate, w_e_up, w_e_down, l, te=expert_tile)
    return _moe_combine(dest, ys, x, rec_t, mod_l, ln_gain[l, 1:2], ln_bias[l, 1:2],
                        row_of_tile=rows(TM_COMBINE))


def kernel(x_prompt, x_sample, cache_kv_a, cache_kv_b, cache_kv_c, c, c_ctx, w_in, w_gate, w_branch, w_o,
           w_mod, b_mod, ln_gain, ln_bias, diff_lam, diff_subln, qk_gain, sink, w_router, w_e_gate, w_e_up,
           w_e_down):
    batch, seq, _ = x_prompt.shape
    dec_batch, dec_seq, _ = x_sample.shape
    past = cache_kv_a.shape[3]
    t_p, t_s = batch * seq, dec_batch * dec_seq

    cond = jnp.concatenate([c_ctx[None], c, jnp.zeros((MOD_ROWS - 1 - dec_batch, D_MODEL), F32)], axis=0)
    mod = _modulation(cond.T, w_mod, b_mod)
    cos_t, sin_t = _rope_tables(dec_seq)
    w_router_t = w_router.T
    cache_a = cache_kv_a.reshape(dec_batch, DEPTH, 2, past, 4 * 256)
    cache_b = cache_kv_b.reshape(dec_batch, DEPTH, 2, past, 2 * HEAD_DIM)
    cache_c = cache_kv_c.reshape(dec_batch, DEPTH, 2, past, 2 * HEAD_DIM)
    rows_p = _row_of_tile(0, t_p)
    rows_s = _row_of_tile(1, dec_seq)

    y_p = x_prompt.reshape(t_p, D_MODEL)
    y_s = x_sample.reshape(t_s, D_MODEL)
    new_a, new_b, new_c = [], [], []
    for l in range(DEPTH):
        lam_init = 0.8 - 0.6 * math.exp(-0.3 * l)
        mod_l = mod[l].reshape(MOD_ROWS, 1, N_MOD * D_MODEL)
        subln_l = diff_subln[l].reshape(1, 256)
        shared = (w_branch, w_o, ln_gain, ln_bias, w_router_t, w_e_gate, w_e_up, w_e_down)

        p = _in_projection(y_p, mod_l, w_in, l, qk_gain[l], cos_t, sin_t, row_of_tile=rows_p(TM),
                           rope=False, q_scale=1.0, out_dtype=F32)
        gates = _branch_gates(y_p, mod_l, w_gate, l, row_of_tile=rows_p(TM))
        attn = _attention_prompt(p, diff_lam[l], subln_l, sink[l], lam_init=lam_init, seq=seq)
        y_p = _mixer_and_ffn(y_p, (attn, attn, attn), (0, 1, 2), gates, mod_l, rows_p, l, 256, *shared)
        p5 = p.reshape(batch, seq, IN_COLS)
        new_a.append(jnp.stack([p5[..., COL_AK:COL_AV], p5[..., COL_AV:COL_BQ]], axis=1))
        new_b.append(jnp.stack([p5[..., COL_BK:COL_BV], p5[..., COL_BV:COL_CQ]], axis=1))
        new_c.append(jnp.stack([p5[..., COL_CK:COL_CV], p5[..., COL_CV:]], axis=1))

        qkv = _in_projection(y_s, mod_l, w_in, l, qk_gain[l], cos_t, sin_t, row_of_tile=rows_s(TM),
                             rope=True, q_scale=SCALE * LOG2E, out_dtype=BF16)
        gates = _branch_gates(y_s, mod_l, w_gate, l, row_of_tile=rows_s(TM))
        vec = lambda shape: pl.BlockSpec(shape, lambda b, qi, kk: (0, 0))
        a_o = _flash_sample(functools.partial(_diff_sample_kernel, lam_init=lam_init), qkv, cache_a, l,
                            q_col=COL_AQ, k_col=COL_AK, v_col=COL_AV, kv_width=1024, n_state=8,
                            row_sum_scratch=True, extra=(diff_lam[l], subln_l),
                            extra_specs=[vec((4, HEAD_DIM)), vec((1, 256))], name="diff_attention",
                            dec_seq=dec_seq)
        b_o = _flash_sample(_gqa_sample_kernel, qkv, cache_b, l, q_col=COL_BQ, k_col=COL_BK, v_col=COL_BV,
                            kv_width=256, n_state=8, row_sum_scratch=False, extra=(), extra_specs=[],
                            name="gqa_attention", dec_seq=dec_seq)
        c_o = _window_sample(qkv, cache_c, sink[l], l, dec_seq=dec_seq)
        y_s = _mixer_and_ffn(y_s, (a_o, b_o, c_o), (0, 0, 0), gates, mod_l, rows_s, l, 512, *shared)

    new_kv_a = jnp.stack(new_a, axis=1).reshape(batch, DEPTH, 2, seq, 4, 256)
    new_kv_b = jnp.stack(new_b, axis=1).reshape(batch, DEPTH, 2, seq, 2, HEAD_DIM)
    new_kv_c = jnp.stack(new_c, axis=1).reshape(batch, DEPTH, 2, seq, 2, HEAD_DIM)
    return (y_p.reshape(batch, seq, D_MODEL), y_s.reshape(dec_batch, dec_seq, D_MODEL),
            new_kv_a, new_kv_b, new_kv_c)
```

```python
import functools
import math

import jax
import jax.numpy as jnp
from jax import lax
from jax.experimental import pallas as pl
from jax.experimental.pallas import tpu as pltpu

F32 = jnp.float32
BF16 = jnp.bfloat16

D_MODEL = 2048
HEAD_DIM = 128
GRID_W = 64
ROPE_THETA = 10000.0
WINDOW = 128
N_EXPERTS = 16
N_GROUPS = 4
EXPERTS_PER_GROUP = N_EXPERTS // N_GROUPS
D_EXPERT = 512
N_MOD = 6
DEPTH = 2
ALPHA = (2 * DEPTH) ** 0.25
EPS = 1e-6
IN_COLS = 6144
BRANCH_WIDTH = 1024
SCALE = HEAD_DIM ** -0.5
LOG2E = math.log2(math.e)
NEG_BIG = -1e30

COL_AQ, COL_AK, COL_AV = 0, 1024, 2048
COL_BQ, COL_BK, COL_BV = 3072, 4096, 4352
COL_CQ, COL_CK, COL_CV = 4608, 5632, 5888

LANES = 128
VMEM_LIMIT = 56 * 1024 * 1024

TM = 1024
TN_PROJ = 1024
TN_PROJ_F32 = 512
TN_GATE = 1024
TM_SMALL = 512


def _cparams(sem):
    return pltpu.CompilerParams(dimension_semantics=sem, vmem_limit_bytes=VMEM_LIMIT)


def _dot(a, b):
    return jnp.dot(a, b, preferred_element_type=F32)


def _dot_nt(a, b):
    return lax.dot_general(a, b, (((1,), (1,)), ((), ())), preferred_element_type=F32)


def _norm_rows(x):
    mu = jnp.mean(x, axis=-1, keepdims=True)
    xc = x - mu
    var = jnp.mean(xc * xc, axis=-1, keepdims=True)
    return xc * lax.rsqrt(var + EPS)


def _rms(v, gain):
    ms = jnp.mean(v * v, axis=-1, keepdims=True)
    return v * lax.rsqrt(ms + EPS) * gain


def _rope(v, cos, sin_signed):
    lane = lax.broadcasted_iota(jnp.int32, v.shape, 1)
    first_half = (lane % 64) < 32
    partner = jnp.where(first_half, pltpu.roll(v, 96, 1), pltpu.roll(v, 32, 1))
    return v * cos + partner * sin_signed


N_COND = 3
MOD_ROWS = 8
TN_MOD = 1024


def _mod_kernel(cond_ref, w_ref, b_ref, o_ref):
    w = w_ref[...]
    row_idx = lax.broadcasted_iota(jnp.int32, (MOD_ROWS, TN_MOD), 0)
    out = jnp.zeros((MOD_ROWS, TN_MOD), F32)
    for r in range(N_COND):
        c = cond_ref[:, r:r + 1]
        s = c / (1.0 + jnp.exp(-c))
        m = jnp.sum(w * s, axis=0, keepdims=True) + b_ref[...]
        out = jnp.where(row_idx == r, m, out)
    o_ref[...] = out


def _modulation(cond_t, w_mod, b_mod):
    n = N_MOD * D_MODEL
    return pl.pallas_call(
        _mod_kernel,
        out_shape=jax.ShapeDtypeStruct((DEPTH, MOD_ROWS, n), F32),
        grid=(DEPTH, n // TN_MOD),
        in_specs=[
            pl.BlockSpec((D_MODEL, MOD_ROWS), lambda l, j: (0, 0)),
            pl.BlockSpec((None, D_MODEL, TN_MOD), lambda l, j: (l, 0, j)),
            pl.BlockSpec((None, 1, TN_MOD), lambda l, j: (l, 0, j)),
        ],
        out_specs=pl.BlockSpec((None, MOD_ROWS, TN_MOD), lambda l, j: (l, 0, j)),
        compiler_params=_cparams(("parallel", "parallel")),
        name="modulation",
    )(cond_t, w_mod, b_mod.reshape(DEPTH, 1, n))


def _modulate_to_scratch(x_ref, sh_ref, sc_ref, h_scr):
    h = _norm_rows(x_ref[...]) * (1.0 + sc_ref[...]) + sh_ref[...]
    h_scr[...] = h.astype(BF16)


def _head_chunk_kinds():
    kinds = []
    for n_chunks, kind in ((8, (None, True, True)), (8, (None, True, False)), (8, (None, False, False)),
                           (8, ("q", True, True)), (2, ("k", True, False)), (2, (None, False, False)),
                           (8, (None, True, True)), (2, (None, True, False)), (2, (None, False, False))):
        kinds.extend([kind] * n_chunks)
    return kinds


_HEAD_CHUNK_KINDS = _head_chunk_kinds()


def _inproj_kernel(x_ref, sh_ref, sc_ref, w_ref, qk_gain_ref, cos_ref, sin_ref, o_ref, h_scr, *, rope,
                   q_scale):
    j = pl.program_id(1)
    gain_q = qk_gain_ref[0:1, :]
    gain_k = qk_gain_ref[1:2, :]

    @pl.when(j == 0)
    def _():
        _modulate_to_scratch(x_ref, sh_ref, sc_ref, h_scr)

    acc = _dot(h_scr[...], w_ref[...].astype(BF16))
    tn = o_ref.shape[1]
    n_chunks = tn // LANES

    def chunk(c):
        return acc[:, c * LANES:(c + 1) * LANES]

    def store(c, v):
        o_ref[:, c * LANES:(c + 1) * LANES] = v.astype(o_ref.dtype)

    def rp(v):
        return _rope(v, cos_ref[...], sin_ref[...]) if rope else v

    def qs(v):
        return v * q_scale if q_scale != 1.0 else v

    def finish(c, kind):
        norm, rotate, is_query = kind
        v = chunk(c)
        if norm is not None:
            v = _rms(v, gain_q if norm == "q" else gain_k)
        if rotate:
            v = rp(v)
        store(c, qs(v) if is_query else v)

    tile_kinds = [tuple(_HEAD_CHUNK_KINDS[t * n_chunks:(t + 1) * n_chunks]) for t in range(IN_COLS // tn)]
    for kinds in dict.fromkeys(tile_kinds):
        tiles = [t for t, k in enumerate(tile_kinds) if k == kinds]
        cond = j == tiles[0]
        for t in tiles[1:]:
            cond = cond | (j == t)

        @pl.when(cond)
        def _(kinds=kinds):
            for c, kind in enumerate(kinds):
                finish(c, kind)


def _gate_kernel(x_ref, sh_ref, sc_ref, w_ref, o_ref, h_scr):
    @pl.when(pl.program_id(1) == 0)
    def _():
        _modulate_to_scratch(x_ref, sh_ref, sc_ref, h_scr)

    acc = _dot(h_scr[...], w_ref[...].astype(BF16))
    o_ref[...] = (1.0 / (1.0 + jnp.exp(-acc))).astype(o_ref.dtype)


def _mod_spec(which, row_of_tile):
    return pl.BlockSpec((None, 1, D_MODEL), lambda i, *_: (row_of_tile(i), 0, which))


def _in_projection(x, mod_l, w_in, l, qk_gain_l, cos_t, sin_t, *, row_of_tile, rope, q_scale, out_dtype):
    t = x.shape[0]
    tiles_per_seq = cos_t.shape[0] // TM
    tn = TN_PROJ_F32 if out_dtype == F32 else TN_PROJ
    return pl.pallas_call(
        functools.partial(_inproj_kernel, rope=rope, q_scale=q_scale),
        out_shape=jax.ShapeDtypeStruct((t, IN_COLS), out_dtype),
        grid=(t // TM, IN_COLS // tn),
        in_specs=[
            pl.BlockSpec((TM, D_MODEL), lambda i, j: (i, 0)),
            _mod_spec(0, row_of_tile),
            _mod_spec(1, row_of_tile),
            pl.BlockSpec((None, D_MODEL, tn), lambda i, j: (l, 0, j)),
            pl.BlockSpec((2, HEAD_DIM), lambda i, j: (0, 0)),
            pl.BlockSpec((TM, HEAD_DIM), lambda i, j: (i % tiles_per_seq, 0)),
            pl.BlockSpec((TM, HEAD_DIM), lambda i, j: (i % tiles_per_seq, 0)),
        ],
        out_specs=pl.BlockSpec((TM, tn), lambda i, j: (i, j)),
        scratch_shapes=[pltpu.VMEM((TM, D_MODEL), BF16)],
        compiler_params=_cparams(("parallel", "arbitrary")),
        name="in_projection",
    )(x, mod_l, mod_l, w_in, qk_gain_l, cos_t, sin_t)


def _branch_gates(x, mod_l, w_gate, l, *, row_of_tile):
    t = x.shape[0]
    n = w_gate.shape[2]
    return pl.pallas_call(
        _gate_kernel,
        out_shape=jax.ShapeDtypeStruct((t, n), BF16),
        grid=(t // TM, n // TN_GATE),
        in_specs=[
            pl.BlockSpec((TM, D_MODEL), lambda i, j: (i, 0)),
            _mod_spec(0, row_of_tile),
            _mod_spec(1, row_of_tile),
            pl.BlockSpec((None, D_MODEL, TN_GATE), lambda i, j: (l, 0, j)),
        ],
        out_specs=pl.BlockSpec((TM, TN_GATE), lambda i, j: (i, j)),
        scratch_shapes=[pltpu.VMEM((TM, D_MODEL), BF16)],
        compiler_params=_cparams(("parallel", "arbitrary")),
        name="branch_gates",
    )(x, mod_l, mod_l, w_gate)


def _diff_lambda(lam_ref, lam_init):
    lp = lam_ref[...]
    t1 = jnp.sum(lp[0:1] * lp[1:2], axis=-1, keepdims=True)
    t2 = jnp.sum(lp[2:3] * lp[3:4], axis=-1, keepdims=True)
    return jnp.exp(t1) - jnp.exp(t2) + lam_init


def _softmax_rows(s, sink=None):
    m = jnp.max(s, axis=-1, keepdims=True)
    if sink is not None:
        m = jnp.maximum(m, sink)
    e = jnp.exp(s - m)
    den = jnp.sum(e, axis=-1, keepdims=True)
    if sink is not None:
        den = den + jnp.exp(sink - m)
    return e * (1.0 / den)


def _attn_prompt_kernel(p_ref, lam_ref, subln_ref, sink_ref, o_ref, *, lam_init):
    lam = _diff_lambda(lam_ref, lam_init)

    def blk(c0, w):
        return p_ref[:, c0:c0 + w].astype(BF16)

    for h in range(4):
        probs = []
        for m in range(2):
            q = blk(COL_AQ + h * 256 + m * HEAD_DIM, HEAD_DIM)
            k = blk(COL_AK + h * 256 + m * HEAD_DIM, HEAD_DIM)
            probs.append(_softmax_rows(_dot_nt(q, k) * SCALE))
        w = probs[0] - lam * probs[1]
        o = _dot(w.astype(BF16), blk(COL_AV + h * 256, 256))
        o = _rms(o, subln_ref[...]) * (1.0 - lam_init)
        o_ref[:, h * 256:(h + 1) * 256] = o.astype(o_ref.dtype)

    for mixer, (cq, ck, cv) in enumerate(((COL_BQ, COL_BK, COL_BV), (COL_CQ, COL_CK, COL_CV))):
        for kvh in range(2):
            k = blk(ck + kvh * HEAD_DIM, HEAD_DIM)
            v = blk(cv + kvh * HEAD_DIM, HEAD_DIM)
            for g in range(4):
                hq = kvh * 4 + g
                q = blk(cq + hq * HEAD_DIM, HEAD_DIM)
                sink = sink_ref[hq] if mixer == 1 else None
                p = _softmax_rows(_dot_nt(q, k) * SCALE, sink)
                o = _dot(p.astype(BF16), v)
                c0 = BRANCH_WIDTH * (1 + mixer) + hq * HEAD_DIM
                o_ref[:, c0:c0 + HEAD_DIM] = o.astype(o_ref.dtype)


def _attention_prompt(p, diff_lam_l, subln_l, sink_l, *, lam_init, seq):
    t = p.shape[0]
    return pl.pallas_call(
        functools.partial(_attn_prompt_kernel, lam_init=lam_init),
        out_shape=jax.ShapeDtypeStruct((t, 3 * BRANCH_WIDTH), BF16),
        grid=(t // seq,),
        in_specs=[
            pl.BlockSpec((seq, IN_COLS), lambda b: (b, 0)),
            pl.BlockSpec((4, HEAD_DIM), lambda b: (0, 0)),
            pl.BlockSpec((1, 256), lambda b: (0, 0)),
            pl.BlockSpec(memory_space=pltpu.SMEM),
        ],
        out_specs=pl.BlockSpec((seq, 3 * BRANCH_WIDTH), lambda b: (b, 0)),
        compiler_params=_cparams(("parallel",)),
        name="attention_prompt",
    )(p, diff_lam_l, subln_l, sink_l)


TQ = 512
TK = 1024


def _tile_lanes(v, n):
    return jnp.concatenate([v] * n, axis=-1) if n > 1 else v


def _online_softmax(idx, s, m_scr):
    m_prev = m_scr[idx]
    m_new = jnp.maximum(m_prev, jnp.max(s, axis=-1, keepdims=True))
    m_scr[idx] = m_new
    alpha = jnp.exp2(m_prev - m_new)
    p = jnp.exp2(s - _tile_lanes(m_new, s.shape[1] // LANES))
    return p, alpha


def _diff_sample_kernel(q_ref, k_ref, v_ref, kc_ref, vc_ref, lam_ref, subln_ref, o_ref,
                        m_scr, l_scr, acc_scr, *, lam_init):
    kk = pl.program_id(2)

    @pl.when(kk == 0)
    def _():
        m_scr[...] = jnp.full(m_scr.shape, NEG_BIG, F32)
        l_scr[...] = jnp.zeros(l_scr.shape, F32)
        acc_scr[...] = jnp.zeros(acc_scr.shape, F32)

    def process(kb_ref, vb_ref):
        for h in range(4):
            v = vb_ref[:, h * 256:(h + 1) * 256].astype(BF16)
            for m in range(2):
                c0 = h * 256 + m * HEAD_DIM
                idx = h * 2 + m
                k = kb_ref[:, c0:c0 + HEAD_DIM].astype(BF16)
                p, alpha = _online_softmax(idx, _dot_nt(q_ref[:, c0:c0 + HEAD_DIM], k), m_scr)
                part = p[:, 0:LANES]
                for c in range(1, p.shape[1] // LANES):
                    part = part + p[:, c * LANES:(c + 1) * LANES]
                l_scr[idx] = alpha * l_scr[idx] + part
                acc_scr[idx] = _tile_lanes(alpha, 2) * acc_scr[idx] + _dot(p.astype(BF16), v)

    @pl.when(kk == 0)
    def _():
        process(kc_ref, vc_ref)

    @pl.when(kk > 0)
    def _():
        process(k_ref, v_ref)

    @pl.when(kk == pl.num_programs(2) - 1)
    def _():
        lam = _diff_lambda(lam_ref, lam_init)
        for h in range(4):
            l1 = jnp.sum(l_scr[2 * h], axis=-1, keepdims=True)
            l2 = jnp.sum(l_scr[2 * h + 1], axis=-1, keepdims=True)
            o1 = acc_scr[2 * h] * (1.0 / l1)
            o2 = acc_scr[2 * h + 1] * (1.0 / l2)
            o = _rms(o1 - lam * o2, subln_ref[...]) * (1.0 - lam_init)
            o_ref[:, h * 256:(h + 1) * 256] = o.astype(o_ref.dtype)


def _gqa_sample_kernel(q_ref, k_ref, v_ref, kc_ref, vc_ref, o_ref, m_scr, acc_scr):
    kk = pl.program_id(2)

    @pl.when(kk == 0)
    def _():
        m_scr[...] = jnp.full(m_scr.shape, NEG_BIG, F32)
        acc_scr[...] = jnp.zeros(acc_scr.shape, F32)

    def process(kb_ref, vb_ref):
        for kvh in range(2):
            k = kb_ref[:, kvh * HEAD_DIM:(kvh + 1) * HEAD_DIM].astype(BF16)
            v = vb_ref[:, kvh * HEAD_DIM:(kvh + 1) * HEAD_DIM].astype(BF16)
            v_ones = jnp.concatenate([v, jnp.ones_like(v)], axis=-1)
            for g in range(4):
                hq = kvh * 4 + g
                s = _dot_nt(q_ref[:, hq * HEAD_DIM:(hq + 1) * HEAD_DIM], k)
                p, alpha = _online_softmax(hq, s, m_scr)
                acc_scr[hq] = _tile_lanes(alpha, 2) * acc_scr[hq] + _dot(p.astype(BF16), v_ones)

    @pl.when(kk == 0)
    def _():
        process(kc_ref, vc_ref)

    @pl.when(kk > 0)
    def _():
        process(k_ref, v_ref)

    @pl.when(kk == pl.num_programs(2) - 1)
    def _():
        for hq in range(8):
            o = acc_scr[hq, :, 0:HEAD_DIM] / acc_scr[hq, :, HEAD_DIM:2 * HEAD_DIM]
            o_ref[:, hq * HEAD_DIM:(hq + 1) * HEAD_DIM] = o.astype(o_ref.dtype)


def _flash_sample(kernel, qkv, cache, l, *, q_col, k_col, v_col, kv_width, n_state, row_sum_scratch, extra,
                  extra_specs, name, dec_seq):
    t = qkv.shape[0]
    nb = t // dec_seq
    nq = dec_seq // TQ
    nk = dec_seq // TK
    past = cache.shape[3]
    q_blk, k_blk, v_blk = q_col // BRANCH_WIDTH, k_col // kv_width, v_col // kv_width
    kv_row = lambda b, qi, kk: b * nk + jnp.maximum(kk - 1, 0)
    return pl.pallas_call(
        kernel,
        out_shape=jax.ShapeDtypeStruct((t, BRANCH_WIDTH), BF16),
        grid=(nb, nq, nk + 1),
        in_specs=[
            pl.BlockSpec((TQ, BRANCH_WIDTH), lambda b, qi, kk: (b * nq + qi, q_blk)),
            pl.BlockSpec((TK, kv_width), lambda b, qi, kk: (kv_row(b, qi, kk), k_blk)),
            pl.BlockSpec((TK, kv_width), lambda b, qi, kk: (kv_row(b, qi, kk), v_blk)),
            pl.BlockSpec((None, None, None, past, kv_width), lambda b, qi, kk: (b, l, 0, 0, 0)),
            pl.BlockSpec((None, None, None, past, kv_width), lambda b, qi, kk: (b, l, 1, 0, 0)),
        ] + extra_specs,
        out_specs=pl.BlockSpec((TQ, BRANCH_WIDTH), lambda b, qi, kk: (b * nq + qi, 0)),
        scratch_shapes=[pltpu.VMEM((n_state, TQ, LANES), F32)] * (2 if row_sum_scratch else 1)
        + [pltpu.VMEM((n_state, TQ, 2 * LANES), F32)],
        compiler_params=_cparams(("parallel", "parallel", "arbitrary")),
        name=name,
    )(qkv, qkv, qkv, cache, cache, *extra)


def _window_sample_kernel(q0_ref, q1_ref, kp_ref, kc_ref, kn_ref, vp_ref, vc_ref, vn_ref, kctx_ref, vctx_ref,
                          sink_ref, o_ref, *, dec_seq):
    qi = pl.program_id(1)
    q_start = qi * TQ
    half = TQ // 2
    q_pos = q_start + lax.broadcasted_iota(jnp.int32, (TQ, 1), 0)
    segs = ((kp_ref, vp_ref, q_start - half, half), (kc_ref, vc_ref, q_start, TQ),
            (kn_ref, vn_ref, q_start + TQ, half))
    valid = []
    for _, _, start, n in segs:
        k_pos = start + lax.broadcasted_iota(jnp.int32, (TQ, n), 1)
        valid.append((jnp.abs(q_pos - k_pos) <= WINDOW) & (k_pos >= 0) & (k_pos < dec_seq))
    for kvh in range(2):
        q_ref = q0_ref if kvh == 0 else q1_ref
        lo, hi = kvh * HEAD_DIM, (kvh + 1) * HEAD_DIM
        kctx = kctx_ref[:, lo:hi].astype(BF16)
        vctx = vctx_ref[:, lo:hi].astype(BF16)
        for g in range(4):
            hq = kvh * 4 + g
            q = q_ref[:, g * HEAD_DIM:(g + 1) * HEAD_DIM]
            sink = sink_ref[hq] * LOG2E
            scores = [_dot_nt(q, kctx)]
            for (k_ref, _, _, _), ok in zip(segs, valid):
                scores.append(jnp.where(ok, _dot_nt(q, k_ref[:, lo:hi]), NEG_BIG))
            m = jnp.maximum(scores[0].max(axis=-1, keepdims=True), sink)
            for s in scores[1:]:
                m = jnp.maximum(m, s.max(axis=-1, keepdims=True))
            es = [jnp.exp2(s - m) for s in scores]
            den = jnp.exp2(sink - m)
            for e in es:
                den = den + jnp.sum(e, axis=-1, keepdims=True)
            o = _dot(es[0].astype(BF16), vctx)
            for e, (_, v_ref, _, _) in zip(es[1:], segs):
                o = o + _dot(e.astype(BF16), v_ref[:, lo:hi])
            o = o * (1.0 / den)
            o_ref[:, hq * HEAD_DIM:(hq + 1) * HEAD_DIM] = o.astype(o_ref.dtype)


def _window_sample(qkv, cache, sink_l, l, *, dec_seq):
    t = qkv.shape[0]
    nb = t // dec_seq
    nq = dec_seq // TQ
    half = TQ // 2
    n_half = dec_seq // half
    past = cache.shape[3]
    kvw = 2 * HEAD_DIM
    q_blk = COL_CQ // 512
    k_blk, v_blk = COL_CK // kvw, COL_CV // kvw
    prev_row = lambda b, qi: b * n_half + jnp.maximum(2 * qi - 1, 0)
    next_row = lambda b, qi: b * n_half + jnp.minimum(2 * qi + 2, n_half - 1)
    return pl.pallas_call(
        functools.partial(_window_sample_kernel, dec_seq=dec_seq),
        out_shape=jax.ShapeDtypeStruct((t, BRANCH_WIDTH), BF16),
        grid=(nb, nq),
        in_specs=[
            pl.BlockSpec((TQ, 512), lambda b, qi: (b * nq + qi, q_blk)),
            pl.BlockSpec((TQ, 512), lambda b, qi: (b * nq + qi, q_blk + 1)),
            pl.BlockSpec((half, kvw), lambda b, qi: (prev_row(b, qi), k_blk)),
            pl.BlockSpec((TQ, kvw), lambda b, qi: (b * nq + qi, k_blk)),
            pl.BlockSpec((half, kvw), lambda b, qi: (next_row(b, qi), k_blk)),
            pl.BlockSpec((half, kvw), lambda b, qi: (prev_row(b, qi), v_blk)),
            pl.BlockSpec((TQ, kvw), lambda b, qi: (b * nq + qi, v_blk)),
            pl.BlockSpec((half, kvw), lambda b, qi: (next_row(b, qi), v_blk)),
            pl.BlockSpec((None, None, None, past, kvw), lambda b, qi: (b, l, 0, 0, 0)),
            pl.BlockSpec((None, None, None, past, kvw), lambda b, qi: (b, l, 1, 0, 0)),
            pl.BlockSpec(memory_space=pltpu.SMEM),
        ],
        out_specs=pl.BlockSpec((TQ, BRANCH_WIDTH), lambda b, qi: (b * nq + qi, 0)),
        compiler_params=_cparams(("parallel", "parallel")),
        name="window_attention",
    )(qkv, qkv, qkv, qkv, qkv, qkv, qkv, qkv, cache, cache, sink_l)


def _merge_kernel(oa_ref, ob_ref, oc_ref, g_ref, w_ref, y_ref, acc_scr):
    r = pl.program_id(2)

    def contrib(o_ref):
        return g_ref[...].astype(F32) * _dot(o_ref[...], w_ref[...].astype(BF16))

    @pl.when(r == 0)
    def _():
        acc_scr[...] = contrib(oa_ref)

    @pl.when(r == 1)
    def _():
        acc_scr[...] += contrib(ob_ref)

    @pl.when(r == 2)
    def _():
        y_ref[...] = (acc_scr[...] + contrib(oc_ref)).astype(y_ref.dtype)


def _merge_branches(o_arrays, o_blocks, gates, w_branch, l):
    t = gates.shape[0]
    tn = 1024
    nn = D_MODEL // tn
    o_specs = [pl.BlockSpec((TM, BRANCH_WIDTH), functools.partial(lambda i, n, r, blk: (i, blk), blk=blk))
               for blk in o_blocks]
    return pl.pallas_call(
        _merge_kernel,
        out_shape=jax.ShapeDtypeStruct((t, D_MODEL), BF16),
        grid=(t // TM, nn, 3),
        in_specs=o_specs + [
            pl.BlockSpec((TM, tn), lambda i, n, r: (i, r * nn + n)),
            pl.BlockSpec((None, None, BRANCH_WIDTH, tn), lambda i, n, r: (l, r, 0, n)),
        ],
        out_specs=pl.BlockSpec((TM, tn), lambda i, n, r: (i, n)),
        scratch_shapes=[pltpu.VMEM((TM, tn), F32)],
        compiler_params=_cparams(("parallel", "parallel", "arbitrary")),
        name="merge_branches",
    )(*o_arrays, gates, w_branch)


def _layer_norm_rows(v, gain, bias):
    return _norm_rows(v) * gain + bias


def _outproj_kernel(y_ref, w_ref, x_ref, g_ref, gain_ref, bias_ref, o_ref, z_scr):
    n = pl.program_id(1)
    n_blocks, _, tn = z_scr.shape
    z_scr[n] = _dot(y_ref[...], w_ref[...].astype(BF16))

    @pl.when(n == n_blocks - 1)
    def _():
        for b in range(n_blocks):
            cols = slice(b * tn, (b + 1) * tn)
            o_ref[:, cols] = ALPHA * x_ref[:, cols] + g_ref[:, cols] * z_scr[b]
        o_ref[...] = _layer_norm_rows(o_ref[...], gain_ref[...], bias_ref[...])


def _out_projection(y, w_o, l, x, mod_l, ln_gain_l, ln_bias_l, *, row_of_tile):
    t = x.shape[0]
    tn = 512
    return pl.pallas_call(
        _outproj_kernel,
        out_shape=jax.ShapeDtypeStruct((t, D_MODEL), F32),
        grid=(t // TM_SMALL, D_MODEL // tn),
        in_specs=[
            pl.BlockSpec((TM_SMALL, D_MODEL), lambda i, n: (i, 0)),
            pl.BlockSpec((None, D_MODEL, tn), lambda i, n: (l, 0, n)),
            pl.BlockSpec((TM_SMALL, D_MODEL), lambda i, n: (i, 0)),
            _mod_spec(2, row_of_tile),
            pl.BlockSpec((1, D_MODEL), lambda i, n: (0, 0)),
            pl.BlockSpec((1, D_MODEL), lambda i, n: (0, 0)),
        ],
        out_specs=pl.BlockSpec((TM_SMALL, D_MODEL), lambda i, n: (i, 0)),
        scratch_shapes=[pltpu.VMEM((D_MODEL // tn, TM_SMALL, tn), F32)],
        compiler_params=_cparams(("parallel", "arbitrary")),
        name="out_projection",
    )(y, w_o, x, mod_l, ln_gain_l, ln_bias_l)


def _route(p):
    rows = [p[e:e + 1, :] for e in range(N_EXPERTS)]
    best_score, best_group = None, None
    for g in range(N_GROUPS):
        members = rows[g * EXPERTS_PER_GROUP:(g + 1) * EXPERTS_PER_GROUP]
        score = None
        for a in range(EXPERTS_PER_GROUP):
            for b in range(a + 1, EXPERTS_PER_GROUP):
                pair = members[a] + members[b]
                score = pair if score is None else jnp.maximum(score, pair)
        if g == 0:
            best_score, best_group = score, jnp.zeros(score.shape, F32)
        else:
            better = score > best_score
            best_group = jnp.where(better, float(g), best_group)
            best_score = jnp.where(better, score, best_score)
    e_idx = lax.broadcasted_iota(jnp.int32, p.shape, 0).astype(F32)
    g_idx = jnp.floor(e_idx * (1.0 / EXPERTS_PER_GROUP))
    masked = jnp.where(g_idx == best_group, p, -1.0)
    w1 = jnp.max(masked, axis=0, keepdims=True)
    i1 = jnp.min(jnp.where(masked == w1, e_idx, float(N_EXPERTS)), axis=0, keepdims=True)
    masked2 = jnp.where(e_idx == i1, -2.0, masked)
    w2 = jnp.max(masked2, axis=0, keepdims=True)
    i2 = jnp.min(jnp.where(masked2 == w2, e_idx, float(N_EXPERTS)), axis=0, keepdims=True)
    tot = w1 + w2
    return e_idx, i1, i2, w1 / tot, w2 / tot


ROUTE_ROWS = 8


def _moe_route_kernel(x_ref, sh_ref, sc_ref, wr_ref, h_ref, rec_ref, rec_t_ref, cnt_ref, carry_scr):
    @pl.when(pl.program_id(0) == 0)
    def _():
        carry_scr[...] = jnp.zeros(carry_scr.shape, F32)

    h = _norm_rows(x_ref[...]) * (1.0 + sc_ref[...]) + sh_ref[...]
    h_ref[...] = h
    logits = _dot_nt(wr_ref[...].astype(BF16), h.astype(BF16))
    m = jnp.max(logits, axis=0, keepdims=True)
    e = jnp.exp(logits - m)
    probs = e / jnp.sum(e, axis=0, keepdims=True)
    e_idx, i1, i2, w1, w2 = _route(probs)
    tm = probs.shape[1]
    oh1 = (e_idx == i1).astype(F32)
    oh2 = (e_idx == i2).astype(F32)
    oh = oh1 + oh2
    earlier = (lax.broadcasted_iota(jnp.int32, (tm, tm), 0) < lax.broadcasted_iota(jnp.int32, (tm, tm), 1))
    rank = carry_scr[:, 0:1] + _dot(oh.astype(BF16), earlier.astype(BF16))
    r1 = jnp.sum(oh1 * rank, axis=0, keepdims=True)
    r2 = jnp.sum(oh2 * rank, axis=0, keepdims=True)
    carry_scr[...] = carry_scr[...] + jnp.sum(oh, axis=1, keepdims=True)
    cnt_ref[...] = carry_scr[...]
    row = lax.broadcasted_iota(jnp.int32, (ROUTE_ROWS, tm), 0)
    rec = jnp.zeros((ROUTE_ROWS, tm), F32)
    for k, v in enumerate((i1, i2, r1, r2, w1, w2)):
        rec = jnp.where(row == k, v, rec)
    rec_ref[...] = rec
    pad = jnp.zeros((LANES - ROUTE_ROWS, tm), F32)
    rec_t_ref[...] = jnp.concatenate([rec, pad], axis=0).T


def _moe_route(x, mod_l, w_router_t, *, row_of_tile):
    t = x.shape[0]
    tm = TM_SMALL
    return pl.pallas_call(
        _moe_route_kernel,
        out_shape=(jax.ShapeDtypeStruct((t, D_MODEL), F32), jax.ShapeDtypeStruct((ROUTE_ROWS, t), F32),
                   jax.ShapeDtypeStruct((t, LANES), F32), jax.ShapeDtypeStruct((N_EXPERTS, LANES), F32)),
        grid=(t // tm,),
        in_specs=[
            pl.BlockSpec((tm, D_MODEL), lambda i: (i, 0)),
            _mod_spec(3, row_of_tile),
            _mod_spec(4, row_of_tile),
            pl.BlockSpec((N_EXPERTS, D_MODEL), lambda i: (0, 0)),
        ],
        out_specs=(pl.BlockSpec((tm, D_MODEL), lambda i: (i, 0)),
                   pl.BlockSpec((ROUTE_ROWS, tm), lambda i: (0, i)),
                   pl.BlockSpec((tm, LANES), lambda i: (i, 0)),
                   pl.BlockSpec((N_EXPERTS, LANES), lambda i: (0, 0))),
        scratch_shapes=[pltpu.VMEM((N_EXPERTS, LANES), F32)],
        compiler_params=_cparams(("arbitrary",)),
        name="moe_route",
    )(x, mod_l, mod_l, w_router_t)


ROW_DMA_UNROLL = 8


def _row_copy(src_hbm, src_row, dst_buf, dst_row, sem):
    return pltpu.make_async_copy(src_hbm.at[pl.ds(src_row, 1)], dst_buf.at[pl.ds(dst_row, 1)], sem)


def _moe_dispatch_kernel(n_tiles_ref, src_ref, h_hbm, xs_ref, x_buf, sems, *, te):
    j = pl.program_id(0)
    n_valid = n_tiles_ref[0]
    slot = j % 2

    def start_gather(tile, s):
        def body(r, carry):
            _row_copy(h_hbm, src_ref[tile * te + r], x_buf.at[s], r, sems.at[s]).start()
            return carry
        lax.fori_loop(0, te, body, 0, unroll=ROW_DMA_UNROLL)

    def wait_gather(s):
        def body(r, carry):
            _row_copy(h_hbm, 0, x_buf.at[s], r, sems.at[s]).wait()
            return carry
        lax.fori_loop(0, te, body, 0, unroll=ROW_DMA_UNROLL)

    @pl.when((j == 0) & (n_valid > 0))
    def _():
        start_gather(0, 0)

    @pl.when(j + 1 < n_valid)
    def _():
        start_gather(j + 1, 1 - slot)

    @pl.when(j < n_valid)
    def _():
        wait_gather(slot)
        xs_ref[...] = x_buf[slot].astype(BF16)

    @pl.when(j >= n_valid)
    def _():
        xs_ref[...] = jnp.zeros(xs_ref.shape, BF16)


def _moe_dispatch(h, n_tiles, src, *, te):
    n_rows = src.shape[0]
    return pl.pallas_call(
        functools.partial(_moe_dispatch_kernel, te=te),
        out_shape=jax.ShapeDtypeStruct((n_rows, D_MODEL), BF16),
        grid_spec=pltpu.PrefetchScalarGridSpec(
            num_scalar_prefetch=2,
            grid=(n_rows // te,),
            in_specs=[pl.BlockSpec(memory_space=pltpu.HBM)],
            out_specs=pl.BlockSpec((te, D_MODEL), lambda j, *_: (j, 0)),
            scratch_shapes=[pltpu.VMEM((2, te, D_MODEL), F32), pltpu.SemaphoreType.DMA((2,))],
        ),
        compiler_params=_cparams(("arbitrary",)),
        name="moe_dispatch",
    )(n_tiles, src, h)


def _moe_expert_kernel(tile_expert_ref, n_tiles_ref, xs_ref, wg_ref, wu_ref, wd_ref, ys_ref):
    del tile_expert_ref
    j = pl.program_id(0)

    @pl.when(j < n_tiles_ref[0])
    def _():
        x = xs_ref[...]
        gate = _dot(x, wg_ref[...].astype(BF16))
        up = _dot(x, wu_ref[...].astype(BF16))
        hid = gate / (1.0 + jnp.exp(-gate)) * up
        ys_ref[...] = _dot(hid.astype(BF16), wd_ref[...].astype(BF16))

    @pl.when(j >= n_tiles_ref[0])
    def _():
        ys_ref[...] = jnp.zeros(ys_ref.shape, F32)


def _moe_experts(xs, tile_expert, n_tiles, wg, wu, wd, l, *, te):
    n_rows = xs.shape[0]
    w_in_spec = pl.BlockSpec((None, None, D_MODEL, D_EXPERT), lambda j, te_ref, nt_ref: (l, te_ref[j], 0, 0))
    w_out_spec = pl.BlockSpec((None, None, D_EXPERT, D_MODEL), lambda j, te_ref, nt_ref: (l, te_ref[j], 0, 0))
    xs_spec = pl.BlockSpec((te, D_MODEL), lambda j, te_ref, nt_ref: (j, 0))
    return pl.pallas_call(
        _moe_expert_kernel,
        out_shape=jax.ShapeDtypeStruct((n_rows, D_MODEL), F32),
        grid_spec=pltpu.PrefetchScalarGridSpec(
            num_scalar_prefetch=2,
            grid=(n_rows // te,),
            in_specs=[xs_spec, w_in_spec, w_in_spec, w_out_spec],
            out_specs=pl.BlockSpec((te, D_MODEL), lambda j, *_: (j, 0)),
        ),
        compiler_params=_cparams(("arbitrary",)),
        name="moe_experts",
    )(tile_expert, n_tiles, xs, wg, wu, wd)


TM_COMBINE = 256


def _moe_combine_kernel(dest_ref, ys_hbm, x_ref, rec_t_ref, g_ref, gain_ref, bias_ref, o_ref, y_buf, sems,
                        *, n_tokens):
    i = pl.program_id(0)
    tm = x_ref.shape[0]
    slot = i % 2

    def start_gather(tile, s):
        def body(r, carry):
            for k in range(2):
                row = dest_ref[k * n_tokens + tile * tm + r]
                _row_copy(ys_hbm, row, y_buf.at[s, k], r, sems.at[s]).start()
            return carry
        lax.fori_loop(0, tm, body, 0, unroll=ROW_DMA_UNROLL)

    def wait_gather(s):
        def body(r, carry):
            for k in range(2):
                _row_copy(ys_hbm, 0, y_buf.at[s, k], r, sems.at[s]).wait()
            return carry
        lax.fori_loop(0, tm, body, 0, unroll=ROW_DMA_UNROLL)

    @pl.when(i == 0)
    def _():
        start_gather(0, 0)

    @pl.when(i + 1 < pl.num_programs(0))
    def _():
        start_gather(i + 1, 1 - slot)

    wait_gather(slot)
    z = rec_t_ref[:, 4:5] * y_buf[slot, 0] + rec_t_ref[:, 5:6] * y_buf[slot, 1]
    v = ALPHA * x_ref[...] + g_ref[...] * z
    o_ref[...] = _layer_norm_rows(v, gain_ref[...], bias_ref[...])


def _moe_combine(dest, ys, x, rec_t, mod_l, ln_gain_l, ln_bias_l, *, row_of_tile):
    t = x.shape[0]
    tm = TM_COMBINE
    vec = pl.BlockSpec((1, D_MODEL), lambda i, *_: (0, 0))
    return pl.pallas_call(
        functools.partial(_moe_combine_kernel, n_tokens=t),
        out_shape=jax.ShapeDtypeStruct((t, D_MODEL), F32),
        grid_spec=pltpu.PrefetchScalarGridSpec(
            num_scalar_prefetch=1,
            grid=(t // tm,),
            in_specs=[
                pl.BlockSpec(memory_space=pltpu.HBM),
                pl.BlockSpec((tm, D_MODEL), lambda i, *_: (i, 0)),
                pl.BlockSpec((tm, LANES), lambda i, *_: (i, 0)),
                _mod_spec(5, row_of_tile),
                vec, vec,
            ],
            out_specs=pl.BlockSpec((tm, D_MODEL), lambda i, *_: (i, 0)),
            scratch_shapes=[pltpu.VMEM((2, 2, tm, D_MODEL), F32), pltpu.SemaphoreType.DMA((2,))],
        ),
        compiler_params=_cparams(("arbitrary",)),
        name="moe_combine",
    )(dest, ys, x, rec_t, mod_l, ln_gain_l, ln_bias_l)


def _dispatch_tables(rec, cnt, *, te):
    t = rec.shape[1]
    n_rows = 2 * t + N_EXPERTS * te
    e12 = rec[0:2].astype(jnp.int32)
    r12 = rec[2:4].astype(jnp.int32)
    counts = cnt[:, 0].astype(jnp.int32)
    padded = (counts + te - 1) // te * te
    ends = jnp.cumsum(padded)
    offsets = ends - padded
    expert_ids = jnp.arange(N_EXPERTS, dtype=jnp.int32)[:, None, None]
    dest = (jnp.sum(jnp.where(e12[None] == expert_ids, offsets[:, None, None], 0), axis=0) + r12).reshape(2 * t)
    tokens = jnp.tile(jnp.arange(t, dtype=jnp.int32), 2)
    src = jnp.zeros((n_rows,), jnp.int32).at[dest].set(tokens, unique_indices=True)
    tile_start = jnp.arange(n_rows // te, dtype=jnp.int32) * te
    tile_expert = jnp.minimum(jnp.searchsorted(ends, tile_start, side="right"), N_EXPERTS - 1).astype(jnp.int32)
    n_tiles = (ends[-1:] // te).astype(jnp.int32)
    return dest, src, tile_expert, n_tiles


def _rope_tables(n_tokens):
    rows = n_tokens // GRID_W
    row = jnp.repeat(jnp.arange(rows), GRID_W).astype(F32)
    col = jnp.tile(jnp.arange(GRID_W), rows).astype(F32)
    quarter = HEAD_DIM // 4
    inv_freq = ROPE_THETA ** (-jnp.arange(quarter, dtype=F32) / quarter)
    ang_r, ang_c = row[:, None] * inv_freq, col[:, None] * inv_freq
    cos = jnp.concatenate([jnp.cos(ang_r), jnp.cos(ang_r), jnp.cos(ang_c), jnp.cos(ang_c)], axis=-1)
    sin = jnp.concatenate([-jnp.sin(ang_r), jnp.sin(ang_r), -jnp.sin(ang_c), jnp.sin(ang_c)], axis=-1)
    return cos, sin


def _row_of_tile(first_row, tokens_per_row):
    def for_tile(tile):
        return lambda i: first_row + (i * tile) // tokens_per_row
    return for_tile


def _mixer_and_ffn(x, attn_arrays, attn_blocks, gates, mod_l, rows, l, expert_tile, w_branch, w_o, ln_gain,
                   ln_bias, w_router_t, w_e_gate, w_e_up, w_e_down):
    y = _merge_branches(attn_arrays, attn_blocks, gates, w_branch, l)
    x = _out_projection(y, w_o, l, x, mod_l, ln_gain[l, 0:1], ln_bias[l, 0:1], row_of_tile=rows(TM_SMALL))
    h2, rec, rec_t, cnt = _moe_route(x, mod_l, w_router_t, row_of_tile=rows(TM_SMALL))
    dest, src, tile_expert, n_tiles = _dispatch_tables(rec, cnt, te=expert_tile)
    xs = _moe_dispatch(h2, n_tiles, src, te=expert_tile)
    ys = _moe_experts(xs, tile_expert, n_tiles, w_e_gate, w_e_up, w_e_down, l, te=expert_tile)
    return _moe_combine(dest, ys, x, rec_t, mod_l, ln_gain[l, 1:2], ln_bias[l, 1:2],
                        row_of_tile=rows(TM_COMBINE))


def kernel(x_prompt, x_sample, cache_kv_a, cache_kv_b, cache_kv_c, c, c_ctx, w_in, w_gate, w_branch, w_o,
           w_mod, b_mod, ln_gain, ln_bias, diff_lam, diff_subln, qk_gain, sink, w_router, w_e_gate, w_e_up,
           w_e_down):
    batch, seq, _ = x_prompt.shape
    dec_batch, dec_seq, _ = x_sample.shape
    past = cache_kv_a.shape[3]
    t_p, t_s = batch * seq, dec_batch * dec_seq

    cond = jnp.concatenate([c_ctx[None], c, jnp.zeros((MOD_ROWS - 1 - dec_batch, D_MODEL), F32)], axis=0)
    mod = _modulation(cond.T, w_mod, b_mod)
    cos_t, sin_t = _rope_tables(dec_seq)
    w_router_t = w_router.T
    cache_a = cache_kv_a.reshape(dec_batch, DEPTH, 2, past, 4 * 256)
    cache_b = cache_kv_b.reshape(dec_batch, DEPTH, 2, past, 2 * HEAD_DIM)
    cache_c = cache_kv_c.reshape(dec_batch, DEPTH, 2, past, 2 * HEAD_DIM)
    rows_p = _row_of_tile(0, t_p)
    rows_s = _row_of_tile(1, dec_seq)

    y_p = x_prompt.reshape(t_p, D_MODEL)
    y_s = x_sample.reshape(t_s, D_MODEL)
    new_a, new_b, new_c = [], [], []
    for l in range(DEPTH):
        lam_init = 0.8 - 0.6 * math.exp(-0.3 * l)
        mod_l = mod[l].reshape(MOD_ROWS, 1, N_MOD * D_MODEL)
        subln_l = diff_subln[l].reshape(1, 256)
        shared = (w_branch, w_o, ln_gain, ln_bias, w_router_t, w_e_gate, w_e_up, w_e_down)

        p = _in_projection(y_p, mod_l, w_in, l, qk_gain[l], cos_t, sin_t, row_of_tile=rows_p(TM),
                           rope=False, q_scale=1.0, out_dtype=F32)
        gates = _branch_gates(y_p, mod_l, w_gate, l, row_of_tile=rows_p(TM))
        attn = _attention_prompt(p, diff_lam[l], subln_l, sink[l], lam_init=lam_init, seq=seq)
        y_p = _mixer_and_ffn(y_p, (attn, attn, attn), (0, 1, 2), gates, mod_l, rows_p, l, 256, *shared)
        p5 = p.reshape(batch, seq, IN_COLS)
        new_a.append(jnp.stack([p5[..., COL_AK:COL_AV], p5[..., COL_AV:COL_BQ]], axis=1))
        new_b.append(jnp.stack([p5[..., COL_BK:COL_BV], p5[..., COL_BV:COL_CQ]], axis=1))
        new_c.append(jnp.stack([p5[..., COL_CK:COL_CV], p5[..., COL_CV:]], axis=1))

        qkv = _in_projection(y_s, mod_l, w_in, l, qk_gain[l], cos_t, sin_t, row_of_tile=rows_s(TM),
                             rope=True, q_scale=SCALE * LOG2E, out_dtype=BF16)
        gates = _branch_gates(y_s, mod_l, w_gate, l, row_of_tile=rows_s(TM))
        vec = lambda shape: pl.BlockSpec(shape, lambda b, qi, kk: (0, 0))
        a_o = _flash_sample(functools.partial(_diff_sample_kernel, lam_init=lam_init), qkv, cache_a, l,
                            q_col=COL_AQ, k_col=COL_AK, v_col=COL_AV, kv_width=1024, n_state=8,
                            row_sum_scratch=True, extra=(diff_lam[l], subln_l),
                            extra_specs=[vec((4, HEAD_DIM)), vec((1, 256))], name="diff_attention",
                            dec_seq=dec_seq)
        b_o = _flash_sample(_gqa_sample_kernel, qkv, cache_b, l, q_col=COL_BQ, k_col=COL_BK, v_col=COL_BV,
                            kv_width=256, n_state=8, row_sum_scratch=False, extra=(), extra_specs=[],
                            name="gqa_attention", dec_seq=dec_seq)
        c_o = _window_sample(qkv, cache_c, sink[l], l, dec_seq=dec_seq)
        y_s = _mixer_and_ffn(y_s, (a_o, b_o, c_o), (0, 0, 0), gates, mod_l, rows_s, l, 512, *shared)

    new_kv_a = jnp.stack(new_a, axis=1).reshape(batch, DEPTH, 2, seq, 4, 256)
    new_kv_b = jnp.stack(new_b, axis=1).reshape(batch, DEPTH, 2, seq, 2, HEAD_DIM)
    new_kv_c = jnp.stack(new_c, axis=1).reshape(batch, DEPTH, 2, seq, 2, HEAD_DIM)
    return (y_p.reshape(batch, seq, D_MODEL), y_s.reshape(dec_batch, dec_seq, D_MODEL),
            new_kv_a, new_kv_b, new_kv_c)
```

```python
import functools
import math

import jax
import jax.numpy as jnp
from jax import lax
from jax.experimental import pallas as pl
from jax.experimental.pallas import tpu as pltpu

F32 = jnp.float32
BF16 = jnp.bfloat16

D_MODEL = 2048
HEAD_DIM = 128
GRID_W = 64
ROPE_THETA = 10000.0
WINDOW = 128
N_EXPERTS = 16
N_GROUPS = 4
EXPERTS_PER_GROUP = N_EXPERTS // N_GROUPS
D_EXPERT = 512
N_MOD = 6
DEPTH = 2
ALPHA = (2 * DEPTH) ** 0.25
EPS = 1e-6
IN_COLS = 6144
BRANCH_WIDTH = 1024
SCALE = HEAD_DIM ** -0.5
LOG2E = math.log2(math.e)
NEG_BIG = -1e30

COL_AQ, COL_AK, COL_AV = 0, 1024, 2048
COL_BQ, COL_BK, COL_BV = 3072, 4096, 4352
COL_CQ, COL_CK, COL_CV = 4608, 5632, 5888

LANES = 128
VMEM_LIMIT = 56 * 1024 * 1024

TM = 1024
TN_PROJ = 1024
TN_PROJ_F32 = 512
TN_GATE = 1024
TM_SMALL = 512


def _cparams(sem):
    return pltpu.CompilerParams(dimension_semantics=sem, vmem_limit_bytes=VMEM_LIMIT)


def _dot(a, b):
    return jnp.dot(a, b, preferred_element_type=F32)


def _dot_nt(a, b):
    return lax.dot_general(a, b, (((1,), (1,)), ((), ())), preferred_element_type=F32)


def _norm_rows(x):
    mu = jnp.mean(x, axis=-1, keepdims=True)
    xc = x - mu
    var = jnp.mean(xc * xc, axis=-1, keepdims=True)
    return xc * lax.rsqrt(var + EPS)


def _rms(v, gain):
    ms = jnp.mean(v * v, axis=-1, keepdims=True)
    return v * lax.rsqrt(ms + EPS) * gain


def _rope(v, cos, sin_signed):
    lane = lax.broadcasted_iota(jnp.int32, v.shape, 1)
    first_half = (lane % 64) < 32
    partner = jnp.where(first_half, pltpu.roll(v, 96, 1), pltpu.roll(v, 32, 1))
    return v * cos + partner * sin_signed


N_COND = 3
MOD_ROWS = 8
TN_MOD = 1024


def _mod_kernel(cond_ref, w_ref, b_ref, o_ref):
    w = w_ref[...]
    row_idx = lax.broadcasted_iota(jnp.int32, (MOD_ROWS, TN_MOD), 0)
    out = jnp.zeros((MOD_ROWS, TN_MOD), F32)
    for r in range(N_COND):
        c = cond_ref[:, r:r + 1]
        s = c / (1.0 + jnp.exp(-c))
        m = jnp.sum(w * s, axis=0, keepdims=True) + b_ref[...]
        out = jnp.where(row_idx == r, m, out)
    o_ref[...] = out


def _modulation(cond_t, w_mod, b_mod):
    n = N_MOD * D_MODEL
    return pl.pallas_call(
        _mod_kernel,
        out_shape=jax.ShapeDtypeStruct((DEPTH, MOD_ROWS, n), F32),
        grid=(DEPTH, n // TN_MOD),
        in_specs=[
            pl.BlockSpec((D_MODEL, MOD_ROWS), lambda l, j: (0, 0)),
            pl.BlockSpec((None, D_MODEL, TN_MOD), lambda l, j: (l, 0, j)),
            pl.BlockSpec((None, 1, TN_MOD), lambda l, j: (l, 0, j)),
        ],
        out_specs=pl.BlockSpec((None, MOD_ROWS, TN_MOD), lambda l, j: (l, 0, j)),
        compiler_params=_cparams(("parallel", "parallel")),
        name="modulation",
    )(cond_t, w_mod, b_mod.reshape(DEPTH, 1, n))


def _modulate_to_scratch(x_ref, sh_ref, sc_ref, h_scr):
    h = _norm_rows(x_ref[...]) * (1.0 + sc_ref[...]) + sh_ref[...]
    h_scr[...] = h.astype(BF16)


def _head_chunk_kinds():
    kinds = []
    for n_chunks, kind in ((8, (None, True, True)), (8, (None, True, False)), (8, (None, False, False)),
                           (8, ("q", True, True)), (2, ("k", True, False)), (2, (None, False, False)),
                           (8, (None, True, True)), (2, (None, True, False)), (2, (None, False, False))):
        kinds.extend([kind] * n_chunks)
    return kinds


_HEAD_CHUNK_KINDS = _head_chunk_kinds()


def _inproj_kernel(x_ref, sh_ref, sc_ref, w_ref, qk_gain_ref, cos_ref, sin_ref, o_ref, h_scr, *, rope,
                   q_scale):
    j = pl.program_id(1)
    gain_q = qk_gain_ref[0:1, :]
    gain_k = qk_gain_ref[1:2, :]

    @pl.when(j == 0)
    def _():
        _modulate_to_scratch(x_ref, sh_ref, sc_ref, h_scr)

    acc = _dot(h_scr[...], w_ref[...].astype(BF16))
    tn = o_ref.shape[1]
    n_chunks = tn // LANES

    def chunk(c):
        return acc[:, c * LANES:(c + 1) * LANES]

    def store(c, v):
        o_ref[:, c * LANES:(c + 1) * LANES] = v.astype(o_ref.dtype)

    def rp(v):
        return _rope(v, cos_ref[...], sin_ref[...]) if rope else v

    def qs(v):
        return v * q_scale if q_scale != 1.0 else v

    def finish(c, kind):
        norm, rotate, is_query = kind
        v = chunk(c)
        if norm is not None:
            v = _rms(v, gain_q if norm == "q" else gain_k)
        if rotate:
            v = rp(v)
        store(c, qs(v) if is_query else v)

    tile_kinds = [tuple(_HEAD_CHUNK_KINDS[t * n_chunks:(t + 1) * n_chunks]) for t in range(IN_COLS // tn)]
    for kinds in dict.fromkeys(tile_kinds):
        tiles = [t for t, k in enumerate(tile_kinds) if k == kinds]
        cond = j == tiles[0]
        for t in tiles[1:]:
            cond = cond | (j == t)

        @pl.when(cond)
        def _(kinds=kinds):
            for c, kind in enumerate(kinds):
                finish(c, kind)


def _gate_kernel(x_ref, sh_ref, sc_ref, w_ref, o_ref, h_scr):
    @pl.when(pl.program_id(1) == 0)
    def _():
        _modulate_to_scratch(x_ref, sh_ref, sc_ref, h_scr)

    acc = _dot(h_scr[...], w_ref[...].astype(BF16))
    o_ref[...] = (1.0 / (1.0 + jnp.exp(-acc))).astype(o_ref.dtype)


def _mod_spec(which, row_of_tile):
    return pl.BlockSpec((None, 1, D_MODEL), lambda i, *_: (row_of_tile(i), 0, which))


def _in_projection(x, mod_l, w_in, l, qk_gain_l, cos_t, sin_t, *, row_of_tile, rope, q_scale, out_dtype):
    t = x.shape[0]
    tiles_per_seq = cos_t.shape[0] // TM
    tn = TN_PROJ_F32 if out_dtype == F32 else TN_PROJ
    return pl.pallas_call(
        functools.partial(_inproj_kernel, rope=rope, q_scale=q_scale),
        out_shape=jax.ShapeDtypeStruct((t, IN_COLS), out_dtype),
        grid=(t // TM, IN_COLS // tn),
        in_specs=[
            pl.BlockSpec((TM, D_MODEL), lambda i, j: (i, 0)),
            _mod_spec(0, row_of_tile),
            _mod_spec(1, row_of_tile),
            pl.BlockSpec((None, D_MODEL, tn), lambda i, j: (l, 0, j)),
            pl.BlockSpec((2, HEAD_DIM), lambda i, j: (0, 0)),
            pl.BlockSpec((TM, HEAD_DIM), lambda i, j: (i % tiles_per_seq, 0)),
            pl.BlockSpec((TM, HEAD_DIM), lambda i, j: (i % tiles_per_seq, 0)),
        ],
        out_specs=pl.BlockSpec((TM, tn), lambda i, j: (i, j)),
        scratch_shapes=[pltpu.VMEM((TM, D_MODEL), BF16)],
        compiler_params=_cparams(("parallel", "arbitrary")),
        name="in_projection",
    )(x, mod_l, mod_l, w_in, qk_gain_l, cos_t, sin_t)


def _branch_gates(x, mod_l, w_gate, l, *, row_of_tile):
    t = x.shape[0]
    n = w_gate.shape[2]
    return pl.pallas_call(
        _gate_kernel,
        out_shape=jax.ShapeDtypeStruct((t, n), BF16),
        grid=(t // TM, n // TN_GATE),
        in_specs=[
            pl.BlockSpec((TM, D_MODEL), lambda i, j: (i, 0)),
            _mod_spec(0, row_of_tile),
            _mod_spec(1, row_of_tile),
            pl.BlockSpec((None, D_MODEL, TN_GATE), lambda i, j: (l, 0, j)),
        ],
        out_specs=pl.BlockSpec((TM, TN_GATE), lambda i, j: (i, j)),
        scratch_shapes=[pltpu.VMEM((TM, D_MODEL), BF16)],
        compiler_params=_cparams(("parallel", "arbitrary")),
        name="branch_gates",
    )(x, mod_l, mod_l, w_gate)


def _diff_lambda(lam_ref, lam_init):
    lp = lam_ref[...]
    t1 = jnp.sum(lp[0:1] * lp[1:2], axis=-1, keepdims=True)
    t2 = jnp.sum(lp[2:3] * lp[3:4], axis=-1, keepdims=True)
    return jnp.exp(t1) - jnp.exp(t2) + lam_init


def _softmax_rows(s, sink=None):
    m = jnp.max(s, axis=-1, keepdims=True)
    if sink is not None:
        m = jnp.maximum(m, sink)
    e = jnp.exp(s - m)
    den = jnp.sum(e, axis=-1, keepdims=True)
    if sink is not None:
        den = den + jnp.exp(sink - m)
    return e * (1.0 / den)


def _attn_prompt_kernel(p_ref, lam_ref, subln_ref, sink_ref, o_ref, *, lam_init):
    lam = _diff_lambda(lam_ref, lam_init)

    def blk(c0, w):
        return p_ref[:, c0:c0 + w].astype(BF16)

    for h in range(4):
        probs = []
        for m in range(2):
            q = blk(COL_AQ + h * 256 + m * HEAD_DIM, HEAD_DIM)
            k = blk(COL_AK + h * 256 + m * HEAD_DIM, HEAD_DIM)
            probs.append(_softmax_rows(_dot_nt(q, k) * SCALE))
        w = probs[0] - lam * probs[1]
        o = _dot(w.astype(BF16), blk(COL_AV + h * 256, 256))
        o = _rms(o, subln_ref[...]) * (1.0 - lam_init)
        o_ref[:, h * 256:(h + 1) * 256] = o.astype(o_ref.dtype)

    for mixer, (cq, ck, cv) in enumerate(((COL_BQ, COL_BK, COL_BV), (COL_CQ, COL_CK, COL_CV))):
        for kvh in range(2):
            k = blk(ck + kvh * HEAD_DIM, HEAD_DIM)
            v = blk(cv + kvh * HEAD_DIM, HEAD_DIM)
            for g in range(4):
                hq = kvh * 4 + g
                q = blk(cq + hq * HEAD_DIM, HEAD_DIM)
                sink = sink_ref[hq] if mixer == 1 else None
                p = _softmax_rows(_dot_nt(q, k) * SCALE, sink)
                o = _dot(p.astype(BF16), v)
                c0 = BRANCH_WIDTH * (1 + mixer) + hq * HEAD_DIM
                o_ref[:, c0:c0 + HEAD_DIM] = o.astype(o_ref.dtype)


def _attention_prompt(p, diff_lam_l, subln_l, sink_l, *, lam_init, seq):
    t = p.shape[0]
    return pl.pallas_call(
        functools.partial(_attn_prompt_kernel, lam_init=lam_init),
        out_shape=jax.ShapeDtypeStruct((t, 3 * BRANCH_WIDTH), BF16),
        grid=(t // seq,),
        in_specs=[
            pl.BlockSpec((seq, IN_COLS), lambda b: (b, 0)),
            pl.BlockSpec((4, HEAD_DIM), lambda b: (0, 0)),
            pl.BlockSpec((1, 256), lambda b: (0, 0)),
            pl.BlockSpec(memory_space=pltpu.SMEM),
        ],
        out_specs=pl.BlockSpec((seq, 3 * BRANCH_WIDTH), lambda b: (b, 0)),
        compiler_params=_cparams(("parallel",)),
        name="attention_prompt",
    )(p, diff_lam_l, subln_l, sink_l)


TQ = 512
TK = 1024


def _tile_lanes(v, n):
    return jnp.concatenate([v] * n, axis=-1) if n > 1 else v


def _online_softmax(idx, s, m_scr):
    m_prev = m_scr[idx]
    m_new = jnp.maximum(m_prev, jnp.max(s, axis=-1, keepdims=True))
    m_scr[idx] = m_new
    alpha = jnp.exp2(m_prev - m_new)
    p = jnp.exp2(s - _tile_lanes(m_new, s.shape[1] // LANES))
    return p, alpha


def _diff_sample_kernel(q_ref, k_ref, v_ref, kc_ref, vc_ref, lam_ref, subln_ref, o_ref,
                        m_scr, l_scr, acc_scr, *, lam_init):
    kk = pl.program_id(2)

    @pl.when(kk == 0)
    def _():
        m_scr[...] = jnp.full(m_scr.shape, NEG_BIG, F32)
        l_scr[...] = jnp.zeros(l_scr.shape, F32)
        acc_scr[...] = jnp.zeros(acc_scr.shape, F32)

    def process(kb_ref, vb_ref):
        for h in range(4):
            v = vb_ref[:, h * 256:(h + 1) * 256].astype(BF16)
            for m in range(2):
                c0 = h * 256 + m * HEAD_DIM
                idx = h * 2 + m
                k = kb_ref[:, c0:c0 + HEAD_DIM].astype(BF16)
                p, alpha = _online_softmax(idx, _dot_nt(q_ref[:, c0:c0 + HEAD_DIM], k), m_scr)
                part = p[:, 0:LANES]
                for c in range(1, p.shape[1] // LANES):
                    part = part + p[:, c * LANES:(c + 1) * LANES]
                l_scr[idx] = alpha * l_scr[idx] + part
                acc_scr[idx] = _tile_lanes(alpha, 2) * acc_scr[idx] + _dot(p.astype(BF16), v)

    @pl.when(kk == 0)
    def _():
        process(kc_ref, vc_ref)

    @pl.when(kk > 0)
    def _():
        process(k_ref, v_ref)

    @pl.when(kk == pl.num_programs(2) - 1)
    def _():
        lam = _diff_lambda(lam_ref, lam_init)
        for h in range(4):
            l1 = jnp.sum(l_scr[2 * h], axis=-1, keepdims=True)
            l2 = jnp.sum(l_scr[2 * h + 1], axis=-1, keepdims=True)
            o1 = acc_scr[2 * h] * (1.0 / l1)
            o2 = acc_scr[2 * h + 1] * (1.0 / l2)
            o = _rms(o1 - lam * o2, subln_ref[...]) * (1.0 - lam_init)
            o_ref[:, h * 256:(h + 1) * 256] = o.astype(o_ref.dtype)


def _gqa_sample_kernel(q_ref, k_ref, v_ref, kc_ref, vc_ref, o_ref, m_scr, acc_scr):
    kk = pl.program_id(2)

    @pl.when(kk == 0)
    def _():
        m_scr[...] = jnp.full(m_scr.shape, NEG_BIG, F32)
        acc_scr[...] = jnp.zeros(acc_scr.shape, F32)

    def process(kb_ref, vb_ref):
        for kvh in range(2):
            k = kb_ref[:, kvh * HEAD_DIM:(kvh + 1) * HEAD_DIM].astype(BF16)
            v = vb_ref[:, kvh * HEAD_DIM:(kvh + 1) * HEAD_DIM].astype(BF16)
            v_ones = jnp.concatenate([v, jnp.ones_like(v)], axis=-1)
            for g in range(4):
                hq = kvh * 4 + g
                s = _dot_nt(q_ref[:, hq * HEAD_DIM:(hq + 1) * HEAD_DIM], k)
                p, alpha = _online_softmax(hq, s, m_scr)
                acc_scr[hq] = _tile_lanes(alpha, 2) * acc_scr[hq] + _dot(p.astype(BF16), v_ones)

    @pl.when(kk == 0)
    def _():
        process(kc_ref, vc_ref)

    @pl.when(kk > 0)
    def _():
        process(k_ref, v_ref)

    @pl.when(kk == pl.num_programs(2) - 1)
    def _():
        for hq in range(8):
            o = acc_scr[hq, :, 0:HEAD_DIM] / acc_scr[hq, :, HEAD_DIM:2 * HEAD_DIM]
            o_ref[:, hq * HEAD_DIM:(hq + 1) * HEAD_DIM] = o.astype(o_ref.dtype)


def _flash_sample(kernel, qkv, cache, l, *, q_col, k_col, v_col, kv_width, n_state, row_sum_scratch, extra,
                  extra_specs, name, dec_seq):
    t = qkv.shape[0]
    nb = t // dec_seq
    nq = dec_seq // TQ
    nk = dec_seq // TK
    past = cache.shape[3]
    q_blk, k_blk, v_blk = q_col // BRANCH_WIDTH, k_col // kv_width, v_col // kv_width
    kv_row = lambda b, qi, kk: b * nk + jnp.maximum(kk - 1, 0)
    return pl.pallas_call(
        kernel,
        out_shape=jax.ShapeDtypeStruct((t, BRANCH_WIDTH), BF16),
        grid=(nb, nq, nk + 1),
        in_specs=[
            pl.BlockSpec((TQ, BRANCH_WIDTH), lambda b, qi, kk: (b * nq + qi, q_blk)),
            pl.BlockSpec((TK, kv_width), lambda b, qi, kk: (kv_row(b, qi, kk), k_blk)),
            pl.BlockSpec((TK, kv_width), lambda b, qi, kk: (kv_row(b, qi, kk), v_blk)),
            pl.BlockSpec((None, None, None, past, kv_width), lambda b, qi, kk: (b, l, 0, 0, 0)),
            pl.BlockSpec((None, None, None, past, kv_width), lambda b, qi, kk: (b, l, 1, 0, 0)),
        ] + extra_specs,
        out_specs=pl.BlockSpec((TQ, BRANCH_WIDTH), lambda b, qi, kk: (b * nq + qi, 0)),
        scratch_shapes=[pltpu.VMEM((n_state, TQ, LANES), F32)] * (2 if row_sum_scratch else 1)
        + [pltpu.VMEM((n_state, TQ, 2 * LANES), F32)],
        compiler_params=_cparams(("parallel", "parallel", "arbitrary")),
        name=name,
    )(qkv, qkv, qkv, cache, cache, *extra)


def _window_sample_kernel(q0_ref, q1_ref, kp_ref, kc_ref, kn_ref, vp_ref, vc_ref, vn_ref, kctx_ref, vctx_ref,
                          sink_ref, o_ref, *, dec_seq):
    qi = pl.program_id(1)
    q_start = qi * TQ
    half = TQ // 2
    q_pos = q_start + lax.broadcasted_iota(jnp.int32, (TQ, 1), 0)
    segs = ((kp_ref, vp_ref, q_start - half, half), (kc_ref, vc_ref, q_start, TQ),
            (kn_ref, vn_ref, q_start + TQ, half))
    valid = []
    for _, _, start, n in segs:
        k_pos = start + lax.broadcasted_iota(jnp.int32, (TQ, n), 1)
        valid.append((jnp.abs(q_pos - k_pos) <= WINDOW) & (k_pos >= 0) & (k_pos < dec_seq))
    for kvh in range(2):
        q_ref = q0_ref if kvh == 0 else q1_ref
        lo, hi = kvh * HEAD_DIM, (kvh + 1) * HEAD_DIM
        kctx = kctx_ref[:, lo:hi].astype(BF16)
        vctx = vctx_ref[:, lo:hi].astype(BF16)
        for g in range(4):
            hq = kvh * 4 + g
            q = q_ref[:, g * HEAD_DIM:(g + 1) * HEAD_DIM]
            sink = sink_ref[hq] * LOG2E
            scores = [_dot_nt(q, kctx)]
            for (k_ref, _, _, _), ok in zip(segs, valid):
                scores.append(jnp.where(ok, _dot_nt(q, k_ref[:, lo:hi]), NEG_BIG))
            m = jnp.maximum(scores[0].max(axis=-1, keepdims=True), sink)
            for s in scores[1:]:
                m = jnp.maximum(m, s.max(axis=-1, keepdims=True))
            es = [jnp.exp2(s - m) for s in scores]
            den = jnp.exp2(sink - m)
            for e in es:
                den = den + jnp.sum(e, axis=-1, keepdims=True)
            o = _dot(es[0].astype(BF16), vctx)
            for e, (_, v_ref, _, _) in zip(es[1:], segs):
                o = o + _dot(e.astype(BF16), v_ref[:, lo:hi])
            o = o * (1.0 / den)
            o_ref[:, hq * HEAD_DIM:(hq + 1) * HEAD_DIM] = o.astype(o_ref.dtype)


def _window_sample(qkv, cache, sink_l, l, *, dec_seq):
    t = qkv.shape[0]
    nb = t // dec_seq
    nq = dec_seq // TQ
    half = TQ // 2
    n_half = dec_seq // half
    past = cache.shape[3]
    kvw = 2 * HEAD_DIM
    q_blk = COL_CQ // 512
    k_blk, v_blk = COL_CK // kvw, COL_CV // kvw
    prev_row = lambda b, qi: b * n_half + jnp.maximum(2 * qi - 1, 0)
    next_row = lambda b, qi: b * n_half + jnp.minimum(2 * qi + 2, n_half - 1)
    return pl.pallas_call(
        functools.partial(_window_sample_kernel, dec_seq=dec_seq),
        out_shape=jax.ShapeDtypeStruct((t, BRANCH_WIDTH), BF16),
        grid=(nb, nq),
        in_specs=[
            pl.BlockSpec((TQ, 512), lambda b, qi: (b * nq + qi, q_blk)),
            pl.BlockSpec((TQ, 512), lambda b, qi: (b * nq + qi, q_blk + 1)),
            pl.BlockSpec((half, kvw), lambda b, qi: (prev_row(b, qi), k_blk)),
            pl.BlockSpec((TQ, kvw), lambda b, qi: (b * nq + qi, k_blk)),
            pl.BlockSpec((half, kvw), lambda b, qi: (next_row(b, qi), k_blk)),
            pl.BlockSpec((half, kvw), lambda b, qi: (prev_row(b, qi), v_blk)),
            pl.BlockSpec((TQ, kvw), lambda b, qi: (b * nq + qi, v_blk)),
            pl.BlockSpec((half, kvw), lambda b, qi: (next_row(b, qi), v_blk)),
            pl.BlockSpec((None, None, None, past, kvw), lambda b, qi: (b, l, 0, 0, 0)),
            pl.BlockSpec((None, None, None, past, kvw), lambda b, qi: (b, l, 1, 0, 0)),
            pl.BlockSpec(memory_space=pltpu.SMEM),
        ],
        out_specs=pl.BlockSpec((TQ, BRANCH_WIDTH), lambda b, qi: (b * nq + qi, 0)),
        compiler_params=_cparams(("parallel", "parallel")),
        name="window_attention",
    )(qkv, qkv, qkv, qkv, qkv, qkv, qkv, qkv, cache, cache, sink_l)


def _merge_kernel(oa_ref, ob_ref, oc_ref, g_ref, w_ref, y_ref, acc_scr):
    r = pl.program_id(2)

    def contrib(o_ref):
        return g_ref[...].astype(F32) * _dot(o_ref[...], w_ref[...].astype(BF16))

    @pl.when(r == 0)
    def _():
        acc_scr[...] = contrib(oa_ref)

    @pl.when(r == 1)
    def _():
        acc_scr[...] += contrib(ob_ref)

    @pl.when(r == 2)
    def _():
        y_ref[...] = (acc_scr[...] + contrib(oc_ref)).astype(y_ref.dtype)


def _merge_branches(o_arrays, o_blocks, gates, w_branch, l):
    t = gates.shape[0]
    tn = 1024
    nn = D_MODEL // tn
    o_specs = [pl.BlockSpec((TM, BRANCH_WIDTH), functools.partial(lambda i, n, r, blk: (i, blk), blk=blk))
               for blk in o_blocks]
    return pl.pallas_call(
        _merge_kernel,
        out_shape=jax.ShapeDtypeStruct((t, D_MODEL), BF16),
        grid=(t // TM, nn, 3),
        in_specs=o_specs + [
            pl.BlockSpec((TM, tn), lambda i, n, r: (i, r * nn + n)),
            pl.BlockSpec((None, None, BRANCH_WIDTH, tn), lambda i, n, r: (l, r, 0, n)),
        ],
        out_specs=pl.BlockSpec((TM, tn), lambda i, n, r: (i, n)),
        scratch_shapes=[pltpu.VMEM((TM, tn), F32)],
        compiler_params=_cparams(("parallel", "parallel", "arbitrary")),
        name="merge_branches",
    )(*o_arrays, gates, w_branch)


def _layer_norm_rows(v, gain, bias):
    return _norm_rows(v) * gain + bias


def _outproj_kernel(y_ref, w_ref, x_ref, g_ref, gain_ref, bias_ref, o_ref, z_scr):
    n = pl.program_id(1)
    n_blocks, _, tn = z_scr.shape
    z_scr[n] = _dot(y_ref[...], w_ref[...].astype(BF16))

    @pl.when(n == n_blocks - 1)
    def _():
        for b in range(n_blocks):
            cols = slice(b * tn, (b + 1) * tn)
            o_ref[:, cols] = ALPHA * x_ref[:, cols] + g_ref[:, cols] * z_scr[b]
        o_ref[...] = _layer_norm_rows(o_ref[...], gain_ref[...], bias_ref[...])


def _out_projection(y, w_o, l, x, mod_l, ln_gain_l, ln_bias_l, *, row_of_tile):
    t = x.shape[0]
    tn = 512
    return pl.pallas_call(
        _outproj_kernel,
        out_shape=jax.ShapeDtypeStruct((t, D_MODEL), F32),
        grid=(t // TM_SMALL, D_MODEL // tn),
        in_specs=[
            pl.BlockSpec((TM_SMALL, D_MODEL), lambda i, n: (i, 0)),
            pl.BlockSpec((None, D_MODEL, tn), lambda i, n: (l, 0, n)),
            pl.BlockSpec((TM_SMALL, D_MODEL), lambda i, n: (i, 0)),
            _mod_spec(2, row_of_tile),
            pl.BlockSpec((1, D_MODEL), lambda i, n: (0, 0)),
            pl.BlockSpec((1, D_MODEL), lambda i, n: (0, 0)),
        ],
        out_specs=pl.BlockSpec((TM_SMALL, D_MODEL), lambda i, n: (i, 0)),
        scratch_shapes=[pltpu.VMEM((D_MODEL // tn, TM_SMALL, tn), F32)],
        compiler_params=_cparams(("parallel", "arbitrary")),
        name="out_projection",
    )(y, w_o, x, mod_l, ln_gain_l, ln_bias_l)


def _route(p):
    rows = [p[e:e + 1, :] for e in range(N_EXPERTS)]
    best_score, best_group = None, None
    for g in range(N_GROUPS):
        members = rows[g * EXPERTS_PER_GROUP:(g + 1) * EXPERTS_PER_GROUP]
        score = None
        for a in range(EXPERTS_PER_GROUP):
            for b in range(a + 1, EXPERTS_PER_GROUP):
                pair = members[a] + members[b]
                score = pair if score is None else jnp.maximum(score, pair)
        if g == 0:
            best_score, best_group = score, jnp.zeros(score.shape, F32)
        else:
            better = score > best_score
            best_group = jnp.where(better, float(g), best_group)
            best_score = jnp.where(better, score, best_score)
    e_idx = lax.broadcasted_iota(jnp.int32, p.shape, 0).astype(F32)
    g_idx = jnp.floor(e_idx * (1.0 / EXPERTS_PER_GROUP))
    masked = jnp.where(g_idx == best_group, p, -1.0)
    w1 = jnp.max(masked, axis=0, keepdims=True)
    i1 = jnp.min(jnp.where(masked == w1, e_idx, float(N_EXPERTS)), axis=0, keepdims=True)
    masked2 = jnp.where(e_idx == i1, -2.0, masked)
    w2 = jnp.max(masked2, axis=0, keepdims=True)
    i2 = jnp.min(jnp.where(masked2 == w2, e_idx, float(N_EXPERTS)), axis=0, keepdims=True)
    tot = w1 + w2
    return e_idx, i1, i2, w1 / tot, w2 / tot


ROUTE_ROWS = 8
D_PACKED = D_MODEL // 2


def _pack_bf16_pairs(xb):
    half = xb.shape[1] // 2
    lo = lax.bitcast_convert_type(xb[:, :half].astype(F32), jnp.uint32)
    hi = lax.bitcast_convert_type(xb[:, half:].astype(F32), jnp.uint32)
    return (hi & jnp.uint32(0xFFFF0000)) | (lo >> 16)


def _unpack_bf16_pairs(words):
    lo = lax.bitcast_convert_type(words << 16, F32).astype(BF16)
    hi = lax.bitcast_convert_type(words & jnp.uint32(0xFFFF0000), F32).astype(BF16)
    return lo, hi


def _moe_route_kernel(x_ref, sh_ref, sc_ref, wr_ref, h_ref, rec_ref, rec_t_ref, cnt_ref, carry_scr):
    @pl.when(pl.program_id(0) == 0)
    def _():
        carry_scr[...] = jnp.zeros(carry_scr.shape, F32)

    h = _norm_rows(x_ref[...]) * (1.0 + sc_ref[...]) + sh_ref[...]
    hb = h.astype(BF16)
    h_ref[...] = _pack_bf16_pairs(hb)
    logits = _dot_nt(wr_ref[...].astype(BF16), hb)
    m = jnp.max(logits, axis=0, keepdims=True)
    e = jnp.exp(logits - m)
    probs = e / jnp.sum(e, axis=0, keepdims=True)
    e_idx, i1, i2, w1, w2 = _route(probs)
    tm = probs.shape[1]
    oh1 = (e_idx == i1).astype(F32)
    oh2 = (e_idx == i2).astype(F32)
    oh = oh1 + oh2
    earlier = (lax.broadcasted_iota(jnp.int32, (tm, tm), 0) < lax.broadcasted_iota(jnp.int32, (tm, tm), 1))
    rank = carry_scr[:, 0:1] + _dot(oh.astype(BF16), earlier.astype(BF16))
    r1 = jnp.sum(oh1 * rank, axis=0, keepdims=True)
    r2 = jnp.sum(oh2 * rank, axis=0, keepdims=True)
    carry_scr[...] = carry_scr[...] + jnp.sum(oh, axis=1, keepdims=True)
    cnt_ref[...] = carry_scr[...]
    row = lax.broadcasted_iota(jnp.int32, (ROUTE_ROWS, tm), 0)
    rec = jnp.zeros((ROUTE_ROWS, tm), F32)
    for k, v in enumerate((i1, i2, r1, r2, w1, w2)):
        rec = jnp.where(row == k, v, rec)
    rec_ref[...] = rec
    pad = jnp.zeros((LANES - ROUTE_ROWS, tm), F32)
    rec_t_ref[...] = jnp.concatenate([rec, pad], axis=0).T


def _moe_route(x, mod_l, w_router_t, *, row_of_tile):
    t = x.shape[0]
    tm = TM_SMALL
    return pl.pallas_call(
        _moe_route_kernel,
        out_shape=(jax.ShapeDtypeStruct((t, D_PACKED), jnp.uint32), jax.ShapeDtypeStruct((ROUTE_ROWS, t), F32),
                   jax.ShapeDtypeStruct((t, LANES), F32), jax.ShapeDtypeStruct((N_EXPERTS, LANES), F32)),
        grid=(t // tm,),
        in_specs=[
            pl.BlockSpec((tm, D_MODEL), lambda i: (i, 0)),
            _mod_spec(3, row_of_tile),
            _mod_spec(4, row_of_tile),
            pl.BlockSpec((N_EXPERTS, D_MODEL), lambda i: (0, 0)),
        ],
        out_specs=(pl.BlockSpec((tm, D_PACKED), lambda i: (i, 0)),
                   pl.BlockSpec((ROUTE_ROWS, tm), lambda i: (0, i)),
                   pl.BlockSpec((tm, LANES), lambda i: (i, 0)),
                   pl.BlockSpec((N_EXPERTS, LANES), lambda i: (0, 0))),
        scratch_shapes=[pltpu.VMEM((N_EXPERTS, LANES), F32)],
        compiler_params=_cparams(("arbitrary",)),
        name="moe_route",
    )(x, mod_l, mod_l, w_router_t)


ROW_DMA_UNROLL = 8


def _row_copy(src_hbm, src_row, dst_buf, dst_row, sem):
    return pltpu.make_async_copy(src_hbm.at[pl.ds(src_row, 1)], dst_buf.at[pl.ds(dst_row, 1)], sem)


def _moe_dispatch_kernel(n_tiles_ref, src_ref, h_hbm, xs_ref, x_buf, sems, *, te):
    j = pl.program_id(0)
    n_valid = n_tiles_ref[0]
    slot = j % 2

    def start_gather(tile, s):
        def body(r, carry):
            _row_copy(h_hbm, src_ref[tile * te + r], x_buf.at[s], r, sems.at[s]).start()
            return carry
        lax.fori_loop(0, te, body, 0, unroll=ROW_DMA_UNROLL)

    def wait_gather(s):
        def body(r, carry):
            _row_copy(h_hbm, 0, x_buf.at[s], r, sems.at[s]).wait()
            return carry
        lax.fori_loop(0, te, body, 0, unroll=ROW_DMA_UNROLL)

    @pl.when((j == 0) & (n_valid > 0))
    def _():
        start_gather(0, 0)

    @pl.when(j + 1 < n_valid)
    def _():
        start_gather(j + 1, 1 - slot)

    @pl.when(j < n_valid)
    def _():
        wait_gather(slot)
        xs_ref[...] = x_buf[slot]

    @pl.when(j >= n_valid)
    def _():
        xs_ref[...] = jnp.zeros(xs_ref.shape, xs_ref.dtype)


def _moe_dispatch(h, n_tiles, src, *, te):
    n_rows = src.shape[0]
    return pl.pallas_call(
        functools.partial(_moe_dispatch_kernel, te=te),
        out_shape=jax.ShapeDtypeStruct((n_rows, D_PACKED), h.dtype),
        grid_spec=pltpu.PrefetchScalarGridSpec(
            num_scalar_prefetch=2,
            grid=(n_rows // te,),
            in_specs=[pl.BlockSpec(memory_space=pltpu.HBM)],
            out_specs=pl.BlockSpec((te, D_PACKED), lambda j, *_: (j, 0)),
            scratch_shapes=[pltpu.VMEM((2, te, D_PACKED), h.dtype), pltpu.SemaphoreType.DMA((2,))],
        ),
        compiler_params=_cparams(("arbitrary",)),
        name="moe_dispatch",
    )(n_tiles, src, h)


def _moe_expert_kernel(tile_expert_ref, n_tiles_ref, xs_ref, wg_ref, wu_ref, wd_ref, ys_ref):
    del tile_expert_ref
    j = pl.program_id(0)

    @pl.when(j < n_tiles_ref[0])
    def _():
        x_lo, x_hi = _unpack_bf16_pairs(xs_ref[...])

        def project(w_ref):
            return (_dot(x_lo, w_ref[:D_PACKED, :].astype(BF16)) + _dot(x_hi, w_ref[D_PACKED:, :].astype(BF16)))

        gate = project(wg_ref)
        up = project(wu_ref)
        hid = gate / (1.0 + jnp.exp(-gate)) * up
        ys_ref[...] = _dot(hid.astype(BF16), wd_ref[...].astype(BF16))

    @pl.when(j >= n_tiles_ref[0])
    def _():
        ys_ref[...] = jnp.zeros(ys_ref.shape, F32)


def _moe_experts(xs, tile_expert, n_tiles, wg, wu, wd, l, *, te):
    n_rows = xs.shape[0]
    w_in_spec = pl.BlockSpec((None, None, D_MODEL, D_EXPERT), lambda j, te_ref, nt_ref: (l, te_ref[j], 0, 0))
    w_out_spec = pl.BlockSpec((None, None, D_EXPERT, D_MODEL), lambda j, te_ref, nt_ref: (l, te_ref[j], 0, 0))
    xs_spec = pl.BlockSpec((te, D_PACKED), lambda j, te_ref, nt_ref: (j, 0))
    return pl.pallas_call(
        _moe_expert_kernel,
        out_shape=jax.ShapeDtypeStruct((n_rows, D_MODEL), F32),
        grid_spec=pltpu.PrefetchScalarGridSpec(
            num_scalar_prefetch=2,
            grid=(n_rows // te,),
            in_specs=[xs_spec, w_in_spec, w_in_spec, w_out_spec],
            out_specs=pl.BlockSpec((te, D_MODEL), lambda j, *_: (j, 0)),
        ),
        compiler_params=_cparams(("arbitrary",)),
        name="moe_experts",
    )(tile_expert, n_tiles, xs, wg, wu, wd)


TM_COMBINE = 256


def _moe_combine_kernel(dest_ref, ys_hbm, x_ref, rec_t_ref, g_ref, gain_ref, bias_ref, o_ref, y_buf, sems,
                        *, n_tokens):
    i = pl.program_id(0)
    tm = x_ref.shape[0]
    slot = i % 2

    def start_gather(tile, s):
        def body(r, carry):
            for k in range(2):
                row = dest_ref[k * n_tokens + tile * tm + r]
                _row_copy(ys_hbm, row, y_buf.at[s, k], r, sems.at[s]).start()
            return carry
        lax.fori_loop(0, tm, body, 0, unroll=ROW_DMA_UNROLL)

    def wait_gather(s):
        def body(r, carry):
            for k in range(2):
                _row_copy(ys_hbm, 0, y_buf.at[s, k], r, sems.at[s]).wait()
            return carry
        lax.fori_loop(0, tm, body, 0, unroll=ROW_DMA_UNROLL)

    @pl.when(i == 0)
    def _():
        start_gather(0, 0)

    @pl.when(i + 1 < pl.num_programs(0))
    def _():
        start_gather(i + 1, 1 - slot)

    wait_gather(slot)
    z = rec_t_ref[:, 4:5] * y_buf[slot, 0] + rec_t_ref[:, 5:6] * y_buf[slot, 1]
    v = ALPHA * x_ref[...] + g_ref[...] * z
    o_ref[...] = _layer_norm_rows(v, gain_ref[...], bias_ref[...])


def _moe_combine(dest, ys, x, rec_t, mod_l, ln_gain_l, ln_bias_l, *, row_of_tile):
    t = x.shape[0]
    tm = TM_COMBINE
    vec = pl.BlockSpec((1, D_MODEL), lambda i, *_: (0, 0))
    return pl.pallas_call(
        functools.partial(_moe_combine_kernel, n_tokens=t),
        out_shape=jax.ShapeDtypeStruct((t, D_MODEL), F32),
        grid_spec=pltpu.PrefetchScalarGridSpec(
            num_scalar_prefetch=1,
            grid=(t // tm,),
            in_specs=[
                pl.BlockSpec(memory_space=pltpu.HBM),
                pl.BlockSpec((tm, D_MODEL), lambda i, *_: (i, 0)),
                pl.BlockSpec((tm, LANES), lambda i, *_: (i, 0)),
                _mod_spec(5, row_of_tile),
                vec, vec,
            ],
            out_specs=pl.BlockSpec((tm, D_MODEL), lambda i, *_: (i, 0)),
            scratch_shapes=[pltpu.VMEM((2, 2, tm, D_MODEL), F32), pltpu.SemaphoreType.DMA((2,))],
        ),
        compiler_params=_cparams(("arbitrary",)),
        name="moe_combine",
    )(dest, ys, x, rec_t, mod_l, ln_gain_l, ln_bias_l)


def _dispatch_tables(rec, cnt, *, te):
    t = rec.shape[1]
    n_rows = 2 * t + N_EXPERTS * te
    e12 = rec[0:2].astype(jnp.int32)
    r12 = rec[2:4].astype(jnp.int32)
    counts = cnt[:, 0].astype(jnp.int32)
    padded = (counts + te - 1) // te * te
    ends = jnp.cumsum(padded)
    offsets = ends - padded
    expert_ids = jnp.arange(N_EXPERTS, dtype=jnp.int32)[:, None, None]
    dest = (jnp.sum(jnp.where(e12[None] == expert_ids, offsets[:, None, None], 0), axis=0) + r12).reshape(2 * t)
    tokens = jnp.tile(jnp.arange(t, dtype=jnp.int32), 2)
    src = jnp.zeros((n_rows,), jnp.int32).at[dest].set(tokens, unique_indices=True)
    tile_start = jnp.arange(n_rows // te, dtype=jnp.int32) * te
    tile_expert = jnp.minimum(jnp.searchsorted(ends, tile_start, side="right"), N_EXPERTS - 1).astype(jnp.int32)
    n_tiles = (ends[-1:] // te).astype(jnp.int32)
    return dest, src, tile_expert, n_tiles


def _rope_tables(n_tokens):
    rows = n_tokens // GRID_W
    row = jnp.repeat(jnp.arange(rows), GRID_W).astype(F32)
    col = jnp.tile(jnp.arange(GRID_W), rows).astype(F32)
    quarter = HEAD_DIM // 4
    inv_freq = ROPE_THETA ** (-jnp.arange(quarter, dtype=F32) / quarter)
    ang_r, ang_c = row[:, None] * inv_freq, col[:, None] * inv_freq
    cos = jnp.concatenate([jnp.cos(ang_r), jnp.cos(ang_r), jnp.cos(ang_c), jnp.cos(ang_c)], axis=-1)
    sin = jnp.concatenate([-jnp.sin(ang_r), jnp.sin(ang_r), -jnp.sin(ang_c), jnp.sin(ang_c)], axis=-1)
    return cos, sin


def _row_of_tile(first_row, tokens_per_row):
    def for_tile(tile):
        return lambda i: first_row + (i * tile) // tokens_per_row
    return for_tile


def _mixer_and_ffn(x, attn_arrays, attn_blocks, gates, mod_l, rows, l, expert_tile, w_branch, w_o, ln_gain,
                   ln_bias, w_router_t, w_e_gate, w_e_up, w_e_down):
    y = _merge_branches(attn_arrays, attn_blocks, gates, w_branch, l)
    x = _out_projection(y, w_o, l, x, mod_l, ln_gain[l, 0:1], ln_bias[l, 0:1], row_of_tile=rows(TM_SMALL))
    h2, rec, rec_t, cnt = _moe_route(x, mod_l, w_router_t, row_of_tile=rows(TM_SMALL))
    dest, src, tile_expert, n_tiles = _dispatch_tables(rec, cnt, te=expert_tile)
    xs = _moe_dispatch(h2, n_tiles, src, te=expert_tile)
    ys = _moe_experts(xs, tile_expert, n_tiles, w_e_gate, w_e_up, w_e_down, l, te=expert_tile)
    return _moe_combine(dest, ys, x, rec_t, mod_l, ln_gain[l, 1:2], ln_bias[l, 1:2],
                        row_of_tile=rows(TM_COMBINE))


def kernel(x_prompt, x_sample, cache_kv_a, cache_kv_b, cache_kv_c, c, c_ctx, w_in, w_gate, w_branch, w_o,
           w_mod, b_mod, ln_gain, ln_bias, diff_lam, diff_subln, qk_gain, sink, w_router, w_e_gate, w_e_up,
           w_e_down):
    batch, seq, _ = x_prompt.shape
    dec_batch, dec_seq, _ = x_sample.shape
    past = cache_kv_a.shape[3]
    t_p, t_s = batch * seq, dec_batch * dec_seq

    cond = jnp.concatenate([c_ctx[None], c, jnp.zeros((MOD_ROWS - 1 - dec_batch, D_MODEL), F32)], axis=0)
    mod = _modulation(cond.T, w_mod, b_mod)
    cos_t, sin_t = _rope_tables(dec_seq)
    w_router_t = w_router.T
    cache_a = cache_kv_a.reshape(dec_batch, DEPTH, 2, past, 4 * 256)
    cache_b = cache_kv_b.reshape(dec_batch, DEPTH, 2, past, 2 * HEAD_DIM)
    cache_c = cache_kv_c.reshape(dec_batch, DEPTH, 2, past, 2 * HEAD_DIM)
    rows_p = _row_of_tile(0, t_p)
    rows_s = _row_of_tile(1, dec_seq)

    y_p = x_prompt.reshape(t_p, D_MODEL)
    y_s = x_sample.reshape(t_s, D_MODEL)
    new_a, new_b, new_c = [], [], []
    for l in range(DEPTH):
        lam_init = 0.8 - 0.6 * math.exp(-0.3 * l)
        mod_l = mod[l].reshape(MOD_ROWS, 1, N_MOD * D_MODEL)
        subln_l = diff_subln[l].reshape(1, 256)
        shared = (w_branch, w_o, ln_gain, ln_bias, w_router_t, w_e_gate, w_e_up, w_e_down)

        p = _in_projection(y_p, mod_l, w_in, l, qk_gain[l], cos_t, sin_t, row_of_tile=rows_p(TM),
                           rope=False, q_scale=1.0, out_dtype=F32)
        gates = _branch_gates(y_p, mod_l, w_gate, l, row_of_tile=rows_p(TM))
        attn = _attention_prompt(p, diff_lam[l], subln_l, sink[l], lam_init=lam_init, seq=seq)
        y_p = _mixer_and_ffn(y_p, (attn, attn, attn), (0, 1, 2), gates, mod_l, rows_p, l, 256, *shared)
        p5 = p.reshape(batch, seq, IN_COLS)
        new_a.append(jnp.stack([p5[..., COL_AK:COL_AV], p5[..., COL_AV:COL_BQ]], axis=1))
        new_b.append(jnp.stack([p5[..., COL_BK:COL_BV], p5[..., COL_BV:COL_CQ]], axis=1))
        new_c.append(jnp.stack([p5[..., COL_CK:COL_CV], p5[..., COL_CV:]], axis=1))

        qkv = _in_projection(y_s, mod_l, w_in, l, qk_gain[l], cos_t, sin_t, row_of_tile=rows_s(TM),
                             rope=True, q_scale=SCALE * LOG2E, out_dtype=BF16)
        gates = _branch_gates(y_s, mod_l, w_gate, l, row_of_tile=rows_s(TM))
        vec = lambda shape: pl.BlockSpec(shape, lambda b, qi, kk: (0, 0))
        a_o = _flash_sample(functools.partial(_diff_sample_kernel, lam_init=lam_init), qkv, cache_a, l,
                            q_col=COL_AQ, k_col=COL_AK, v_col=COL_AV, kv_width=1024, n_state=8,
                            row_sum_scratch=True, extra=(diff_lam[l], subln_l),
                            extra_specs=[vec((4, HEAD_DIM)), vec((1, 256))], name="diff_attention",
                            dec_seq=dec_seq)
        b_o = _flash_sample(_gqa_sample_kernel, qkv, cache_b, l, q_col=COL_BQ, k_col=COL_BK, v_col=COL_BV,
                            kv_width=256, n_state=8, row_sum_scratch=False, extra=(), extra_specs=[],
                            name="gqa_attention", dec_seq=dec_seq)
        c_o = _window_sample(qkv, cache_c, sink[l], l, dec_seq=dec_seq)
        y_s = _mixer_and_ffn(y_s, (a_o, b_o, c_o), (0, 0, 0), gates, mod_l, rows_s, l, 512, *shared)

    new_kv_a = jnp.stack(new_a, axis=1).reshape(batch, DEPTH, 2, seq, 4, 256)
    new_kv_b = jnp.stack(new_b, axis=1).reshape(batch, DEPTH, 2, seq, 2, HEAD_DIM)
    new_kv_c = jnp.stack(new_c, axis=1).reshape(batch, DEPTH, 2, seq, 2, HEAD_DIM)
    return (y_p.reshape(batch, seq, D_MODEL), y_s.reshape(dec_batch, dec_seq, D_MODEL),
            new_kv_a, new_kv_b, new_kv_c)
```

```python
import functools
import math

import jax
import jax.numpy as jnp
from jax import lax
from jax.experimental import pallas as pl
from jax.experimental.pallas import tpu as pltpu

F32 = jnp.float32
BF16 = jnp.bfloat16

D_MODEL = 2048
HEAD_DIM = 128
GRID_W = 64
ROPE_THETA = 10000.0
WINDOW = 128
N_EXPERTS = 16
N_GROUPS = 4
EXPERTS_PER_GROUP = N_EXPERTS // N_GROUPS
D_EXPERT = 512
N_MOD = 6
DEPTH = 2
ALPHA = (2 * DEPTH) ** 0.25
EPS = 1e-6
IN_COLS = 6144
BRANCH_WIDTH = 1024
SCALE = HEAD_DIM ** -0.5
LOG2E = math.log2(math.e)
NEG_BIG = -1e30

COL_AQ, COL_AK, COL_AV = 0, 1024, 2048
COL_BQ, COL_BK, COL_BV = 3072, 4096, 4352
COL_CQ, COL_CK, COL_CV = 4608, 5632, 5888

LANES = 128
VMEM_LIMIT = 56 * 1024 * 1024

TM = 1024
TN_PROJ = 1024
TN_PROJ_F32 = 512
TN_GATE = 1024
TM_SMALL = 512


def _cparams(sem):
    return pltpu.CompilerParams(dimension_semantics=sem, vmem_limit_bytes=VMEM_LIMIT)


def _dot(a, b):
    return jnp.dot(a, b, preferred_element_type=F32)


def _dot_nt(a, b):
    return lax.dot_general(a, b, (((1,), (1,)), ((), ())), preferred_element_type=F32)


def _norm_rows(x):
    mu = jnp.mean(x, axis=-1, keepdims=True)
    xc = x - mu
    var = jnp.mean(xc * xc, axis=-1, keepdims=True)
    return xc * lax.rsqrt(var + EPS)


def _rms(v, gain):
    ms = jnp.mean(v * v, axis=-1, keepdims=True)
    return v * lax.rsqrt(ms + EPS) * gain


def _rope(v, cos, sin_signed):
    lane = lax.broadcasted_iota(jnp.int32, v.shape, 1)
    first_half = (lane % 64) < 32
    partner = jnp.where(first_half, pltpu.roll(v, 96, 1), pltpu.roll(v, 32, 1))
    return v * cos + partner * sin_signed


N_COND = 3
MOD_ROWS = 8
TN_MOD = 1024


def _mod_kernel(cond_ref, w_ref, b_ref, o_ref):
    w = w_ref[...]
    row_idx = lax.broadcasted_iota(jnp.int32, (MOD_ROWS, TN_MOD), 0)
    out = jnp.zeros((MOD_ROWS, TN_MOD), F32)
    for r in range(N_COND):
        c = cond_ref[:, r:r + 1]
        s = c / (1.0 + jnp.exp(-c))
        m = jnp.sum(w * s, axis=0, keepdims=True) + b_ref[...]
        out = jnp.where(row_idx == r, m, out)
    o_ref[...] = out


def _modulation(cond_t, w_mod, b_mod):
    n = N_MOD * D_MODEL
    return pl.pallas_call(
        _mod_kernel,
        out_shape=jax.ShapeDtypeStruct((DEPTH, MOD_ROWS, n), F32),
        grid=(DEPTH, n // TN_MOD),
        in_specs=[
            pl.BlockSpec((D_MODEL, MOD_ROWS), lambda l, j: (0, 0)),
            pl.BlockSpec((None, D_MODEL, TN_MOD), lambda l, j: (l, 0, j)),
            pl.BlockSpec((None, 1, TN_MOD), lambda l, j: (l, 0, j)),
        ],
        out_specs=pl.BlockSpec((None, MOD_ROWS, TN_MOD), lambda l, j: (l, 0, j)),
        compiler_params=_cparams(("parallel", "parallel")),
        name="modulation",
    )(cond_t, w_mod, b_mod.reshape(DEPTH, 1, n))


def _modulate_to_scratch(x_ref, sh_ref, sc_ref, h_scr):
    h = _norm_rows(x_ref[...]) * (1.0 + sc_ref[...]) + sh_ref[...]
    h_scr[...] = h.astype(BF16)


def _head_chunk_kinds():
    kinds = []
    for n_chunks, kind in ((8, (None, True, True)), (8, (None, True, False)), (8, (None, False, False)),
                           (8, ("q", True, True)), (2, ("k", True, False)), (2, (None, False, False)),
                           (8, (None, True, True)), (2, (None, True, False)), (2, (None, False, False))):
        kinds.extend([kind] * n_chunks)
    return kinds


_HEAD_CHUNK_KINDS = _head_chunk_kinds()


def _inproj_kernel(x_ref, sh_ref, sc_ref, w_ref, qk_gain_ref, cos_ref, sin_ref, o_ref, h_scr, *, rope,
                   q_scale):
    j = pl.program_id(1)
    gain_q = qk_gain_ref[0:1, :]
    gain_k = qk_gain_ref[1:2, :]

    @pl.when(j == 0)
    def _():
        _modulate_to_scratch(x_ref, sh_ref, sc_ref, h_scr)

    acc = _dot(h_scr[...], w_ref[...].astype(BF16))
    tn = o_ref.shape[1]
    n_chunks = tn // LANES

    def chunk(c):
        return acc[:, c * LANES:(c + 1) * LANES]

    def store(c, v):
        o_ref[:, c * LANES:(c + 1) * LANES] = v.astype(o_ref.dtype)

    def rp(v):
        return _rope(v, cos_ref[...], sin_ref[...]) if rope else v

    def qs(v):
        return v * q_scale if q_scale != 1.0 else v

    def finish(c, kind):
        norm, rotate, is_query = kind
        v = chunk(c)
        if norm is not None:
            v = _rms(v, gain_q if norm == "q" else gain_k)
        if rotate:
            v = rp(v)
        store(c, qs(v) if is_query else v)

    tile_kinds = [tuple(_HEAD_CHUNK_KINDS[t * n_chunks:(t + 1) * n_chunks]) for t in range(IN_COLS // tn)]
    for kinds in dict.fromkeys(tile_kinds):
        tiles = [t for t, k in enumerate(tile_kinds) if k == kinds]
        cond = j == tiles[0]
        for t in tiles[1:]:
            cond = cond | (j == t)

        @pl.when(cond)
        def _(kinds=kinds):
            for c, kind in enumerate(kinds):
                finish(c, kind)


def _gate_kernel(x_ref, sh_ref, sc_ref, w_ref, o_ref, h_scr):
    @pl.when(pl.program_id(1) == 0)
    def _():
        _modulate_to_scratch(x_ref, sh_ref, sc_ref, h_scr)

    acc = _dot(h_scr[...], w_ref[...].astype(BF16))
    o_ref[...] = (1.0 / (1.0 + jnp.exp(-acc))).astype(o_ref.dtype)


def _mod_spec(which, row_of_tile):
    return pl.BlockSpec((None, 1, D_MODEL), lambda i, *_: (row_of_tile(i), 0, which))


def _in_projection(x, mod_l, w_in, l, qk_gain_l, cos_t, sin_t, *, row_of_tile, rope, q_scale, out_dtype):
    t = x.shape[0]
    tiles_per_seq = cos_t.shape[0] // TM
    tn = TN_PROJ_F32 if out_dtype == F32 else TN_PROJ
    return pl.pallas_call(
        functools.partial(_inproj_kernel, rope=rope, q_scale=q_scale),
        out_shape=jax.ShapeDtypeStruct((t, IN_COLS), out_dtype),
        grid=(t // TM, IN_COLS // tn),
        in_specs=[
            pl.BlockSpec((TM, D_MODEL), lambda i, j: (i, 0)),
            _mod_spec(0, row_of_tile),
            _mod_spec(1, row_of_tile),
            pl.BlockSpec((None, D_MODEL, tn), lambda i, j: (l, 0, j)),
            pl.BlockSpec((2, HEAD_DIM), lambda i, j: (0, 0)),
            pl.BlockSpec((TM, HEAD_DIM), lambda i, j: (i % tiles_per_seq, 0)),
            pl.BlockSpec((TM, HEAD_DIM), lambda i, j: (i % tiles_per_seq, 0)),
        ],
        out_specs=pl.BlockSpec((TM, tn), lambda i, j: (i, j)),
        scratch_shapes=[pltpu.VMEM((TM, D_MODEL), BF16)],
        compiler_params=_cparams(("parallel", "arbitrary")),
        name="in_projection",
    )(x, mod_l, mod_l, w_in, qk_gain_l, cos_t, sin_t)


def _branch_gates(x, mod_l, w_gate, l, *, row_of_tile):
    t = x.shape[0]
    n = w_gate.shape[2]
    return pl.pallas_call(
        _gate_kernel,
        out_shape=jax.ShapeDtypeStruct((t, n), BF16),
        grid=(t // TM, n // TN_GATE),
        in_specs=[
            pl.BlockSpec((TM, D_MODEL), lambda i, j: (i, 0)),
            _mod_spec(0, row_of_tile),
            _mod_spec(1, row_of_tile),
            pl.BlockSpec((None, D_MODEL, TN_GATE), lambda i, j: (l, 0, j)),
        ],
        out_specs=pl.BlockSpec((TM, TN_GATE), lambda i, j: (i, j)),
        scratch_shapes=[pltpu.VMEM((TM, D_MODEL), BF16)],
        compiler_params=_cparams(("parallel", "arbitrary")),
        name="branch_gates",
    )(x, mod_l, mod_l, w_gate)


def _diff_lambda(lam_ref, lam_init):
    lp = lam_ref[...]
    t1 = jnp.sum(lp[0:1] * lp[1:2], axis=-1, keepdims=True)
    t2 = jnp.sum(lp[2:3] * lp[3:4], axis=-1, keepdims=True)
    return jnp.exp(t1) - jnp.exp(t2) + lam_init


def _softmax_rows(s, sink=None):
    m = jnp.max(s, axis=-1, keepdims=True)
    if sink is not None:
        m = jnp.maximum(m, sink)
    e = jnp.exp(s - m)
    den = jnp.sum(e, axis=-1, keepdims=True)
    if sink is not None:
        den = den + jnp.exp(sink - m)
    return e * (1.0 / den)


def _attn_prompt_kernel(p_ref, lam_ref, subln_ref, sink_ref, o_ref, *, lam_init):
    lam = _diff_lambda(lam_ref, lam_init)

    def blk(c0, w):
        return p_ref[:, c0:c0 + w].astype(BF16)

    for h in range(4):
        probs = []
        for m in range(2):
            q = blk(COL_AQ + h * 256 + m * HEAD_DIM, HEAD_DIM)
            k = blk(COL_AK + h * 256 + m * HEAD_DIM, HEAD_DIM)
            probs.append(_softmax_rows(_dot_nt(q, k) * SCALE))
        w = probs[0] - lam * probs[1]
        o = _dot(w.astype(BF16), blk(COL_AV + h * 256, 256))
        o = _rms(o, subln_ref[...]) * (1.0 - lam_init)
        o_ref[:, h * 256:(h + 1) * 256] = o.astype(o_ref.dtype)

    for mixer, (cq, ck, cv) in enumerate(((COL_BQ, COL_BK, COL_BV), (COL_CQ, COL_CK, COL_CV))):
        for kvh in range(2):
            k = blk(ck + kvh * HEAD_DIM, HEAD_DIM)
            v = blk(cv + kvh * HEAD_DIM, HEAD_DIM)
            for g in range(4):
                hq = kvh * 4 + g
                q = blk(cq + hq * HEAD_DIM, HEAD_DIM)
                sink = sink_ref[hq] if mixer == 1 else None
                p = _softmax_rows(_dot_nt(q, k) * SCALE, sink)
                o = _dot(p.astype(BF16), v)
                c0 = BRANCH_WIDTH * (1 + mixer) + hq * HEAD_DIM
                o_ref[:, c0:c0 + HEAD_DIM] = o.astype(o_ref.dtype)


def _attention_prompt(p, diff_lam_l, subln_l, sink_l, *, lam_init, seq):
    t = p.shape[0]
    return pl.pallas_call(
        functools.partial(_attn_prompt_kernel, lam_init=lam_init),
        out_shape=jax.ShapeDtypeStruct((t, 3 * BRANCH_WIDTH), BF16),
        grid=(t // seq,),
        in_specs=[
            pl.BlockSpec((seq, IN_COLS), lambda b: (b, 0)),
            pl.BlockSpec((4, HEAD_DIM), lambda b: (0, 0)),
            pl.BlockSpec((1, 256), lambda b: (0, 0)),
            pl.BlockSpec(memory_space=pltpu.SMEM),
        ],
        out_specs=pl.BlockSpec((seq, 3 * BRANCH_WIDTH), lambda b: (b, 0)),
        compiler_params=_cparams(("parallel",)),
        name="attention_prompt",
    )(p, diff_lam_l, subln_l, sink_l)


TQ = 512
TK = 1024


def _tile_lanes(v, n):
    return jnp.concatenate([v] * n, axis=-1) if n > 1 else v


def _online_softmax(idx, s, m_scr):
    m_prev = m_scr[idx]
    m_new = jnp.maximum(m_prev, jnp.max(s, axis=-1, keepdims=True))
    m_scr[idx] = m_new
    alpha = jnp.exp2(m_prev - m_new)
    p = jnp.exp2(s - _tile_lanes(m_new, s.shape[1] // LANES))
    return p, alpha


def _diff_sample_kernel(q_ref, k_ref, v_ref, kc_ref, vc_ref, lam_ref, subln_ref, o_ref,
                        m_scr, l_scr, acc_scr, *, lam_init):
    kk = pl.program_id(2)

    @pl.when(kk == 0)
    def _():
        m_scr[...] = jnp.full(m_scr.shape, NEG_BIG, F32)
        l_scr[...] = jnp.zeros(l_scr.shape, F32)
        acc_scr[...] = jnp.zeros(acc_scr.shape, F32)

    def process(kb_ref, vb_ref):
        for h in range(4):
            v = vb_ref[:, h * 256:(h + 1) * 256].astype(BF16)
            for m in range(2):
                c0 = h * 256 + m * HEAD_DIM
                idx = h * 2 + m
                k = kb_ref[:, c0:c0 + HEAD_DIM].astype(BF16)
                p, alpha = _online_softmax(idx, _dot_nt(q_ref[:, c0:c0 + HEAD_DIM], k), m_scr)
                part = p[:, 0:LANES]
                for c in range(1, p.shape[1] // LANES):
                    part = part + p[:, c * LANES:(c + 1) * LANES]
                l_scr[idx] = alpha * l_scr[idx] + part
                acc_scr[idx] = _tile_lanes(alpha, 2) * acc_scr[idx] + _dot(p.astype(BF16), v)

    @pl.when(kk == 0)
    def _():
        process(kc_ref, vc_ref)

    @pl.when(kk > 0)
    def _():
        process(k_ref, v_ref)

    @pl.when(kk == pl.num_programs(2) - 1)
    def _():
        lam = _diff_lambda(lam_ref, lam_init)
        for h in range(4):
            l1 = jnp.sum(l_scr[2 * h], axis=-1, keepdims=True)
            l2 = jnp.sum(l_scr[2 * h + 1], axis=-1, keepdims=True)
            o1 = acc_scr[2 * h] * (1.0 / l1)
            o2 = acc_scr[2 * h + 1] * (1.0 / l2)
            o = _rms(o1 - lam * o2, subln_ref[...]) * (1.0 - lam_init)
            o_ref[:, h * 256:(h + 1) * 256] = o.astype(o_ref.dtype)


def _gqa_sample_kernel(q_ref, k_ref, v_ref, kc_ref, vc_ref, o_ref, m_scr, acc_scr):
    kk = pl.program_id(2)

    @pl.when(kk == 0)
    def _():
        m_scr[...] = jnp.full(m_scr.shape, NEG_BIG, F32)
        acc_scr[...] = jnp.zeros(acc_scr.shape, F32)

    def process(kb_ref, vb_ref):
        for kvh in range(2):
            k = kb_ref[:, kvh * HEAD_DIM:(kvh + 1) * HEAD_DIM].astype(BF16)
            v = vb_ref[:, kvh * HEAD_DIM:(kvh + 1) * HEAD_DIM].astype(BF16)
            v_ones = jnp.concatenate([v, jnp.ones_like(v)], axis=-1)
            for g in range(4):
                hq = kvh * 4 + g
                s = _dot_nt(q_ref[:, hq * HEAD_DIM:(hq + 1) * HEAD_DIM], k)
                p, alpha = _online_softmax(hq, s, m_scr)
                acc_scr[hq] = _tile_lanes(alpha, 2) * acc_scr[hq] + _dot(p.astype(BF16), v_ones)

    @pl.when(kk == 0)
    def _():
        process(kc_ref, vc_ref)

    @pl.when(kk > 0)
    def _():
        process(k_ref, v_ref)

    @pl.when(kk == pl.num_programs(2) - 1)
    def _():
        for hq in range(8):
            o = acc_scr[hq, :, 0:HEAD_DIM] / acc_scr[hq, :, HEAD_DIM:2 * HEAD_DIM]
            o_ref[:, hq * HEAD_DIM:(hq + 1) * HEAD_DIM] = o.astype(o_ref.dtype)


def _flash_sample(kernel, qkv, cache, l, *, q_col, k_col, v_col, kv_width, n_state, row_sum_scratch, extra,
                  extra_specs, name, dec_seq):
    t = qkv.shape[0]
    nb = t // dec_seq
    nq = dec_seq // TQ
    nk = dec_seq // TK
    past = cache.shape[3]
    q_blk, k_blk, v_blk = q_col // BRANCH_WIDTH, k_col // kv_width, v_col // kv_width
    kv_row = lambda b, qi, kk: b * nk + jnp.maximum(kk - 1, 0)
    return pl.pallas_call(
        kernel,
        out_shape=jax.ShapeDtypeStruct((t, BRANCH_WIDTH), BF16),
        grid=(nb, nq, nk + 1),
        in_specs=[
            pl.BlockSpec((TQ, BRANCH_WIDTH), lambda b, qi, kk: (b * nq + qi, q_blk)),
            pl.BlockSpec((TK, kv_width), lambda b, qi, kk: (kv_row(b, qi, kk), k_blk)),
            pl.BlockSpec((TK, kv_width), lambda b, qi, kk: (kv_row(b, qi, kk), v_blk)),
            pl.BlockSpec((None, None, None, past, kv_width), lambda b, qi, kk: (b, l, 0, 0, 0)),
            pl.BlockSpec((None, None, None, past, kv_width), lambda b, qi, kk: (b, l, 1, 0, 0)),
        ] + extra_specs,
        out_specs=pl.BlockSpec((TQ, BRANCH_WIDTH), lambda b, qi, kk: (b * nq + qi, 0)),
        scratch_shapes=[pltpu.VMEM((n_state, TQ, LANES), F32)] * (2 if row_sum_scratch else 1)
        + [pltpu.VMEM((n_state, TQ, 2 * LANES), F32)],
        compiler_params=_cparams(("parallel", "parallel", "arbitrary")),
        name=name,
    )(qkv, qkv, qkv, cache, cache, *extra)


def _window_sample_kernel(q0_ref, q1_ref, kp_ref, kc_ref, kn_ref, vp_ref, vc_ref, vn_ref, kctx_ref, vctx_ref,
                          sink_ref, o_ref, *, dec_seq):
    qi = pl.program_id(1)
    q_start = qi * TQ
    half = TQ // 2
    q_pos = q_start + lax.broadcasted_iota(jnp.int32, (TQ, 1), 0)
    segs = ((kp_ref, vp_ref, q_start - half, half), (kc_ref, vc_ref, q_start, TQ),
            (kn_ref, vn_ref, q_start + TQ, half))
    valid = []
    for _, _, start, n in segs:
        k_pos = start + lax.broadcasted_iota(jnp.int32, (TQ, n), 1)
        valid.append((jnp.abs(q_pos - k_pos) <= WINDOW) & (k_pos >= 0) & (k_pos < dec_seq))
    for kvh in range(2):
        q_ref = q0_ref if kvh == 0 else q1_ref
        lo, hi = kvh * HEAD_DIM, (kvh + 1) * HEAD_DIM
        kctx = kctx_ref[:, lo:hi].astype(BF16)
        vctx = vctx_ref[:, lo:hi].astype(BF16)
        for g in range(4):
            hq = kvh * 4 + g
            q = q_ref[:, g * HEAD_DIM:(g + 1) * HEAD_DIM]
            sink = sink_ref[hq] * LOG2E
            scores = [_dot_nt(q, kctx)]
            for (k_ref, _, _, _), ok in zip(segs, valid):
                scores.append(jnp.where(ok, _dot_nt(q, k_ref[:, lo:hi]), NEG_BIG))
            m = jnp.maximum(scores[0].max(axis=-1, keepdims=True), sink)
            for s in scores[1:]:
                m = jnp.maximum(m, s.max(axis=-1, keepdims=True))
            es = [jnp.exp2(s - m) for s in scores]
            den = jnp.exp2(sink - m)
            for e in es:
                den = den + jnp.sum(e, axis=-1, keepdims=True)
            o = _dot(es[0].astype(BF16), vctx)
            for e, (_, v_ref, _, _) in zip(es[1:], segs):
                o = o + _dot(e.astype(BF16), v_ref[:, lo:hi])
            o = o * (1.0 / den)
            o_ref[:, hq * HEAD_DIM:(hq + 1) * HEAD_DIM] = o.astype(o_ref.dtype)


def _window_sample(qkv, cache, sink_l, l, *, dec_seq):
    t = qkv.shape[0]
    nb = t // dec_seq
    nq = dec_seq // TQ
    half = TQ // 2
    n_half = dec_seq // half
    past = cache.shape[3]
    kvw = 2 * HEAD_DIM
    q_blk = COL_CQ // 512
    k_blk, v_blk = COL_CK // kvw, COL_CV // kvw
    prev_row = lambda b, qi: b * n_half + jnp.maximum(2 * qi - 1, 0)
    next_row = lambda b, qi: b * n_half + jnp.minimum(2 * qi + 2, n_half - 1)
    return pl.pallas_call(
        functools.partial(_window_sample_kernel, dec_seq=dec_seq),
        out_shape=jax.ShapeDtypeStruct((t, BRANCH_WIDTH), BF16),
        grid=(nb, nq),
        in_specs=[
            pl.BlockSpec((TQ, 512), lambda b, qi: (b * nq + qi, q_blk)),
            pl.BlockSpec((TQ, 512), lambda b, qi: (b * nq + qi, q_blk + 1)),
            pl.BlockSpec((half, kvw), lambda b, qi: (prev_row(b, qi), k_blk)),
            pl.BlockSpec((TQ, kvw), lambda b, qi: (b * nq + qi, k_blk)),
            pl.BlockSpec((half, kvw), lambda b, qi: (next_row(b, qi), k_blk)),
            pl.BlockSpec((half, kvw), lambda b, qi: (prev_row(b, qi), v_blk)),
            pl.BlockSpec((TQ, kvw), lambda b, qi: (b * nq + qi, v_blk)),
            pl.BlockSpec((half, kvw), lambda b, qi: (next_row(b, qi), v_blk)),
            pl.BlockSpec((None, None, None, past, kvw), lambda b, qi: (b, l, 0, 0, 0)),
            pl.BlockSpec((None, None, None, past, kvw), lambda b, qi: (b, l, 1, 0, 0)),
            pl.BlockSpec(memory_space=pltpu.SMEM),
        ],
        out_specs=pl.BlockSpec((TQ, BRANCH_WIDTH), lambda b, qi: (b * nq + qi, 0)),
        compiler_params=_cparams(("parallel", "parallel")),
        name="window_attention",
    )(qkv, qkv, qkv, qkv, qkv, qkv, qkv, qkv, cache, cache, sink_l)


def _merge_kernel(oa_ref, ob_ref, oc_ref, g_ref, w_ref, y_ref, acc_scr):
    r = pl.program_id(2)

    def contrib(o_ref):
        return g_ref[...].astype(F32) * _dot(o_ref[...], w_ref[...].astype(BF16))

    @pl.when(r == 0)
    def _():
        acc_scr[...] = contrib(oa_ref)

    @pl.when(r == 1)
    def _():
        acc_scr[...] += contrib(ob_ref)

    @pl.when(r == 2)
    def _():
        y_ref[...] = (acc_scr[...] + contrib(oc_ref)).astype(y_ref.dtype)


def _merge_branches(o_arrays, o_blocks, gates, w_branch, l):
    t = gates.shape[0]
    tn = 1024
    nn = D_MODEL // tn
    o_specs = [pl.BlockSpec((TM, BRANCH_WIDTH), functools.partial(lambda i, n, r, blk: (i, blk), blk=blk))
               for blk in o_blocks]
    return pl.pallas_call(
        _merge_kernel,
        out_shape=jax.ShapeDtypeStruct((t, D_MODEL), BF16),
        grid=(t // TM, nn, 3),
        in_specs=o_specs + [
            pl.BlockSpec((TM, tn), lambda i, n, r: (i, r * nn + n)),
            pl.BlockSpec((None, None, BRANCH_WIDTH, tn), lambda i, n, r: (l, r, 0, n)),
        ],
        out_specs=pl.BlockSpec((TM, tn), lambda i, n, r: (i, n)),
        scratch_shapes=[pltpu.VMEM((TM, tn), F32)],
        compiler_params=_cparams(("parallel", "parallel", "arbitrary")),
        name="merge_branches",
    )(*o_arrays, gates, w_branch)


def _layer_norm_rows(v, gain, bias):
    return _norm_rows(v) * gain + bias


def _outproj_kernel(y_ref, w_ref, x_ref, g_ref, gain_ref, bias_ref, o_ref, z_scr):
    n = pl.program_id(1)
    n_blocks, _, tn = z_scr.shape
    z_scr[n] = _dot(y_ref[...], w_ref[...].astype(BF16))

    @pl.when(n == n_blocks - 1)
    def _():
        for b in range(n_blocks):
            cols = slice(b * tn, (b + 1) * tn)
            o_ref[:, cols] = ALPHA * x_ref[:, cols] + g_ref[:, cols] * z_scr[b]
        o_ref[...] = _layer_norm_rows(o_ref[...], gain_ref[...], bias_ref[...])


def _out_projection(y, w_o, l, x, mod_l, ln_gain_l, ln_bias_l, *, row_of_tile):
    t = x.shape[0]
    tn = 512
    return pl.pallas_call(
        _outproj_kernel,
        out_shape=jax.ShapeDtypeStruct((t, D_MODEL), F32),
        grid=(t // TM_SMALL, D_MODEL // tn),
        in_specs=[
            pl.BlockSpec((TM_SMALL, D_MODEL), lambda i, n: (i, 0)),
            pl.BlockSpec((None, D_MODEL, tn), lambda i, n: (l, 0, n)),
            pl.BlockSpec((TM_SMALL, D_MODEL), lambda i, n: (i, 0)),
            _mod_spec(2, row_of_tile),
            pl.BlockSpec((1, D_MODEL), lambda i, n: (0, 0)),
            pl.BlockSpec((1, D_MODEL), lambda i, n: (0, 0)),
        ],
        out_specs=pl.BlockSpec((TM_SMALL, D_MODEL), lambda i, n: (i, 0)),
        scratch_shapes=[pltpu.VMEM((D_MODEL // tn, TM_SMALL, tn), F32)],
        compiler_params=_cparams(("parallel", "arbitrary")),
        name="out_projection",
    )(y, w_o, x, mod_l, ln_gain_l, ln_bias_l)


def _route(p):
    rows = [p[e:e + 1, :] for e in range(N_EXPERTS)]
    best_score, best_group = None, None
    for g in range(N_GROUPS):
        members = rows[g * EXPERTS_PER_GROUP:(g + 1) * EXPERTS_PER_GROUP]
        score = None
        for a in range(EXPERTS_PER_GROUP):
            for b in range(a + 1, EXPERTS_PER_GROUP):
                pair = members[a] + members[b]
                score = pair if score is None else jnp.maximum(score, pair)
        if g == 0:
            best_score, best_group = score, jnp.zeros(score.shape, F32)
        else:
            better = score > best_score
            best_group = jnp.where(better, float(g), best_group)
            best_score = jnp.where(better, score, best_score)
    e_idx = lax.broadcasted_iota(jnp.int32, p.shape, 0).astype(F32)
    g_idx = jnp.floor(e_idx * (1.0 / EXPERTS_PER_GROUP))
    masked = jnp.where(g_idx == best_group, p, -1.0)
    w1 = jnp.max(masked, axis=0, keepdims=True)
    i1 = jnp.min(jnp.where(masked == w1, e_idx, float(N_EXPERTS)), axis=0, keepdims=True)
    masked2 = jnp.where(e_idx == i1, -2.0, masked)
    w2 = jnp.max(masked2, axis=0, keepdims=True)
    i2 = jnp.min(jnp.where(masked2 == w2, e_idx, float(N_EXPERTS)), axis=0, keepdims=True)
    tot = w1 + w2
    return e_idx, i1, i2, w1 / tot, w2 / tot


ROUTE_ROWS = 8
D_PACKED = D_MODEL // 2


def _pack_bf16_pairs(xb):
    half = xb.shape[1] // 2
    lo = lax.bitcast_convert_type(xb[:, :half].astype(F32), jnp.uint32)
    hi = lax.bitcast_convert_type(xb[:, half:].astype(F32), jnp.uint32)
    return (hi & jnp.uint32(0xFFFF0000)) | (lo >> 16)


def _unpack_bf16_pairs(words):
    lo = lax.bitcast_convert_type(words << 16, F32).astype(BF16)
    hi = lax.bitcast_convert_type(words & jnp.uint32(0xFFFF0000), F32).astype(BF16)
    return lo, hi


def _moe_route_kernel(x_ref, sh_ref, sc_ref, wr_ref, h_ref, rec_ref, rec_t_ref, cnt_ref, carry_scr):
    @pl.when(pl.program_id(0) == 0)
    def _():
        carry_scr[...] = jnp.zeros(carry_scr.shape, F32)

    h = _norm_rows(x_ref[...]) * (1.0 + sc_ref[...]) + sh_ref[...]
    hb = h.astype(BF16)
    h_ref[...] = _pack_bf16_pairs(hb)
    logits = _dot_nt(wr_ref[...].astype(BF16), hb)
    m = jnp.max(logits, axis=0, keepdims=True)
    e = jnp.exp(logits - m)
    probs = e / jnp.sum(e, axis=0, keepdims=True)
    e_idx, i1, i2, w1, w2 = _route(probs)
    tm = probs.shape[1]
    oh1 = (e_idx == i1).astype(F32)
    oh2 = (e_idx == i2).astype(F32)
    oh = oh1 + oh2
    earlier = (lax.broadcasted_iota(jnp.int32, (tm, tm), 0) < lax.broadcasted_iota(jnp.int32, (tm, tm), 1))
    rank = carry_scr[:, 0:1] + _dot(oh.astype(BF16), earlier.astype(BF16))
    r1 = jnp.sum(oh1 * rank, axis=0, keepdims=True)
    r2 = jnp.sum(oh2 * rank, axis=0, keepdims=True)
    carry_scr[...] = carry_scr[...] + jnp.sum(oh, axis=1, keepdims=True)
    cnt_ref[...] = carry_scr[...]
    row = lax.broadcasted_iota(jnp.int32, (ROUTE_ROWS, tm), 0)
    rec = jnp.zeros((ROUTE_ROWS, tm), F32)
    for k, v in enumerate((i1, i2, r1, r2, w1, w2)):
        rec = jnp.where(row == k, v, rec)
    rec_ref[...] = rec
    pad = jnp.zeros((LANES - ROUTE_ROWS, tm), F32)
    rec_t_ref[...] = jnp.concatenate([rec, pad], axis=0).T


def _moe_route(x, mod_l, w_router_t, *, row_of_tile):
    t = x.shape[0]
    tm = TM_SMALL
    return pl.pallas_call(
        _moe_route_kernel,
        out_shape=(jax.ShapeDtypeStruct((t, D_PACKED), jnp.uint32), jax.ShapeDtypeStruct((ROUTE_ROWS, t), F32),
                   jax.ShapeDtypeStruct((t, LANES), F32), jax.ShapeDtypeStruct((N_EXPERTS, LANES), F32)),
        grid=(t // tm,),
        in_specs=[
            pl.BlockSpec((tm, D_MODEL), lambda i: (i, 0)),
            _mod_spec(3, row_of_tile),
            _mod_spec(4, row_of_tile),
            pl.BlockSpec((N_EXPERTS, D_MODEL), lambda i: (0, 0)),
        ],
        out_specs=(pl.BlockSpec((tm, D_PACKED), lambda i: (i, 0)),
                   pl.BlockSpec((ROUTE_ROWS, tm), lambda i: (0, i)),
                   pl.BlockSpec((tm, LANES), lambda i: (i, 0)),
                   pl.BlockSpec((N_EXPERTS, LANES), lambda i: (0, 0))),
        scratch_shapes=[pltpu.VMEM((N_EXPERTS, LANES), F32)],
        compiler_params=_cparams(("arbitrary",)),
        name="moe_route",
    )(x, mod_l, mod_l, w_router_t)


ROW_DMA_UNROLL = 8
GATHER_ORDER_STRIDE = 37


def _row_copy(src_hbm, src_row, dst_buf, dst_row, sem):
    return pltpu.make_async_copy(src_hbm.at[pl.ds(src_row, 1)], dst_buf.at[pl.ds(dst_row, 1)], sem)


def _moe_dispatch_kernel(n_tiles_ref, src_ref, h_hbm, xs_ref, x_buf, sems, *, te):
    j = pl.program_id(0)
    n_valid = n_tiles_ref[0]
    slot = j % 2

    def start_gather(tile, s):
        def body(r, carry):
            row = (r * GATHER_ORDER_STRIDE) % te
            _row_copy(h_hbm, src_ref[tile * te + row], x_buf.at[s], row, sems.at[s]).start()
            return carry
        lax.fori_loop(0, te, body, 0, unroll=ROW_DMA_UNROLL)

    def wait_gather(s):
        def body(r, carry):
            _row_copy(h_hbm, 0, x_buf.at[s], r, sems.at[s]).wait()
            return carry
        lax.fori_loop(0, te, body, 0, unroll=ROW_DMA_UNROLL)

    @pl.when((j == 0) & (n_valid > 0))
    def _():
        start_gather(0, 0)

    @pl.when(j + 1 < n_valid)
    def _():
        start_gather(j + 1, 1 - slot)

    @pl.when(j < n_valid)
    def _():
        wait_gather(slot)
        xs_ref[...] = x_buf[slot]

    @pl.when(j >= n_valid)
    def _():
        xs_ref[...] = jnp.zeros(xs_ref.shape, xs_ref.dtype)


def _moe_dispatch(h, n_tiles, src, *, te):
    n_rows = src.shape[0]
    return pl.pallas_call(
        functools.partial(_moe_dispatch_kernel, te=te),
        out_shape=jax.ShapeDtypeStruct((n_rows, D_PACKED), h.dtype),
        grid_spec=pltpu.PrefetchScalarGridSpec(
            num_scalar_prefetch=2,
            grid=(n_rows // te,),
            in_specs=[pl.BlockSpec(memory_space=pltpu.HBM)],
            out_specs=pl.BlockSpec((te, D_PACKED), lambda j, *_: (j, 0)),
            scratch_shapes=[pltpu.VMEM((2, te, D_PACKED), h.dtype), pltpu.SemaphoreType.DMA((2,))],
        ),
        compiler_params=_cparams(("arbitrary",)),
        name="moe_dispatch",
    )(n_tiles, src, h)


def _moe_expert_kernel(tile_expert_ref, n_tiles_ref, xs_ref, wg_ref, wu_ref, wd_ref, ys_ref):
    del tile_expert_ref
    j = pl.program_id(0)

    @pl.when(j < n_tiles_ref[0])
    def _():
        x_lo, x_hi = _unpack_bf16_pairs(xs_ref[...])

        def project(w_ref):
            return (_dot(x_lo, w_ref[:D_PACKED, :].astype(BF16)) + _dot(x_hi, w_ref[D_PACKED:, :].astype(BF16)))

        gate = project(wg_ref)
        up = project(wu_ref)
        hid = gate / (1.0 + jnp.exp(-gate)) * up
        ys_ref[...] = _dot(hid.astype(BF16), wd_ref[...].astype(BF16))

    @pl.when(j >= n_tiles_ref[0])
    def _():
        ys_ref[...] = jnp.zeros(ys_ref.shape, F32)


def _moe_experts(xs, tile_expert, n_tiles, wg, wu, wd, l, *, te):
    n_rows = xs.shape[0]
    w_in_spec = pl.BlockSpec((None, None, D_MODEL, D_EXPERT), lambda j, te_ref, nt_ref: (l, te_ref[j], 0, 0))
    w_out_spec = pl.BlockSpec((None, None, D_EXPERT, D_MODEL), lambda j, te_ref, nt_ref: (l, te_ref[j], 0, 0))
    xs_spec = pl.BlockSpec((te, D_PACKED), lambda j, te_ref, nt_ref: (j, 0))
    return pl.pallas_call(
        _moe_expert_kernel,
        out_shape=jax.ShapeDtypeStruct((n_rows, D_MODEL), F32),
        grid_spec=pltpu.PrefetchScalarGridSpec(
            num_scalar_prefetch=2,
            grid=(n_rows // te,),
            in_specs=[xs_spec, w_in_spec, w_in_spec, w_out_spec],
            out_specs=pl.BlockSpec((te, D_MODEL), lambda j, *_: (j, 0)),
        ),
        compiler_params=_cparams(("arbitrary",)),
        name="moe_experts",
    )(tile_expert, n_tiles, xs, wg, wu, wd)


TM_COMBINE = 256


def _moe_combine_kernel(dest_ref, ys_hbm, x_ref, rec_t_ref, g_ref, gain_ref, bias_ref, o_ref, y_buf, sems,
                        *, n_tokens):
    i = pl.program_id(0)
    tm = x_ref.shape[0]
    slot = i % 2

    def start_gather(tile, s):
        def body(r, carry):
            for k in range(2):
                row = dest_ref[k * n_tokens + tile * tm + r]
                _row_copy(ys_hbm, row, y_buf.at[s, k], r, sems.at[s]).start()
            return carry
        lax.fori_loop(0, tm, body, 0, unroll=ROW_DMA_UNROLL)

    def wait_gather(s):
        def body(r, carry):
            for k in range(2):
                _row_copy(ys_hbm, 0, y_buf.at[s, k], r, sems.at[s]).wait()
            return carry
        lax.fori_loop(0, tm, body, 0, unroll=ROW_DMA_UNROLL)

    @pl.when(i == 0)
    def _():
        start_gather(0, 0)

    @pl.when(i + 1 < pl.num_programs(0))
    def _():
        start_gather(i + 1, 1 - slot)

    wait_gather(slot)
    z = rec_t_ref[:, 4:5] * y_buf[slot, 0] + rec_t_ref[:, 5:6] * y_buf[slot, 1]
    v = ALPHA * x_ref[...] + g_ref[...] * z
    o_ref[...] = _layer_norm_rows(v, gain_ref[...], bias_ref[...])


def _moe_combine(dest, ys, x, rec_t, mod_l, ln_gain_l, ln_bias_l, *, row_of_tile):
    t = x.shape[0]
    tm = TM_COMBINE
    vec = pl.BlockSpec((1, D_MODEL), lambda i, *_: (0, 0))
    return pl.pallas_call(
        functools.partial(_moe_combine_kernel, n_tokens=t),
        out_shape=jax.ShapeDtypeStruct((t, D_MODEL), F32),
        grid_spec=pltpu.PrefetchScalarGridSpec(
            num_scalar_prefetch=1,
            grid=(t // tm,),
            in_specs=[
                pl.BlockSpec(memory_space=pltpu.HBM),
                pl.BlockSpec((tm, D_MODEL), lambda i, *_: (i, 0)),
                pl.BlockSpec((tm, LANES), lambda i, *_: (i, 0)),
                _mod_spec(5, row_of_tile),
                vec, vec,
            ],
            out_specs=pl.BlockSpec((tm, D_MODEL), lambda i, *_: (i, 0)),
            scratch_shapes=[pltpu.VMEM((2, 2, tm, D_MODEL), F32), pltpu.SemaphoreType.DMA((2,))],
        ),
        compiler_params=_cparams(("arbitrary",)),
        name="moe_combine",
    )(dest, ys, x, rec_t, mod_l, ln_gain_l, ln_bias_l)


def _dispatch_tables(rec, cnt, *, te):
    t = rec.shape[1]
    n_rows = 2 * t + N_EXPERTS * te
    e12 = rec[0:2].astype(jnp.int32)
    r12 = rec[2:4].astype(jnp.int32)
    counts = cnt[:, 0].astype(jnp.int32)
    padded = (counts + te - 1) // te * te
    ends = jnp.cumsum(padded)
    offsets = ends - padded
    expert_ids = jnp.arange(N_EXPERTS, dtype=jnp.int32)[:, None, None]
    dest = (jnp.sum(jnp.where(e12[None] == expert_ids, offsets[:, None, None], 0), axis=0) + r12).reshape(2 * t)
    tokens = jnp.tile(jnp.arange(t, dtype=jnp.int32), 2)
    filler = jnp.arange(n_rows, dtype=jnp.int32) % t
    src = filler.at[dest].set(tokens, unique_indices=True)
    tile_start = jnp.arange(n_rows // te, dtype=jnp.int32) * te
    tile_expert = jnp.minimum(jnp.searchsorted(ends, tile_start, side="right"), N_EXPERTS - 1).astype(jnp.int32)
    n_tiles = (ends[-1:] // te).astype(jnp.int32)
    return dest, src, tile_expert, n_tiles


def _rope_tables(n_tokens):
    rows = n_tokens // GRID_W
    row = jnp.repeat(jnp.arange(rows), GRID_W).astype(F32)
    col = jnp.tile(jnp.arange(GRID_W), rows).astype(F32)
    quarter = HEAD_DIM // 4
    inv_freq = ROPE_THETA ** (-jnp.arange(quarter, dtype=F32) / quarter)
    ang_r, ang_c = row[:, None] * inv_freq, col[:, None] * inv_freq
    cos = jnp.concatenate([jnp.cos(ang_r), jnp.cos(ang_r), jnp.cos(ang_c), jnp.cos(ang_c)], axis=-1)
    sin = jnp.concatenate([-jnp.sin(ang_r), jnp.sin(ang_r), -jnp.sin(ang_c), jnp.sin(ang_c)], axis=-1)
    return cos, sin


def _row_of_tile(first_row, tokens_per_row):
    def for_tile(tile):
        return lambda i: first_row + (i * tile) // tokens_per_row
    return for_tile


def _mixer_and_ffn(x, attn_arrays, attn_blocks, gates, mod_l, rows, l, expert_tile, w_branch, w_o, ln_gain,
                   ln_bias, w_router_t, w_e_gate, w_e_up, w_e_down):
    y = _merge_branches(attn_arrays, attn_blocks, gates, w_branch, l)
    x = _out_projection(y, w_o, l, x, mod_l, ln_gain[l, 0:1], ln_bias[l, 0:1], row_of_tile=rows(TM_SMALL))
    h2, rec, rec_t, cnt = _moe_route(x, mod_l, w_router_t, row_of_tile=rows(TM_SMALL))
    dest, src, tile_expert, n_tiles = _dispatch_tables(rec, cnt, te=expert_tile)
    xs = _moe_dispatch(h2, n_tiles, src, te=expert_tile)
    ys = _moe_experts(xs, tile_expert, n_tiles, w_e_gate, w_e_up, w_e_down, l, te=expert_tile)
    return _moe_combine(dest, ys, x, rec_t, mod_l, ln_gain[l, 1:2], ln_bias[l, 1:2],
                        row_of_tile=rows(TM_COMBINE))


def kernel(x_prompt, x_sample, cache_kv_a, cache_kv_b, cache_kv_c, c, c_ctx, w_in, w_gate, w_branch, w_o,
           w_mod, b_mod, ln_gain, ln_bias, diff_lam, diff_subln, qk_gain, sink, w_router, w_e_gate, w_e_up,
           w_e_down):
    batch, seq, _ = x_prompt.shape
    dec_batch, dec_seq, _ = x_sample.shape
    past = cache_kv_a.shape[3]
    t_p, t_s = batch * seq, dec_batch * dec_seq

    cond = jnp.concatenate([c_ctx[None], c, jnp.zeros((MOD_ROWS - 1 - dec_batch, D_MODEL), F32)], axis=0)
    mod = _modulation(cond.T, w_mod, b_mod)
    cos_t, sin_t = _rope_tables(dec_seq)
    w_router_t = w_router.T
    cache_a = cache_kv_a.reshape(dec_batch, DEPTH, 2, past, 4 * 256)
    cache_b = cache_kv_b.reshape(dec_batch, DEPTH, 2, past, 2 * HEAD_DIM)
    cache_c = cache_kv_c.reshape(dec_batch, DEPTH, 2, past, 2 * HEAD_DIM)
    rows_p = _row_of_tile(0, t_p)
    rows_s = _row_of_tile(1, dec_seq)

    y_p = x_prompt.reshape(t_p, D_MODEL)
    y_s = x_sample.reshape(t_s, D_MODEL)
    new_a, new_b, new_c = [], [], []
    for l in range(DEPTH):
        lam_init = 0.8 - 0.6 * math.exp(-0.3 * l)
        mod_l = mod[l].reshape(MOD_ROWS, 1, N_MOD * D_MODEL)
        subln_l = diff_subln[l].reshape(1, 256)
        shared = (w_branch, w_o, ln_gain, ln_bias, w_router_t, w_e_gate, w_e_up, w_e_down)

        p = _in_projection(y_p, mod_l, w_in, l, qk_gain[l], cos_t, sin_t, row_of_tile=rows_p(TM),
                           rope=False, q_scale=1.0, out_dtype=F32)
        gates = _branch_gates(y_p, mod_l, w_gate, l, row_of_tile=rows_p(TM))
        attn = _attention_prompt(p, diff_lam[l], subln_l, sink[l], lam_init=lam_init, seq=seq)
        y_p = _mixer_and_ffn(y_p, (attn, attn, attn), (0, 1, 2), gates, mod_l, rows_p, l, 256, *shared)
        p5 = p.reshape(batch, seq, IN_COLS)
        new_a.append(jnp.stack([p5[..., COL_AK:COL_AV], p5[..., COL_AV:COL_BQ]], axis=1))
        new_b.append(jnp.stack([p5[..., COL_BK:COL_BV], p5[..., COL_BV:COL_CQ]], axis=1))
        new_c.append(jnp.stack([p5[..., COL_CK:COL_CV], p5[..., COL_CV:]], axis=1))

        qkv = _in_projection(y_s, mod_l, w_in, l, qk_gain[l], cos_t, sin_t, row_of_tile=rows_s(TM),
                             rope=True, q_scale=SCALE * LOG2E, out_dtype=BF16)
        gates = _branch_gates(y_s, mod_l, w_gate, l, row_of_tile=rows_s(TM))
        vec = lambda shape: pl.BlockSpec(shape, lambda b, qi, kk: (0, 0))
        a_o = _flash_sample(functools.partial(_diff_sample_kernel, lam_init=lam_init), qkv, cache_a, l,
                            q_col=COL_AQ, k_col=COL_AK, v_col=COL_AV, kv_width=1024, n_state=8,
                            row_sum_scratch=True, extra=(diff_lam[l], subln_l),
                            extra_specs=[vec((4, HEAD_DIM)), vec((1, 256))], name="diff_attention",
                            dec_seq=dec_seq)
        b_o = _flash_sample(_gqa_sample_kernel, qkv, cache_b, l, q_col=COL_BQ, k_col=COL_BK, v_col=COL_BV,
                            kv_width=256, n_state=8, row_sum_scratch=False, extra=(), extra_specs=[],
                            name="gqa_attention", dec_seq=dec_seq)
        c_o = _window_sample(qkv, cache_c, sink[l], l, dec_seq=dec_seq)
        y_s = _mixer_and_ffn(y_s, (a_o, b_o, c_o), (0, 0, 0), gates, mod_l, rows_s, l, 512, *shared)

    new_kv_a = jnp.stack(new_a, axis=1).reshape(batch, DEPTH, 2, seq, 4, 256)
    new_kv_b = jnp.stack(new_b, axis=1).reshape(batch, DEPTH, 2, seq, 2, HEAD_DIM)
    new_kv_c = jnp.stack(new_c, axis=1).reshape(batch, DEPTH, 2, seq, 2, HEAD_DIM)
    return (y_p.reshape(batch, seq, D_MODEL), y_s.reshape(dec_batch, dec_seq, D_MODEL),
            new_kv_a, new_kv_b, new_kv_c)
```

```python
import functools
import math

import jax
import jax.numpy as jnp
from jax import lax
from jax.experimental import pallas as pl
from jax.experimental.pallas import tpu as pltpu

F32 = jnp.float32
BF16 = jnp.bfloat16

D_MODEL = 2048
HEAD_DIM = 128
GRID_W = 64
ROPE_THETA = 10000.0
WINDOW = 128
N_EXPERTS = 16
N_GROUPS = 4
EXPERTS_PER_GROUP = N_EXPERTS // N_GROUPS
D_EXPERT = 512
N_MOD = 6
DEPTH = 2
ALPHA = (2 * DEPTH) ** 0.25
EPS = 1e-6
IN_COLS = 6144
BRANCH_WIDTH = 1024
SCALE = HEAD_DIM ** -0.5
LOG2E = math.log2(math.e)
NEG_BIG = -1e30

COL_AQ, COL_AK, COL_AV = 0, 1024, 2048
COL_BQ, COL_BK, COL_BV = 3072, 4096, 4352
COL_CQ, COL_CK, COL_CV = 4608, 5632, 5888

LANES = 128
VMEM_LIMIT = 56 * 1024 * 1024

TM = 1024
TN_PROJ = 1024
TN_PROJ_F32 = 1024
TN_GATE = 1024
TM_SMALL = 512


def _cparams(sem):
    return pltpu.CompilerParams(dimension_semantics=sem, vmem_limit_bytes=VMEM_LIMIT)


def _dot(a, b):
    return jnp.dot(a, b, preferred_element_type=F32)


def _dot_nt(a, b):
    return lax.dot_general(a, b, (((1,), (1,)), ((), ())), preferred_element_type=F32)


def _norm_rows(x):
    mu = jnp.mean(x, axis=-1, keepdims=True)
    xc = x - mu
    var = jnp.mean(xc * xc, axis=-1, keepdims=True)
    return xc * lax.rsqrt(var + EPS)


def _rms(v, gain):
    ms = jnp.mean(v * v, axis=-1, keepdims=True)
    return v * lax.rsqrt(ms + EPS) * gain


def _rope(v, cos, sin_signed):
    lane = lax.broadcasted_iota(jnp.int32, v.shape, 1)
    first_half = (lane % 64) < 32
    partner = jnp.where(first_half, pltpu.roll(v, 96, 1), pltpu.roll(v, 32, 1))
    return v * cos + partner * sin_signed


N_COND = 3
MOD_ROWS = 8
TN_MOD = 1024


def _mod_kernel(cond_ref, w_ref, b_ref, o_ref):
    w = w_ref[...]
    row_idx = lax.broadcasted_iota(jnp.int32, (MOD_ROWS, TN_MOD), 0)
    out = jnp.zeros((MOD_ROWS, TN_MOD), F32)
    for r in range(N_COND):
        c = cond_ref[:, r:r + 1]
        s = c / (1.0 + jnp.exp(-c))
        m = jnp.sum(w * s, axis=0, keepdims=True) + b_ref[...]
        out = jnp.where(row_idx == r, m, out)
    o_ref[...] = out


def _modulation(cond_t, w_mod, b_mod):
    n = N_MOD * D_MODEL
    return pl.pallas_call(
        _mod_kernel,
        out_shape=jax.ShapeDtypeStruct((DEPTH, MOD_ROWS, n), F32),
        grid=(DEPTH, n // TN_MOD),
        in_specs=[
            pl.BlockSpec((D_MODEL, MOD_ROWS), lambda l, j: (0, 0)),
            pl.BlockSpec((None, D_MODEL, TN_MOD), lambda l, j: (l, 0, j)),
            pl.BlockSpec((None, 1, TN_MOD), lambda l, j: (l, 0, j)),
        ],
        out_specs=pl.BlockSpec((None, MOD_ROWS, TN_MOD), lambda l, j: (l, 0, j)),
        compiler_params=_cparams(("parallel", "parallel")),
        name="modulation",
    )(cond_t, w_mod, b_mod.reshape(DEPTH, 1, n))


def _modulate_to_scratch(x_ref, sh_ref, sc_ref, h_scr):
    h = _norm_rows(x_ref[...]) * (1.0 + sc_ref[...]) + sh_ref[...]
    h_scr[...] = h.astype(BF16)


def _head_chunk_kinds():
    kinds = []
    for n_chunks, kind in ((8, (None, True, True)), (8, (None, True, False)), (8, (None, False, False)),
                           (8, ("q", True, True)), (2, ("k", True, False)), (2, (None, False, False)),
                           (8, (None, True, True)), (2, (None, True, False)), (2, (None, False, False))):
        kinds.extend([kind] * n_chunks)
    return kinds


_HEAD_CHUNK_KINDS = _head_chunk_kinds()


def _inproj_kernel(x_ref, sh_ref, sc_ref, w_ref, qk_gain_ref, cos_ref, sin_ref, o_ref, h_scr, *, rope,
                   q_scale):
    j = pl.program_id(1)
    gain_q = qk_gain_ref[0:1, :]
    gain_k = qk_gain_ref[1:2, :]

    @pl.when(j == 0)
    def _():
        _modulate_to_scratch(x_ref, sh_ref, sc_ref, h_scr)

    acc = _dot(h_scr[...], w_ref[...])
    tn = o_ref.shape[1]
    n_chunks = tn // LANES

    def chunk(c):
        return acc[:, c * LANES:(c + 1) * LANES]

    def store(c, v):
        o_ref[:, c * LANES:(c + 1) * LANES] = v.astype(o_ref.dtype)

    def rp(v):
        return _rope(v, cos_ref[...], sin_ref[...]) if rope else v

    def qs(v):
        return v * q_scale if q_scale != 1.0 else v

    def finish(c, kind):
        norm, rotate, is_query = kind
        v = chunk(c)
        if norm is not None:
            v = _rms(v, gain_q if norm == "q" else gain_k)
        if rotate:
            v = rp(v)
        store(c, qs(v) if is_query else v)

    tile_kinds = [tuple(_HEAD_CHUNK_KINDS[t * n_chunks:(t + 1) * n_chunks]) for t in range(IN_COLS // tn)]
    for kinds in dict.fromkeys(tile_kinds):
        tiles = [t for t, k in enumerate(tile_kinds) if k == kinds]
        cond = j == tiles[0]
        for t in tiles[1:]:
            cond = cond | (j == t)

        @pl.when(cond)
        def _(kinds=kinds):
            for c, kind in enumerate(kinds):
                finish(c, kind)


def _gate_kernel(x_ref, sh_ref, sc_ref, w_ref, o_ref, h_scr):
    @pl.when(pl.program_id(1) == 0)
    def _():
        _modulate_to_scratch(x_ref, sh_ref, sc_ref, h_scr)

    acc = _dot(h_scr[...], w_ref[...])
    o_ref[...] = (1.0 / (1.0 + jnp.exp(-acc))).astype(o_ref.dtype)


def _mod_spec(which, row_of_tile):
    return pl.BlockSpec((None, 1, D_MODEL), lambda i, *_: (row_of_tile(i), 0, which))


def _in_projection(x, mod_l, w_in, l, qk_gain_l, cos_t, sin_t, *, row_of_tile, rope, q_scale, out_dtype):
    t = x.shape[0]
    tiles_per_seq = cos_t.shape[0] // TM
    tn = TN_PROJ_F32 if out_dtype == F32 else TN_PROJ
    return pl.pallas_call(
        functools.partial(_inproj_kernel, rope=rope, q_scale=q_scale),
        out_shape=jax.ShapeDtypeStruct((t, IN_COLS), out_dtype),
        grid=(t // TM, IN_COLS // tn),
        in_specs=[
            pl.BlockSpec((TM, D_MODEL), lambda i, j: (i, 0)),
            _mod_spec(0, row_of_tile),
            _mod_spec(1, row_of_tile),
            pl.BlockSpec((None, D_MODEL, tn), lambda i, j: (l, 0, j)),
            pl.BlockSpec((2, HEAD_DIM), lambda i, j: (0, 0)),
            pl.BlockSpec((TM, HEAD_DIM), lambda i, j: (i % tiles_per_seq, 0)),
            pl.BlockSpec((TM, HEAD_DIM), lambda i, j: (i % tiles_per_seq, 0)),
        ],
        out_specs=pl.BlockSpec((TM, tn), lambda i, j: (i, j)),
        scratch_shapes=[pltpu.VMEM((TM, D_MODEL), BF16)],
        compiler_params=_cparams(("parallel", "arbitrary")),
        name="in_projection",
    )(x, mod_l, mod_l, w_in, qk_gain_l, cos_t, sin_t)


def _branch_gates(x, mod_l, w_gate, l, *, row_of_tile):
    t = x.shape[0]
    n = w_gate.shape[2]
    return pl.pallas_call(
        _gate_kernel,
        out_shape=jax.ShapeDtypeStruct((t, n), BF16),
        grid=(t // TM, n // TN_GATE),
        in_specs=[
            pl.BlockSpec((TM, D_MODEL), lambda i, j: (i, 0)),
            _mod_spec(0, row_of_tile),
            _mod_spec(1, row_of_tile),
            pl.BlockSpec((None, D_MODEL, TN_GATE), lambda i, j: (l, 0, j)),
        ],
        out_specs=pl.BlockSpec((TM, TN_GATE), lambda i, j: (i, j)),
        scratch_shapes=[pltpu.VMEM((TM, D_MODEL), BF16)],
        compiler_params=_cparams(("parallel", "arbitrary")),
        name="branch_gates",
    )(x, mod_l, mod_l, w_gate)


def _diff_lambda(lam_ref, lam_init):
    lp = lam_ref[...]
    t1 = jnp.sum(lp[0:1] * lp[1:2], axis=-1, keepdims=True)
    t2 = jnp.sum(lp[2:3] * lp[3:4], axis=-1, keepdims=True)
    return jnp.exp(t1) - jnp.exp(t2) + lam_init


def _softmax_rows(s, sink=None):
    m = jnp.max(s, axis=-1, keepdims=True)
    if sink is not None:
        m = jnp.maximum(m, sink)
    e = jnp.exp(s - m)
    den = jnp.sum(e, axis=-1, keepdims=True)
    if sink is not None:
        den = den + jnp.exp(sink - m)
    return e * (1.0 / den)


def _attn_prompt_kernel(p_ref, lam_ref, subln_ref, sink_ref, o_ref, *, lam_init):
    lam = _diff_lambda(lam_ref, lam_init)

    def blk(c0, w):
        return p_ref[:, c0:c0 + w].astype(BF16)

    for h in range(4):
        probs = []
        for m in range(2):
            q = blk(COL_AQ + h * 256 + m * HEAD_DIM, HEAD_DIM)
            k = blk(COL_AK + h * 256 + m * HEAD_DIM, HEAD_DIM)
            probs.append(_softmax_rows(_dot_nt(q, k) * SCALE))
        w = probs[0] - lam * probs[1]
        o = _dot(w.astype(BF16), blk(COL_AV + h * 256, 256))
        o = _rms(o, subln_ref[...]) * (1.0 - lam_init)
        o_ref[:, h * 256:(h + 1) * 256] = o.astype(o_ref.dtype)

    for mixer, (cq, ck, cv) in enumerate(((COL_BQ, COL_BK, COL_BV), (COL_CQ, COL_CK, COL_CV))):
        for kvh in range(2):
            k = blk(ck + kvh * HEAD_DIM, HEAD_DIM)
            v = blk(cv + kvh * HEAD_DIM, HEAD_DIM)
            for g in range(4):
                hq = kvh * 4 + g
                q = blk(cq + hq * HEAD_DIM, HEAD_DIM)
                sink = sink_ref[hq] if mixer == 1 else None
                p = _softmax_rows(_dot_nt(q, k) * SCALE, sink)
                o = _dot(p.astype(BF16), v)
                c0 = BRANCH_WIDTH * (1 + mixer) + hq * HEAD_DIM
                o_ref[:, c0:c0 + HEAD_DIM] = o.astype(o_ref.dtype)


def _attention_prompt(p, diff_lam_l, subln_l, sink_l, *, lam_init, seq):
    t = p.shape[0]
    return pl.pallas_call(
        functools.partial(_attn_prompt_kernel, lam_init=lam_init),
        out_shape=jax.ShapeDtypeStruct((t, 3 * BRANCH_WIDTH), BF16),
        grid=(t // seq,),
        in_specs=[
            pl.BlockSpec((seq, IN_COLS), lambda b: (b, 0)),
            pl.BlockSpec((4, HEAD_DIM), lambda b: (0, 0)),
            pl.BlockSpec((1, 256), lambda b: (0, 0)),
            pl.BlockSpec(memory_space=pltpu.SMEM),
        ],
        out_specs=pl.BlockSpec((seq, 3 * BRANCH_WIDTH), lambda b: (b, 0)),
        compiler_params=_cparams(("parallel",)),
        name="attention_prompt",
    )(p, diff_lam_l, subln_l, sink_l)


TQ = 512
TK = 1024


def _tile_lanes(v, n):
    return jnp.concatenate([v] * n, axis=-1) if n > 1 else v


def _online_softmax(idx, s, m_scr):
    m_prev = m_scr[idx]
    m_new = jnp.maximum(m_prev, jnp.max(s, axis=-1, keepdims=True))
    m_scr[idx] = m_new
    alpha = jnp.exp2(m_prev - m_new)
    p = jnp.exp2(s - _tile_lanes(m_new, s.shape[1] // LANES))
    return p, alpha


def _diff_sample_kernel(q_ref, k_ref, v_ref, kc_ref, vc_ref, lam_ref, subln_ref, o_ref,
                        m_scr, l_scr, acc_scr, *, lam_init):
    kk = pl.program_id(2)

    @pl.when(kk == 0)
    def _():
        m_scr[...] = jnp.full(m_scr.shape, NEG_BIG, F32)
        l_scr[...] = jnp.zeros(l_scr.shape, F32)
        acc_scr[...] = jnp.zeros(acc_scr.shape, F32)

    def process(kb_ref, vb_ref):
        for h in range(4):
            v = vb_ref[:, h * 256:(h + 1) * 256].astype(BF16)
            for m in range(2):
                c0 = h * 256 + m * HEAD_DIM
                idx = h * 2 + m
                k = kb_ref[:, c0:c0 + HEAD_DIM].astype(BF16)
                p, alpha = _online_softmax(idx, _dot_nt(q_ref[:, c0:c0 + HEAD_DIM], k), m_scr)
                part = p[:, 0:LANES]
                for c in range(1, p.shape[1] // LANES):
                    part = part + p[:, c * LANES:(c + 1) * LANES]
                l_scr[idx] = alpha * l_scr[idx] + part
                acc_scr[idx] = _tile_lanes(alpha, 2) * acc_scr[idx] + _dot(p.astype(BF16), v)

    @pl.when(kk == 0)
    def _():
        process(kc_ref, vc_ref)

    @pl.when(kk > 0)
    def _():
        process(k_ref, v_ref)

    @pl.when(kk == pl.num_programs(2) - 1)
    def _():
        lam = _diff_lambda(lam_ref, lam_init)
        for h in range(4):
            l1 = jnp.sum(l_scr[2 * h], axis=-1, keepdims=True)
            l2 = jnp.sum(l_scr[2 * h + 1], axis=-1, keepdims=True)
            o1 = acc_scr[2 * h] * (1.0 / l1)
            o2 = acc_scr[2 * h + 1] * (1.0 / l2)
            o = _rms(o1 - lam * o2, subln_ref[...]) * (1.0 - lam_init)
            o_ref[:, h * 256:(h + 1) * 256] = o.astype(o_ref.dtype)


def _gqa_sample_kernel(q_ref, k_ref, v_ref, kc_ref, vc_ref, o_ref, m_scr, acc_scr):
    kk = pl.program_id(2)

    @pl.when(kk == 0)
    def _():
        m_scr[...] = jnp.full(m_scr.shape, NEG_BIG, F32)
        acc_scr[...] = jnp.zeros(acc_scr.shape, F32)

    def process(kb_ref, vb_ref):
        for kvh in range(2):
            k = kb_ref[:, kvh * HEAD_DIM:(kvh + 1) * HEAD_DIM].astype(BF16)
            v = vb_ref[:, kvh * HEAD_DIM:(kvh + 1) * HEAD_DIM].astype(BF16)
            v_ones = jnp.concatenate([v, jnp.ones_like(v)], axis=-1)
            for g in range(4):
                hq = kvh * 4 + g
                s = _dot_nt(q_ref[:, hq * HEAD_DIM:(hq + 1) * HEAD_DIM], k)
                p, alpha = _online_softmax(hq, s, m_scr)
                acc_scr[hq] = _tile_lanes(alpha, 2) * acc_scr[hq] + _dot(p.astype(BF16), v_ones)

    @pl.when(kk == 0)
    def _():
        process(kc_ref, vc_ref)

    @pl.when(kk > 0)
    def _():
        process(k_ref, v_ref)

    @pl.when(kk == pl.num_programs(2) - 1)
    def _():
        for hq in range(8):
            o = acc_scr[hq, :, 0:HEAD_DIM] / acc_scr[hq, :, HEAD_DIM:2 * HEAD_DIM]
            o_ref[:, hq * HEAD_DIM:(hq + 1) * HEAD_DIM] = o.astype(o_ref.dtype)


def _flash_sample(kernel, qkv, cache, l, *, q_col, k_col, v_col, kv_width, n_state, row_sum_scratch, extra,
                  extra_specs, name, dec_seq):
    t = qkv.shape[0]
    nb = t // dec_seq
    nq = dec_seq // TQ
    nk = dec_seq // TK
    past = cache.shape[3]
    q_blk, k_blk, v_blk = q_col // BRANCH_WIDTH, k_col // kv_width, v_col // kv_width
    kv_row = lambda b, qi, kk: b * nk + jnp.maximum(kk - 1, 0)
    return pl.pallas_call(
        kernel,
        out_shape=jax.ShapeDtypeStruct((t, BRANCH_WIDTH), BF16),
        grid=(nb, nq, nk + 1),
        in_specs=[
            pl.BlockSpec((TQ, BRANCH_WIDTH), lambda b, qi, kk: (b * nq + qi, q_blk)),
            pl.BlockSpec((TK, kv_width), lambda b, qi, kk: (kv_row(b, qi, kk), k_blk)),
            pl.BlockSpec((TK, kv_width), lambda b, qi, kk: (kv_row(b, qi, kk), v_blk)),
            pl.BlockSpec((None, None, None, past, kv_width), lambda b, qi, kk: (b, l, 0, 0, 0)),
            pl.BlockSpec((None, None, None, past, kv_width), lambda b, qi, kk: (b, l, 1, 0, 0)),
        ] + extra_specs,
        out_specs=pl.BlockSpec((TQ, BRANCH_WIDTH), lambda b, qi, kk: (b * nq + qi, 0)),
        scratch_shapes=[pltpu.VMEM((n_state, TQ, LANES), F32)] * (2 if row_sum_scratch else 1)
        + [pltpu.VMEM((n_state, TQ, 2 * LANES), F32)],
        compiler_params=_cparams(("parallel", "parallel", "arbitrary")),
        name=name,
    )(qkv, qkv, qkv, cache, cache, *extra)


def _window_sample_kernel(q0_ref, q1_ref, kp_ref, kc_ref, kn_ref, vp_ref, vc_ref, vn_ref, kctx_ref, vctx_ref,
                          sink_ref, o_ref, *, dec_seq):
    qi = pl.program_id(1)
    q_start = qi * TQ
    half = TQ // 2
    q_pos = q_start + lax.broadcasted_iota(jnp.int32, (TQ, 1), 0)
    segs = ((kp_ref, vp_ref, q_start - half, half), (kc_ref, vc_ref, q_start, TQ),
            (kn_ref, vn_ref, q_start + TQ, half))
    valid = []
    for _, _, start, n in segs:
        k_pos = start + lax.broadcasted_iota(jnp.int32, (TQ, n), 1)
        valid.append((jnp.abs(q_pos - k_pos) <= WINDOW) & (k_pos >= 0) & (k_pos < dec_seq))
    for kvh in range(2):
        q_ref = q0_ref if kvh == 0 else q1_ref
        lo, hi = kvh * HEAD_DIM, (kvh + 1) * HEAD_DIM
        kctx = kctx_ref[:, lo:hi].astype(BF16)
        vctx = vctx_ref[:, lo:hi].astype(BF16)
        for g in range(4):
            hq = kvh * 4 + g
            q = q_ref[:, g * HEAD_DIM:(g + 1) * HEAD_DIM]
            sink = sink_ref[hq] * LOG2E
            scores = [_dot_nt(q, kctx)]
            for (k_ref, _, _, _), ok in zip(segs, valid):
                scores.append(jnp.where(ok, _dot_nt(q, k_ref[:, lo:hi]), NEG_BIG))
            m = jnp.maximum(scores[0].max(axis=-1, keepdims=True), sink)
            for s in scores[1:]:
                m = jnp.maximum(m, s.max(axis=-1, keepdims=True))
            es = [jnp.exp2(s - m) for s in scores]
            den = jnp.exp2(sink - m)
            for e in es:
                den = den + jnp.sum(e, axis=-1, keepdims=True)
            o = _dot(es[0].astype(BF16), vctx)
            for e, (_, v_ref, _, _) in zip(es[1:], segs):
                o = o + _dot(e.astype(BF16), v_ref[:, lo:hi])
            o = o * (1.0 / den)
            o_ref[:, hq * HEAD_DIM:(hq + 1) * HEAD_DIM] = o.astype(o_ref.dtype)


def _window_sample(qkv, cache, sink_l, l, *, dec_seq):
    t = qkv.shape[0]
    nb = t // dec_seq
    nq = dec_seq // TQ
    half = TQ // 2
    n_half = dec_seq // half
    past = cache.shape[3]
    kvw = 2 * HEAD_DIM
    q_blk = COL_CQ // 512
    k_blk, v_blk = COL_CK // kvw, COL_CV // kvw
    prev_row = lambda b, qi: b * n_half + jnp.maximum(2 * qi - 1, 0)
    next_row = lambda b, qi: b * n_half + jnp.minimum(2 * qi + 2, n_half - 1)
    return pl.pallas_call(
        functools.partial(_window_sample_kernel, dec_seq=dec_seq),
        out_shape=jax.ShapeDtypeStruct((t, BRANCH_WIDTH), BF16),
        grid=(nb, nq),
        in_specs=[
            pl.BlockSpec((TQ, 512), lambda b, qi: (b * nq + qi, q_blk)),
            pl.BlockSpec((TQ, 512), lambda b, qi: (b * nq + qi, q_blk + 1)),
            pl.BlockSpec((half, kvw), lambda b, qi: (prev_row(b, qi), k_blk)),
            pl.BlockSpec((TQ, kvw), lambda b, qi: (b * nq + qi, k_blk)),
            pl.BlockSpec((half, kvw), lambda b, qi: (next_row(b, qi), k_blk)),
            pl.BlockSpec((half, kvw), lambda b, qi: (prev_row(b, qi), v_blk)),
            pl.BlockSpec((TQ, kvw), lambda b, qi: (b * nq + qi, v_blk)),
            pl.BlockSpec((half, kvw), lambda b, qi: (next_row(b, qi), v_blk)),
            pl.BlockSpec((None, None, None, past, kvw), lambda b, qi: (b, l, 0, 0, 0)),
            pl.BlockSpec((None, None, None, past, kvw), lambda b, qi: (b, l, 1, 0, 0)),
            pl.BlockSpec(memory_space=pltpu.SMEM),
        ],
        out_specs=pl.BlockSpec((TQ, BRANCH_WIDTH), lambda b, qi: (b * nq + qi, 0)),
        compiler_params=_cparams(("parallel", "parallel")),
        name="window_attention",
    )(qkv, qkv, qkv, qkv, qkv, qkv, qkv, qkv, cache, cache, sink_l)


def _merge_kernel(oa_ref, ob_ref, oc_ref, g_ref, w_ref, y_ref, acc_scr):
    r = pl.program_id(2)

    def contrib(o_ref):
        return g_ref[...].astype(F32) * _dot(o_ref[...], w_ref[...])

    @pl.when(r == 0)
    def _():
        acc_scr[...] = contrib(oa_ref)

    @pl.when(r == 1)
    def _():
        acc_scr[...] += contrib(ob_ref)

    @pl.when(r == 2)
    def _():
        y_ref[...] = (acc_scr[...] + contrib(oc_ref)).astype(y_ref.dtype)


def _merge_branches(o_arrays, o_blocks, gates, w_branch, l):
    t = gates.shape[0]
    tn = D_MODEL
    nn = D_MODEL // tn
    o_specs = [pl.BlockSpec((TM, BRANCH_WIDTH), functools.partial(lambda i, n, r, blk: (i, blk), blk=blk))
               for blk in o_blocks]
    return pl.pallas_call(
        _merge_kernel,
        out_shape=jax.ShapeDtypeStruct((t, D_MODEL), BF16),
        grid=(t // TM, nn, 3),
        in_specs=o_specs + [
            pl.BlockSpec((TM, tn), lambda i, n, r: (i, r * nn + n)),
            pl.BlockSpec((None, None, BRANCH_WIDTH, tn), lambda i, n, r: (l, r, 0, n)),
        ],
        out_specs=pl.BlockSpec((TM, tn), lambda i, n, r: (i, n)),
        scratch_shapes=[pltpu.VMEM((TM, tn), F32)],
        compiler_params=_cparams(("parallel", "parallel", "arbitrary")),
        name="merge_branches",
    )(*o_arrays, gates, w_branch)


def _layer_norm_rows(v, gain, bias):
    return _norm_rows(v) * gain + bias


def _outproj_kernel(y_ref, w_ref, x_ref, g_ref, gain_ref, bias_ref, o_ref):
    n = pl.program_id(1)
    tn = w_ref.shape[1]
    n_blocks = o_ref.shape[1] // tn
    z = _dot(y_ref[...], w_ref[...])
    for b in range(n_blocks):
        @pl.when(n == b)
        def _(b=b):
            cols = slice(b * tn, (b + 1) * tn)
            o_ref[:, cols] = ALPHA * x_ref[:, cols] + g_ref[:, cols] * z

    @pl.when(n == n_blocks - 1)
    def _():
        o_ref[...] = _layer_norm_rows(o_ref[...], gain_ref[...], bias_ref[...])


def _out_projection(y, w_o, l, x, mod_l, ln_gain_l, ln_bias_l, *, row_of_tile):
    t = x.shape[0]
    tn = 512
    return pl.pallas_call(
        _outproj_kernel,
        out_shape=jax.ShapeDtypeStruct((t, D_MODEL), F32),
        grid=(t // TM, D_MODEL // tn),
        in_specs=[
            pl.BlockSpec((TM, D_MODEL), lambda i, n: (i, 0)),
            pl.BlockSpec((None, D_MODEL, tn), lambda i, n: (l, 0, n)),
            pl.BlockSpec((TM, D_MODEL), lambda i, n: (i, 0)),
            _mod_spec(2, row_of_tile),
            pl.BlockSpec((1, D_MODEL), lambda i, n: (0, 0)),
            pl.BlockSpec((1, D_MODEL), lambda i, n: (0, 0)),
        ],
        out_specs=pl.BlockSpec((TM, D_MODEL), lambda i, n: (i, 0)),
        compiler_params=_cparams(("parallel", "arbitrary")),
        name="out_projection",
    )(y, w_o, x, mod_l, ln_gain_l, ln_bias_l)


def _route(p):
    rows = [p[e:e + 1, :] for e in range(N_EXPERTS)]
    best_score, best_group = None, None
    for g in range(N_GROUPS):
        members = rows[g * EXPERTS_PER_GROUP:(g + 1) * EXPERTS_PER_GROUP]
        score = None
        for a in range(EXPERTS_PER_GROUP):
            for b in range(a + 1, EXPERTS_PER_GROUP):
                pair = members[a] + members[b]
                score = pair if score is None else jnp.maximum(score, pair)
        if g == 0:
            best_score, best_group = score, jnp.zeros(score.shape, F32)
        else:
            better = score > best_score
            best_group = jnp.where(better, float(g), best_group)
            best_score = jnp.where(better, score, best_score)
    e_idx = lax.broadcasted_iota(jnp.int32, p.shape, 0).astype(F32)
    g_idx = jnp.floor(e_idx * (1.0 / EXPERTS_PER_GROUP))
    masked = jnp.where(g_idx == best_group, p, -1.0)
    w1 = jnp.max(masked, axis=0, keepdims=True)
    i1 = jnp.min(jnp.where(masked == w1, e_idx, float(N_EXPERTS)), axis=0, keepdims=True)
    masked2 = jnp.where(e_idx == i1, -2.0, masked)
    w2 = jnp.max(masked2, axis=0, keepdims=True)
    i2 = jnp.min(jnp.where(masked2 == w2, e_idx, float(N_EXPERTS)), axis=0, keepdims=True)
    tot = w1 + w2
    return e_idx, i1, i2, w1 / tot, w2 / tot


ROUTE_ROWS = 8
D_PACKED = D_MODEL // 2


def _pack_bf16_pairs(xb):
    half = xb.shape[1] // 2
    lo = lax.bitcast_convert_type(xb[:, :half].astype(F32), jnp.uint32)
    hi = lax.bitcast_convert_type(xb[:, half:].astype(F32), jnp.uint32)
    return (hi & jnp.uint32(0xFFFF0000)) | (lo >> 16)


def _unpack_bf16_pairs(words):
    lo = lax.bitcast_convert_type(words << 16, F32).astype(BF16)
    hi = lax.bitcast_convert_type(words & jnp.uint32(0xFFFF0000), F32).astype(BF16)
    return lo, hi


def _moe_route_kernel(x_ref, sh_ref, sc_ref, wr_ref, h_ref, rec_ref, rec_t_ref, cnt_ref, carry_scr):
    @pl.when(pl.program_id(0) == 0)
    def _():
        carry_scr[...] = jnp.zeros(carry_scr.shape, F32)

    h = _norm_rows(x_ref[...]) * (1.0 + sc_ref[...]) + sh_ref[...]
    hb = h.astype(BF16)
    h_ref[...] = _pack_bf16_pairs(hb)
    logits = _dot_nt(wr_ref[...].astype(BF16), hb)
    m = jnp.max(logits, axis=0, keepdims=True)
    e = jnp.exp(logits - m)
    probs = e / jnp.sum(e, axis=0, keepdims=True)
    e_idx, i1, i2, w1, w2 = _route(probs)
    tm = probs.shape[1]
    oh1 = (e_idx == i1).astype(F32)
    oh2 = (e_idx == i2).astype(F32)
    oh = oh1 + oh2
    earlier = (lax.broadcasted_iota(jnp.int32, (tm, tm), 0) < lax.broadcasted_iota(jnp.int32, (tm, tm), 1))
    rank = carry_scr[:, 0:1] + _dot(oh.astype(BF16), earlier.astype(BF16))
    r1 = jnp.sum(oh1 * rank, axis=0, keepdims=True)
    r2 = jnp.sum(oh2 * rank, axis=0, keepdims=True)
    carry_scr[...] = carry_scr[...] + jnp.sum(oh, axis=1, keepdims=True)
    cnt_ref[...] = carry_scr[...]
    row = lax.broadcasted_iota(jnp.int32, (ROUTE_ROWS, tm), 0)
    rec = jnp.zeros((ROUTE_ROWS, tm), F32)
    for k, v in enumerate((i1, i2, r1, r2, w1, w2)):
        rec = jnp.where(row == k, v, rec)
    rec_ref[...] = rec
    pad = jnp.zeros((LANES - ROUTE_ROWS, tm), F32)
    rec_t_ref[...] = jnp.concatenate([rec, pad], axis=0).T


def _moe_route(x, mod_l, w_router_t, *, row_of_tile):
    t = x.shape[0]
    tm = TM_SMALL
    return pl.pallas_call(
        _moe_route_kernel,
        out_shape=(jax.ShapeDtypeStruct((t, D_PACKED), jnp.uint32), jax.ShapeDtypeStruct((ROUTE_ROWS, t), F32),
                   jax.ShapeDtypeStruct((t, LANES), F32), jax.ShapeDtypeStruct((N_EXPERTS, LANES), F32)),
        grid=(t // tm,),
        in_specs=[
            pl.BlockSpec((tm, D_MODEL), lambda i: (i, 0)),
            _mod_spec(3, row_of_tile),
            _mod_spec(4, row_of_tile),
            pl.BlockSpec((N_EXPERTS, D_MODEL), lambda i: (0, 0)),
        ],
        out_specs=(pl.BlockSpec((tm, D_PACKED), lambda i: (i, 0)),
                   pl.BlockSpec((ROUTE_ROWS, tm), lambda i: (0, i)),
                   pl.BlockSpec((tm, LANES), lambda i: (i, 0)),
                   pl.BlockSpec((N_EXPERTS, LANES), lambda i: (0, 0))),
        scratch_shapes=[pltpu.VMEM((N_EXPERTS, LANES), F32)],
        compiler_params=_cparams(("arbitrary",)),
        name="moe_route",
    )(x, mod_l, mod_l, w_router_t)


ROW_DMA_UNROLL = 8
GATHER_ORDER_STRIDE = 37


def _row_copy(src_hbm, src_row, dst_buf, dst_row, sem):
    return pltpu.make_async_copy(src_hbm.at[pl.ds(src_row, 1)], dst_buf.at[pl.ds(dst_row, 1)], sem)


def _moe_dispatch_kernel(n_tiles_ref, src_ref, h_hbm, xs_ref, x_buf, sems, *, te):
    j = pl.program_id(0)
    n_valid = n_tiles_ref[0]
    slot = j % 2

    def start_gather(tile, s):
        def body(r, carry):
            row = (r * GATHER_ORDER_STRIDE) % te
            _row_copy(h_hbm, src_ref[tile * te + row], x_buf.at[s], row, sems.at[s]).start()
            return carry
        lax.fori_loop(0, te, body, 0, unroll=ROW_DMA_UNROLL)

    def wait_gather(s):
        def body(r, carry):
            _row_copy(h_hbm, 0, x_buf.at[s], r, sems.at[s]).wait()
            return carry
        lax.fori_loop(0, te, body, 0, unroll=ROW_DMA_UNROLL)

    @pl.when((j == 0) & (n_valid > 0))
    def _():
        start_gather(0, 0)

    @pl.when(j + 1 < n_valid)
    def _():
        start_gather(j + 1, 1 - slot)

    @pl.when(j < n_valid)
    def _():
        wait_gather(slot)
        xs_ref[...] = x_buf[slot]

    @pl.when(j >= n_valid)
    def _():
        xs_ref[...] = jnp.zeros(xs_ref.shape, xs_ref.dtype)


def _moe_dispatch(h, n_tiles, src, *, te):
    n_rows = src.shape[0]
    return pl.pallas_call(
        functools.partial(_moe_dispatch_kernel, te=te),
        out_shape=jax.ShapeDtypeStruct((n_rows, D_PACKED), h.dtype),
        grid_spec=pltpu.PrefetchScalarGridSpec(
            num_scalar_prefetch=2,
            grid=(n_rows // te,),
            in_specs=[pl.BlockSpec(memory_space=pltpu.HBM)],
            out_specs=pl.BlockSpec((te, D_PACKED), lambda j, *_: (j, 0)),
            scratch_shapes=[pltpu.VMEM((2, te, D_PACKED), h.dtype), pltpu.SemaphoreType.DMA((2,))],
        ),
        compiler_params=_cparams(("arbitrary",)),
        name="moe_dispatch",
    )(n_tiles, src, h)


def _moe_expert_kernel(tile_expert_ref, n_tiles_ref, xs_ref, wg_ref, wu_ref, wd_ref, ys_ref):
    del tile_expert_ref
    j = pl.program_id(0)

    @pl.when(j < n_tiles_ref[0])
    def _():
        x_lo, x_hi = _unpack_bf16_pairs(xs_ref[...])

        def project(w_ref):
            return (_dot(x_lo, w_ref[:D_PACKED, :].astype(BF16)) + _dot(x_hi, w_ref[D_PACKED:, :].astype(BF16)))

        gate = project(wg_ref)
        up = project(wu_ref)
        hid = gate / (1.0 + jnp.exp(-gate)) * up
        ys_ref[...] = _dot(hid.astype(BF16), wd_ref[...].astype(BF16))

    @pl.when(j >= n_tiles_ref[0])
    def _():
        ys_ref[...] = jnp.zeros(ys_ref.shape, F32)


def _moe_experts(xs, tile_expert, n_tiles, wg, wu, wd, l, *, te):
    n_rows = xs.shape[0]
    w_in_spec = pl.BlockSpec((None, None, D_MODEL, D_EXPERT), lambda j, te_ref, nt_ref: (l, te_ref[j], 0, 0))
    w_out_spec = pl.BlockSpec((None, None, D_EXPERT, D_MODEL), lambda j, te_ref, nt_ref: (l, te_ref[j], 0, 0))
    xs_spec = pl.BlockSpec((te, D_PACKED), lambda j, te_ref, nt_ref: (j, 0))
    return pl.pallas_call(
        _moe_expert_kernel,
        out_shape=jax.ShapeDtypeStruct((n_rows, D_MODEL), F32),
        grid_spec=pltpu.PrefetchScalarGridSpec(
            num_scalar_prefetch=2,
            grid=(n_rows // te,),
            in_specs=[xs_spec, w_in_spec, w_in_spec, w_out_spec],
            out_specs=pl.BlockSpec((te, D_MODEL), lambda j, *_: (j, 0)),
        ),
        compiler_params=_cparams(("arbitrary",)),
        name="moe_experts",
    )(tile_expert, n_tiles, xs, wg, wu, wd)


TM_COMBINE = 256


def _moe_combine_kernel(dest_ref, ys_hbm, x_ref, rec_t_ref, g_ref, gain_ref, bias_ref, o_ref, y_buf, sems,
                        *, n_tokens):
    i = pl.program_id(0)
    tm = x_ref.shape[0]
    slot = i % 2

    def start_gather(tile, s):
        def body(r, carry):
            for k in range(2):
                row = dest_ref[k * n_tokens + tile * tm + r]
                _row_copy(ys_hbm, row, y_buf.at[s, k], r, sems.at[s]).start()
            return carry
        lax.fori_loop(0, tm, body, 0, unroll=ROW_DMA_UNROLL)

    def wait_gather(s):
        def body(r, carry):
            for k in range(2):
                _row_copy(ys_hbm, 0, y_buf.at[s, k], r, sems.at[s]).wait()
            return carry
        lax.fori_loop(0, tm, body, 0, unroll=ROW_DMA_UNROLL)

    @pl.when(i == 0)
    def _():
        start_gather(0, 0)

    @pl.when(i + 1 < pl.num_programs(0))
    def _():
        start_gather(i + 1, 1 - slot)

    wait_gather(slot)
    z = rec_t_ref[:, 4:5] * y_buf[slot, 0] + rec_t_ref[:, 5:6] * y_buf[slot, 1]
    v = ALPHA * x_ref[...] + g_ref[...] * z
    o_ref[...] = _layer_norm_rows(v, gain_ref[...], bias_ref[...])


def _moe_combine(dest, ys, x, rec_t, mod_l, ln_gain_l, ln_bias_l, *, row_of_tile):
    t = x.shape[0]
    tm = TM_COMBINE
    vec = pl.BlockSpec((1, D_MODEL), lambda i, *_: (0, 0))
    return pl.pallas_call(
        functools.partial(_moe_combine_kernel, n_tokens=t),
        out_shape=jax.ShapeDtypeStruct((t, D_MODEL), F32),
        grid_spec=pltpu.PrefetchScalarGridSpec(
            num_scalar_prefetch=1,
            grid=(t // tm,),
            in_specs=[
                pl.BlockSpec(memory_space=pltpu.HBM),
                pl.BlockSpec((tm, D_MODEL), lambda i, *_: (i, 0)),
                pl.BlockSpec((tm, LANES), lambda i, *_: (i, 0)),
                _mod_spec(5, row_of_tile),
                vec, vec,
            ],
            out_specs=pl.BlockSpec((tm, D_MODEL), lambda i, *_: (i, 0)),
            scratch_shapes=[pltpu.VMEM((2, 2, tm, D_MODEL), F32), pltpu.SemaphoreType.DMA((2,))],
        ),
        compiler_params=_cparams(("arbitrary",)),
        name="moe_combine",
    )(dest, ys, x, rec_t, mod_l, ln_gain_l, ln_bias_l)


def _dispatch_tables(rec, cnt, *, te):
    t = rec.shape[1]
    n_rows = 2 * t + N_EXPERTS * te
    e12 = rec[0:2].astype(jnp.int32)
    r12 = rec[2:4].astype(jnp.int32)
    counts = cnt[:, 0].astype(jnp.int32)
    padded = (counts + te - 1) // te * te
    ends = jnp.cumsum(padded)
    offsets = ends - padded
    expert_ids = jnp.arange(N_EXPERTS, dtype=jnp.int32)[:, None, None]
    dest = (jnp.sum(jnp.where(e12[None] == expert_ids, offsets[:, None, None], 0), axis=0) + r12).reshape(2 * t)
    tokens = jnp.tile(jnp.arange(t, dtype=jnp.int32), 2)
    filler = jnp.arange(n_rows, dtype=jnp.int32) % t
    src = filler.at[dest].set(tokens, unique_indices=True)
    tile_start = jnp.arange(n_rows // te, dtype=jnp.int32) * te
    tile_expert = jnp.minimum(jnp.searchsorted(ends, tile_start, side="right"), N_EXPERTS - 1).astype(jnp.int32)
    n_tiles = (ends[-1:] // te).astype(jnp.int32)
    return dest, src, tile_expert, n_tiles


def _rope_tables(n_tokens):
    rows = n_tokens // GRID_W
    row = jnp.repeat(jnp.arange(rows), GRID_W).astype(F32)
    col = jnp.tile(jnp.arange(GRID_W), rows).astype(F32)
    quarter = HEAD_DIM // 4
    inv_freq = ROPE_THETA ** (-jnp.arange(quarter, dtype=F32) / quarter)
    ang_r, ang_c = row[:, None] * inv_freq, col[:, None] * inv_freq
    cos = jnp.concatenate([jnp.cos(ang_r), jnp.cos(ang_r), jnp.cos(ang_c), jnp.cos(ang_c)], axis=-1)
    sin = jnp.concatenate([-jnp.sin(ang_r), jnp.sin(ang_r), -jnp.sin(ang_c), jnp.sin(ang_c)], axis=-1)
    return cos, sin


def _row_of_tile(first_row, tokens_per_row):
    def for_tile(tile):
        return lambda i: first_row + (i * tile) // tokens_per_row
    return for_tile


def _mixer_and_ffn(x, attn_arrays, attn_blocks, gates, mod_l, rows, l, expert_tile, w_branch, w_o, ln_gain,
                   ln_bias, w_router_t, w_e_gate, w_e_up, w_e_down):
    y = _merge_branches(attn_arrays, attn_blocks, gates, w_branch, l)
    x = _out_projection(y, w_o, l, x, mod_l, ln_gain[l, 0:1], ln_bias[l, 0:1], row_of_tile=rows(TM))
    h2, rec, rec_t, cnt = _moe_route(x, mod_l, w_router_t, row_of_tile=rows(TM_SMALL))
    dest, src, tile_expert, n_tiles = _dispatch_tables(rec, cnt, te=expert_tile)
    xs = _moe_dispatch(h2, n_tiles, src, te=expert_tile)
    ys = _moe_experts(xs, tile_expert, n_tiles, w_e_gate, w_e_up, w_e_down, l, te=expert_tile)
    return _moe_combine(dest, ys, x, rec_t, mod_l, ln_gain[l, 1:2], ln_bias[l, 1:2],
                        row_of_tile=rows(TM_COMBINE))


def kernel(x_prompt, x_sample, cache_kv_a, cache_kv_b, cache_kv_c, c, c_ctx, w_in, w_gate, w_branch, w_o,
           w_mod, b_mod, ln_gain, ln_bias, diff_lam, diff_subln, qk_gain, sink, w_router, w_e_gate, w_e_up,
           w_e_down):
    batch, seq, _ = x_prompt.shape
    dec_batch, dec_seq, _ = x_sample.shape
    past = cache_kv_a.shape[3]
    t_p, t_s = batch * seq, dec_batch * dec_seq

    cond = jnp.concatenate([c_ctx[None], c, jnp.zeros((MOD_ROWS - 1 - dec_batch, D_MODEL), F32)], axis=0)
    mod = _modulation(cond.T, w_mod, b_mod)
    cos_t, sin_t = _rope_tables(dec_seq)
    w_router_t = w_router.T
    w_in, w_gate, w_branch, w_o = (w.astype(BF16) for w in (w_in, w_gate, w_branch, w_o))
    cache_a = cache_kv_a.reshape(dec_batch, DEPTH, 2, past, 4 * 256)
    cache_b = cache_kv_b.reshape(dec_batch, DEPTH, 2, past, 2 * HEAD_DIM)
    cache_c = cache_kv_c.reshape(dec_batch, DEPTH, 2, past, 2 * HEAD_DIM)
    rows_p = _row_of_tile(0, t_p)
    rows_s = _row_of_tile(1, dec_seq)

    y_p = x_prompt.reshape(t_p, D_MODEL)
    y_s = x_sample.reshape(t_s, D_MODEL)
    new_a, new_b, new_c = [], [], []
    for l in range(DEPTH):
        lam_init = 0.8 - 0.6 * math.exp(-0.3 * l)
        mod_l = mod[l].reshape(MOD_ROWS, 1, N_MOD * D_MODEL)
        subln_l = diff_subln[l].reshape(1, 256)
        shared = (w_branch, w_o, ln_gain, ln_bias, w_router_t, w_e_gate, w_e_up, w_e_down)

        p = _in_projection(y_p, mod_l, w_in, l, qk_gain[l], cos_t, sin_t, row_of_tile=rows_p(TM),
                           rope=False, q_scale=1.0, out_dtype=F32)
        gates = _branch_gates(y_p, mod_l, w_gate, l, row_of_tile=rows_p(TM))
        attn = _attention_prompt(p, diff_lam[l], subln_l, sink[l], lam_init=lam_init, seq=seq)
        y_p = _mixer_and_ffn(y_p, (attn, attn, attn), (0, 1, 2), gates, mod_l, rows_p, l, 256, *shared)
        p5 = p.reshape(batch, seq, IN_COLS)
        new_a.append(jnp.stack([p5[..., COL_AK:COL_AV], p5[..., COL_AV:COL_BQ]], axis=1))
        new_b.append(jnp.stack([p5[..., COL_BK:COL_BV], p5[..., COL_BV:COL_CQ]], axis=1))
        new_c.append(jnp.stack([p5[..., COL_CK:COL_CV], p5[..., COL_CV:]], axis=1))

        qkv = _in_projection(y_s, mod_l, w_in, l, qk_gain[l], cos_t, sin_t, row_of_tile=rows_s(TM),
                             rope=True, q_scale=SCALE * LOG2E, out_dtype=BF16)
        gates = _branch_gates(y_s, mod_l, w_gate, l, row_of_tile=rows_s(TM))
        vec = lambda shape: pl.BlockSpec(shape, lambda b, qi, kk: (0, 0))
        a_o = _flash_sample(functools.partial(_diff_sample_kernel, lam_init=lam_init), qkv, cache_a, l,
                            q_col=COL_AQ, k_col=COL_AK, v_col=COL_AV, kv_width=1024, n_state=8,
                            row_sum_scratch=True, extra=(diff_lam[l], subln_l),
                            extra_specs=[vec((4, HEAD_DIM)), vec((1, 256))], name="diff_attention",
                            dec_seq=dec_seq)
        b_o = _flash_sample(_gqa_sample_kernel, qkv, cache_b, l, q_col=COL_BQ, k_col=COL_BK, v_col=COL_BV,
                            kv_width=256, n_state=8, row_sum_scratch=False, extra=(), extra_specs=[],
                            name="gqa_attention", dec_seq=dec_seq)
        c_o = _window_sample(qkv, cache_c, sink[l], l, dec_seq=dec_seq)
        y_s = _mixer_and_ffn(y_s, (a_o, b_o, c_o), (0, 0, 0), gates, mod_l, rows_s, l, 512, *shared)

    new_kv_a = jnp.stack(new_a, axis=1).reshape(batch, DEPTH, 2, seq, 4, 256)
    new_kv_b = jnp.stack(new_b, axis=1).reshape(batch, DEPTH, 2, seq, 2, HEAD_DIM)
    new_kv_c = jnp.stack(new_c, axis=1).reshape(batch, DEPTH, 2, seq, 2, HEAD_DIM)
    return (y_p.reshape(batch, seq, D_MODEL), y_s.reshape(dec_batch, dec_seq, D_MODEL),
            new_kv_a, new_kv_b, new_kv_c)
```

```python
import functools
import math

import jax
import jax.numpy as jnp
from jax import lax
from jax.experimental import pallas as pl
from jax.experimental.pallas import tpu as pltpu

F32 = jnp.float32
BF16 = jnp.bfloat16

D_MODEL = 2048
HEAD_DIM = 128
GRID_W = 64
ROPE_THETA = 10000.0
WINDOW = 128
N_EXPERTS = 16
N_GROUPS = 4
EXPERTS_PER_GROUP = N_EXPERTS // N_GROUPS
D_EXPERT = 512
N_MOD = 6
DEPTH = 2
ALPHA = (2 * DEPTH) ** 0.25
EPS = 1e-6
IN_COLS = 6144
BRANCH_WIDTH = 1024
SCALE = HEAD_DIM ** -0.5
LOG2E = math.log2(math.e)
NEG_BIG = -1e30

COL_AQ, COL_AK, COL_AV = 0, 1024, 2048
COL_BQ, COL_BK, COL_BV = 3072, 4096, 4352
COL_CQ, COL_CK, COL_CV = 4608, 5632, 5888

LANES = 128
VMEM_LIMIT = 56 * 1024 * 1024

TM = 1024
TN_PROJ = 1024
TN_PROJ_F32 = 1024
TN_GATE = 1024
TM_SMALL = 512


def _cparams(sem):
    return pltpu.CompilerParams(dimension_semantics=sem, vmem_limit_bytes=VMEM_LIMIT)


def _dot(a, b):
    return jnp.dot(a, b, preferred_element_type=F32)


def _dot_nt(a, b):
    return lax.dot_general(a, b, (((1,), (1,)), ((), ())), preferred_element_type=F32)


def _norm_rows(x):
    mu = jnp.mean(x, axis=-1, keepdims=True)
    xc = x - mu
    var = jnp.mean(xc * xc, axis=-1, keepdims=True)
    return xc * lax.rsqrt(var + EPS)


def _rms(v, gain):
    ms = jnp.mean(v * v, axis=-1, keepdims=True)
    return v * lax.rsqrt(ms + EPS) * gain


def _rope(v, cos, sin_from_upper, sin_from_lower):
    return v * cos + pltpu.roll(v, 96, 1) * sin_from_upper + pltpu.roll(v, 32, 1) * sin_from_lower


N_COND = 3
MOD_ROWS = 8
TN_MOD = 1024


def _mod_kernel(cond_ref, w_ref, b_ref, o_ref):
    w = w_ref[...]
    row_idx = lax.broadcasted_iota(jnp.int32, (MOD_ROWS, TN_MOD), 0)
    out = jnp.zeros((MOD_ROWS, TN_MOD), F32)
    for r in range(N_COND):
        c = cond_ref[:, r:r + 1]
        s = c / (1.0 + jnp.exp(-c))
        m = jnp.sum(w * s, axis=0, keepdims=True) + b_ref[...]
        out = jnp.where(row_idx == r, m, out)
    o_ref[...] = out


def _modulation(cond_t, w_mod, b_mod):
    n = N_MOD * D_MODEL
    return pl.pallas_call(
        _mod_kernel,
        out_shape=jax.ShapeDtypeStruct((DEPTH, MOD_ROWS, n), F32),
        grid=(DEPTH, n // TN_MOD),
        in_specs=[
            pl.BlockSpec((D_MODEL, MOD_ROWS), lambda l, j: (0, 0)),
            pl.BlockSpec((None, D_MODEL, TN_MOD), lambda l, j: (l, 0, j)),
            pl.BlockSpec((None, 1, TN_MOD), lambda l, j: (l, 0, j)),
        ],
        out_specs=pl.BlockSpec((None, MOD_ROWS, TN_MOD), lambda l, j: (l, 0, j)),
        compiler_params=_cparams(("parallel", "parallel")),
        name="modulation",
    )(cond_t, w_mod, b_mod.reshape(DEPTH, 1, n))


def _modulate_to_scratch(x_ref, sh_ref, sc_ref, h_scr):
    h = _norm_rows(x_ref[...]) * (1.0 + sc_ref[...]) + sh_ref[...]
    h_scr[...] = h.astype(BF16)


def _head_chunk_kinds():
    kinds = []
    for n_chunks, kind in ((8, (None, True, True)), (8, (None, True, False)), (8, (None, False, False)),
                           (8, ("q", True, True)), (2, ("k", True, False)), (2, (None, False, False)),
                           (8, (None, True, True)), (2, (None, True, False)), (2, (None, False, False))):
        kinds.extend([kind] * n_chunks)
    return kinds


_HEAD_CHUNK_KINDS = _head_chunk_kinds()


def _inproj_kernel(x_ref, sh_ref, sc_ref, w_ref, qk_gain_ref, cos_ref, sin_ref, o_ref, h_scr, *, rope,
                   q_scale):
    j = pl.program_id(1)
    gain_q = qk_gain_ref[0:1, :]
    gain_k = qk_gain_ref[1:2, :]

    @pl.when(j == 0)
    def _():
        _modulate_to_scratch(x_ref, sh_ref, sc_ref, h_scr)

    acc = _dot(h_scr[...], w_ref[...])
    tn = o_ref.shape[1]
    n_chunks = tn // LANES

    def chunk(c):
        return acc[:, c * LANES:(c + 1) * LANES]

    def store(c, v):
        o_ref[:, c * LANES:(c + 1) * LANES] = v.astype(o_ref.dtype)

    def rp(v):
        return _rope(v, cos_ref[...], sin_ref[:, :HEAD_DIM], sin_ref[:, HEAD_DIM:]) if rope else v

    def qs(v):
        return v * q_scale if q_scale != 1.0 else v

    def finish(c, kind):
        norm, rotate, is_query = kind
        v = chunk(c)
        if norm is not None:
            v = _rms(v, gain_q if norm == "q" else gain_k)
        if rotate:
            v = rp(v)
        store(c, qs(v) if is_query else v)

    tile_kinds = [tuple(_HEAD_CHUNK_KINDS[t * n_chunks:(t + 1) * n_chunks]) for t in range(IN_COLS // tn)]
    for kinds in dict.fromkeys(tile_kinds):
        tiles = [t for t, k in enumerate(tile_kinds) if k == kinds]
        cond = j == tiles[0]
        for t in tiles[1:]:
            cond = cond | (j == t)

        @pl.when(cond)
        def _(kinds=kinds):
            for c, kind in enumerate(kinds):
                finish(c, kind)


def _gate_kernel(x_ref, sh_ref, sc_ref, w_ref, o_ref, h_scr):
    @pl.when(pl.program_id(1) == 0)
    def _():
        _modulate_to_scratch(x_ref, sh_ref, sc_ref, h_scr)

    acc = _dot(h_scr[...], w_ref[...])
    o_ref[...] = (1.0 / (1.0 + jnp.exp(-acc))).astype(o_ref.dtype)


def _mod_spec(which, row_of_tile):
    return pl.BlockSpec((None, 1, D_MODEL), lambda i, *_: (row_of_tile(i), 0, which))


def _in_projection(x, mod_l, w_in, l, qk_gain_l, cos_t, sin_t, *, row_of_tile, rope, q_scale, out_dtype):
    t = x.shape[0]
    tiles_per_seq = cos_t.shape[0] // TM
    tn = TN_PROJ_F32 if out_dtype == F32 else TN_PROJ
    return pl.pallas_call(
        functools.partial(_inproj_kernel, rope=rope, q_scale=q_scale),
        out_shape=jax.ShapeDtypeStruct((t, IN_COLS), out_dtype),
        grid=(t // TM, IN_COLS // tn),
        in_specs=[
            pl.BlockSpec((TM, D_MODEL), lambda i, j: (i, 0)),
            _mod_spec(0, row_of_tile),
            _mod_spec(1, row_of_tile),
            pl.BlockSpec((None, D_MODEL, tn), lambda i, j: (l, 0, j)),
            pl.BlockSpec((2, HEAD_DIM), lambda i, j: (0, 0)),
            pl.BlockSpec((TM, HEAD_DIM), lambda i, j: (i % tiles_per_seq, 0)),
            pl.BlockSpec((TM, 2 * HEAD_DIM), lambda i, j: (i % tiles_per_seq, 0)),
        ],
        out_specs=pl.BlockSpec((TM, tn), lambda i, j: (i, j)),
        scratch_shapes=[pltpu.VMEM((TM, D_MODEL), BF16)],
        compiler_params=_cparams(("parallel", "arbitrary")),
        name="in_projection",
    )(x, mod_l, mod_l, w_in, qk_gain_l, cos_t, sin_t)


def _branch_gates(x, mod_l, w_gate, l, *, row_of_tile):
    t = x.shape[0]
    n = w_gate.shape[2]
    return pl.pallas_call(
        _gate_kernel,
        out_shape=jax.ShapeDtypeStruct((t, n), BF16),
        grid=(t // TM, n // TN_GATE),
        in_specs=[
            pl.BlockSpec((TM, D_MODEL), lambda i, j: (i, 0)),
            _mod_spec(0, row_of_tile),
            _mod_spec(1, row_of_tile),
            pl.BlockSpec((None, D_MODEL, TN_GATE), lambda i, j: (l, 0, j)),
        ],
        out_specs=pl.BlockSpec((TM, TN_GATE), lambda i, j: (i, j)),
        scratch_shapes=[pltpu.VMEM((TM, D_MODEL), BF16)],
        compiler_params=_cparams(("parallel", "arbitrary")),
        name="branch_gates",
    )(x, mod_l, mod_l, w_gate)


def _diff_lambda(lam_ref, lam_init):
    lp = lam_ref[...]
    t1 = jnp.sum(lp[0:1] * lp[1:2], axis=-1, keepdims=True)
    t2 = jnp.sum(lp[2:3] * lp[3:4], axis=-1, keepdims=True)
    return jnp.exp(t1) - jnp.exp(t2) + lam_init


def _softmax_rows(s, sink=None):
    m = jnp.max(s, axis=-1, keepdims=True)
    if sink is not None:
        m = jnp.maximum(m, sink)
    e = jnp.exp(s - m)
    den = jnp.sum(e, axis=-1, keepdims=True)
    if sink is not None:
        den = den + jnp.exp(sink - m)
    return e * (1.0 / den)


def _attn_prompt_kernel(p_ref, lam_ref, subln_ref, sink_ref, o_ref, *, lam_init):
    lam = _diff_lambda(lam_ref, lam_init)

    def blk(c0, w):
        return p_ref[:, c0:c0 + w].astype(BF16)

    for h in range(4):
        probs = []
        for m in range(2):
            q = blk(COL_AQ + h * 256 + m * HEAD_DIM, HEAD_DIM)
            k = blk(COL_AK + h * 256 + m * HEAD_DIM, HEAD_DIM)
            probs.append(_softmax_rows(_dot_nt(q, k) * SCALE))
        w = probs[0] - lam * probs[1]
        o = _dot(w.astype(BF16), blk(COL_AV + h * 256, 256))
        o = _rms(o, subln_ref[...]) * (1.0 - lam_init)
        o_ref[:, h * 256:(h + 1) * 256] = o.astype(o_ref.dtype)

    for mixer, (cq, ck, cv) in enumerate(((COL_BQ, COL_BK, COL_BV), (COL_CQ, COL_CK, COL_CV))):
        for kvh in range(2):
            k = blk(ck + kvh * HEAD_DIM, HEAD_DIM)
            v = blk(cv + kvh * HEAD_DIM, HEAD_DIM)
            for g in range(4):
                hq = kvh * 4 + g
                q = blk(cq + hq * HEAD_DIM, HEAD_DIM)
                sink = sink_ref[hq] if mixer == 1 else None
                p = _softmax_rows(_dot_nt(q, k) * SCALE, sink)
                o = _dot(p.astype(BF16), v)
                c0 = BRANCH_WIDTH * (1 + mixer) + hq * HEAD_DIM
                o_ref[:, c0:c0 + HEAD_DIM] = o.astype(o_ref.dtype)


def _attention_prompt(p, diff_lam_l, subln_l, sink_l, *, lam_init, seq):
    t = p.shape[0]
    return pl.pallas_call(
        functools.partial(_attn_prompt_kernel, lam_init=lam_init),
        out_shape=jax.ShapeDtypeStruct((t, 3 * BRANCH_WIDTH), BF16),
        grid=(t // seq,),
        in_specs=[
            pl.BlockSpec((seq, IN_COLS), lambda b: (b, 0)),
            pl.BlockSpec((4, HEAD_DIM), lambda b: (0, 0)),
            pl.BlockSpec((1, 256), lambda b: (0, 0)),
            pl.BlockSpec(memory_space=pltpu.SMEM),
        ],
        out_specs=pl.BlockSpec((seq, 3 * BRANCH_WIDTH), lambda b: (b, 0)),
        compiler_params=_cparams(("parallel",)),
        name="attention_prompt",
    )(p, diff_lam_l, subln_l, sink_l)


TQ = 512
TK = 1024


def _tile_lanes(v, n):
    return jnp.concatenate([v] * n, axis=-1) if n > 1 else v


def _online_softmax(idx, s, m_scr):
    m_prev = m_scr[idx]
    m_new = jnp.maximum(m_prev, jnp.max(s, axis=-1, keepdims=True))
    m_scr[idx] = m_new
    alpha = jnp.exp2(m_prev - m_new)
    p = jnp.exp2(s - _tile_lanes(m_new, s.shape[1] // LANES))
    return p, alpha


def _diff_sample_kernel(q_ref, k_ref, v_ref, kc_ref, vc_ref, lam_ref, subln_ref, o_ref,
                        m_scr, l_scr, acc_scr, *, lam_init):
    kk = pl.program_id(2)

    @pl.when(kk == 0)
    def _():
        m_scr[...] = jnp.full(m_scr.shape, NEG_BIG, F32)
        l_scr[...] = jnp.zeros(l_scr.shape, F32)
        acc_scr[...] = jnp.zeros(acc_scr.shape, F32)

    def process(kb_ref, vb_ref):
        for h in range(4):
            v = vb_ref[:, h * 256:(h + 1) * 256].astype(BF16)
            for m in range(2):
                c0 = h * 256 + m * HEAD_DIM
                idx = h * 2 + m
                k = kb_ref[:, c0:c0 + HEAD_DIM].astype(BF16)
                p, alpha = _online_softmax(idx, _dot_nt(q_ref[:, c0:c0 + HEAD_DIM], k), m_scr)
                part = p[:, 0:LANES]
                for c in range(1, p.shape[1] // LANES):
                    part = part + p[:, c * LANES:(c + 1) * LANES]
                l_scr[idx] = alpha * l_scr[idx] + part
                acc_scr[idx] = _tile_lanes(alpha, 2) * acc_scr[idx] + _dot(p.astype(BF16), v)

    @pl.when(kk == 0)
    def _():
        process(kc_ref, vc_ref)

    @pl.when(kk > 0)
    def _():
        process(k_ref, v_ref)

    @pl.when(kk == pl.num_programs(2) - 1)
    def _():
        lam = _diff_lambda(lam_ref, lam_init)
        for h in range(4):
            l1 = jnp.sum(l_scr[2 * h], axis=-1, keepdims=True)
            l2 = jnp.sum(l_scr[2 * h + 1], axis=-1, keepdims=True)
            o1 = acc_scr[2 * h] * (1.0 / l1)
            o2 = acc_scr[2 * h + 1] * (1.0 / l2)
            o = _rms(o1 - lam * o2, subln_ref[...]) * (1.0 - lam_init)
            o_ref[:, h * 256:(h + 1) * 256] = o.astype(o_ref.dtype)


def _gqa_sample_kernel(q_ref, k_ref, v_ref, kc_ref, vc_ref, o_ref, m_scr, acc_scr):
    kk = pl.program_id(2)

    @pl.when(kk == 0)
    def _():
        m_scr[...] = jnp.full(m_scr.shape, NEG_BIG, F32)
        acc_scr[...] = jnp.zeros(acc_scr.shape, F32)

    def process(kb_ref, vb_ref):
        for kvh in range(2):
            k = kb_ref[:, kvh * HEAD_DIM:(kvh + 1) * HEAD_DIM].astype(BF16)
            v = vb_ref[:, kvh * HEAD_DIM:(kvh + 1) * HEAD_DIM].astype(BF16)
            v_ones = jnp.concatenate([v, jnp.ones_like(v)], axis=-1)
            for g in range(4):
                hq = kvh * 4 + g
                s = _dot_nt(q_ref[:, hq * HEAD_DIM:(hq + 1) * HEAD_DIM], k)
                p, alpha = _online_softmax(hq, s, m_scr)
                acc_scr[hq] = _tile_lanes(alpha, 2) * acc_scr[hq] + _dot(p.astype(BF16), v_ones)

    @pl.when(kk == 0)
    def _():
        process(kc_ref, vc_ref)

    @pl.when(kk > 0)
    def _():
        process(k_ref, v_ref)

    @pl.when(kk == pl.num_programs(2) - 1)
    def _():
        for hq in range(8):
            o = acc_scr[hq, :, 0:HEAD_DIM] / acc_scr[hq, :, HEAD_DIM:2 * HEAD_DIM]
            o_ref[:, hq * HEAD_DIM:(hq + 1) * HEAD_DIM] = o.astype(o_ref.dtype)


def _flash_sample(kernel, qkv, cache, l, *, q_col, k_col, v_col, kv_width, n_state, row_sum_scratch, extra,
                  extra_specs, name, dec_seq):
    t = qkv.shape[0]
    nb = t // dec_seq
    nq = dec_seq // TQ
    nk = dec_seq // TK
    past = cache.shape[3]
    q_blk, k_blk, v_blk = q_col // BRANCH_WIDTH, k_col // kv_width, v_col // kv_width
    kv_row = lambda b, qi, kk: b * nk + jnp.maximum(kk - 1, 0)
    return pl.pallas_call(
        kernel,
        out_shape=jax.ShapeDtypeStruct((t, BRANCH_WIDTH), BF16),
        grid=(nb, nq, nk + 1),
        in_specs=[
            pl.BlockSpec((TQ, BRANCH_WIDTH), lambda b, qi, kk: (b * nq + qi, q_blk)),
            pl.BlockSpec((TK, kv_width), lambda b, qi, kk: (kv_row(b, qi, kk), k_blk)),
            pl.BlockSpec((TK, kv_width), lambda b, qi, kk: (kv_row(b, qi, kk), v_blk)),
            pl.BlockSpec((None, None, None, past, kv_width), lambda b, qi, kk: (b, l, 0, 0, 0)),
            pl.BlockSpec((None, None, None, past, kv_width), lambda b, qi, kk: (b, l, 1, 0, 0)),
        ] + extra_specs,
        out_specs=pl.BlockSpec((TQ, BRANCH_WIDTH), lambda b, qi, kk: (b * nq + qi, 0)),
        scratch_shapes=[pltpu.VMEM((n_state, TQ, LANES), F32)] * (2 if row_sum_scratch else 1)
        + [pltpu.VMEM((n_state, TQ, 2 * LANES), F32)],
        compiler_params=_cparams(("parallel", "parallel", "arbitrary")),
        name=name,
    )(qkv, qkv, qkv, cache, cache, *extra)


def _window_sample_kernel(q0_ref, q1_ref, kp_ref, kc_ref, kn_ref, vp_ref, vc_ref, vn_ref, kctx_ref, vctx_ref,
                          sink_ref, o_ref, *, dec_seq):
    qi = pl.program_id(1)
    q_start = qi * TQ
    half = TQ // 2
    q_pos = q_start + lax.broadcasted_iota(jnp.int32, (TQ, 1), 0)
    segs = ((kp_ref, vp_ref, q_start - half, half), (kc_ref, vc_ref, q_start, TQ),
            (kn_ref, vn_ref, q_start + TQ, half))
    valid = []
    for _, _, start, n in segs:
        k_pos = start + lax.broadcasted_iota(jnp.int32, (TQ, n), 1)
        valid.append((jnp.abs(q_pos - k_pos) <= WINDOW) & (k_pos >= 0) & (k_pos < dec_seq))
    for kvh in range(2):
        q_ref = q0_ref if kvh == 0 else q1_ref
        lo, hi = kvh * HEAD_DIM, (kvh + 1) * HEAD_DIM
        kctx = kctx_ref[:, lo:hi].astype(BF16)
        vctx = vctx_ref[:, lo:hi].astype(BF16)
        for g in range(4):
            hq = kvh * 4 + g
            q = q_ref[:, g * HEAD_DIM:(g + 1) * HEAD_DIM]
            sink = sink_ref[hq] * LOG2E
            scores = [_dot_nt(q, kctx)]
            for (k_ref, _, _, _), ok in zip(segs, valid):
                scores.append(jnp.where(ok, _dot_nt(q, k_ref[:, lo:hi]), NEG_BIG))
            m = jnp.maximum(scores[0].max(axis=-1, keepdims=True), sink)
            for s in scores[1:]:
                m = jnp.maximum(m, s.max(axis=-1, keepdims=True))
            es = [jnp.exp2(s - m) for s in scores]
            den = jnp.exp2(sink - m)
            for e in es:
                den = den + jnp.sum(e, axis=-1, keepdims=True)
            o = _dot(es[0].astype(BF16), vctx)
            for e, (_, v_ref, _, _) in zip(es[1:], segs):
                o = o + _dot(e.astype(BF16), v_ref[:, lo:hi])
            o = o * (1.0 / den)
            o_ref[:, hq * HEAD_DIM:(hq + 1) * HEAD_DIM] = o.astype(o_ref.dtype)


def _window_sample(qkv, cache, sink_l, l, *, dec_seq):
    t = qkv.shape[0]
    nb = t // dec_seq
    nq = dec_seq // TQ
    half = TQ // 2
    n_half = dec_seq // half
    past = cache.shape[3]
    kvw = 2 * HEAD_DIM
    q_blk = COL_CQ // 512
    k_blk, v_blk = COL_CK // kvw, COL_CV // kvw
    prev_row = lambda b, qi: b * n_half + jnp.maximum(2 * qi - 1, 0)
    next_row = lambda b, qi: b * n_half + jnp.minimum(2 * qi + 2, n_half - 1)
    return pl.pallas_call(
        functools.partial(_window_sample_kernel, dec_seq=dec_seq),
        out_shape=jax.ShapeDtypeStruct((t, BRANCH_WIDTH), BF16),
        grid=(nb, nq),
        in_specs=[
            pl.BlockSpec((TQ, 512), lambda b, qi: (b * nq + qi, q_blk)),
            pl.BlockSpec((TQ, 512), lambda b, qi: (b * nq + qi, q_blk + 1)),
            pl.BlockSpec((half, kvw), lambda b, qi: (prev_row(b, qi), k_blk)),
            pl.BlockSpec((TQ, kvw), lambda b, qi: (b * nq + qi, k_blk)),
            pl.BlockSpec((half, kvw), lambda b, qi: (next_row(b, qi), k_blk)),
            pl.BlockSpec((half, kvw), lambda b, qi: (prev_row(b, qi), v_blk)),
            pl.BlockSpec((TQ, kvw), lambda b, qi: (b * nq + qi, v_blk)),
            pl.BlockSpec((half, kvw), lambda b, qi: (next_row(b, qi), v_blk)),
            pl.BlockSpec((None, None, None, past, kvw), lambda b, qi: (b, l, 0, 0, 0)),
            pl.BlockSpec((None, None, None, past, kvw), lambda b, qi: (b, l, 1, 0, 0)),
            pl.BlockSpec(memory_space=pltpu.SMEM),
        ],
        out_specs=pl.BlockSpec((TQ, BRANCH_WIDTH), lambda b, qi: (b * nq + qi, 0)),
        compiler_params=_cparams(("parallel", "parallel")),
        name="window_attention",
    )(qkv, qkv, qkv, qkv, qkv, qkv, qkv, qkv, cache, cache, sink_l)


def _merge_kernel(oa_ref, ob_ref, oc_ref, g_ref, w_ref, y_ref, acc_scr):
    r = pl.program_id(2)

    def contrib(o_ref):
        return g_ref[...].astype(F32) * _dot(o_ref[...], w_ref[...])

    @pl.when(r == 0)
    def _():
        acc_scr[...] = contrib(oa_ref)

    @pl.when(r == 1)
    def _():
        acc_scr[...] += contrib(ob_ref)

    @pl.when(r == 2)
    def _():
        y_ref[...] = (acc_scr[...] + contrib(oc_ref)).astype(y_ref.dtype)


def _merge_branches(o_arrays, o_blocks, gates, w_branch, l):
    t = gates.shape[0]
    tn = D_MODEL
    nn = D_MODEL // tn
    o_specs = [pl.BlockSpec((TM, BRANCH_WIDTH), functools.partial(lambda i, n, r, blk: (i, blk), blk=blk))
               for blk in o_blocks]
    return pl.pallas_call(
        _merge_kernel,
        out_shape=jax.ShapeDtypeStruct((t, D_MODEL), BF16),
        grid=(t // TM, nn, 3),
        in_specs=o_specs + [
            pl.BlockSpec((TM, tn), lambda i, n, r: (i, r * nn + n)),
            pl.BlockSpec((None, None, BRANCH_WIDTH, tn), lambda i, n, r: (l, r, 0, n)),
        ],
        out_specs=pl.BlockSpec((TM, tn), lambda i, n, r: (i, n)),
        scratch_shapes=[pltpu.VMEM((TM, tn), F32)],
        compiler_params=_cparams(("parallel", "parallel", "arbitrary")),
        name="merge_branches",
    )(*o_arrays, gates, w_branch)


def _layer_norm_rows(v, gain, bias):
    return _norm_rows(v) * gain + bias


def _outproj_kernel(y_ref, w_ref, x_ref, g_ref, gain_ref, bias_ref, o_ref):
    n = pl.program_id(1)
    tn = w_ref.shape[1]
    n_blocks = o_ref.shape[1] // tn
    z = _dot(y_ref[...], w_ref[...])
    for b in range(n_blocks):
        @pl.when(n == b)
        def _(b=b):
            cols = slice(b * tn, (b + 1) * tn)
            o_ref[:, cols] = ALPHA * x_ref[:, cols] + g_ref[:, cols] * z

    @pl.when(n == n_blocks - 1)
    def _():
        o_ref[...] = _layer_norm_rows(o_ref[...], gain_ref[...], bias_ref[...])


def _out_projection(y, w_o, l, x, mod_l, ln_gain_l, ln_bias_l, *, row_of_tile):
    t = x.shape[0]
    tn = 512
    return pl.pallas_call(
        _outproj_kernel,
        out_shape=jax.ShapeDtypeStruct((t, D_MODEL), F32),
        grid=(t // TM, D_MODEL // tn),
        in_specs=[
            pl.BlockSpec((TM, D_MODEL), lambda i, n: (i, 0)),
            pl.BlockSpec((None, D_MODEL, tn), lambda i, n: (l, 0, n)),
            pl.BlockSpec((TM, D_MODEL), lambda i, n: (i, 0)),
            _mod_spec(2, row_of_tile),
            pl.BlockSpec((1, D_MODEL), lambda i, n: (0, 0)),
            pl.BlockSpec((1, D_MODEL), lambda i, n: (0, 0)),
        ],
        out_specs=pl.BlockSpec((TM, D_MODEL), lambda i, n: (i, 0)),
        compiler_params=_cparams(("parallel", "arbitrary")),
        name="out_projection",
    )(y, w_o, x, mod_l, ln_gain_l, ln_bias_l)


def _route(p):
    rows = [p[e:e + 1, :] for e in range(N_EXPERTS)]
    best_score, best_group = None, None
    for g in range(N_GROUPS):
        members = rows[g * EXPERTS_PER_GROUP:(g + 1) * EXPERTS_PER_GROUP]
        score = None
        for a in range(EXPERTS_PER_GROUP):
            for b in range(a + 1, EXPERTS_PER_GROUP):
                pair = members[a] + members[b]
                score = pair if score is None else jnp.maximum(score, pair)
        if g == 0:
            best_score, best_group = score, jnp.zeros(score.shape, F32)
        else:
            better = score > best_score
            best_group = jnp.where(better, float(g), best_group)
            best_score = jnp.where(better, score, best_score)
    e_idx = lax.broadcasted_iota(jnp.int32, p.shape, 0).astype(F32)
    g_idx = jnp.floor(e_idx * (1.0 / EXPERTS_PER_GROUP))
    masked = jnp.where(g_idx == best_group, p, -1.0)
    w1 = jnp.max(masked, axis=0, keepdims=True)
    i1 = jnp.min(jnp.where(masked == w1, e_idx, float(N_EXPERTS)), axis=0, keepdims=True)
    masked2 = jnp.where(e_idx == i1, -2.0, masked)
    w2 = jnp.max(masked2, axis=0, keepdims=True)
    i2 = jnp.min(jnp.where(masked2 == w2, e_idx, float(N_EXPERTS)), axis=0, keepdims=True)
    tot = w1 + w2
    return e_idx, i1, i2, w1 / tot, w2 / tot


ROUTE_ROWS = 8
D_PACKED = D_MODEL // 2


def _pack_bf16_pairs(xb):
    half = xb.shape[1] // 2
    lo = lax.bitcast_convert_type(xb[:, :half].astype(F32), jnp.uint32)
    hi = lax.bitcast_convert_type(xb[:, half:].astype(F32), jnp.uint32)
    return (hi & jnp.uint32(0xFFFF0000)) | (lo >> 16)


def _unpack_bf16_pairs(words):
    lo = lax.bitcast_convert_type(words << 16, F32).astype(BF16)
    hi = lax.bitcast_convert_type(words & jnp.uint32(0xFFFF0000), F32).astype(BF16)
    return lo, hi


def _moe_route_kernel(x_ref, sh_ref, sc_ref, wr_ref, h_ref, rec_ref, rec_t_ref, cnt_ref, carry_scr):
    @pl.when(pl.program_id(0) == 0)
    def _():
        carry_scr[...] = jnp.zeros(carry_scr.shape, F32)

    h = _norm_rows(x_ref[...]) * (1.0 + sc_ref[...]) + sh_ref[...]
    hb = h.astype(BF16)
    h_ref[...] = _pack_bf16_pairs(hb)
    logits = _dot_nt(wr_ref[...].astype(BF16), hb)
    m = jnp.max(logits, axis=0, keepdims=True)
    e = jnp.exp(logits - m)
    probs = e / jnp.sum(e, axis=0, keepdims=True)
    e_idx, i1, i2, w1, w2 = _route(probs)
    tm = probs.shape[1]
    oh1 = (e_idx == i1).astype(F32)
    oh2 = (e_idx == i2).astype(F32)
    oh = oh1 + oh2
    earlier = (lax.broadcasted_iota(jnp.int32, (tm, tm), 0) < lax.broadcasted_iota(jnp.int32, (tm, tm), 1))
    rank = carry_scr[:, 0:1] + _dot(oh.astype(BF16), earlier.astype(BF16))
    r1 = jnp.sum(oh1 * rank, axis=0, keepdims=True)
    r2 = jnp.sum(oh2 * rank, axis=0, keepdims=True)
    carry_scr[...] = carry_scr[...] + jnp.sum(oh, axis=1, keepdims=True)
    cnt_ref[...] = carry_scr[...]
    row = lax.broadcasted_iota(jnp.int32, (ROUTE_ROWS, tm), 0)
    rec = jnp.zeros((ROUTE_ROWS, tm), F32)
    for k, v in enumerate((i1, i2, r1, r2, w1, w2)):
        rec = jnp.where(row == k, v, rec)
    rec_ref[...] = rec
    pad = jnp.zeros((LANES - ROUTE_ROWS, tm), F32)
    rec_t_ref[...] = jnp.concatenate([rec, pad], axis=0).T


def _moe_route(x, mod_l, w_router_t, *, row_of_tile):
    t = x.shape[0]
    tm = TM_SMALL
    return pl.pallas_call(
        _moe_route_kernel,
        out_shape=(jax.ShapeDtypeStruct((t, D_PACKED), jnp.uint32), jax.ShapeDtypeStruct((ROUTE_ROWS, t), F32),
                   jax.ShapeDtypeStruct((t, LANES), F32), jax.ShapeDtypeStruct((N_EXPERTS, LANES), F32)),
        grid=(t // tm,),
        in_specs=[
            pl.BlockSpec((tm, D_MODEL), lambda i: (i, 0)),
            _mod_spec(3, row_of_tile),
            _mod_spec(4, row_of_tile),
            pl.BlockSpec((N_EXPERTS, D_MODEL), lambda i: (0, 0)),
        ],
        out_specs=(pl.BlockSpec((tm, D_PACKED), lambda i: (i, 0)),
                   pl.BlockSpec((ROUTE_ROWS, tm), lambda i: (0, i)),
                   pl.BlockSpec((tm, LANES), lambda i: (i, 0)),
                   pl.BlockSpec((N_EXPERTS, LANES), lambda i: (0, 0))),
        scratch_shapes=[pltpu.VMEM((N_EXPERTS, LANES), F32)],
        compiler_params=_cparams(("arbitrary",)),
        name="moe_route",
    )(x, mod_l, mod_l, w_router_t)


ROW_DMA_UNROLL = 8
GATHER_ORDER_STRIDE = 37


def _row_copy(src_hbm, src_row, dst_buf, dst_row, sem):
    return pltpu.make_async_copy(src_hbm.at[pl.ds(src_row, 1)], dst_buf.at[pl.ds(dst_row, 1)], sem)


def _moe_dispatch_kernel(n_tiles_ref, src_ref, h_hbm, xs_ref, x_buf, sems, *, te):
    j = pl.program_id(0)
    n_valid = n_tiles_ref[0]
    slot = j % 2

    def start_gather(tile, s):
        def body(pair, carry):
            for queue in range(2):
                row = ((2 * pair + queue) * GATHER_ORDER_STRIDE) % te
                _row_copy(h_hbm, src_ref[tile * te + row], x_buf.at[s], row, sems.at[s]).start(priority=queue)
            return carry
        lax.fori_loop(0, te // 2, body, 0, unroll=ROW_DMA_UNROLL // 2)

    def wait_gather(s):
        def body(r, carry):
            _row_copy(h_hbm, 0, x_buf.at[s], r, sems.at[s]).wait()
            return carry
        lax.fori_loop(0, te, body, 0, unroll=ROW_DMA_UNROLL)

    @pl.when((j == 0) & (n_valid > 0))
    def _():
        start_gather(0, 0)

    @pl.when(j + 1 < n_valid)
    def _():
        start_gather(j + 1, 1 - slot)

    @pl.when(j < n_valid)
    def _():
        wait_gather(slot)
        xs_ref[...] = x_buf[slot]

    @pl.when(j >= n_valid)
    def _():
        xs_ref[...] = jnp.zeros(xs_ref.shape, xs_ref.dtype)


def _moe_dispatch(h, n_tiles, src, *, te):
    n_rows = src.shape[0]
    return pl.pallas_call(
        functools.partial(_moe_dispatch_kernel, te=te),
        out_shape=jax.ShapeDtypeStruct((n_rows, D_PACKED), h.dtype),
        grid_spec=pltpu.PrefetchScalarGridSpec(
            num_scalar_prefetch=2,
            grid=(n_rows // te,),
            in_specs=[pl.BlockSpec(memory_space=pltpu.HBM)],
            out_specs=pl.BlockSpec((te, D_PACKED), lambda j, *_: (j, 0)),
            scratch_shapes=[pltpu.VMEM((2, te, D_PACKED), h.dtype), pltpu.SemaphoreType.DMA((2,))],
        ),
        compiler_params=_cparams(("arbitrary",)),
        name="moe_dispatch",
    )(n_tiles, src, h)


def _moe_expert_kernel(tile_expert_ref, n_tiles_ref, xs_ref, wg_ref, wu_ref, wd_ref, ys_ref):
    del tile_expert_ref
    j = pl.program_id(0)

    @pl.when(j < n_tiles_ref[0])
    def _():
        x_lo, x_hi = _unpack_bf16_pairs(xs_ref[...])

        def project(w_ref):
            return (_dot(x_lo, w_ref[:D_PACKED, :].astype(BF16)) + _dot(x_hi, w_ref[D_PACKED:, :].astype(BF16)))

        gate = project(wg_ref)
        up = project(wu_ref)
        hid = gate / (1.0 + jnp.exp(-gate)) * up
        ys_ref[...] = _dot(hid.astype(BF16), wd_ref[...].astype(BF16))

    @pl.when(j >= n_tiles_ref[0])
    def _():
        ys_ref[...] = jnp.zeros(ys_ref.shape, F32)


def _moe_experts(xs, tile_expert, n_tiles, wg, wu, wd, l, *, te):
    n_rows = xs.shape[0]
    w_in_spec = pl.BlockSpec((None, None, D_MODEL, D_EXPERT), lambda j, te_ref, nt_ref: (l, te_ref[j], 0, 0))
    w_out_spec = pl.BlockSpec((None, None, D_EXPERT, D_MODEL), lambda j, te_ref, nt_ref: (l, te_ref[j], 0, 0))
    xs_spec = pl.BlockSpec((te, D_PACKED), lambda j, te_ref, nt_ref: (j, 0))
    return pl.pallas_call(
        _moe_expert_kernel,
        out_shape=jax.ShapeDtypeStruct((n_rows, D_MODEL), F32),
        grid_spec=pltpu.PrefetchScalarGridSpec(
            num_scalar_prefetch=2,
            grid=(n_rows // te,),
            in_specs=[xs_spec, w_in_spec, w_in_spec, w_out_spec],
            out_specs=pl.BlockSpec((te, D_MODEL), lambda j, *_: (j, 0)),
        ),
        compiler_params=_cparams(("arbitrary",)),
        name="moe_experts",
    )(tile_expert, n_tiles, xs, wg, wu, wd)


TM_COMBINE = 256


def _moe_combine_kernel(dest_ref, ys_hbm, x_ref, rec_t_ref, g_ref, gain_ref, bias_ref, o_ref, y_buf, sems,
                        *, n_tokens):
    i = pl.program_id(0)
    tm = x_ref.shape[0]
    slot = i % 2

    def start_gather(tile, s):
        def body(r, carry):
            for k in range(2):
                row = dest_ref[k * n_tokens + tile * tm + r]
                _row_copy(ys_hbm, row, y_buf.at[s, k], r, sems.at[s]).start(priority=k)
            return carry
        lax.fori_loop(0, tm, body, 0, unroll=ROW_DMA_UNROLL)

    def wait_gather(s):
        def body(r, carry):
            for k in range(2):
                _row_copy(ys_hbm, 0, y_buf.at[s, k], r, sems.at[s]).wait()
            return carry
        lax.fori_loop(0, tm, body, 0, unroll=ROW_DMA_UNROLL)

    @pl.when(i == 0)
    def _():
        start_gather(0, 0)

    @pl.when(i + 1 < pl.num_programs(0))
    def _():
        start_gather(i + 1, 1 - slot)

    wait_gather(slot)
    z = rec_t_ref[:, 4:5] * y_buf[slot, 0] + rec_t_ref[:, 5:6] * y_buf[slot, 1]
    v = ALPHA * x_ref[...] + g_ref[...] * z
    o_ref[...] = _layer_norm_rows(v, gain_ref[...], bias_ref[...])


def _moe_combine(dest, ys, x, rec_t, mod_l, ln_gain_l, ln_bias_l, *, row_of_tile):
    t = x.shape[0]
    tm = TM_COMBINE
    vec = pl.BlockSpec((1, D_MODEL), lambda i, *_: (0, 0))
    return pl.pallas_call(
        functools.partial(_moe_combine_kernel, n_tokens=t),
        out_shape=jax.ShapeDtypeStruct((t, D_MODEL), F32),
        grid_spec=pltpu.PrefetchScalarGridSpec(
            num_scalar_prefetch=1,
            grid=(t // tm,),
            in_specs=[
                pl.BlockSpec(memory_space=pltpu.HBM),
                pl.BlockSpec((tm, D_MODEL), lambda i, *_: (i, 0)),
                pl.BlockSpec((tm, LANES), lambda i, *_: (i, 0)),
                _mod_spec(5, row_of_tile),
                vec, vec,
            ],
            out_specs=pl.BlockSpec((tm, D_MODEL), lambda i, *_: (i, 0)),
            scratch_shapes=[pltpu.VMEM((2, 2, tm, D_MODEL), F32), pltpu.SemaphoreType.DMA((2,))],
        ),
        compiler_params=_cparams(("arbitrary",)),
        name="moe_combine",
    )(dest, ys, x, rec_t, mod_l, ln_gain_l, ln_bias_l)


def _dispatch_tables(rec, cnt, *, te):
    t = rec.shape[1]
    n_rows = 2 * t + N_EXPERTS * te
    e12 = rec[0:2].astype(jnp.int32)
    r12 = rec[2:4].astype(jnp.int32)
    counts = cnt[:, 0].astype(jnp.int32)
    padded = (counts + te - 1) // te * te
    ends = jnp.cumsum(padded)
    offsets = ends - padded
    expert_ids = jnp.arange(N_EXPERTS, dtype=jnp.int32)[:, None, None]
    dest = (jnp.sum(jnp.where(e12[None] == expert_ids, offsets[:, None, None], 0), axis=0) + r12).reshape(2 * t)
    tokens = jnp.tile(jnp.arange(t, dtype=jnp.int32), 2)
    filler = jnp.arange(n_rows, dtype=jnp.int32) % t
    src = filler.at[dest].set(tokens, unique_indices=True)
    tile_start = jnp.arange(n_rows // te, dtype=jnp.int32) * te
    tile_expert = jnp.minimum(jnp.searchsorted(ends, tile_start, side="right"), N_EXPERTS - 1).astype(jnp.int32)
    n_tiles = (ends[-1:] // te).astype(jnp.int32)
    return dest, src, tile_expert, n_tiles


def _rope_tables(n_tokens):
    rows = n_tokens // GRID_W
    row = jnp.repeat(jnp.arange(rows), GRID_W).astype(F32)
    col = jnp.tile(jnp.arange(GRID_W), rows).astype(F32)
    quarter = HEAD_DIM // 4
    inv_freq = ROPE_THETA ** (-jnp.arange(quarter, dtype=F32) / quarter)
    ang_r, ang_c = row[:, None] * inv_freq, col[:, None] * inv_freq
    cos = jnp.concatenate([jnp.cos(ang_r), jnp.cos(ang_r), jnp.cos(ang_c), jnp.cos(ang_c)], axis=-1)
    zero = jnp.zeros_like(ang_r)
    sin = jnp.concatenate([-jnp.sin(ang_r), zero, -jnp.sin(ang_c), zero,
                           zero, jnp.sin(ang_r), zero, jnp.sin(ang_c)], axis=-1)
    return cos, sin


def _row_of_tile(first_row, tokens_per_row):
    def for_tile(tile):
        return lambda i: first_row + (i * tile) // tokens_per_row
    return for_tile


def _mixer_and_ffn(x, attn_arrays, attn_blocks, gates, mod_l, rows, l, expert_tile, w_branch, w_o, ln_gain,
                   ln_bias, w_router_t, w_e_gate, w_e_up, w_e_down):
    y = _merge_branches(attn_arrays, attn_blocks, gates, w_branch, l)
    x = _out_projection(y, w_o, l, x, mod_l, ln_gain[l, 0:1], ln_bias[l, 0:1], row_of_tile=rows(TM))
    h2, rec, rec_t, cnt = _moe_route(x, mod_l, w_router_t, row_of_tile=rows(TM_SMALL))
    dest, src, tile_expert, n_tiles = _dispatch_tables(rec, cnt, te=expert_tile)
    xs = _moe_dispatch(h2, n_tiles, src, te=expert_tile)
    ys = _moe_experts(xs, tile_expert, n_tiles, w_e_gate, w_e_up, w_e_down, l, te=expert_tile)
    return _moe_combine(dest, ys, x, rec_t, mod_l, ln_gain[l, 1:2], ln_bias[l, 1:2],
                        row_of_tile=rows(TM_COMBINE))


def kernel(x_prompt, x_sample, cache_kv_a, cache_kv_b, cache_kv_c, c, c_ctx, w_in, w_gate, w_branch, w_o,
           w_mod, b_mod, ln_gain, ln_bias, diff_lam, diff_subln, qk_gain, sink, w_router, w_e_gate, w_e_up,
           w_e_down):
    batch, seq, _ = x_prompt.shape
    dec_batch, dec_seq, _ = x_sample.shape
    past = cache_kv_a.shape[3]
    t_p, t_s = batch * seq, dec_batch * dec_seq

    cond = jnp.concatenate([c_ctx[None], c, jnp.zeros((MOD_ROWS - 1 - dec_batch, D_MODEL), F32)], axis=0)
    mod = _modulation(cond.T, w_mod, b_mod)
    cos_t, sin_t = _rope_tables(dec_seq)
    w_router_t = w_router.T
    w_in, w_gate, w_branch, w_o = (w.astype(BF16) for w in (w_in, w_gate, w_branch, w_o))
    cache_a = cache_kv_a.reshape(dec_batch, DEPTH, 2, past, 4 * 256)
    cache_b = cache_kv_b.reshape(dec_batch, DEPTH, 2, past, 2 * HEAD_DIM)
    cache_c = cache_kv_c.reshape(dec_batch, DEPTH, 2, past, 2 * HEAD_DIM)
    rows_p = _row_of_tile(0, t_p)
    rows_s = _row_of_tile(1, dec_seq)

    y_p = x_prompt.reshape(t_p, D_MODEL)
    y_s = x_sample.reshape(t_s, D_MODEL)
    new_a, new_b, new_c = [], [], []
    for l in range(DEPTH):
        lam_init = 0.8 - 0.6 * math.exp(-0.3 * l)
        mod_l = mod[l].reshape(MOD_ROWS, 1, N_MOD * D_MODEL)
        subln_l = diff_subln[l].reshape(1, 256)
        shared = (w_branch, w_o, ln_gain, ln_bias, w_router_t, w_e_gate, w_e_up, w_e_down)

        p = _in_projection(y_p, mod_l, w_in, l, qk_gain[l], cos_t, sin_t, row_of_tile=rows_p(TM),
                           rope=False, q_scale=1.0, out_dtype=F32)
        gates = _branch_gates(y_p, mod_l, w_gate, l, row_of_tile=rows_p(TM))
        attn = _attention_prompt(p, diff_lam[l], subln_l, sink[l], lam_init=lam_init, seq=seq)
        y_p = _mixer_and_ffn(y_p, (attn, attn, attn), (0, 1, 2), gates, mod_l, rows_p, l, 256, *shared)
        p5 = p.reshape(batch, seq, IN_COLS)
        new_a.append(jnp.stack([p5[..., COL_AK:COL_AV], p5[..., COL_AV:COL_BQ]], axis=1))
        new_b.append(jnp.stack([p5[..., COL_BK:COL_BV], p5[..., COL_BV:COL_CQ]], axis=1))
        new_c.append(jnp.stack([p5[..., COL_CK:COL_CV], p5[..., COL_CV:]], axis=1))

        qkv = _in_projection(y_s, mod_l, w_in, l, qk_gain[l], cos_t, sin_t, row_of_tile=rows_s(TM),
                             rope=True, q_scale=SCALE * LOG2E, out_dtype=BF16)
        gates = _branch_gates(y_s, mod_l, w_gate, l, row_of_tile=rows_s(TM))
        vec = lambda shape: pl.BlockSpec(shape, lambda b, qi, kk: (0, 0))
        a_o = _flash_sample(functools.partial(_diff_sample_kernel, lam_init=lam_init), qkv, cache_a, l,
                            q_col=COL_AQ, k_col=COL_AK, v_col=COL_AV, kv_width=1024, n_state=8,
                            row_sum_scratch=True, extra=(diff_lam[l], subln_l),
                            extra_specs=[vec((4, HEAD_DIM)), vec((1, 256))], name="diff_attention",
                            dec_seq=dec_seq)
        b_o = _flash_sample(_gqa_sample_kernel, qkv, cache_b, l, q_col=COL_BQ, k_col=COL_BK, v_col=COL_BV,
                            kv_width=256, n_state=8, row_sum_scratch=False, extra=(), extra_specs=[],
                            name="gqa_attention", dec_seq=dec_seq)
        c_o = _window_sample(qkv, cache_c, sink[l], l, dec_seq=dec_seq)
        y_s = _mixer_and_ffn(y_s, (a_o, b_o, c_o), (0, 0, 0), gates, mod_l, rows_s, l, 512, *shared)

    new_kv_a = jnp.stack(new_a, axis=1).reshape(batch, DEPTH, 2, seq, 4, 256)
    new_kv_b = jnp.stack(new_b, axis=1).reshape(batch, DEPTH, 2, seq, 2, HEAD_DIM)
    new_kv_c = jnp.stack(new_c, axis=1).reshape(batch, DEPTH, 2, seq, 2, HEAD_DIM)
    return (y_p.reshape(batch, seq, D_MODEL), y_s.reshape(dec_batch, dec_seq, D_MODEL),
            new_kv_a, new_kv_b, new_kv_c)
```

```python
import functools
import math

import jax
import jax.numpy as jnp
from jax import lax
from jax.experimental import pallas as pl
from jax.experimental.pallas import tpu as pltpu

F32 = jnp.float32
BF16 = jnp.bfloat16

D_MODEL = 2048
HEAD_DIM = 128
GRID_W = 64
ROPE_THETA = 10000.0
WINDOW = 128
N_EXPERTS = 16
N_GROUPS = 4
EXPERTS_PER_GROUP = N_EXPERTS // N_GROUPS
D_EXPERT = 512
N_MOD = 6
DEPTH = 2
ALPHA = (2 * DEPTH) ** 0.25
EPS = 1e-6
IN_COLS = 6144
BRANCH_WIDTH = 1024
SCALE = HEAD_DIM ** -0.5
LOG2E = math.log2(math.e)
NEG_BIG = -1e30

COL_AQ, COL_AK, COL_AV = 0, 1024, 2048
COL_BQ, COL_BK, COL_BV = 3072, 4096, 4352
COL_CQ, COL_CK, COL_CV = 4608, 5632, 5888

LANES = 128
VMEM_LIMIT = 56 * 1024 * 1024

TM = 1024
TN_PROJ = 1024
TN_PROJ_F32 = 1024
TN_GATE = 1024
TM_SMALL = 512


def _cparams(sem):
    return pltpu.CompilerParams(dimension_semantics=sem, vmem_limit_bytes=VMEM_LIMIT)


def _dot(a, b):
    return jnp.dot(a, b, preferred_element_type=F32)


def _dot_nt(a, b):
    return lax.dot_general(a, b, (((1,), (1,)), ((), ())), preferred_element_type=F32)


def _norm_rows(x):
    mu = jnp.mean(x, axis=-1, keepdims=True)
    xc = x - mu
    var = jnp.mean(xc * xc, axis=-1, keepdims=True)
    return xc * lax.rsqrt(var + EPS)


def _rms(v, gain):
    ms = jnp.mean(v * v, axis=-1, keepdims=True)
    return v * lax.rsqrt(ms + EPS) * gain


def _rope(v, cos, sin_from_upper, sin_from_lower):
    return v * cos + pltpu.roll(v, 96, 1) * sin_from_upper + pltpu.roll(v, 32, 1) * sin_from_lower


N_COND = 3
MOD_ROWS = 8
TN_MOD = 1024


def _mod_kernel(cond_ref, w_ref, b_ref, o_ref):
    w = w_ref[...]
    row_idx = lax.broadcasted_iota(jnp.int32, (MOD_ROWS, TN_MOD), 0)
    out = jnp.zeros((MOD_ROWS, TN_MOD), F32)
    for r in range(N_COND):
        c = cond_ref[:, r:r + 1]
        s = c / (1.0 + jnp.exp(-c))
        m = jnp.sum(w * s, axis=0, keepdims=True) + b_ref[...]
        out = jnp.where(row_idx == r, m, out)
    o_ref[...] = out


def _modulation(cond_t, w_mod, b_mod):
    n = N_MOD * D_MODEL
    return pl.pallas_call(
        _mod_kernel,
        out_shape=jax.ShapeDtypeStruct((DEPTH, MOD_ROWS, n), F32),
        grid=(DEPTH, n // TN_MOD),
        in_specs=[
            pl.BlockSpec((D_MODEL, MOD_ROWS), lambda l, j: (0, 0)),
            pl.BlockSpec((None, D_MODEL, TN_MOD), lambda l, j: (l, 0, j)),
            pl.BlockSpec((None, 1, TN_MOD), lambda l, j: (l, 0, j)),
        ],
        out_specs=pl.BlockSpec((None, MOD_ROWS, TN_MOD), lambda l, j: (l, 0, j)),
        compiler_params=_cparams(("parallel", "parallel")),
        name="modulation",
    )(cond_t, w_mod, b_mod.reshape(DEPTH, 1, n))


def _modulate_to_scratch(x_ref, sh_ref, sc_ref, h_scr):
    h = _norm_rows(x_ref[...]) * (1.0 + sc_ref[...]) + sh_ref[...]
    h_scr[...] = h.astype(BF16)


def _head_chunk_kinds():
    kinds = []
    for n_chunks, kind in ((8, (None, True, True)), (8, (None, True, False)), (8, (None, False, False)),
                           (8, ("q", True, True)), (2, ("k", True, False)), (2, (None, False, False)),
                           (8, (None, True, True)), (2, (None, True, False)), (2, (None, False, False))):
        kinds.extend([kind] * n_chunks)
    return kinds


_HEAD_CHUNK_KINDS = _head_chunk_kinds()


def _inproj_kernel(x_ref, sh_ref, sc_ref, w_ref, qk_gain_ref, cos_ref, sin_ref, o_ref, h_scr, *, rope,
                   q_scale):
    j = pl.program_id(1)
    gain_q = qk_gain_ref[0:1, :]
    gain_k = qk_gain_ref[1:2, :]

    @pl.when(j == 0)
    def _():
        _modulate_to_scratch(x_ref, sh_ref, sc_ref, h_scr)

    acc = _dot(h_scr[...], w_ref[...])
    tn = o_ref.shape[1]
    n_chunks = tn // LANES

    def chunk(c):
        return acc[:, c * LANES:(c + 1) * LANES]

    def store(c, v):
        o_ref[:, c * LANES:(c + 1) * LANES] = v.astype(o_ref.dtype)

    def rp(v):
        return _rope(v, cos_ref[...], sin_ref[:, :HEAD_DIM], sin_ref[:, HEAD_DIM:]) if rope else v

    def qs(v):
        return v * q_scale if q_scale != 1.0 else v

    def finish(c, kind):
        norm, rotate, is_query = kind
        v = chunk(c)
        if norm is not None:
            v = _rms(v, gain_q if norm == "q" else gain_k)
        if rotate:
            v = rp(v)
        store(c, qs(v) if is_query else v)

    tile_kinds = [tuple(_HEAD_CHUNK_KINDS[t * n_chunks:(t + 1) * n_chunks]) for t in range(IN_COLS // tn)]
    for kinds in dict.fromkeys(tile_kinds):
        tiles = [t for t, k in enumerate(tile_kinds) if k == kinds]
        cond = j == tiles[0]
        for t in tiles[1:]:
            cond = cond | (j == t)

        @pl.when(cond)
        def _(kinds=kinds):
            for c, kind in enumerate(kinds):
                finish(c, kind)


def _gate_kernel(x_ref, sh_ref, sc_ref, w_ref, o_ref, h_scr):
    @pl.when(pl.program_id(1) == 0)
    def _():
        _modulate_to_scratch(x_ref, sh_ref, sc_ref, h_scr)

    acc = _dot(h_scr[...], w_ref[...])
    o_ref[...] = (1.0 / (1.0 + jnp.exp(-acc))).astype(o_ref.dtype)


def _mod_spec(which, row_of_tile):
    return pl.BlockSpec((None, 1, D_MODEL), lambda i, *_: (row_of_tile(i), 0, which))


def _in_projection(x, mod_l, w_in, l, qk_gain_l, cos_t, sin_t, *, row_of_tile, rope, q_scale, out_dtype):
    t = x.shape[0]
    tiles_per_seq = cos_t.shape[0] // TM
    tn = TN_PROJ_F32 if out_dtype == F32 else TN_PROJ
    return pl.pallas_call(
        functools.partial(_inproj_kernel, rope=rope, q_scale=q_scale),
        out_shape=jax.ShapeDtypeStruct((t, IN_COLS), out_dtype),
        grid=(t // TM, IN_COLS // tn),
        in_specs=[
            pl.BlockSpec((TM, D_MODEL), lambda i, j: (i, 0)),
            _mod_spec(0, row_of_tile),
            _mod_spec(1, row_of_tile),
            pl.BlockSpec((None, D_MODEL, tn), lambda i, j: (l, 0, j)),
            pl.BlockSpec((2, HEAD_DIM), lambda i, j: (0, 0)),
            pl.BlockSpec((TM, HEAD_DIM), lambda i, j: (i % tiles_per_seq, 0)),
            pl.BlockSpec((TM, 2 * HEAD_DIM), lambda i, j: (i % tiles_per_seq, 0)),
        ],
        out_specs=pl.BlockSpec((TM, tn), lambda i, j: (i, j)),
        scratch_shapes=[pltpu.VMEM((TM, D_MODEL), BF16)],
        compiler_params=_cparams(("parallel", "arbitrary")),
        name="in_projection",
    )(x, mod_l, mod_l, w_in, qk_gain_l, cos_t, sin_t)


def _branch_gates(x, mod_l, w_gate, l, *, row_of_tile):
    t = x.shape[0]
    n = w_gate.shape[2]
    return pl.pallas_call(
        _gate_kernel,
        out_shape=jax.ShapeDtypeStruct((t, n), BF16),
        grid=(t // TM, n // TN_GATE),
        in_specs=[
            pl.BlockSpec((TM, D_MODEL), lambda i, j: (i, 0)),
            _mod_spec(0, row_of_tile),
            _mod_spec(1, row_of_tile),
            pl.BlockSpec((None, D_MODEL, TN_GATE), lambda i, j: (l, 0, j)),
        ],
        out_specs=pl.BlockSpec((TM, TN_GATE), lambda i, j: (i, j)),
        scratch_shapes=[pltpu.VMEM((TM, D_MODEL), BF16)],
        compiler_params=_cparams(("parallel", "arbitrary")),
        name="branch_gates",
    )(x, mod_l, mod_l, w_gate)


def _diff_lambda(lam_ref, lam_init):
    lp = lam_ref[...]
    t1 = jnp.sum(lp[0:1] * lp[1:2], axis=-1, keepdims=True)
    t2 = jnp.sum(lp[2:3] * lp[3:4], axis=-1, keepdims=True)
    return jnp.exp(t1) - jnp.exp(t2) + lam_init


def _softmax_rows(s, sink=None):
    m = jnp.max(s, axis=-1, keepdims=True)
    if sink is not None:
        m = jnp.maximum(m, sink)
    e = jnp.exp(s - m)
    den = jnp.sum(e, axis=-1, keepdims=True)
    if sink is not None:
        den = den + jnp.exp(sink - m)
    return e * (1.0 / den)


def _attn_prompt_kernel(p_ref, lam_ref, subln_ref, sink_ref, o_ref, *, lam_init):
    lam = _diff_lambda(lam_ref, lam_init)

    def blk(c0, w):
        return p_ref[:, c0:c0 + w].astype(BF16)

    for h in range(4):
        probs = []
        for m in range(2):
            q = blk(COL_AQ + h * 256 + m * HEAD_DIM, HEAD_DIM)
            k = blk(COL_AK + h * 256 + m * HEAD_DIM, HEAD_DIM)
            probs.append(_softmax_rows(_dot_nt(q, k) * SCALE))
        w = probs[0] - lam * probs[1]
        o = _dot(w.astype(BF16), blk(COL_AV + h * 256, 256))
        o = _rms(o, subln_ref[...]) * (1.0 - lam_init)
        o_ref[:, h * 256:(h + 1) * 256] = o.astype(o_ref.dtype)

    for mixer, (cq, ck, cv) in enumerate(((COL_BQ, COL_BK, COL_BV), (COL_CQ, COL_CK, COL_CV))):
        for kvh in range(2):
            k = blk(ck + kvh * HEAD_DIM, HEAD_DIM)
            v = blk(cv + kvh * HEAD_DIM, HEAD_DIM)
            for g in range(4):
                hq = kvh * 4 + g
                q = blk(cq + hq * HEAD_DIM, HEAD_DIM)
                sink = sink_ref[hq] if mixer == 1 else None
                p = _softmax_rows(_dot_nt(q, k) * SCALE, sink)
                o = _dot(p.astype(BF16), v)
                c0 = BRANCH_WIDTH * (1 + mixer) + hq * HEAD_DIM
                o_ref[:, c0:c0 + HEAD_DIM] = o.astype(o_ref.dtype)


def _attention_prompt(p, diff_lam_l, subln_l, sink_l, *, lam_init, seq):
    t = p.shape[0]
    return pl.pallas_call(
        functools.partial(_attn_prompt_kernel, lam_init=lam_init),
        out_shape=jax.ShapeDtypeStruct((t, 3 * BRANCH_WIDTH), BF16),
        grid=(t // seq,),
        in_specs=[
            pl.BlockSpec((seq, IN_COLS), lambda b: (b, 0)),
            pl.BlockSpec((4, HEAD_DIM), lambda b: (0, 0)),
            pl.BlockSpec((1, 256), lambda b: (0, 0)),
            pl.BlockSpec(memory_space=pltpu.SMEM),
        ],
        out_specs=pl.BlockSpec((seq, 3 * BRANCH_WIDTH), lambda b: (b, 0)),
        compiler_params=_cparams(("parallel",)),
        name="attention_prompt",
    )(p, diff_lam_l, subln_l, sink_l)


TQ = 512
TQ_FLASH = 1024
TK = 1024


def _tile_lanes(v, n):
    return jnp.concatenate([v] * n, axis=-1) if n > 1 else v


def _online_softmax(idx, s, m_scr):
    m_prev = m_scr[idx]
    m_new = jnp.maximum(m_prev, jnp.max(s, axis=-1, keepdims=True))
    m_scr[idx] = m_new
    alpha = jnp.exp2(m_prev - m_new)
    p = jnp.exp2(s - _tile_lanes(m_new, s.shape[1] // LANES))
    return p, alpha


def _diff_sample_kernel(q_ref, k_ref, v_ref, kc_ref, vc_ref, lam_ref, subln_ref, o_ref,
                        m_scr, l_scr, acc_scr, *, lam_init):
    kk = pl.program_id(2)

    @pl.when(kk == 0)
    def _():
        m_scr[...] = jnp.full(m_scr.shape, NEG_BIG, F32)
        l_scr[...] = jnp.zeros(l_scr.shape, F32)
        acc_scr[...] = jnp.zeros(acc_scr.shape, F32)

    def process(kb_ref, vb_ref):
        for h in range(4):
            v = vb_ref[:, h * 256:(h + 1) * 256].astype(BF16)
            for m in range(2):
                c0 = h * 256 + m * HEAD_DIM
                idx = h * 2 + m
                k = kb_ref[:, c0:c0 + HEAD_DIM].astype(BF16)
                p, alpha = _online_softmax(idx, _dot_nt(q_ref[:, c0:c0 + HEAD_DIM], k), m_scr)
                part = p[:, 0:LANES]
                for c in range(1, p.shape[1] // LANES):
                    part = part + p[:, c * LANES:(c + 1) * LANES]
                l_scr[idx] = alpha * l_scr[idx] + part
                acc_scr[idx] = _tile_lanes(alpha, 2) * acc_scr[idx] + _dot(p.astype(BF16), v)

    @pl.when(kk == 0)
    def _():
        process(kc_ref, vc_ref)

    @pl.when(kk > 0)
    def _():
        process(k_ref, v_ref)

    @pl.when(kk == pl.num_programs(2) - 1)
    def _():
        lam = _diff_lambda(lam_ref, lam_init)
        for h in range(4):
            l1 = jnp.sum(l_scr[2 * h], axis=-1, keepdims=True)
            l2 = jnp.sum(l_scr[2 * h + 1], axis=-1, keepdims=True)
            o1 = acc_scr[2 * h] * (1.0 / l1)
            o2 = acc_scr[2 * h + 1] * (1.0 / l2)
            o = _rms(o1 - lam * o2, subln_ref[...]) * (1.0 - lam_init)
            o_ref[:, h * 256:(h + 1) * 256] = o.astype(o_ref.dtype)


def _gqa_sample_kernel(q_ref, k_ref, v_ref, kc_ref, vc_ref, o_ref, m_scr, acc_scr):
    kk = pl.program_id(2)

    @pl.when(kk == 0)
    def _():
        m_scr[...] = jnp.full(m_scr.shape, NEG_BIG, F32)
        acc_scr[...] = jnp.zeros(acc_scr.shape, F32)

    def process(kb_ref, vb_ref):
        for kvh in range(2):
            k = kb_ref[:, kvh * HEAD_DIM:(kvh + 1) * HEAD_DIM].astype(BF16)
            v = vb_ref[:, kvh * HEAD_DIM:(kvh + 1) * HEAD_DIM].astype(BF16)
            v_ones = jnp.concatenate([v, jnp.ones_like(v)], axis=-1)
            for g in range(4):
                hq = kvh * 4 + g
                s = _dot_nt(q_ref[:, hq * HEAD_DIM:(hq + 1) * HEAD_DIM], k)
                p, alpha = _online_softmax(hq, s, m_scr)
                acc_scr[hq] = _tile_lanes(alpha, 2) * acc_scr[hq] + _dot(p.astype(BF16), v_ones)

    @pl.when(kk == 0)
    def _():
        process(kc_ref, vc_ref)

    @pl.when(kk > 0)
    def _():
        process(k_ref, v_ref)

    @pl.when(kk == pl.num_programs(2) - 1)
    def _():
        for hq in range(8):
            o = acc_scr[hq, :, 0:HEAD_DIM] / acc_scr[hq, :, HEAD_DIM:2 * HEAD_DIM]
            o_ref[:, hq * HEAD_DIM:(hq + 1) * HEAD_DIM] = o.astype(o_ref.dtype)


def _flash_sample(kernel, qkv, cache, l, *, q_col, k_col, v_col, kv_width, n_state, row_sum_scratch, extra,
                  extra_specs, name, dec_seq):
    t = qkv.shape[0]
    nb = t // dec_seq
    tq = TQ_FLASH
    nq = dec_seq // tq
    nk = dec_seq // TK
    past = cache.shape[3]
    q_blk, k_blk, v_blk = q_col // BRANCH_WIDTH, k_col // kv_width, v_col // kv_width
    kv_row = lambda b, qi, kk: b * nk + jnp.maximum(kk - 1, 0)
    return pl.pallas_call(
        kernel,
        out_shape=jax.ShapeDtypeStruct((t, BRANCH_WIDTH), BF16),
        grid=(nb, nq, nk + 1),
        in_specs=[
            pl.BlockSpec((tq, BRANCH_WIDTH), lambda b, qi, kk: (b * nq + qi, q_blk)),
            pl.BlockSpec((TK, kv_width), lambda b, qi, kk: (kv_row(b, qi, kk), k_blk)),
            pl.BlockSpec((TK, kv_width), lambda b, qi, kk: (kv_row(b, qi, kk), v_blk)),
            pl.BlockSpec((None, None, None, past, kv_width), lambda b, qi, kk: (b, l, 0, 0, 0)),
            pl.BlockSpec((None, None, None, past, kv_width), lambda b, qi, kk: (b, l, 1, 0, 0)),
        ] + extra_specs,
        out_specs=pl.BlockSpec((tq, BRANCH_WIDTH), lambda b, qi, kk: (b * nq + qi, 0)),
        scratch_shapes=[pltpu.VMEM((n_state, tq, LANES), F32)] * (2 if row_sum_scratch else 1)
        + [pltpu.VMEM((n_state, tq, 2 * LANES), F32)],
        compiler_params=_cparams(("parallel", "parallel", "arbitrary")),
        name=name,
    )(qkv, qkv, qkv, cache, cache, *extra)


def _window_sample_kernel(q0_ref, q1_ref, kp_ref, kc_ref, kn_ref, vp_ref, vc_ref, vn_ref, kctx_ref, vctx_ref,
                          sink_ref, o_ref, *, dec_seq):
    qi = pl.program_id(1)
    q_start = qi * TQ
    half = TQ // 2
    q_pos = q_start + lax.broadcasted_iota(jnp.int32, (TQ, 1), 0)
    segs = ((kp_ref, vp_ref, q_start - half, half), (kc_ref, vc_ref, q_start, TQ),
            (kn_ref, vn_ref, q_start + TQ, half))
    valid = []
    for _, _, start, n in segs:
        k_pos = start + lax.broadcasted_iota(jnp.int32, (TQ, n), 1)
        valid.append((jnp.abs(q_pos - k_pos) <= WINDOW) & (k_pos >= 0) & (k_pos < dec_seq))
    for kvh in range(2):
        q_ref = q0_ref if kvh == 0 else q1_ref
        lo, hi = kvh * HEAD_DIM, (kvh + 1) * HEAD_DIM
        kctx = kctx_ref[:, lo:hi].astype(BF16)
        vctx = vctx_ref[:, lo:hi].astype(BF16)
        for g in range(4):
            hq = kvh * 4 + g
            q = q_ref[:, g * HEAD_DIM:(g + 1) * HEAD_DIM]
            sink = sink_ref[hq] * LOG2E
            scores = [_dot_nt(q, kctx)]
            for (k_ref, _, _, _), ok in zip(segs, valid):
                scores.append(jnp.where(ok, _dot_nt(q, k_ref[:, lo:hi]), NEG_BIG))
            m = jnp.maximum(scores[0].max(axis=-1, keepdims=True), sink)
            for s in scores[1:]:
                m = jnp.maximum(m, s.max(axis=-1, keepdims=True))
            es = [jnp.exp2(s - m) for s in scores]
            den = jnp.exp2(sink - m)
            for e in es:
                den = den + jnp.sum(e, axis=-1, keepdims=True)
            o = _dot(es[0].astype(BF16), vctx)
            for e, (_, v_ref, _, _) in zip(es[1:], segs):
                o = o + _dot(e.astype(BF16), v_ref[:, lo:hi])
            o = o * (1.0 / den)
            o_ref[:, hq * HEAD_DIM:(hq + 1) * HEAD_DIM] = o.astype(o_ref.dtype)


def _window_sample(qkv, cache, sink_l, l, *, dec_seq):
    t = qkv.shape[0]
    nb = t // dec_seq
    nq = dec_seq // TQ
    half = TQ // 2
    n_half = dec_seq // half
    past = cache.shape[3]
    kvw = 2 * HEAD_DIM
    q_blk = COL_CQ // 512
    k_blk, v_blk = COL_CK // kvw, COL_CV // kvw
    prev_row = lambda b, qi: b * n_half + jnp.maximum(2 * qi - 1, 0)
    next_row = lambda b, qi: b * n_half + jnp.minimum(2 * qi + 2, n_half - 1)
    return pl.pallas_call(
        functools.partial(_window_sample_kernel, dec_seq=dec_seq),
        out_shape=jax.ShapeDtypeStruct((t, BRANCH_WIDTH), BF16),
        grid=(nb, nq),
        in_specs=[
            pl.BlockSpec((TQ, 512), lambda b, qi: (b * nq + qi, q_blk)),
            pl.BlockSpec((TQ, 512), lambda b, qi: (b * nq + qi, q_blk + 1)),
            pl.BlockSpec((half, kvw), lambda b, qi: (prev_row(b, qi), k_blk)),
            pl.BlockSpec((TQ, kvw), lambda b, qi: (b * nq + qi, k_blk)),
            pl.BlockSpec((half, kvw), lambda b, qi: (next_row(b, qi), k_blk)),
            pl.BlockSpec((half, kvw), lambda b, qi: (prev_row(b, qi), v_blk)),
            pl.BlockSpec((TQ, kvw), lambda b, qi: (b * nq + qi, v_blk)),
            pl.BlockSpec((half, kvw), lambda b, qi: (next_row(b, qi), v_blk)),
            pl.BlockSpec((None, None, None, past, kvw), lambda b, qi: (b, l, 0, 0, 0)),
            pl.BlockSpec((None, None, None, past, kvw), lambda b, qi: (b, l, 1, 0, 0)),
            pl.BlockSpec(memory_space=pltpu.SMEM),
        ],
        out_specs=pl.BlockSpec((TQ, BRANCH_WIDTH), lambda b, qi: (b * nq + qi, 0)),
        compiler_params=_cparams(("parallel", "parallel")),
        name="window_attention",
    )(qkv, qkv, qkv, qkv, qkv, qkv, qkv, qkv, cache, cache, sink_l)


def _merge_kernel(oa_ref, ob_ref, oc_ref, g_ref, w_ref, y_ref, acc_scr):
    r = pl.program_id(2)

    def contrib(o_ref):
        return g_ref[...].astype(F32) * _dot(o_ref[...], w_ref[...])

    @pl.when(r == 0)
    def _():
        acc_scr[...] = contrib(oa_ref)

    @pl.when(r == 1)
    def _():
        acc_scr[...] += contrib(ob_ref)

    @pl.when(r == 2)
    def _():
        y_ref[...] = (acc_scr[...] + contrib(oc_ref)).astype(y_ref.dtype)


def _merge_branches(o_arrays, o_blocks, gates, w_branch, l):
    t = gates.shape[0]
    tn = D_MODEL
    nn = D_MODEL // tn
    o_specs = [pl.BlockSpec((TM, BRANCH_WIDTH), functools.partial(lambda i, n, r, blk: (i, blk), blk=blk))
               for blk in o_blocks]
    return pl.pallas_call(
        _merge_kernel,
        out_shape=jax.ShapeDtypeStruct((t, D_MODEL), BF16),
        grid=(t // TM, nn, 3),
        in_specs=o_specs + [
            pl.BlockSpec((TM, tn), lambda i, n, r: (i, r * nn + n)),
            pl.BlockSpec((None, None, BRANCH_WIDTH, tn), lambda i, n, r: (l, r, 0, n)),
        ],
        out_specs=pl.BlockSpec((TM, tn), lambda i, n, r: (i, n)),
        scratch_shapes=[pltpu.VMEM((TM, tn), F32)],
        compiler_params=_cparams(("parallel", "parallel", "arbitrary")),
        name="merge_branches",
    )(*o_arrays, gates, w_branch)


def _layer_norm_rows(v, gain, bias):
    return _norm_rows(v) * gain + bias


def _outproj_kernel(y_ref, w_ref, x_ref, g_ref, gain_ref, bias_ref, o_ref):
    n = pl.program_id(1)
    tn = w_ref.shape[1]
    n_blocks = o_ref.shape[1] // tn
    z = _dot(y_ref[...], w_ref[...])
    for b in range(n_blocks):
        @pl.when(n == b)
        def _(b=b):
            cols = slice(b * tn, (b + 1) * tn)
            o_ref[:, cols] = ALPHA * x_ref[:, cols] + g_ref[:, cols] * z

    @pl.when(n == n_blocks - 1)
    def _():
        o_ref[...] = _layer_norm_rows(o_ref[...], gain_ref[...], bias_ref[...])


def _out_projection(y, w_o, l, x, mod_l, ln_gain_l, ln_bias_l, *, row_of_tile):
    t = x.shape[0]
    tn = 512
    return pl.pallas_call(
        _outproj_kernel,
        out_shape=jax.ShapeDtypeStruct((t, D_MODEL), F32),
        grid=(t // TM, D_MODEL // tn),
        in_specs=[
            pl.BlockSpec((TM, D_MODEL), lambda i, n: (i, 0)),
            pl.BlockSpec((None, D_MODEL, tn), lambda i, n: (l, 0, n)),
            pl.BlockSpec((TM, D_MODEL), lambda i, n: (i, 0)),
            _mod_spec(2, row_of_tile),
            pl.BlockSpec((1, D_MODEL), lambda i, n: (0, 0)),
            pl.BlockSpec((1, D_MODEL), lambda i, n: (0, 0)),
        ],
        out_specs=pl.BlockSpec((TM, D_MODEL), lambda i, n: (i, 0)),
        compiler_params=_cparams(("parallel", "arbitrary")),
        name="out_projection",
    )(y, w_o, x, mod_l, ln_gain_l, ln_bias_l)


def _route(p):
    rows = [p[e:e + 1, :] for e in range(N_EXPERTS)]
    best_score, best_group = None, None
    for g in range(N_GROUPS):
        members = rows[g * EXPERTS_PER_GROUP:(g + 1) * EXPERTS_PER_GROUP]
        score = None
        for a in range(EXPERTS_PER_GROUP):
            for b in range(a + 1, EXPERTS_PER_GROUP):
                pair = members[a] + members[b]
                score = pair if score is None else jnp.maximum(score, pair)
        if g == 0:
            best_score, best_group = score, jnp.zeros(score.shape, F32)
        else:
            better = score > best_score
            best_group = jnp.where(better, float(g), best_group)
            best_score = jnp.where(better, score, best_score)
    e_idx = lax.broadcasted_iota(jnp.int32, p.shape, 0).astype(F32)
    g_idx = jnp.floor(e_idx * (1.0 / EXPERTS_PER_GROUP))
    masked = jnp.where(g_idx == best_group, p, -1.0)
    w1 = jnp.max(masked, axis=0, keepdims=True)
    i1 = jnp.min(jnp.where(masked == w1, e_idx, float(N_EXPERTS)), axis=0, keepdims=True)
    masked2 = jnp.where(e_idx == i1, -2.0, masked)
    w2 = jnp.max(masked2, axis=0, keepdims=True)
    i2 = jnp.min(jnp.where(masked2 == w2, e_idx, float(N_EXPERTS)), axis=0, keepdims=True)
    tot = w1 + w2
    return e_idx, i1, i2, w1 / tot, w2 / tot


ROUTE_ROWS = 8
D_PACKED = D_MODEL // 2


def _pack_bf16_pairs(xb):
    half = xb.shape[1] // 2
    lo = lax.bitcast_convert_type(xb[:, :half].astype(F32), jnp.uint32)
    hi = lax.bitcast_convert_type(xb[:, half:].astype(F32), jnp.uint32)
    return (hi & jnp.uint32(0xFFFF0000)) | (lo >> 16)


def _unpack_bf16_pairs(words):
    lo = lax.bitcast_convert_type(words << 16, F32).astype(BF16)
    hi = lax.bitcast_convert_type(words & jnp.uint32(0xFFFF0000), F32).astype(BF16)
    return lo, hi


def _moe_route_kernel(x_ref, sh_ref, sc_ref, wr_ref, h_ref, rec_ref, rec_t_ref, cnt_ref, carry_scr):
    @pl.when(pl.program_id(0) == 0)
    def _():
        carry_scr[...] = jnp.zeros(carry_scr.shape, F32)

    h = _norm_rows(x_ref[...]) * (1.0 + sc_ref[...]) + sh_ref[...]
    hb = h.astype(BF16)
    h_ref[...] = _pack_bf16_pairs(hb)
    logits = _dot_nt(wr_ref[...].astype(BF16), hb)
    m = jnp.max(logits, axis=0, keepdims=True)
    e = jnp.exp(logits - m)
    probs = e / jnp.sum(e, axis=0, keepdims=True)
    e_idx, i1, i2, w1, w2 = _route(probs)
    tm = probs.shape[1]
    oh1 = (e_idx == i1).astype(F32)
    oh2 = (e_idx == i2).astype(F32)
    oh = oh1 + oh2
    earlier = (lax.broadcasted_iota(jnp.int32, (tm, tm), 0) < lax.broadcasted_iota(jnp.int32, (tm, tm), 1))
    rank = carry_scr[:, 0:1] + _dot(oh.astype(BF16), earlier.astype(BF16))
    r1 = jnp.sum(oh1 * rank, axis=0, keepdims=True)
    r2 = jnp.sum(oh2 * rank, axis=0, keepdims=True)
    carry_scr[...] = carry_scr[...] + jnp.sum(oh, axis=1, keepdims=True)
    cnt_ref[...] = carry_scr[...]
    row = lax.broadcasted_iota(jnp.int32, (ROUTE_ROWS, tm), 0)
    rec = jnp.zeros((ROUTE_ROWS, tm), F32)
    for k, v in enumerate((i1, i2, r1, r2, w1, w2)):
        rec = jnp.where(row == k, v, rec)
    rec_ref[...] = rec
    pad = jnp.zeros((LANES - ROUTE_ROWS, tm), F32)
    rec_t_ref[...] = jnp.concatenate([rec, pad], axis=0).T


def _moe_route(x, mod_l, w_router_t, *, row_of_tile):
    t = x.shape[0]
    tm = TM_SMALL
    return pl.pallas_call(
        _moe_route_kernel,
        out_shape=(jax.ShapeDtypeStruct((t, D_PACKED), jnp.uint32), jax.ShapeDtypeStruct((ROUTE_ROWS, t), F32),
                   jax.ShapeDtypeStruct((t, LANES), F32), jax.ShapeDtypeStruct((N_EXPERTS, LANES), F32)),
        grid=(t // tm,),
        in_specs=[
            pl.BlockSpec((tm, D_MODEL), lambda i: (i, 0)),
            _mod_spec(3, row_of_tile),
            _mod_spec(4, row_of_tile),
            pl.BlockSpec((N_EXPERTS, D_MODEL), lambda i: (0, 0)),
        ],
        out_specs=(pl.BlockSpec((tm, D_PACKED), lambda i: (i, 0)),
                   pl.BlockSpec((ROUTE_ROWS, tm), lambda i: (0, i)),
                   pl.BlockSpec((tm, LANES), lambda i: (i, 0)),
                   pl.BlockSpec((N_EXPERTS, LANES), lambda i: (0, 0))),
        scratch_shapes=[pltpu.VMEM((N_EXPERTS, LANES), F32)],
        compiler_params=_cparams(("arbitrary",)),
        name="moe_route",
    )(x, mod_l, mod_l, w_router_t)


ROW_DMA_UNROLL = 8
GATHER_ORDER_STRIDE = 37


def _row_copy(src_hbm, src_row, dst_buf, dst_row, sem):
    return pltpu.make_async_copy(src_hbm.at[pl.ds(src_row, 1)], dst_buf.at[pl.ds(dst_row, 1)], sem)


def _moe_dispatch_kernel(n_tiles_ref, src_ref, h_hbm, xs_ref, x_buf, sems, *, te):
    j = pl.program_id(0)
    n_valid = n_tiles_ref[0]
    slot = j % 2

    def start_gather(tile, s):
        def body(r, carry):
            row = (r * GATHER_ORDER_STRIDE) % te
            _row_copy(h_hbm, src_ref[tile * te + row], x_buf.at[s], row, sems.at[s]).start()
            return carry
        lax.fori_loop(0, te, body, 0, unroll=ROW_DMA_UNROLL)

    def wait_gather(s):
        def body(r, carry):
            _row_copy(h_hbm, 0, x_buf.at[s], r, sems.at[s]).wait()
            return carry
        lax.fori_loop(0, te, body, 0, unroll=ROW_DMA_UNROLL)

    @pl.when((j == 0) & (n_valid > 0))
    def _():
        start_gather(0, 0)

    @pl.when(j + 1 < n_valid)
    def _():
        start_gather(j + 1, 1 - slot)

    @pl.when(j < n_valid)
    def _():
        wait_gather(slot)
        xs_ref[...] = x_buf[slot]

    @pl.when(j >= n_valid)
    def _():
        xs_ref[...] = jnp.zeros(xs_ref.shape, xs_ref.dtype)


def _moe_dispatch(h, n_tiles, src, *, te):
    n_rows = src.shape[0]
    return pl.pallas_call(
        functools.partial(_moe_dispatch_kernel, te=te),
        out_shape=jax.ShapeDtypeStruct((n_rows, D_PACKED), h.dtype),
        grid_spec=pltpu.PrefetchScalarGridSpec(
            num_scalar_prefetch=2,
            grid=(n_rows // te,),
            in_specs=[pl.BlockSpec(memory_space=pltpu.HBM)],
            out_specs=pl.BlockSpec((te, D_PACKED), lambda j, *_: (j, 0)),
            scratch_shapes=[pltpu.VMEM((2, te, D_PACKED), h.dtype), pltpu.SemaphoreType.DMA((2,))],
        ),
        compiler_params=_cparams(("arbitrary",)),
        name="moe_dispatch",
    )(n_tiles, src, h)


def _moe_expert_kernel(tile_expert_ref, n_tiles_ref, xs_ref, wg_ref, wu_ref, wd_ref, ys_ref,
                       wg_bf, wu_bf, wd_bf):
    j = pl.program_id(0)
    valid = j < n_tiles_ref[0]
    new_expert = (j == 0) | (tile_expert_ref[j] != tile_expert_ref[jnp.maximum(j - 1, 0)])

    @pl.when(valid & new_expert)
    def _():
        wg_bf[...] = wg_ref[...].astype(BF16)
        wu_bf[...] = wu_ref[...].astype(BF16)
        wd_bf[...] = wd_ref[...].astype(BF16)

    @pl.when(valid)
    def _():
        x_lo, x_hi = _unpack_bf16_pairs(xs_ref[...])

        def project(w_bf):
            return _dot(x_lo, w_bf[:D_PACKED, :]) + _dot(x_hi, w_bf[D_PACKED:, :])

        gate = project(wg_bf)
        up = project(wu_bf)
        hid = gate / (1.0 + jnp.exp(-gate)) * up
        ys_ref[...] = _dot(hid.astype(BF16), wd_bf[...])

    @pl.when(jnp.logical_not(valid))
    def _():
        ys_ref[...] = jnp.zeros(ys_ref.shape, F32)


def _moe_experts(xs, tile_expert, n_tiles, wg, wu, wd, l, *, te):
    n_rows = xs.shape[0]
    w_in_spec = pl.BlockSpec((None, None, D_MODEL, D_EXPERT), lambda j, te_ref, nt_ref: (l, te_ref[j], 0, 0))
    w_out_spec = pl.BlockSpec((None, None, D_EXPERT, D_MODEL), lambda j, te_ref, nt_ref: (l, te_ref[j], 0, 0))
    xs_spec = pl.BlockSpec((te, D_PACKED), lambda j, te_ref, nt_ref: (j, 0))
    return pl.pallas_call(
        _moe_expert_kernel,
        out_shape=jax.ShapeDtypeStruct((n_rows, D_MODEL), F32),
        grid_spec=pltpu.PrefetchScalarGridSpec(
            num_scalar_prefetch=2,
            grid=(n_rows // te,),
            in_specs=[xs_spec, w_in_spec, w_in_spec, w_out_spec],
            out_specs=pl.BlockSpec((te, D_MODEL), lambda j, *_: (j, 0)),
            scratch_shapes=[pltpu.VMEM((D_MODEL, D_EXPERT), BF16), pltpu.VMEM((D_MODEL, D_EXPERT), BF16),
                            pltpu.VMEM((D_EXPERT, D_MODEL), BF16)],
        ),
        compiler_params=_cparams(("arbitrary",)),
        name="moe_experts",
    )(tile_expert, n_tiles, xs, wg, wu, wd)


TM_COMBINE = 256


def _moe_combine_kernel(dest_ref, ys_hbm, x_ref, rec_t_ref, g_ref, gain_ref, bias_ref, o_ref, y_buf, sems,
                        *, n_tokens):
    i = pl.program_id(0)
    tm = x_ref.shape[0]
    slot = i % 2

    def start_gather(tile, s):
        def body(r, carry):
            for k in range(2):
                row = dest_ref[k * n_tokens + tile * tm + r]
                _row_copy(ys_hbm, row, y_buf.at[s, k], r, sems.at[s]).start()
            return carry
        lax.fori_loop(0, tm, body, 0, unroll=ROW_DMA_UNROLL)

    def wait_gather(s):
        def body(r, carry):
            for k in range(2):
                _row_copy(ys_hbm, 0, y_buf.at[s, k], r, sems.at[s]).wait()
            return carry
        lax.fori_loop(0, tm, body, 0, unroll=ROW_DMA_UNROLL)

    @pl.when(i == 0)
    def _():
        start_gather(0, 0)

    @pl.when(i + 1 < pl.num_programs(0))
    def _():
        start_gather(i + 1, 1 - slot)

    wait_gather(slot)
    z = rec_t_ref[:, 4:5] * y_buf[slot, 0] + rec_t_ref[:, 5:6] * y_buf[slot, 1]
    v = ALPHA * x_ref[...] + g_ref[...] * z
    o_ref[...] = _layer_norm_rows(v, gain_ref[...], bias_ref[...])


def _moe_combine(dest, ys, x, rec_t, mod_l, ln_gain_l, ln_bias_l, *, row_of_tile):
    t = x.shape[0]
    tm = TM_COMBINE
    vec = pl.BlockSpec((1, D_MODEL), lambda i, *_: (0, 0))
    return pl.pallas_call(
        functools.partial(_moe_combine_kernel, n_tokens=t),
        out_shape=jax.ShapeDtypeStruct((t, D_MODEL), F32),
        grid_spec=pltpu.PrefetchScalarGridSpec(
            num_scalar_prefetch=1,
            grid=(t // tm,),
            in_specs=[
                pl.BlockSpec(memory_space=pltpu.HBM),
                pl.BlockSpec((tm, D_MODEL), lambda i, *_: (i, 0)),
                pl.BlockSpec((tm, LANES), lambda i, *_: (i, 0)),
                _mod_spec(5, row_of_tile),
                vec, vec,
            ],
            out_specs=pl.BlockSpec((tm, D_MODEL), lambda i, *_: (i, 0)),
            scratch_shapes=[pltpu.VMEM((2, 2, tm, D_MODEL), F32), pltpu.SemaphoreType.DMA((2,))],
        ),
        compiler_params=_cparams(("arbitrary",)),
        name="moe_combine",
    )(dest, ys, x, rec_t, mod_l, ln_gain_l, ln_bias_l)


def _dispatch_tables(rec, cnt, *, te):
    t = rec.shape[1]
    n_rows = 2 * t + N_EXPERTS * te
    e12 = rec[0:2].astype(jnp.int32)
    r12 = rec[2:4].astype(jnp.int32)
    counts = cnt[:, 0].astype(jnp.int32)
    padded = (counts + te - 1) // te * te
    ends = jnp.cumsum(padded)
    offsets = ends - padded
    expert_ids = jnp.arange(N_EXPERTS, dtype=jnp.int32)[:, None, None]
    dest = (jnp.sum(jnp.where(e12[None] == expert_ids, offsets[:, None, None], 0), axis=0) + r12).reshape(2 * t)
    tokens = jnp.tile(jnp.arange(t, dtype=jnp.int32), 2)
    filler = jnp.arange(n_rows, dtype=jnp.int32) % t
    src = filler.at[dest].set(tokens, unique_indices=True)
    tile_start = jnp.arange(n_rows // te, dtype=jnp.int32) * te
    tile_expert = jnp.minimum(jnp.searchsorted(ends, tile_start, side="right"), N_EXPERTS - 1).astype(jnp.int32)
    n_tiles = (ends[-1:] // te).astype(jnp.int32)
    return dest, src, tile_expert, n_tiles


def _rope_tables(n_tokens):
    rows = n_tokens // GRID_W
    row = jnp.repeat(jnp.arange(rows), GRID_W).astype(F32)
    col = jnp.tile(jnp.arange(GRID_W), rows).astype(F32)
    quarter = HEAD_DIM // 4
    inv_freq = ROPE_THETA ** (-jnp.arange(quarter, dtype=F32) / quarter)
    ang_r, ang_c = row[:, None] * inv_freq, col[:, None] * inv_freq
    cos = jnp.concatenate([jnp.cos(ang_r), jnp.cos(ang_r), jnp.cos(ang_c), jnp.cos(ang_c)], axis=-1)
    zero = jnp.zeros_like(ang_r)
    sin = jnp.concatenate([-jnp.sin(ang_r), zero, -jnp.sin(ang_c), zero,
                           zero, jnp.sin(ang_r), zero, jnp.sin(ang_c)], axis=-1)
    return cos, sin


def _row_of_tile(first_row, tokens_per_row):
    def for_tile(tile):
        return lambda i: first_row + (i * tile) // tokens_per_row
    return for_tile


def _mixer_and_ffn(x, attn_arrays, attn_blocks, gates, mod_l, rows, l, expert_tile, w_branch, w_o, ln_gain,
                   ln_bias, w_router_t, w_e_gate, w_e_up, w_e_down):
    y = _merge_branches(attn_arrays, attn_blocks, gates, w_branch, l)
    x = _out_projection(y, w_o, l, x, mod_l, ln_gain[l, 0:1], ln_bias[l, 0:1], row_of_tile=rows(TM))
    h2, rec, rec_t, cnt = _moe_route(x, mod_l, w_router_t, row_of_tile=rows(TM_SMALL))
    dest, src, tile_expert, n_tiles = _dispatch_tables(rec, cnt, te=expert_tile)
    xs = _moe_dispatch(h2, n_tiles, src, te=expert_tile)
    ys = _moe_experts(xs, tile_expert, n_tiles, w_e_gate, w_e_up, w_e_down, l, te=expert_tile)
    return _moe_combine(dest, ys, x, rec_t, mod_l, ln_gain[l, 1:2], ln_bias[l, 1:2],
                        row_of_tile=rows(TM_COMBINE))


def kernel(x_prompt, x_sample, cache_kv_a, cache_kv_b, cache_kv_c, c, c_ctx, w_in, w_gate, w_branch, w_o,
           w_mod, b_mod, ln_gain, ln_bias, diff_lam, diff_subln, qk_gain, sink, w_router, w_e_gate, w_e_up,
           w_e_down):
    batch, seq, _ = x_prompt.shape
    dec_batch, dec_seq, _ = x_sample.shape
    past = cache_kv_a.shape[3]
    t_p, t_s = batch * seq, dec_batch * dec_seq

    cond = jnp.concatenate([c_ctx[None], c, jnp.zeros((MOD_ROWS - 1 - dec_batch, D_MODEL), F32)], axis=0)
    mod = _modulation(cond.T, w_mod, b_mod)
    cos_t, sin_t = _rope_tables(dec_seq)
    w_router_t = w_router.T
    w_in, w_gate, w_branch, w_o = (w.astype(BF16) for w in (w_in, w_gate, w_branch, w_o))
    cache_a = cache_kv_a.reshape(dec_batch, DEPTH, 2, past, 4 * 256)
    cache_b = cache_kv_b.reshape(dec_batch, DEPTH, 2, past, 2 * HEAD_DIM)
    cache_c = cache_kv_c.reshape(dec_batch, DEPTH, 2, past, 2 * HEAD_DIM)
    rows_p = _row_of_tile(0, t_p)
    rows_s = _row_of_tile(1, dec_seq)

    y_p = x_prompt.reshape(t_p, D_MODEL)
    y_s = x_sample.reshape(t_s, D_MODEL)
    new_a, new_b, new_c = [], [], []
    for l in range(DEPTH):
        lam_init = 0.8 - 0.6 * math.exp(-0.3 * l)
        mod_l = mod[l].reshape(MOD_ROWS, 1, N_MOD * D_MODEL)
        subln_l = diff_subln[l].reshape(1, 256)
        shared = (w_branch, w_o, ln_gain, ln_bias, w_router_t, w_e_gate, w_e_up, w_e_down)

        p = _in_projection(y_p, mod_l, w_in, l, qk_gain[l], cos_t, sin_t, row_of_tile=rows_p(TM),
                           rope=False, q_scale=1.0, out_dtype=F32)
        gates = _branch_gates(y_p, mod_l, w_gate, l, row_of_tile=rows_p(TM))
        attn = _attention_prompt(p, diff_lam[l], subln_l, sink[l], lam_init=lam_init, seq=seq)
        y_p = _mixer_and_ffn(y_p, (attn, attn, attn), (0, 1, 2), gates, mod_l, rows_p, l, 256, *shared)
        p5 = p.reshape(batch, seq, IN_COLS)
        new_a.append(jnp.stack([p5[..., COL_AK:COL_AV], p5[..., COL_AV:COL_BQ]], axis=1))
        new_b.append(jnp.stack([p5[..., COL_BK:COL_BV], p5[..., COL_BV:COL_CQ]], axis=1))
        new_c.append(jnp.stack([p5[..., COL_CK:COL_CV], p5[..., COL_CV:]], axis=1))

        qkv = _in_projection(y_s, mod_l, w_in, l, qk_gain[l], cos_t, sin_t, row_of_tile=rows_s(TM),
                             rope=True, q_scale=SCALE * LOG2E, out_dtype=BF16)
        gates = _branch_gates(y_s, mod_l, w_gate, l, row_of_tile=rows_s(TM))
        vec = lambda shape: pl.BlockSpec(shape, lambda b, qi, kk: (0, 0))
        a_o = _flash_sample(functools.partial(_diff_sample_kernel, lam_init=lam_init), qkv, cache_a, l,
                            q_col=COL_AQ, k_col=COL_AK, v_col=COL_AV, kv_width=1024, n_state=8,
                            row_sum_scratch=True, extra=(diff_lam[l], subln_l),
                            extra_specs=[vec((4, HEAD_DIM)), vec((1, 256))], name="diff_attention",
                            dec_seq=dec_seq)
        b_o = _flash_sample(_gqa_sample_kernel, qkv, cache_b, l, q_col=COL_BQ, k_col=COL_BK, v_col=COL_BV,
                            kv_width=256, n_state=8, row_sum_scratch=False, extra=(), extra_specs=[],
                            name="gqa_attention", dec_seq=dec_seq)
        c_o = _window_sample(qkv, cache_c, sink[l], l, dec_seq=dec_seq)
        y_s = _mixer_and_ffn(y_s, (a_o, b_o, c_o), (0, 0, 0), gates, mod_l, rows_s, l, 512, *shared)

    new_kv_a = jnp.stack(new_a, axis=1).reshape(batch, DEPTH, 2, seq, 4, 256)
    new_kv_b = jnp.stack(new_b, axis=1).reshape(batch, DEPTH, 2, seq, 2, HEAD_DIM)
    new_kv_c = jnp.stack(new_c, axis=1).reshape(batch, DEPTH, 2, seq, 2, HEAD_DIM)
    return (y_p.reshape(batch, seq, D_MODEL), y_s.reshape(dec_batch, dec_seq, D_MODEL),
            new_kv_a, new_kv_b, new_kv_c)
```

```python
import functools
import math

import jax
import jax.numpy as jnp
from jax import lax
from jax.experimental import pallas as pl
from jax.experimental.pallas import tpu as pltpu

F32 = jnp.float32
BF16 = jnp.bfloat16

D_MODEL = 2048
HEAD_DIM = 128
GRID_W = 64
ROPE_THETA = 10000.0
WINDOW = 128
N_EXPERTS = 16
N_GROUPS = 4
EXPERTS_PER_GROUP = N_EXPERTS // N_GROUPS
D_EXPERT = 512
N_MOD = 6
DEPTH = 2
ALPHA = (2 * DEPTH) ** 0.25
EPS = 1e-6
IN_COLS = 6144
BRANCH_WIDTH = 1024
SCALE = HEAD_DIM ** -0.5
LOG2E = math.log2(math.e)
NEG_BIG = -1e30

COL_AQ, COL_AK, COL_AV = 0, 1024, 2048
COL_BQ, COL_BK, COL_BV = 3072, 4096, 4352
COL_CQ, COL_CK, COL_CV = 4608, 5632, 5888

LANES = 128
VMEM_LIMIT = 56 * 1024 * 1024

TM = 1024
TN_PROJ = 1024
TN_PROJ_F32 = 1024
TN_GATE = 1024
TM_SMALL = 512


def _cparams(sem):
    return pltpu.CompilerParams(dimension_semantics=sem, vmem_limit_bytes=VMEM_LIMIT)


def _dot(a, b):
    return jnp.dot(a, b, preferred_element_type=F32)


def _dot_nt(a, b):
    return lax.dot_general(a, b, (((1,), (1,)), ((), ())), preferred_element_type=F32)


def _norm_rows(x):
    mu = jnp.mean(x, axis=-1, keepdims=True)
    xc = x - mu
    var = jnp.mean(xc * xc, axis=-1, keepdims=True)
    return xc * lax.rsqrt(var + EPS)


def _rms(v, gain):
    ms = jnp.mean(v * v, axis=-1, keepdims=True)
    return v * lax.rsqrt(ms + EPS) * gain


def _rope(v, cos, sin_from_upper, sin_from_lower):
    return v * cos + pltpu.roll(v, 96, 1) * sin_from_upper + pltpu.roll(v, 32, 1) * sin_from_lower


N_COND = 3
MOD_ROWS = 8
TN_MOD = 1024


def _mod_kernel(cond_ref, w_ref, b_ref, o_ref):
    w = w_ref[...]
    row_idx = lax.broadcasted_iota(jnp.int32, (MOD_ROWS, TN_MOD), 0)
    out = jnp.zeros((MOD_ROWS, TN_MOD), F32)
    for r in range(N_COND):
        c = cond_ref[:, r:r + 1]
        s = c / (1.0 + jnp.exp(-c))
        m = jnp.sum(w * s, axis=0, keepdims=True) + b_ref[...]
        out = jnp.where(row_idx == r, m, out)
    o_ref[...] = out


def _modulation(cond_t, w_mod, b_mod):
    n = N_MOD * D_MODEL
    return pl.pallas_call(
        _mod_kernel,
        out_shape=jax.ShapeDtypeStruct((DEPTH, MOD_ROWS, n), F32),
        grid=(DEPTH, n // TN_MOD),
        in_specs=[
            pl.BlockSpec((D_MODEL, MOD_ROWS), lambda l, j: (0, 0)),
            pl.BlockSpec((None, D_MODEL, TN_MOD), lambda l, j: (l, 0, j)),
            pl.BlockSpec((None, 1, TN_MOD), lambda l, j: (l, 0, j)),
        ],
        out_specs=pl.BlockSpec((None, MOD_ROWS, TN_MOD), lambda l, j: (l, 0, j)),
        compiler_params=_cparams(("parallel", "parallel")),
        name="modulation",
    )(cond_t, w_mod, b_mod.reshape(DEPTH, 1, n))


def _modulate_to_scratch(x_ref, sh_ref, sc_ref, h_scr):
    h = _norm_rows(x_ref[...]) * (1.0 + sc_ref[...]) + sh_ref[...]
    h_scr[...] = h.astype(BF16)


def _head_chunk_kinds():
    kinds = []
    for n_chunks, kind in ((8, (None, True, True)), (8, (None, True, False)), (8, (None, False, False)),
                           (8, ("q", True, True)), (2, ("k", True, False)), (2, (None, False, False)),
                           (8, (None, True, True)), (2, (None, True, False)), (2, (None, False, False))):
        kinds.extend([kind] * n_chunks)
    return kinds


_HEAD_CHUNK_KINDS = _head_chunk_kinds()


def _inproj_kernel(x_ref, sh_ref, sc_ref, w_ref, qk_gain_ref, cos_ref, sin_ref, o_ref, h_scr, *, rope,
                   q_scale):
    j = pl.program_id(1)
    gain_q = qk_gain_ref[0:1, :]
    gain_k = qk_gain_ref[1:2, :]

    @pl.when(j == 0)
    def _():
        _modulate_to_scratch(x_ref, sh_ref, sc_ref, h_scr)

    acc = _dot(h_scr[...], w_ref[...])
    tn = o_ref.shape[1]
    n_chunks = tn // LANES

    def chunk(c):
        return acc[:, c * LANES:(c + 1) * LANES]

    def store(c, v):
        o_ref[:, c * LANES:(c + 1) * LANES] = v.astype(o_ref.dtype)

    def rp(v):
        return _rope(v, cos_ref[...], sin_ref[:, :HEAD_DIM], sin_ref[:, HEAD_DIM:]) if rope else v

    def qs(v):
        return v * q_scale if q_scale != 1.0 else v

    def finish(c, kind):
        norm, rotate, is_query = kind
        v = chunk(c)
        if norm is not None:
            v = _rms(v, gain_q if norm == "q" else gain_k)
        if rotate:
            v = rp(v)
        store(c, qs(v) if is_query else v)

    tile_kinds = [tuple(_HEAD_CHUNK_KINDS[t * n_chunks:(t + 1) * n_chunks]) for t in range(IN_COLS // tn)]
    for kinds in dict.fromkeys(tile_kinds):
        tiles = [t for t, k in enumerate(tile_kinds) if k == kinds]
        cond = j == tiles[0]
        for t in tiles[1:]:
            cond = cond | (j == t)

        @pl.when(cond)
        def _(kinds=kinds):
            for c, kind in enumerate(kinds):
                finish(c, kind)


def _gate_kernel(x_ref, sh_ref, sc_ref, w_ref, o_ref, h_scr):
    @pl.when(pl.program_id(1) == 0)
    def _():
        _modulate_to_scratch(x_ref, sh_ref, sc_ref, h_scr)

    acc = _dot(h_scr[...], w_ref[...])
    o_ref[...] = (1.0 / (1.0 + jnp.exp(-acc))).astype(o_ref.dtype)


def _mod_spec(which, row_of_tile):
    return pl.BlockSpec((None, 1, D_MODEL), lambda i, *_: (row_of_tile(i), 0, which))


def _in_projection(x, mod_l, w_in, l, qk_gain_l, cos_t, sin_t, *, row_of_tile, rope, q_scale, out_dtype):
    t = x.shape[0]
    tiles_per_seq = cos_t.shape[0] // TM
    tn = TN_PROJ_F32 if out_dtype == F32 else TN_PROJ
    return pl.pallas_call(
        functools.partial(_inproj_kernel, rope=rope, q_scale=q_scale),
        out_shape=jax.ShapeDtypeStruct((t, IN_COLS), out_dtype),
        grid=(t // TM, IN_COLS // tn),
        in_specs=[
            pl.BlockSpec((TM, D_MODEL), lambda i, j: (i, 0)),
            _mod_spec(0, row_of_tile),
            _mod_spec(1, row_of_tile),
            pl.BlockSpec((None, D_MODEL, tn), lambda i, j: (l, 0, j)),
            pl.BlockSpec((2, HEAD_DIM), lambda i, j: (0, 0)),
            pl.BlockSpec((TM, HEAD_DIM), lambda i, j: (i % tiles_per_seq, 0)),
            pl.BlockSpec((TM, 2 * HEAD_DIM), lambda i, j: (i % tiles_per_seq, 0)),
        ],
        out_specs=pl.BlockSpec((TM, tn), lambda i, j: (i, j)),
        scratch_shapes=[pltpu.VMEM((TM, D_MODEL), BF16)],
        compiler_params=_cparams(("parallel", "arbitrary")),
        name="in_projection",
    )(x, mod_l, mod_l, w_in, qk_gain_l, cos_t, sin_t)


def _branch_gates(x, mod_l, w_gate, l, *, row_of_tile):
    t = x.shape[0]
    n = w_gate.shape[2]
    return pl.pallas_call(
        _gate_kernel,
        out_shape=jax.ShapeDtypeStruct((t, n), BF16),
        grid=(t // TM, n // TN_GATE),
        in_specs=[
            pl.BlockSpec((TM, D_MODEL), lambda i, j: (i, 0)),
            _mod_spec(0, row_of_tile),
            _mod_spec(1, row_of_tile),
            pl.BlockSpec((None, D_MODEL, TN_GATE), lambda i, j: (l, 0, j)),
        ],
        out_specs=pl.BlockSpec((TM, TN_GATE), lambda i, j: (i, j)),
        scratch_shapes=[pltpu.VMEM((TM, D_MODEL), BF16)],
        compiler_params=_cparams(("parallel", "arbitrary")),
        name="branch_gates",
    )(x, mod_l, mod_l, w_gate)


def _diff_lambda(lam_ref, lam_init):
    lp = lam_ref[...]
    t1 = jnp.sum(lp[0:1] * lp[1:2], axis=-1, keepdims=True)
    t2 = jnp.sum(lp[2:3] * lp[3:4], axis=-1, keepdims=True)
    return jnp.exp(t1) - jnp.exp(t2) + lam_init


def _softmax_rows(s, sink=None):
    m = jnp.max(s, axis=-1, keepdims=True)
    if sink is not None:
        m = jnp.maximum(m, sink)
    e = jnp.exp(s - m)
    den = jnp.sum(e, axis=-1, keepdims=True)
    if sink is not None:
        den = den + jnp.exp(sink - m)
    return e * (1.0 / den)


def _attn_prompt_kernel(p_ref, lam_ref, subln_ref, sink_ref, o_ref, *, lam_init):
    lam = _diff_lambda(lam_ref, lam_init)

    def blk(c0, w):
        return p_ref[:, c0:c0 + w].astype(BF16)

    for h in range(4):
        probs = []
        for m in range(2):
            q = blk(COL_AQ + h * 256 + m * HEAD_DIM, HEAD_DIM)
            k = blk(COL_AK + h * 256 + m * HEAD_DIM, HEAD_DIM)
            probs.append(_softmax_rows(_dot_nt(q, k) * SCALE))
        w = probs[0] - lam * probs[1]
        o = _dot(w.astype(BF16), blk(COL_AV + h * 256, 256))
        o = _rms(o, subln_ref[...]) * (1.0 - lam_init)
        o_ref[:, h * 256:(h + 1) * 256] = o.astype(o_ref.dtype)

    for mixer, (cq, ck, cv) in enumerate(((COL_BQ, COL_BK, COL_BV), (COL_CQ, COL_CK, COL_CV))):
        for kvh in range(2):
            k = blk(ck + kvh * HEAD_DIM, HEAD_DIM)
            v = blk(cv + kvh * HEAD_DIM, HEAD_DIM)
            for g in range(4):
                hq = kvh * 4 + g
                q = blk(cq + hq * HEAD_DIM, HEAD_DIM)
                sink = sink_ref[hq] if mixer == 1 else None
                p = _softmax_rows(_dot_nt(q, k) * SCALE, sink)
                o = _dot(p.astype(BF16), v)
                c0 = BRANCH_WIDTH * (1 + mixer) + hq * HEAD_DIM
                o_ref[:, c0:c0 + HEAD_DIM] = o.astype(o_ref.dtype)


def _attention_prompt(p, diff_lam_l, subln_l, sink_l, *, lam_init, seq):
    t = p.shape[0]
    return pl.pallas_call(
        functools.partial(_attn_prompt_kernel, lam_init=lam_init),
        out_shape=jax.ShapeDtypeStruct((t, 3 * BRANCH_WIDTH), BF16),
        grid=(t // seq,),
        in_specs=[
            pl.BlockSpec((seq, IN_COLS), lambda b: (b, 0)),
            pl.BlockSpec((4, HEAD_DIM), lambda b: (0, 0)),
            pl.BlockSpec((1, 256), lambda b: (0, 0)),
            pl.BlockSpec(memory_space=pltpu.SMEM),
        ],
        out_specs=pl.BlockSpec((seq, 3 * BRANCH_WIDTH), lambda b: (b, 0)),
        compiler_params=_cparams(("parallel",)),
        name="attention_prompt",
    )(p, diff_lam_l, subln_l, sink_l)


TQ = 512
TQ_FLASH = 1024
TK = 1024


def _tile_lanes(v, n):
    return jnp.concatenate([v] * n, axis=-1) if n > 1 else v


def _online_softmax(idx, s, m_scr):
    m_prev = m_scr[idx]
    m_new = jnp.maximum(m_prev, jnp.max(s, axis=-1, keepdims=True))
    m_scr[idx] = m_new
    alpha = jnp.exp2(m_prev - m_new)
    p = jnp.exp2(s - _tile_lanes(m_new, s.shape[1] // LANES))
    return p, alpha


def _diff_sample_kernel(q_ref, k_ref, v_ref, kc_ref, vc_ref, lam_ref, subln_ref, o_ref,
                        m_scr, l_scr, acc_scr, *, lam_init):
    kk = pl.program_id(2)

    @pl.when(kk == 0)
    def _():
        m_scr[...] = jnp.full(m_scr.shape, NEG_BIG, F32)
        l_scr[...] = jnp.zeros(l_scr.shape, F32)
        acc_scr[...] = jnp.zeros(acc_scr.shape, F32)

    def process(kb_ref, vb_ref):
        for h in range(4):
            v = vb_ref[:, h * 256:(h + 1) * 256].astype(BF16)
            for m in range(2):
                c0 = h * 256 + m * HEAD_DIM
                idx = h * 2 + m
                k = kb_ref[:, c0:c0 + HEAD_DIM].astype(BF16)
                p, alpha = _online_softmax(idx, _dot_nt(q_ref[:, c0:c0 + HEAD_DIM], k), m_scr)
                part = p[:, 0:LANES]
                for c in range(1, p.shape[1] // LANES):
                    part = part + p[:, c * LANES:(c + 1) * LANES]
                l_scr[idx] = alpha * l_scr[idx] + part
                acc_scr[idx] = _tile_lanes(alpha, 2) * acc_scr[idx] + _dot(p.astype(BF16), v)

    @pl.when(kk == 0)
    def _():
        process(kc_ref, vc_ref)

    @pl.when(kk > 0)
    def _():
        process(k_ref, v_ref)

    @pl.when(kk == pl.num_programs(2) - 1)
    def _():
        lam = _diff_lambda(lam_ref, lam_init)
        for h in range(4):
            l1 = jnp.sum(l_scr[2 * h], axis=-1, keepdims=True)
            l2 = jnp.sum(l_scr[2 * h + 1], axis=-1, keepdims=True)
            o1 = acc_scr[2 * h] * (1.0 / l1)
            o2 = acc_scr[2 * h + 1] * (1.0 / l2)
            o = _rms(o1 - lam * o2, subln_ref[...]) * (1.0 - lam_init)
            o_ref[:, h * 256:(h + 1) * 256] = o.astype(o_ref.dtype)


def _gqa_sample_kernel(q_ref, k_ref, v_ref, kc_ref, vc_ref, o_ref, m_scr, acc_scr):
    kk = pl.program_id(2)

    @pl.when(kk == 0)
    def _():
        m_scr[...] = jnp.full(m_scr.shape, NEG_BIG, F32)
        acc_scr[...] = jnp.zeros(acc_scr.shape, F32)

    def process(kb_ref, vb_ref):
        for kvh in range(2):
            k = kb_ref[:, kvh * HEAD_DIM:(kvh + 1) * HEAD_DIM].astype(BF16)
            v = vb_ref[:, kvh * HEAD_DIM:(kvh + 1) * HEAD_DIM].astype(BF16)
            v_ones = jnp.concatenate([v, jnp.ones_like(v)], axis=-1)
            for g in range(4):
                hq = kvh * 4 + g
                s = _dot_nt(q_ref[:, hq * HEAD_DIM:(hq + 1) * HEAD_DIM], k)
                p, alpha = _online_softmax(hq, s, m_scr)
                acc_scr[hq] = _tile_lanes(alpha, 2) * acc_scr[hq] + _dot(p.astype(BF16), v_ones)

    @pl.when(kk == 0)
    def _():
        process(kc_ref, vc_ref)

    @pl.when(kk > 0)
    def _():
        process(k_ref, v_ref)

    @pl.when(kk == pl.num_programs(2) - 1)
    def _():
        for hq in range(8):
            o = acc_scr[hq, :, 0:HEAD_DIM] / acc_scr[hq, :, HEAD_DIM:2 * HEAD_DIM]
            o_ref[:, hq * HEAD_DIM:(hq + 1) * HEAD_DIM] = o.astype(o_ref.dtype)


def _flash_sample(kernel, qkv, cache, l, *, q_col, k_col, v_col, kv_width, n_state, row_sum_scratch, extra,
                  extra_specs, name, dec_seq):
    t = qkv.shape[0]
    nb = t // dec_seq
    tq = TQ_FLASH
    nq = dec_seq // tq
    nk = dec_seq // TK
    past = cache.shape[3]
    q_blk, k_blk, v_blk = q_col // BRANCH_WIDTH, k_col // kv_width, v_col // kv_width
    kv_row = lambda b, qi, kk: b * nk + jnp.maximum(kk - 1, 0)
    return pl.pallas_call(
        kernel,
        out_shape=jax.ShapeDtypeStruct((t, BRANCH_WIDTH), BF16),
        grid=(nb, nq, nk + 1),
        in_specs=[
            pl.BlockSpec((tq, BRANCH_WIDTH), lambda b, qi, kk: (b * nq + qi, q_blk)),
            pl.BlockSpec((TK, kv_width), lambda b, qi, kk: (kv_row(b, qi, kk), k_blk)),
            pl.BlockSpec((TK, kv_width), lambda b, qi, kk: (kv_row(b, qi, kk), v_blk)),
            pl.BlockSpec((None, None, None, past, kv_width), lambda b, qi, kk: (b, l, 0, 0, 0)),
            pl.BlockSpec((None, None, None, past, kv_width), lambda b, qi, kk: (b, l, 1, 0, 0)),
        ] + extra_specs,
        out_specs=pl.BlockSpec((tq, BRANCH_WIDTH), lambda b, qi, kk: (b * nq + qi, 0)),
        scratch_shapes=[pltpu.VMEM((n_state, tq, LANES), F32)] * (2 if row_sum_scratch else 1)
        + [pltpu.VMEM((n_state, tq, 2 * LANES), F32)],
        compiler_params=_cparams(("parallel", "parallel", "arbitrary")),
        name=name,
    )(qkv, qkv, qkv, cache, cache, *extra)


def _window_sample_kernel(q0_ref, q1_ref, kp_ref, kc_ref, kn_ref, vp_ref, vc_ref, vn_ref, kctx_ref, vctx_ref,
                          sink_ref, o_ref, *, dec_seq):
    qi = pl.program_id(1)
    q_start = qi * TQ
    half = TQ // 2
    q_pos = q_start + lax.broadcasted_iota(jnp.int32, (TQ, 1), 0)
    segs = ((kp_ref, vp_ref, q_start - half, half), (kc_ref, vc_ref, q_start, TQ),
            (kn_ref, vn_ref, q_start + TQ, half))
    valid = []
    for _, _, start, n in segs:
        k_pos = start + lax.broadcasted_iota(jnp.int32, (TQ, n), 1)
        valid.append((jnp.abs(q_pos - k_pos) <= WINDOW) & (k_pos >= 0) & (k_pos < dec_seq))
    for kvh in range(2):
        q_ref = q0_ref if kvh == 0 else q1_ref
        lo, hi = kvh * HEAD_DIM, (kvh + 1) * HEAD_DIM
        kctx = kctx_ref[:, lo:hi].astype(BF16)
        vctx = vctx_ref[:, lo:hi].astype(BF16)
        for g in range(4):
            hq = kvh * 4 + g
            q = q_ref[:, g * HEAD_DIM:(g + 1) * HEAD_DIM]
            sink = sink_ref[hq] * LOG2E
            scores = [_dot_nt(q, kctx)]
            for (k_ref, _, _, _), ok in zip(segs, valid):
                scores.append(jnp.where(ok, _dot_nt(q, k_ref[:, lo:hi]), NEG_BIG))
            m = jnp.maximum(scores[0].max(axis=-1, keepdims=True), sink)
            for s in scores[1:]:
                m = jnp.maximum(m, s.max(axis=-1, keepdims=True))
            es = [jnp.exp2(s - m) for s in scores]
            den = jnp.exp2(sink - m)
            for e in es:
                den = den + jnp.sum(e, axis=-1, keepdims=True)
            o = _dot(es[0].astype(BF16), vctx)
            for e, (_, v_ref, _, _) in zip(es[1:], segs):
                o = o + _dot(e.astype(BF16), v_ref[:, lo:hi])
            o = o * (1.0 / den)
            o_ref[:, hq * HEAD_DIM:(hq + 1) * HEAD_DIM] = o.astype(o_ref.dtype)


def _window_sample(qkv, cache, sink_l, l, *, dec_seq):
    t = qkv.shape[0]
    nb = t // dec_seq
    nq = dec_seq // TQ
    half = TQ // 2
    n_half = dec_seq // half
    past = cache.shape[3]
    kvw = 2 * HEAD_DIM
    q_blk = COL_CQ // 512
    k_blk, v_blk = COL_CK // kvw, COL_CV // kvw
    prev_row = lambda b, qi: b * n_half + jnp.maximum(2 * qi - 1, 0)
    next_row = lambda b, qi: b * n_half + jnp.minimum(2 * qi + 2, n_half - 1)
    return pl.pallas_call(
        functools.partial(_window_sample_kernel, dec_seq=dec_seq),
        out_shape=jax.ShapeDtypeStruct((t, BRANCH_WIDTH), BF16),
        grid=(nb, nq),
        in_specs=[
            pl.BlockSpec((TQ, 512), lambda b, qi: (b * nq + qi, q_blk)),
            pl.BlockSpec((TQ, 512), lambda b, qi: (b * nq + qi, q_blk + 1)),
            pl.BlockSpec((half, kvw), lambda b, qi: (prev_row(b, qi), k_blk)),
            pl.BlockSpec((TQ, kvw), lambda b, qi: (b * nq + qi, k_blk)),
            pl.BlockSpec((half, kvw), lambda b, qi: (next_row(b, qi), k_blk)),
            pl.BlockSpec((half, kvw), lambda b, qi: (prev_row(b, qi), v_blk)),
            pl.BlockSpec((TQ, kvw), lambda b, qi: (b * nq + qi, v_blk)),
            pl.BlockSpec((half, kvw), lambda b, qi: (next_row(b, qi), v_blk)),
            pl.BlockSpec((None, None, None, past, kvw), lambda b, qi: (b, l, 0, 0, 0)),
            pl.BlockSpec((None, None, None, past, kvw), lambda b, qi: (b, l, 1, 0, 0)),
            pl.BlockSpec(memory_space=pltpu.SMEM),
        ],
        out_specs=pl.BlockSpec((TQ, BRANCH_WIDTH), lambda b, qi: (b * nq + qi, 0)),
        compiler_params=_cparams(("parallel", "parallel")),
        name="window_attention",
    )(qkv, qkv, qkv, qkv, qkv, qkv, qkv, qkv, cache, cache, sink_l)


def _merge_kernel(oa_ref, ob_ref, oc_ref, g_ref, w_ref, y_ref, acc_scr):
    r = pl.program_id(2)

    def contrib(o_ref):
        return g_ref[...].astype(F32) * _dot(o_ref[...], w_ref[...])

    @pl.when(r == 0)
    def _():
        acc_scr[...] = contrib(oa_ref)

    @pl.when(r == 1)
    def _():
        acc_scr[...] += contrib(ob_ref)

    @pl.when(r == 2)
    def _():
        y_ref[...] = (acc_scr[...] + contrib(oc_ref)).astype(y_ref.dtype)


def _merge_branches(o_arrays, o_blocks, gates, w_branch, l):
    t = gates.shape[0]
    tn = D_MODEL
    nn = D_MODEL // tn
    o_specs = [pl.BlockSpec((TM, BRANCH_WIDTH), functools.partial(lambda i, n, r, blk: (i, blk), blk=blk))
               for blk in o_blocks]
    return pl.pallas_call(
        _merge_kernel,
        out_shape=jax.ShapeDtypeStruct((t, D_MODEL), BF16),
        grid=(t // TM, nn, 3),
        in_specs=o_specs + [
            pl.BlockSpec((TM, tn), lambda i, n, r: (i, r * nn + n)),
            pl.BlockSpec((None, None, BRANCH_WIDTH, tn), lambda i, n, r: (l, r, 0, n)),
        ],
        out_specs=pl.BlockSpec((TM, tn), lambda i, n, r: (i, n)),
        scratch_shapes=[pltpu.VMEM((TM, tn), F32)],
        compiler_params=_cparams(("parallel", "parallel", "arbitrary")),
        name="merge_branches",
    )(*o_arrays, gates, w_branch)


def _layer_norm_rows(v, gain, bias):
    return _norm_rows(v) * gain + bias


def _outproj_kernel(y_ref, w_ref, x_ref, g_ref, gain_ref, bias_ref, o_ref):
    n = pl.program_id(1)
    tn = w_ref.shape[1]
    n_blocks = o_ref.shape[1] // tn
    z = _dot(y_ref[...], w_ref[...])
    for b in range(n_blocks):
        @pl.when(n == b)
        def _(b=b):
            cols = slice(b * tn, (b + 1) * tn)
            o_ref[:, cols] = ALPHA * x_ref[:, cols] + g_ref[:, cols] * z

    @pl.when(n == n_blocks - 1)
    def _():
        o_ref[...] = _layer_norm_rows(o_ref[...], gain_ref[...], bias_ref[...])


def _out_projection(y, w_o, l, x, mod_l, ln_gain_l, ln_bias_l, *, row_of_tile):
    t = x.shape[0]
    tn = 512
    return pl.pallas_call(
        _outproj_kernel,
        out_shape=jax.ShapeDtypeStruct((t, D_MODEL), F32),
        grid=(t // TM, D_MODEL // tn),
        in_specs=[
            pl.BlockSpec((TM, D_MODEL), lambda i, n: (i, 0)),
            pl.BlockSpec((None, D_MODEL, tn), lambda i, n: (l, 0, n)),
            pl.BlockSpec((TM, D_MODEL), lambda i, n: (i, 0)),
            _mod_spec(2, row_of_tile),
            pl.BlockSpec((1, D_MODEL), lambda i, n: (0, 0)),
            pl.BlockSpec((1, D_MODEL), lambda i, n: (0, 0)),
        ],
        out_specs=pl.BlockSpec((TM, D_MODEL), lambda i, n: (i, 0)),
        compiler_params=_cparams(("parallel", "arbitrary")),
        name="out_projection",
    )(y, w_o, x, mod_l, ln_gain_l, ln_bias_l)


def _route(p):
    rows = [p[e:e + 1, :] for e in range(N_EXPERTS)]
    best_score, best_group = None, None
    for g in range(N_GROUPS):
        members = rows[g * EXPERTS_PER_GROUP:(g + 1) * EXPERTS_PER_GROUP]
        score = None
        for a in range(EXPERTS_PER_GROUP):
            for b in range(a + 1, EXPERTS_PER_GROUP):
                pair = members[a] + members[b]
                score = pair if score is None else jnp.maximum(score, pair)
        if g == 0:
            best_score, best_group = score, jnp.zeros(score.shape, F32)
        else:
            better = score > best_score
            best_group = jnp.where(better, float(g), best_group)
            best_score = jnp.where(better, score, best_score)
    e_idx = lax.broadcasted_iota(jnp.int32, p.shape, 0).astype(F32)
    g_idx = jnp.floor(e_idx * (1.0 / EXPERTS_PER_GROUP))
    masked = jnp.where(g_idx == best_group, p, -1.0)
    w1 = jnp.max(masked, axis=0, keepdims=True)
    i1 = jnp.min(jnp.where(masked == w1, e_idx, float(N_EXPERTS)), axis=0, keepdims=True)
    masked2 = jnp.where(e_idx == i1, -2.0, masked)
    w2 = jnp.max(masked2, axis=0, keepdims=True)
    i2 = jnp.min(jnp.where(masked2 == w2, e_idx, float(N_EXPERTS)), axis=0, keepdims=True)
    tot = w1 + w2
    return e_idx, i1, i2, w1 / tot, w2 / tot


ROUTE_ROWS = 8
D_PACKED = D_MODEL // 2


def _pack_bf16_pairs(xb):
    half = xb.shape[1] // 2
    lo = lax.bitcast_convert_type(xb[:, :half].astype(F32), jnp.uint32)
    hi = lax.bitcast_convert_type(xb[:, half:].astype(F32), jnp.uint32)
    return (hi & jnp.uint32(0xFFFF0000)) | (lo >> 16)


def _unpack_bf16_pairs(words):
    lo = lax.bitcast_convert_type(words << 16, F32).astype(BF16)
    hi = lax.bitcast_convert_type(words & jnp.uint32(0xFFFF0000), F32).astype(BF16)
    return lo, hi


def _moe_route_kernel(xa_ref, xb_ref, sh_ref, sc_ref, wr_ref, h_ref, rec_ref, rec_t_ref, cnt_ref, carry_scr,
                      *, tiles_a):
    @pl.when(pl.program_id(0) == 0)
    def _():
        carry_scr[...] = jnp.zeros(carry_scr.shape, F32)

    x = jnp.where(pl.program_id(0) < tiles_a, xa_ref[...], xb_ref[...])
    h = _norm_rows(x) * (1.0 + sc_ref[...]) + sh_ref[...]
    hb = h.astype(BF16)
    h_ref[...] = _pack_bf16_pairs(hb)
    logits = _dot_nt(wr_ref[...].astype(BF16), hb)
    m = jnp.max(logits, axis=0, keepdims=True)
    e = jnp.exp(logits - m)
    probs = e / jnp.sum(e, axis=0, keepdims=True)
    e_idx, i1, i2, w1, w2 = _route(probs)
    tm = probs.shape[1]
    oh1 = (e_idx == i1).astype(F32)
    oh2 = (e_idx == i2).astype(F32)
    oh = oh1 + oh2
    earlier = (lax.broadcasted_iota(jnp.int32, (tm, tm), 0) < lax.broadcasted_iota(jnp.int32, (tm, tm), 1))
    rank = carry_scr[:, 0:1] + _dot(oh.astype(BF16), earlier.astype(BF16))
    r1 = jnp.sum(oh1 * rank, axis=0, keepdims=True)
    r2 = jnp.sum(oh2 * rank, axis=0, keepdims=True)
    carry_scr[...] = carry_scr[...] + jnp.sum(oh, axis=1, keepdims=True)
    cnt_ref[...] = carry_scr[...]
    row = lax.broadcasted_iota(jnp.int32, (ROUTE_ROWS, tm), 0)
    rec = jnp.zeros((ROUTE_ROWS, tm), F32)
    for k, v in enumerate((i1, i2, r1, r2, w1, w2)):
        rec = jnp.where(row == k, v, rec)
    rec_ref[...] = rec
    pad = jnp.zeros((LANES - ROUTE_ROWS, tm), F32)
    rec_t_ref[...] = jnp.concatenate([rec, pad], axis=0).T


def _moe_route(x_a, x_b, mod_l, w_router_t, *, rows_a, rows_b):
    tm = TM_SMALL
    tiles_a = x_a.shape[0] // tm
    t = x_a.shape[0] + x_b.shape[0]
    row_a, row_b = rows_a(tm), rows_b(tm)
    row_of_tile = lambda i: jnp.where(i < tiles_a, row_a(i), row_b(i - tiles_a))
    return pl.pallas_call(
        functools.partial(_moe_route_kernel, tiles_a=tiles_a),
        out_shape=(jax.ShapeDtypeStruct((t, D_PACKED), jnp.uint32), jax.ShapeDtypeStruct((ROUTE_ROWS, t), F32),
                   jax.ShapeDtypeStruct((t, LANES), F32), jax.ShapeDtypeStruct((N_EXPERTS, LANES), F32)),
        grid=(t // tm,),
        in_specs=[
            pl.BlockSpec((tm, D_MODEL), lambda i: (jnp.minimum(i, tiles_a - 1), 0)),
            pl.BlockSpec((tm, D_MODEL), lambda i: (jnp.maximum(i - tiles_a, 0), 0)),
            _mod_spec(3, row_of_tile),
            _mod_spec(4, row_of_tile),
            pl.BlockSpec((N_EXPERTS, D_MODEL), lambda i: (0, 0)),
        ],
        out_specs=(pl.BlockSpec((tm, D_PACKED), lambda i: (i, 0)),
                   pl.BlockSpec((ROUTE_ROWS, tm), lambda i: (0, i)),
                   pl.BlockSpec((tm, LANES), lambda i: (i, 0)),
                   pl.BlockSpec((N_EXPERTS, LANES), lambda i: (0, 0))),
        scratch_shapes=[pltpu.VMEM((N_EXPERTS, LANES), F32)],
        compiler_params=_cparams(("arbitrary",)),
        name="moe_route",
    )(x_a, x_b, mod_l, mod_l, w_router_t)


ROW_DMA_UNROLL = 8
GATHER_ORDER_STRIDE = 37


def _row_copy(src_hbm, src_row, dst_buf, dst_row, sem):
    return pltpu.make_async_copy(src_hbm.at[pl.ds(src_row, 1)], dst_buf.at[pl.ds(dst_row, 1)], sem)


def _moe_dispatch_kernel(n_tiles_ref, src_ref, h_hbm, xs_ref, x_buf, sems, *, te):
    j = pl.program_id(0)
    n_valid = n_tiles_ref[0]
    slot = j % 2

    def start_gather(tile, s):
        def body(r, carry):
            row = (r * GATHER_ORDER_STRIDE) % te
            _row_copy(h_hbm, src_ref[tile * te + row], x_buf.at[s], row, sems.at[s]).start()
            return carry
        lax.fori_loop(0, te, body, 0, unroll=ROW_DMA_UNROLL)

    def wait_gather(s):
        def body(r, carry):
            _row_copy(h_hbm, 0, x_buf.at[s], r, sems.at[s]).wait()
            return carry
        lax.fori_loop(0, te, body, 0, unroll=ROW_DMA_UNROLL)

    @pl.when((j == 0) & (n_valid > 0))
    def _():
        start_gather(0, 0)

    @pl.when(j + 1 < n_valid)
    def _():
        start_gather(j + 1, 1 - slot)

    @pl.when(j < n_valid)
    def _():
        wait_gather(slot)
        xs_ref[...] = x_buf[slot]

    @pl.when(j >= n_valid)
    def _():
        xs_ref[...] = jnp.zeros(xs_ref.shape, xs_ref.dtype)


def _moe_dispatch(h, n_tiles, src, *, te):
    n_rows = src.shape[0]
    return pl.pallas_call(
        functools.partial(_moe_dispatch_kernel, te=te),
        out_shape=jax.ShapeDtypeStruct((n_rows, D_PACKED), h.dtype),
        grid_spec=pltpu.PrefetchScalarGridSpec(
            num_scalar_prefetch=2,
            grid=(n_rows // te,),
            in_specs=[pl.BlockSpec(memory_space=pltpu.HBM)],
            out_specs=pl.BlockSpec((te, D_PACKED), lambda j, *_: (j, 0)),
            scratch_shapes=[pltpu.VMEM((2, te, D_PACKED), h.dtype), pltpu.SemaphoreType.DMA((2,))],
        ),
        compiler_params=_cparams(("arbitrary",)),
        name="moe_dispatch",
    )(n_tiles, src, h)


def _moe_expert_kernel(tile_expert_ref, n_tiles_ref, xs_ref, wg_ref, wu_ref, wd_ref, ys_ref):
    del tile_expert_ref
    j = pl.program_id(0)

    @pl.when(j < n_tiles_ref[0])
    def _():
        x_lo, x_hi = _unpack_bf16_pairs(xs_ref[...])

        def project(w_ref):
            return (_dot(x_lo, w_ref[:D_PACKED, :].astype(BF16)) + _dot(x_hi, w_ref[D_PACKED:, :].astype(BF16)))

        gate = project(wg_ref)
        up = project(wu_ref)
        hid = gate / (1.0 + jnp.exp(-gate)) * up
        ys_ref[...] = _pack_bf16_pairs(_dot(hid.astype(BF16), wd_ref[...].astype(BF16)).astype(BF16))

    @pl.when(j >= n_tiles_ref[0])
    def _():
        ys_ref[...] = jnp.zeros(ys_ref.shape, ys_ref.dtype)


def _moe_experts(xs, tile_expert, n_tiles, wg, wu, wd, l, *, te):
    n_rows = xs.shape[0]
    w_in_spec = pl.BlockSpec((None, None, D_MODEL, D_EXPERT), lambda j, te_ref, nt_ref: (l, te_ref[j], 0, 0))
    w_out_spec = pl.BlockSpec((None, None, D_EXPERT, D_MODEL), lambda j, te_ref, nt_ref: (l, te_ref[j], 0, 0))
    xs_spec = pl.BlockSpec((te, D_PACKED), lambda j, te_ref, nt_ref: (j, 0))
    return pl.pallas_call(
        _moe_expert_kernel,
        out_shape=jax.ShapeDtypeStruct((n_rows, D_PACKED), jnp.uint32),
        grid_spec=pltpu.PrefetchScalarGridSpec(
            num_scalar_prefetch=2,
            grid=(n_rows // te,),
            in_specs=[xs_spec, w_in_spec, w_in_spec, w_out_spec],
            out_specs=pl.BlockSpec((te, D_PACKED), lambda j, *_: (j, 0)),
        ),
        compiler_params=_cparams(("arbitrary",)),
        name="moe_experts",
    )(tile_expert, n_tiles, xs, wg, wu, wd)


TM_COMBINE = 256


def _moe_combine_kernel(dest_ref, ys_hbm, x_ref, rec_t_ref, g_ref, gain_ref, bias_ref, o_ref, y_buf, sems,
                        *, n_tokens, token_offset):
    i = pl.program_id(0)
    tm = x_ref.shape[0]
    slot = i % 2

    def start_gather(tile, s):
        def body(r, carry):
            for k in range(2):
                row = dest_ref[k * n_tokens + token_offset + tile * tm + r]
                _row_copy(ys_hbm, row, y_buf.at[s, k], r, sems.at[s]).start()
            return carry
        lax.fori_loop(0, tm, body, 0, unroll=ROW_DMA_UNROLL)

    def wait_gather(s):
        def body(r, carry):
            for k in range(2):
                _row_copy(ys_hbm, 0, y_buf.at[s, k], r, sems.at[s]).wait()
            return carry
        lax.fori_loop(0, tm, body, 0, unroll=ROW_DMA_UNROLL)

    @pl.when(i == 0)
    def _():
        start_gather(0, 0)

    @pl.when(i + 1 < pl.num_programs(0))
    def _():
        start_gather(i + 1, 1 - slot)

    wait_gather(slot)
    w1, w2 = rec_t_ref[:, 4:5], rec_t_ref[:, 5:6]
    y1_lo, y1_hi = _unpack_bf16_pairs(y_buf[slot, 0])
    y2_lo, y2_hi = _unpack_bf16_pairs(y_buf[slot, 1])
    for half, (y1, y2) in enumerate(((y1_lo, y2_lo), (y1_hi, y2_hi))):
        cols = slice(half * D_PACKED, (half + 1) * D_PACKED)
        z = w1 * y1.astype(F32) + w2 * y2.astype(F32)
        o_ref[:, cols] = ALPHA * x_ref[:, cols] + g_ref[:, cols] * z
    o_ref[...] = _layer_norm_rows(o_ref[...], gain_ref[...], bias_ref[...])


def _moe_combine(dest, ys, x, rec_t, mod_l, ln_gain_l, ln_bias_l, *, row_of_tile, token_offset):
    t = x.shape[0]
    tm = TM_COMBINE
    first_tile = token_offset // tm
    vec = pl.BlockSpec((1, D_MODEL), lambda i, *_: (0, 0))
    return pl.pallas_call(
        functools.partial(_moe_combine_kernel, n_tokens=rec_t.shape[0], token_offset=token_offset),
        out_shape=jax.ShapeDtypeStruct((t, D_MODEL), F32),
        grid_spec=pltpu.PrefetchScalarGridSpec(
            num_scalar_prefetch=1,
            grid=(t // tm,),
            in_specs=[
                pl.BlockSpec(memory_space=pltpu.HBM),
                pl.BlockSpec((tm, D_MODEL), lambda i, *_: (i, 0)),
                pl.BlockSpec((tm, LANES), lambda i, *_: (first_tile + i, 0)),
                _mod_spec(5, row_of_tile),
                vec, vec,
            ],
            out_specs=pl.BlockSpec((tm, D_MODEL), lambda i, *_: (i, 0)),
            scratch_shapes=[pltpu.VMEM((2, 2, tm, D_PACKED), jnp.uint32), pltpu.SemaphoreType.DMA((2,))],
        ),
        compiler_params=_cparams(("arbitrary",)),
        name="moe_combine",
    )(dest, ys, x, rec_t, mod_l, ln_gain_l, ln_bias_l)


def _dispatch_tables(rec, cnt, *, te):
    t = rec.shape[1]
    n_rows = 2 * t + N_EXPERTS * te
    e12 = rec[0:2].astype(jnp.int32)
    r12 = rec[2:4].astype(jnp.int32)
    counts = cnt[:, 0].astype(jnp.int32)
    padded = (counts + te - 1) // te * te
    ends = jnp.cumsum(padded)
    offsets = ends - padded
    expert_ids = jnp.arange(N_EXPERTS, dtype=jnp.int32)[:, None, None]
    dest = (jnp.sum(jnp.where(e12[None] == expert_ids, offsets[:, None, None], 0), axis=0) + r12).reshape(2 * t)
    tokens = jnp.tile(jnp.arange(t, dtype=jnp.int32), 2)
    filler = jnp.arange(n_rows, dtype=jnp.int32) % t
    src = filler.at[dest].set(tokens, unique_indices=True)
    tile_start = jnp.arange(n_rows // te, dtype=jnp.int32) * te
    tile_expert = jnp.minimum(jnp.searchsorted(ends, tile_start, side="right"), N_EXPERTS - 1).astype(jnp.int32)
    n_tiles = (ends[-1:] // te).astype(jnp.int32)
    return dest, src, tile_expert, n_tiles


def _rope_tables(n_tokens):
    rows = n_tokens // GRID_W
    row = jnp.repeat(jnp.arange(rows), GRID_W).astype(F32)
    col = jnp.tile(jnp.arange(GRID_W), rows).astype(F32)
    quarter = HEAD_DIM // 4
    inv_freq = ROPE_THETA ** (-jnp.arange(quarter, dtype=F32) / quarter)
    ang_r, ang_c = row[:, None] * inv_freq, col[:, None] * inv_freq
    cos = jnp.concatenate([jnp.cos(ang_r), jnp.cos(ang_r), jnp.cos(ang_c), jnp.cos(ang_c)], axis=-1)
    zero = jnp.zeros_like(ang_r)
    sin = jnp.concatenate([-jnp.sin(ang_r), zero, -jnp.sin(ang_c), zero,
                           zero, jnp.sin(ang_r), zero, jnp.sin(ang_c)], axis=-1)
    return cos, sin


def _row_of_tile(first_row, tokens_per_row):
    def for_tile(tile):
        return lambda i: first_row + (i * tile) // tokens_per_row
    return for_tile


def _mixer_output(x, attn_arrays, attn_blocks, gates, mod_l, rows, l, w_branch, w_o, ln_gain, ln_bias):
    y = _merge_branches(attn_arrays, attn_blocks, gates, w_branch, l)
    return _out_projection(y, w_o, l, x, mod_l, ln_gain[l, 0:1], ln_bias[l, 0:1], row_of_tile=rows(TM))


EXPERT_TILE = 512


def _moe_ffn(x_a, x_b, mod_l, rows_a, rows_b, l, ln_gain, ln_bias, w_router_t, w_e_gate, w_e_up, w_e_down):
    h2, rec, rec_t, cnt = _moe_route(x_a, x_b, mod_l, w_router_t, rows_a=rows_a, rows_b=rows_b)
    dest, src, tile_expert, n_tiles = _dispatch_tables(rec, cnt, te=EXPERT_TILE)
    xs = _moe_dispatch(h2, n_tiles, src, te=EXPERT_TILE)
    ys = _moe_experts(xs, tile_expert, n_tiles, w_e_gate, w_e_up, w_e_down, l, te=EXPERT_TILE)
    outs = []
    for x, rows, offset in ((x_a, rows_a, 0), (x_b, rows_b, x_a.shape[0])):
        outs.append(_moe_combine(dest, ys, x, rec_t, mod_l, ln_gain[l, 1:2], ln_bias[l, 1:2],
                                 row_of_tile=rows(TM_COMBINE), token_offset=offset))
    return outs


def kernel(x_prompt, x_sample, cache_kv_a, cache_kv_b, cache_kv_c, c, c_ctx, w_in, w_gate, w_branch, w_o,
           w_mod, b_mod, ln_gain, ln_bias, diff_lam, diff_subln, qk_gain, sink, w_router, w_e_gate, w_e_up,
           w_e_down):
    batch, seq, _ = x_prompt.shape
    dec_batch, dec_seq, _ = x_sample.shape
    past = cache_kv_a.shape[3]
    t_p, t_s = batch * seq, dec_batch * dec_seq

    cond = jnp.concatenate([c_ctx[None], c, jnp.zeros((MOD_ROWS - 1 - dec_batch, D_MODEL), F32)], axis=0)
    mod = _modulation(cond.T, w_mod, b_mod)
    cos_t, sin_t = _rope_tables(dec_seq)
    w_router_t = w_router.T
    w_in, w_gate, w_branch, w_o = (w.astype(BF16) for w in (w_in, w_gate, w_branch, w_o))
    cache_a = cache_kv_a.reshape(dec_batch, DEPTH, 2, past, 4 * 256)
    cache_b = cache_kv_b.reshape(dec_batch, DEPTH, 2, past, 2 * HEAD_DIM)
    cache_c = cache_kv_c.reshape(dec_batch, DEPTH, 2, past, 2 * HEAD_DIM)
    rows_p = _row_of_tile(0, t_p)
    rows_s = _row_of_tile(1, dec_seq)

    y_p = x_prompt.reshape(t_p, D_MODEL)
    y_s = x_sample.reshape(t_s, D_MODEL)
    new_a, new_b, new_c = [], [], []
    for l in range(DEPTH):
        lam_init = 0.8 - 0.6 * math.exp(-0.3 * l)
        mod_l = mod[l].reshape(MOD_ROWS, 1, N_MOD * D_MODEL)
        subln_l = diff_subln[l].reshape(1, 256)
        shared = (w_branch, w_o, ln_gain, ln_bias)

        p = _in_projection(y_p, mod_l, w_in, l, qk_gain[l], cos_t, sin_t, row_of_tile=rows_p(TM),
                           rope=False, q_scale=1.0, out_dtype=F32)
        gates = _branch_gates(y_p, mod_l, w_gate, l, row_of_tile=rows_p(TM))
        attn = _attention_prompt(p, diff_lam[l], subln_l, sink[l], lam_init=lam_init, seq=seq)
        y_p = _mixer_output(y_p, (attn, attn, attn), (0, 1, 2), gates, mod_l, rows_p, l, *shared)
        p5 = p.reshape(batch, seq, IN_COLS)
        new_a.append(jnp.stack([p5[..., COL_AK:COL_AV], p5[..., COL_AV:COL_BQ]], axis=1))
        new_b.append(jnp.stack([p5[..., COL_BK:COL_BV], p5[..., COL_BV:COL_CQ]], axis=1))
        new_c.append(jnp.stack([p5[..., COL_CK:COL_CV], p5[..., COL_CV:]], axis=1))

        qkv = _in_projection(y_s, mod_l, w_in, l, qk_gain[l], cos_t, sin_t, row_of_tile=rows_s(TM),
                             rope=True, q_scale=SCALE * LOG2E, out_dtype=BF16)
        gates = _branch_gates(y_s, mod_l, w_gate, l, row_of_tile=rows_s(TM))
        vec = lambda shape: pl.BlockSpec(shape, lambda b, qi, kk: (0, 0))
        a_o = _flash_sample(functools.partial(_diff_sample_kernel, lam_init=lam_init), qkv, cache_a, l,
                            q_col=COL_AQ, k_col=COL_AK, v_col=COL_AV, kv_width=1024, n_state=8,
                            row_sum_scratch=True, extra=(diff_lam[l], subln_l),
                            extra_specs=[vec((4, HEAD_DIM)), vec((1, 256))], name="diff_attention",
                            dec_seq=dec_seq)
        b_o = _flash_sample(_gqa_sample_kernel, qkv, cache_b, l, q_col=COL_BQ, k_col=COL_BK, v_col=COL_BV,
                            kv_width=256, n_state=8, row_sum_scratch=False, extra=(), extra_specs=[],
                            name="gqa_attention", dec_seq=dec_seq)
        c_o = _window_sample(qkv, cache_c, sink[l], l, dec_seq=dec_seq)
        y_s = _mixer_output(y_s, (a_o, b_o, c_o), (0, 0, 0), gates, mod_l, rows_s, l, *shared)

        y_p, y_s = _moe_ffn(y_p, y_s, mod_l, rows_p, rows_s, l, ln_gain, ln_bias, w_router_t, w_e_gate, w_e_up,
                            w_e_down)

    new_kv_a = jnp.stack(new_a, axis=1).reshape(batch, DEPTH, 2, seq, 4, 256)
    new_kv_b = jnp.stack(new_b, axis=1).reshape(batch, DEPTH, 2, seq, 2, HEAD_DIM)
    new_kv_c = jnp.stack(new_c, axis=1).reshape(batch, DEPTH, 2, seq, 2, HEAD_DIM)
    return (y_p.reshape(batch, seq, D_MODEL), y_s.reshape(dec_batch, dec_seq, D_MODEL),
            new_kv_a, new_kv_b, new_kv_c)
```

```python
import functools
import math

import jax
import jax.numpy as jnp
from jax import lax
from jax.experimental import pallas as pl
from jax.experimental.pallas import tpu as pltpu

F32 = jnp.float32
BF16 = jnp.bfloat16

D_MODEL = 2048
HEAD_DIM = 128
GRID_W = 64
ROPE_THETA = 10000.0
WINDOW = 128
N_EXPERTS = 16
N_GROUPS = 4
EXPERTS_PER_GROUP = N_EXPERTS // N_GROUPS
D_EXPERT = 512
N_MOD = 6
DEPTH = 2
ALPHA = (2 * DEPTH) ** 0.25
EPS = 1e-6
IN_COLS = 6144
BRANCH_WIDTH = 1024
SCALE = HEAD_DIM ** -0.5
LOG2E = math.log2(math.e)
NEG_BIG = -1e30

COL_AQ, COL_AK, COL_AV = 0, 1024, 2048
COL_BQ, COL_BK, COL_BV = 3072, 4096, 4352
COL_CQ, COL_CK, COL_CV = 4608, 5632, 5888

LANES = 128
VMEM_LIMIT = 56 * 1024 * 1024

TM = 1024
TN_PROJ = 1024
TN_PROJ_F32 = 1024
TN_GATE = 1024
TM_SMALL = 512


def _cparams(sem):
    return pltpu.CompilerParams(dimension_semantics=sem, vmem_limit_bytes=VMEM_LIMIT)


def _dot(a, b):
    return jnp.dot(a, b, preferred_element_type=F32)


def _dot_nt(a, b):
    return lax.dot_general(a, b, (((1,), (1,)), ((), ())), preferred_element_type=F32)


def _norm_rows(x):
    mu = jnp.mean(x, axis=-1, keepdims=True)
    xc = x - mu
    var = jnp.mean(xc * xc, axis=-1, keepdims=True)
    return xc * lax.rsqrt(var + EPS)


def _rms(v, gain):
    ms = jnp.mean(v * v, axis=-1, keepdims=True)
    return v * lax.rsqrt(ms + EPS) * gain


def _rope(v, cos, sin_from_upper, sin_from_lower):
    return v * cos + pltpu.roll(v, 96, 1) * sin_from_upper + pltpu.roll(v, 32, 1) * sin_from_lower


N_COND = 3
MOD_ROWS = 8
TN_MOD = 1024


def _mod_kernel(cond_ref, w_ref, b_ref, o_ref):
    w = w_ref[...]
    row_idx = lax.broadcasted_iota(jnp.int32, (MOD_ROWS, TN_MOD), 0)
    out = jnp.zeros((MOD_ROWS, TN_MOD), F32)
    for r in range(N_COND):
        c = cond_ref[:, r:r + 1]
        s = c / (1.0 + jnp.exp(-c))
        m = jnp.sum(w * s, axis=0, keepdims=True) + b_ref[...]
        out = jnp.where(row_idx == r, m, out)
    o_ref[...] = out


def _modulation(cond_t, w_mod, b_mod):
    n = N_MOD * D_MODEL
    return pl.pallas_call(
        _mod_kernel,
        out_shape=jax.ShapeDtypeStruct((DEPTH, MOD_ROWS, n), F32),
        grid=(DEPTH, n // TN_MOD),
        in_specs=[
            pl.BlockSpec((D_MODEL, MOD_ROWS), lambda l, j: (0, 0)),
            pl.BlockSpec((None, D_MODEL, TN_MOD), lambda l, j: (l, 0, j)),
            pl.BlockSpec((None, 1, TN_MOD), lambda l, j: (l, 0, j)),
        ],
        out_specs=pl.BlockSpec((None, MOD_ROWS, TN_MOD), lambda l, j: (l, 0, j)),
        compiler_params=_cparams(("parallel", "parallel")),
        name="modulation",
    )(cond_t, w_mod, b_mod.reshape(DEPTH, 1, n))


def _modulate_to_scratch(x_ref, sh_ref, sc_ref, h_scr):
    h = _norm_rows(x_ref[...]) * (1.0 + sc_ref[...]) + sh_ref[...]
    h_scr[...] = h.astype(BF16)


def _head_chunk_kinds():
    kinds = []
    for n_chunks, kind in ((8, (None, True, True)), (8, (None, True, False)), (8, (None, False, False)),
                           (8, ("q", True, True)), (2, ("k", True, False)), (2, (None, False, False)),
                           (8, (None, True, True)), (2, (None, True, False)), (2, (None, False, False))):
        kinds.extend([kind] * n_chunks)
    return kinds


_HEAD_CHUNK_KINDS = _head_chunk_kinds()


def _inproj_kernel(x_ref, sh_ref, sc_ref, w_ref, qk_gain_ref, cos_ref, sin_ref, o_ref, h_scr, *, rope,
                   q_scale):
    j = pl.program_id(1)
    gain_q = qk_gain_ref[0:1, :]
    gain_k = qk_gain_ref[1:2, :]

    @pl.when(j == 0)
    def _():
        _modulate_to_scratch(x_ref, sh_ref, sc_ref, h_scr)

    acc = _dot(h_scr[...], w_ref[...])
    tn = o_ref.shape[1]
    n_chunks = tn // LANES

    def chunk(c):
        return acc[:, c * LANES:(c + 1) * LANES]

    def store(c, v):
        o_ref[:, c * LANES:(c + 1) * LANES] = v.astype(o_ref.dtype)

    def rp(v):
        return _rope(v, cos_ref[...], sin_ref[:, :HEAD_DIM], sin_ref[:, HEAD_DIM:]) if rope else v

    def qs(v):
        return v * q_scale if q_scale != 1.0 else v

    def finish(c, kind):
        norm, rotate, is_query = kind
        v = chunk(c)
        if norm is not None:
            v = _rms(v, gain_q if norm == "q" else gain_k)
        if rotate:
            v = rp(v)
        store(c, qs(v) if is_query else v)

    tile_kinds = [tuple(_HEAD_CHUNK_KINDS[t * n_chunks:(t + 1) * n_chunks]) for t in range(IN_COLS // tn)]
    for kinds in dict.fromkeys(tile_kinds):
        tiles = [t for t, k in enumerate(tile_kinds) if k == kinds]
        cond = j == tiles[0]
        for t in tiles[1:]:
            cond = cond | (j == t)

        @pl.when(cond)
        def _(kinds=kinds):
            for c, kind in enumerate(kinds):
                finish(c, kind)


def _gate_kernel(x_ref, sh_ref, sc_ref, w_ref, o_ref, h_scr):
    @pl.when(pl.program_id(1) == 0)
    def _():
        _modulate_to_scratch(x_ref, sh_ref, sc_ref, h_scr)

    acc = _dot(h_scr[...], w_ref[...])
    o_ref[...] = (1.0 / (1.0 + jnp.exp(-acc))).astype(o_ref.dtype)


def _mod_spec(which, row_of_tile):
    return pl.BlockSpec((None, 1, D_MODEL), lambda i, *_: (row_of_tile(i), 0, which))


def _in_projection(x, mod_l, w_in, l, qk_gain_l, cos_t, sin_t, *, row_of_tile, rope, q_scale, out_dtype):
    t = x.shape[0]
    tiles_per_seq = cos_t.shape[0] // TM
    tn = TN_PROJ_F32 if out_dtype == F32 else TN_PROJ
    return pl.pallas_call(
        functools.partial(_inproj_kernel, rope=rope, q_scale=q_scale),
        out_shape=jax.ShapeDtypeStruct((t, IN_COLS), out_dtype),
        grid=(t // TM, IN_COLS // tn),
        in_specs=[
            pl.BlockSpec((TM, D_MODEL), lambda i, j: (i, 0)),
            _mod_spec(0, row_of_tile),
            _mod_spec(1, row_of_tile),
            pl.BlockSpec((None, D_MODEL, tn), lambda i, j: (l, 0, j)),
            pl.BlockSpec((2, HEAD_DIM), lambda i, j: (0, 0)),
            pl.BlockSpec((TM, HEAD_DIM), lambda i, j: (i % tiles_per_seq, 0)),
            pl.BlockSpec((TM, 2 * HEAD_DIM), lambda i, j: (i % tiles_per_seq, 0)),
        ],
        out_specs=pl.BlockSpec((TM, tn), lambda i, j: (i, j)),
        scratch_shapes=[pltpu.VMEM((TM, D_MODEL), BF16)],
        compiler_params=_cparams(("parallel", "arbitrary")),
        name="in_projection",
    )(x, mod_l, mod_l, w_in, qk_gain_l, cos_t, sin_t)


def _branch_gates(x, mod_l, w_gate, l, *, row_of_tile):
    t = x.shape[0]
    n = w_gate.shape[2]
    return pl.pallas_call(
        _gate_kernel,
        out_shape=jax.ShapeDtypeStruct((t, n), BF16),
        grid=(t // TM, n // TN_GATE),
        in_specs=[
            pl.BlockSpec((TM, D_MODEL), lambda i, j: (i, 0)),
            _mod_spec(0, row_of_tile),
            _mod_spec(1, row_of_tile),
            pl.BlockSpec((None, D_MODEL, TN_GATE), lambda i, j: (l, 0, j)),
        ],
        out_specs=pl.BlockSpec((TM, TN_GATE), lambda i, j: (i, j)),
        scratch_shapes=[pltpu.VMEM((TM, D_MODEL), BF16)],
        compiler_params=_cparams(("parallel", "arbitrary")),
        name="branch_gates",
    )(x, mod_l, mod_l, w_gate)


def _diff_lambda(lam_ref, lam_init):
    lp = lam_ref[...]
    t1 = jnp.sum(lp[0:1] * lp[1:2], axis=-1, keepdims=True)
    t2 = jnp.sum(lp[2:3] * lp[3:4], axis=-1, keepdims=True)
    return jnp.exp(t1) - jnp.exp(t2) + lam_init


def _softmax_rows(s, sink=None):
    m = jnp.max(s, axis=-1, keepdims=True)
    if sink is not None:
        m = jnp.maximum(m, sink)
    e = jnp.exp(s - m)
    den = jnp.sum(e, axis=-1, keepdims=True)
    if sink is not None:
        den = den + jnp.exp(sink - m)
    return e * (1.0 / den)


def _attn_prompt_kernel(p_ref, lam_ref, subln_ref, sink_ref, o_ref, *, lam_init):
    lam = _diff_lambda(lam_ref, lam_init)

    def blk(c0, w):
        return p_ref[:, c0:c0 + w].astype(BF16)

    for h in range(4):
        probs = []
        for m in range(2):
            q = blk(COL_AQ + h * 256 + m * HEAD_DIM, HEAD_DIM)
            k = blk(COL_AK + h * 256 + m * HEAD_DIM, HEAD_DIM)
            probs.append(_softmax_rows(_dot_nt(q, k) * SCALE))
        w = probs[0] - lam * probs[1]
        o = _dot(w.astype(BF16), blk(COL_AV + h * 256, 256))
        o = _rms(o, subln_ref[...]) * (1.0 - lam_init)
        o_ref[:, h * 256:(h + 1) * 256] = o.astype(o_ref.dtype)

    for mixer, (cq, ck, cv) in enumerate(((COL_BQ, COL_BK, COL_BV), (COL_CQ, COL_CK, COL_CV))):
        for kvh in range(2):
            k = blk(ck + kvh * HEAD_DIM, HEAD_DIM)
            v = blk(cv + kvh * HEAD_DIM, HEAD_DIM)
            for g in range(4):
                hq = kvh * 4 + g
                q = blk(cq + hq * HEAD_DIM, HEAD_DIM)
                sink = sink_ref[hq] if mixer == 1 else None
                p = _softmax_rows(_dot_nt(q, k) * SCALE, sink)
                o = _dot(p.astype(BF16), v)
                c0 = BRANCH_WIDTH * (1 + mixer) + hq * HEAD_DIM
                o_ref[:, c0:c0 + HEAD_DIM] = o.astype(o_ref.dtype)


def _attention_prompt(p, diff_lam_l, subln_l, sink_l, *, lam_init, seq):
    t = p.shape[0]
    return pl.pallas_call(
        functools.partial(_attn_prompt_kernel, lam_init=lam_init),
        out_shape=jax.ShapeDtypeStruct((t, 3 * BRANCH_WIDTH), BF16),
        grid=(t // seq,),
        in_specs=[
            pl.BlockSpec((seq, IN_COLS), lambda b: (b, 0)),
            pl.BlockSpec((4, HEAD_DIM), lambda b: (0, 0)),
            pl.BlockSpec((1, 256), lambda b: (0, 0)),
            pl.BlockSpec(memory_space=pltpu.SMEM),
        ],
        out_specs=pl.BlockSpec((seq, 3 * BRANCH_WIDTH), lambda b: (b, 0)),
        compiler_params=_cparams(("parallel",)),
        name="attention_prompt",
    )(p, diff_lam_l, subln_l, sink_l)


TQ = 512
TQ_FLASH = 1024
TK = 1024


def _tile_lanes(v, n):
    return jnp.concatenate([v] * n, axis=-1) if n > 1 else v


def _online_softmax(idx, s, m_scr):
    m_prev = m_scr[idx]
    m_new = jnp.maximum(m_prev, jnp.max(s, axis=-1, keepdims=True))
    m_scr[idx] = m_new
    alpha = jnp.exp2(m_prev - m_new)
    p = jnp.exp2(s - _tile_lanes(m_new, s.shape[1] // LANES))
    return p, alpha


def _diff_sample_kernel(q_ref, k_ref, v_ref, kc_ref, vc_ref, lam_ref, subln_ref, o_ref,
                        m_scr, l_scr, acc_scr, *, lam_init):
    kk = pl.program_id(2)

    @pl.when(kk == 0)
    def _():
        m_scr[...] = jnp.full(m_scr.shape, NEG_BIG, F32)
        l_scr[...] = jnp.zeros(l_scr.shape, F32)
        acc_scr[...] = jnp.zeros(acc_scr.shape, F32)

    def process(kb_ref, vb_ref):
        for h in range(4):
            v = vb_ref[:, h * 256:(h + 1) * 256].astype(BF16)
            for m in range(2):
                c0 = h * 256 + m * HEAD_DIM
                idx = h * 2 + m
                k = kb_ref[:, c0:c0 + HEAD_DIM].astype(BF16)
                p, alpha = _online_softmax(idx, _dot_nt(q_ref[:, c0:c0 + HEAD_DIM], k), m_scr)
                part = p[:, 0:LANES]
                for c in range(1, p.shape[1] // LANES):
                    part = part + p[:, c * LANES:(c + 1) * LANES]
                l_scr[idx] = alpha * l_scr[idx] + part
                acc_scr[idx] = _tile_lanes(alpha, 2) * acc_scr[idx] + _dot(p.astype(BF16), v)

    @pl.when(kk == 0)
    def _():
        process(kc_ref, vc_ref)

    @pl.when(kk > 0)
    def _():
        process(k_ref, v_ref)

    @pl.when(kk == pl.num_programs(2) - 1)
    def _():
        lam = _diff_lambda(lam_ref, lam_init)
        for h in range(4):
            l1 = jnp.sum(l_scr[2 * h], axis=-1, keepdims=True)
            l2 = jnp.sum(l_scr[2 * h + 1], axis=-1, keepdims=True)
            o1 = acc_scr[2 * h] * (1.0 / l1)
            o2 = acc_scr[2 * h + 1] * (1.0 / l2)
            o = _rms(o1 - lam * o2, subln_ref[...]) * (1.0 - lam_init)
            o_ref[:, h * 256:(h + 1) * 256] = o.astype(o_ref.dtype)


def _gqa_sample_kernel(q_ref, k_ref, v_ref, kc_ref, vc_ref, o_ref, m_scr, acc_scr):
    kk = pl.program_id(2)

    @pl.when(kk == 0)
    def _():
        m_scr[...] = jnp.full(m_scr.shape, NEG_BIG, F32)
        acc_scr[...] = jnp.zeros(acc_scr.shape, F32)

    def process(kb_ref, vb_ref):
        for kvh in range(2):
            k = kb_ref[:, kvh * HEAD_DIM:(kvh + 1) * HEAD_DIM].astype(BF16)
            v = vb_ref[:, kvh * HEAD_DIM:(kvh + 1) * HEAD_DIM].astype(BF16)
            v_ones = jnp.concatenate([v, jnp.ones_like(v)], axis=-1)
            for g in range(4):
                hq = kvh * 4 + g
                s = _dot_nt(q_ref[:, hq * HEAD_DIM:(hq + 1) * HEAD_DIM], k)
                p, alpha = _online_softmax(hq, s, m_scr)
                acc_scr[hq] = _tile_lanes(alpha, 2) * acc_scr[hq] + _dot(p.astype(BF16), v_ones)

    @pl.when(kk == 0)
    def _():
        process(kc_ref, vc_ref)

    @pl.when(kk > 0)
    def _():
        process(k_ref, v_ref)

    @pl.when(kk == pl.num_programs(2) - 1)
    def _():
        for hq in range(8):
            o = acc_scr[hq, :, 0:HEAD_DIM] / acc_scr[hq, :, HEAD_DIM:2 * HEAD_DIM]
            o_ref[:, hq * HEAD_DIM:(hq + 1) * HEAD_DIM] = o.astype(o_ref.dtype)


def _flash_sample(kernel, qkv, cache, l, *, q_col, k_col, v_col, kv_width, n_state, row_sum_scratch, extra,
                  extra_specs, name, dec_seq):
    t = qkv.shape[0]
    nb = t // dec_seq
    tq = TQ_FLASH
    nq = dec_seq // tq
    nk = dec_seq // TK
    past = cache.shape[3]
    q_blk, k_blk, v_blk = q_col // BRANCH_WIDTH, k_col // kv_width, v_col // kv_width
    kv_row = lambda b, qi, kk: b * nk + jnp.maximum(kk - 1, 0)
    return pl.pallas_call(
        kernel,
        out_shape=jax.ShapeDtypeStruct((t, BRANCH_WIDTH), BF16),
        grid=(nb, nq, nk + 1),
        in_specs=[
            pl.BlockSpec((tq, BRANCH_WIDTH), lambda b, qi, kk: (b * nq + qi, q_blk)),
            pl.BlockSpec((TK, kv_width), lambda b, qi, kk: (kv_row(b, qi, kk), k_blk)),
            pl.BlockSpec((TK, kv_width), lambda b, qi, kk: (kv_row(b, qi, kk), v_blk)),
            pl.BlockSpec((None, None, None, past, kv_width), lambda b, qi, kk: (b, l, 0, 0, 0)),
            pl.BlockSpec((None, None, None, past, kv_width), lambda b, qi, kk: (b, l, 1, 0, 0)),
        ] + extra_specs,
        out_specs=pl.BlockSpec((tq, BRANCH_WIDTH), lambda b, qi, kk: (b * nq + qi, 0)),
        scratch_shapes=[pltpu.VMEM((n_state, tq, LANES), F32)] * (2 if row_sum_scratch else 1)
        + [pltpu.VMEM((n_state, tq, 2 * LANES), F32)],
        compiler_params=_cparams(("parallel", "parallel", "arbitrary")),
        name=name,
    )(qkv, qkv, qkv, cache, cache, *extra)


def _window_sample_kernel(q0_ref, q1_ref, kp_ref, kc_ref, kn_ref, vp_ref, vc_ref, vn_ref, kctx_ref, vctx_ref,
                          sink_ref, o_ref, *, dec_seq):
    qi = pl.program_id(1)
    q_start = qi * TQ
    half = TQ // 2
    q_pos = q_start + lax.broadcasted_iota(jnp.int32, (TQ, 1), 0)
    segs = ((kp_ref, vp_ref, q_start - half, half), (kc_ref, vc_ref, q_start, TQ),
            (kn_ref, vn_ref, q_start + TQ, half))
    valid = []
    for _, _, start, n in segs:
        k_pos = start + lax.broadcasted_iota(jnp.int32, (TQ, n), 1)
        valid.append((jnp.abs(q_pos - k_pos) <= WINDOW) & (k_pos >= 0) & (k_pos < dec_seq))
    for kvh in range(2):
        q_ref = q0_ref if kvh == 0 else q1_ref
        lo, hi = kvh * HEAD_DIM, (kvh + 1) * HEAD_DIM
        kctx = kctx_ref[:, lo:hi].astype(BF16)
        vctx = vctx_ref[:, lo:hi].astype(BF16)
        for g in range(4):
            hq = kvh * 4 + g
            q = q_ref[:, g * HEAD_DIM:(g + 1) * HEAD_DIM]
            sink = sink_ref[hq] * LOG2E
            scores = [_dot_nt(q, kctx)]
            for (k_ref, _, _, _), ok in zip(segs, valid):
                scores.append(jnp.where(ok, _dot_nt(q, k_ref[:, lo:hi]), NEG_BIG))
            m = jnp.maximum(scores[0].max(axis=-1, keepdims=True), sink)
            for s in scores[1:]:
                m = jnp.maximum(m, s.max(axis=-1, keepdims=True))
            es = [jnp.exp2(s - m) for s in scores]
            den = jnp.exp2(sink - m)
            for e in es:
                den = den + jnp.sum(e, axis=-1, keepdims=True)
            o = _dot(es[0].astype(BF16), vctx)
            for e, (_, v_ref, _, _) in zip(es[1:], segs):
                o = o + _dot(e.astype(BF16), v_ref[:, lo:hi])
            o = o * (1.0 / den)
            o_ref[:, hq * HEAD_DIM:(hq + 1) * HEAD_DIM] = o.astype(o_ref.dtype)


def _window_sample(qkv, cache, sink_l, l, *, dec_seq):
    t = qkv.shape[0]
    nb = t // dec_seq
    nq = dec_seq // TQ
    half = TQ // 2
    n_half = dec_seq // half
    past = cache.shape[3]
    kvw = 2 * HEAD_DIM
    q_blk = COL_CQ // 512
    k_blk, v_blk = COL_CK // kvw, COL_CV // kvw
    prev_row = lambda b, qi: b * n_half + jnp.maximum(2 * qi - 1, 0)
    next_row = lambda b, qi: b * n_half + jnp.minimum(2 * qi + 2, n_half - 1)
    return pl.pallas_call(
        functools.partial(_window_sample_kernel, dec_seq=dec_seq),
        out_shape=jax.ShapeDtypeStruct((t, BRANCH_WIDTH), BF16),
        grid=(nb, nq),
        in_specs=[
            pl.BlockSpec((TQ, 512), lambda b, qi: (b * nq + qi, q_blk)),
            pl.BlockSpec((TQ, 512), lambda b, qi: (b * nq + qi, q_blk + 1)),
            pl.BlockSpec((half, kvw), lambda b, qi: (prev_row(b, qi), k_blk)),
            pl.BlockSpec((TQ, kvw), lambda b, qi: (b * nq + qi, k_blk)),
            pl.BlockSpec((half, kvw), lambda b, qi: (next_row(b, qi), k_blk)),
            pl.BlockSpec((half, kvw), lambda b, qi: (prev_row(b, qi), v_blk)),
            pl.BlockSpec((TQ, kvw), lambda b, qi: (b * nq + qi, v_blk)),
            pl.BlockSpec((half, kvw), lambda b, qi: (next_row(b, qi), v_blk)),
            pl.BlockSpec((None, None, None, past, kvw), lambda b, qi: (b, l, 0, 0, 0)),
            pl.BlockSpec((None, None, None, past, kvw), lambda b, qi: (b, l, 1, 0, 0)),
            pl.BlockSpec(memory_space=pltpu.SMEM),
        ],
        out_specs=pl.BlockSpec((TQ, BRANCH_WIDTH), lambda b, qi: (b * nq + qi, 0)),
        compiler_params=_cparams(("parallel", "parallel")),
        name="window_attention",
    )(qkv, qkv, qkv, qkv, qkv, qkv, qkv, qkv, cache, cache, sink_l)


def _merge_kernel(oa_ref, ob_ref, oc_ref, g_ref, w_ref, y_ref, acc_scr):
    r = pl.program_id(2)

    def contrib(o_ref):
        return g_ref[...].astype(F32) * _dot(o_ref[...], w_ref[...])

    @pl.when(r == 0)
    def _():
        acc_scr[...] = contrib(oa_ref)

    @pl.when(r == 1)
    def _():
        acc_scr[...] += contrib(ob_ref)

    @pl.when(r == 2)
    def _():
        y_ref[...] = (acc_scr[...] + contrib(oc_ref)).astype(y_ref.dtype)


def _merge_branches(o_arrays, o_blocks, gates, w_branch, l):
    t = gates.shape[0]
    tn = D_MODEL
    nn = D_MODEL // tn
    o_specs = [pl.BlockSpec((TM, BRANCH_WIDTH), functools.partial(lambda i, n, r, blk: (i, blk), blk=blk))
               for blk in o_blocks]
    return pl.pallas_call(
        _merge_kernel,
        out_shape=jax.ShapeDtypeStruct((t, D_MODEL), BF16),
        grid=(t // TM, nn, 3),
        in_specs=o_specs + [
            pl.BlockSpec((TM, tn), lambda i, n, r: (i, r * nn + n)),
            pl.BlockSpec((None, None, BRANCH_WIDTH, tn), lambda i, n, r: (l, r, 0, n)),
        ],
        out_specs=pl.BlockSpec((TM, tn), lambda i, n, r: (i, n)),
        scratch_shapes=[pltpu.VMEM((TM, tn), F32)],
        compiler_params=_cparams(("parallel", "parallel", "arbitrary")),
        name="merge_branches",
    )(*o_arrays, gates, w_branch)


def _layer_norm_rows(v, gain, bias):
    return _norm_rows(v) * gain + bias


def _outproj_kernel(y_ref, w_ref, x_ref, g_ref, gain_ref, bias_ref, o_ref):
    n = pl.program_id(1)
    tn = w_ref.shape[1]
    n_blocks = o_ref.shape[1] // tn
    z = _dot(y_ref[...], w_ref[...])
    for b in range(n_blocks):
        @pl.when(n == b)
        def _(b=b):
            cols = slice(b * tn, (b + 1) * tn)
            o_ref[:, cols] = ALPHA * x_ref[:, cols] + g_ref[:, cols] * z

    @pl.when(n == n_blocks - 1)
    def _():
        o_ref[...] = _layer_norm_rows(o_ref[...], gain_ref[...], bias_ref[...])


def _out_projection(y, w_o, l, x, mod_l, ln_gain_l, ln_bias_l, *, row_of_tile):
    t = x.shape[0]
    tn = 512
    return pl.pallas_call(
        _outproj_kernel,
        out_shape=jax.ShapeDtypeStruct((t, D_MODEL), F32),
        grid=(t // TM, D_MODEL // tn),
        in_specs=[
            pl.BlockSpec((TM, D_MODEL), lambda i, n: (i, 0)),
            pl.BlockSpec((None, D_MODEL, tn), lambda i, n: (l, 0, n)),
            pl.BlockSpec((TM, D_MODEL), lambda i, n: (i, 0)),
            _mod_spec(2, row_of_tile),
            pl.BlockSpec((1, D_MODEL), lambda i, n: (0, 0)),
            pl.BlockSpec((1, D_MODEL), lambda i, n: (0, 0)),
        ],
        out_specs=pl.BlockSpec((TM, D_MODEL), lambda i, n: (i, 0)),
        compiler_params=_cparams(("parallel", "arbitrary")),
        name="out_projection",
    )(y, w_o, x, mod_l, ln_gain_l, ln_bias_l)


def _route(p):
    rows = [p[e:e + 1, :] for e in range(N_EXPERTS)]
    best_score, best_group = None, None
    for g in range(N_GROUPS):
        members = rows[g * EXPERTS_PER_GROUP:(g + 1) * EXPERTS_PER_GROUP]
        score = None
        for a in range(EXPERTS_PER_GROUP):
            for b in range(a + 1, EXPERTS_PER_GROUP):
                pair = members[a] + members[b]
                score = pair if score is None else jnp.maximum(score, pair)
        if g == 0:
            best_score, best_group = score, jnp.zeros(score.shape, F32)
        else:
            better = score > best_score
            best_group = jnp.where(better, float(g), best_group)
            best_score = jnp.where(better, score, best_score)
    e_idx = lax.broadcasted_iota(jnp.int32, p.shape, 0).astype(F32)
    g_idx = jnp.floor(e_idx * (1.0 / EXPERTS_PER_GROUP))
    masked = jnp.where(g_idx == best_group, p, -1.0)
    w1 = jnp.max(masked, axis=0, keepdims=True)
    i1 = jnp.min(jnp.where(masked == w1, e_idx, float(N_EXPERTS)), axis=0, keepdims=True)
    masked2 = jnp.where(e_idx == i1, -2.0, masked)
    w2 = jnp.max(masked2, axis=0, keepdims=True)
    i2 = jnp.min(jnp.where(masked2 == w2, e_idx, float(N_EXPERTS)), axis=0, keepdims=True)
    tot = w1 + w2
    return e_idx, i1, i2, w1 / tot, w2 / tot


ROUTE_ROWS = 8
D_PACKED = D_MODEL // 2


def _pack_bf16_pairs(xb):
    half = xb.shape[1] // 2
    lo = lax.bitcast_convert_type(xb[:, :half].astype(F32), jnp.uint32)
    hi = lax.bitcast_convert_type(xb[:, half:].astype(F32), jnp.uint32)
    return (hi & jnp.uint32(0xFFFF0000)) | (lo >> 16)


def _unpack_bf16_pairs(words):
    lo = lax.bitcast_convert_type(words << 16, F32).astype(BF16)
    hi = lax.bitcast_convert_type(words & jnp.uint32(0xFFFF0000), F32).astype(BF16)
    return lo, hi


def _moe_route_kernel(xa_ref, xb_ref, sh_ref, sc_ref, wr_ref, h_ref, rec_ref, rec_t_ref, cnt_ref, carry_scr,
                      *, tiles_a):
    @pl.when(pl.program_id(0) == 0)
    def _():
        carry_scr[...] = jnp.zeros(carry_scr.shape, F32)

    x = jnp.where(pl.program_id(0) < tiles_a, xa_ref[...], xb_ref[...])
    h = _norm_rows(x) * (1.0 + sc_ref[...]) + sh_ref[...]
    hb = h.astype(BF16)
    h_ref[...] = _pack_bf16_pairs(hb)
    logits = _dot_nt(wr_ref[...].astype(BF16), hb)
    m = jnp.max(logits, axis=0, keepdims=True)
    e = jnp.exp(logits - m)
    probs = e / jnp.sum(e, axis=0, keepdims=True)
    e_idx, i1, i2, w1, w2 = _route(probs)
    tm = probs.shape[1]
    oh1 = (e_idx == i1).astype(F32)
    oh2 = (e_idx == i2).astype(F32)
    oh = oh1 + oh2
    earlier = (lax.broadcasted_iota(jnp.int32, (tm, tm), 0) < lax.broadcasted_iota(jnp.int32, (tm, tm), 1))
    rank = carry_scr[:, 0:1] + _dot(oh.astype(BF16), earlier.astype(BF16))
    r1 = jnp.sum(oh1 * rank, axis=0, keepdims=True)
    r2 = jnp.sum(oh2 * rank, axis=0, keepdims=True)
    carry_scr[...] = carry_scr[...] + jnp.sum(oh, axis=1, keepdims=True)
    cnt_ref[...] = carry_scr[...]
    row = lax.broadcasted_iota(jnp.int32, (ROUTE_ROWS, tm), 0)
    rec = jnp.zeros((ROUTE_ROWS, tm), F32)
    for k, v in enumerate((i1, i2, r1, r2, w1, w2)):
        rec = jnp.where(row == k, v, rec)
    rec_ref[...] = rec
    pad = jnp.zeros((LANES - ROUTE_ROWS, tm), F32)
    rec_t_ref[...] = jnp.concatenate([rec, pad], axis=0).T


def _moe_route(x_a, x_b, mod_l, w_router_t, *, rows_a, rows_b):
    tm = TM_SMALL
    tiles_a = x_a.shape[0] // tm
    t = x_a.shape[0] + x_b.shape[0]
    row_a, row_b = rows_a(tm), rows_b(tm)
    row_of_tile = lambda i: jnp.where(i < tiles_a, row_a(i), row_b(i - tiles_a))
    return pl.pallas_call(
        functools.partial(_moe_route_kernel, tiles_a=tiles_a),
        out_shape=(jax.ShapeDtypeStruct((t, D_PACKED), jnp.uint32), jax.ShapeDtypeStruct((ROUTE_ROWS, t), F32),
                   jax.ShapeDtypeStruct((t, LANES), F32), jax.ShapeDtypeStruct((N_EXPERTS, LANES), F32)),
        grid=(t // tm,),
        in_specs=[
            pl.BlockSpec((tm, D_MODEL), lambda i: (jnp.minimum(i, tiles_a - 1), 0)),
            pl.BlockSpec((tm, D_MODEL), lambda i: (jnp.maximum(i - tiles_a, 0), 0)),
            _mod_spec(3, row_of_tile),
            _mod_spec(4, row_of_tile),
            pl.BlockSpec((N_EXPERTS, D_MODEL), lambda i: (0, 0)),
        ],
        out_specs=(pl.BlockSpec((tm, D_PACKED), lambda i: (i, 0)),
                   pl.BlockSpec((ROUTE_ROWS, tm), lambda i: (0, i)),
                   pl.BlockSpec((tm, LANES), lambda i: (i, 0)),
                   pl.BlockSpec((N_EXPERTS, LANES), lambda i: (0, 0))),
        scratch_shapes=[pltpu.VMEM((N_EXPERTS, LANES), F32)],
        compiler_params=_cparams(("arbitrary",)),
        name="moe_route",
    )(x_a, x_b, mod_l, mod_l, w_router_t)


ROW_DMA_UNROLL = 8
GATHER_ORDER_STRIDE = 37


def _row_copy(src_hbm, src_row, dst_buf, dst_row, sem):
    return pltpu.make_async_copy(src_hbm.at[pl.ds(src_row, 1)], dst_buf.at[pl.ds(dst_row, 1)], sem)


def _moe_expert_kernel(tile_expert_ref, n_tiles_ref, src_ref, h_hbm, wg_ref, wu_ref, wd_ref, ys_ref,
                       x_buf, sems, *, te):
    del tile_expert_ref
    j = pl.program_id(0)
    n_valid = n_tiles_ref[0]
    slot = j % 2

    def start_gather(tile, s):
        def body(r, carry):
            row = (r * GATHER_ORDER_STRIDE) % te
            _row_copy(h_hbm, src_ref[tile * te + row], x_buf.at[s], row, sems.at[s]).start()
            return carry
        lax.fori_loop(0, te, body, 0, unroll=ROW_DMA_UNROLL)

    def wait_gather(s):
        def body(r, carry):
            _row_copy(h_hbm, 0, x_buf.at[s], r, sems.at[s]).wait()
            return carry
        lax.fori_loop(0, te, body, 0, unroll=ROW_DMA_UNROLL)

    @pl.when((j == 0) & (n_valid > 0))
    def _():
        start_gather(0, 0)

    @pl.when(j + 1 < n_valid)
    def _():
        start_gather(j + 1, 1 - slot)

    @pl.when(j < n_valid)
    def _():
        wait_gather(slot)
        x_lo, x_hi = _unpack_bf16_pairs(x_buf[slot])

        def project(w_ref):
            return (_dot(x_lo, w_ref[:D_PACKED, :].astype(BF16)) + _dot(x_hi, w_ref[D_PACKED:, :].astype(BF16)))

        gate = project(wg_ref)
        up = project(wu_ref)
        hid = gate / (1.0 + jnp.exp(-gate)) * up
        ys_ref[...] = _pack_bf16_pairs(_dot(hid.astype(BF16), wd_ref[...].astype(BF16)).astype(BF16))

    @pl.when(j >= n_valid)
    def _():
        ys_ref[...] = jnp.zeros(ys_ref.shape, ys_ref.dtype)


def _moe_experts(h, tile_expert, n_tiles, src, wg, wu, wd, l, *, te):
    n_rows = src.shape[0]
    w_in_spec = pl.BlockSpec((None, None, D_MODEL, D_EXPERT), lambda j, te_ref, *_: (l, te_ref[j], 0, 0))
    w_out_spec = pl.BlockSpec((None, None, D_EXPERT, D_MODEL), lambda j, te_ref, *_: (l, te_ref[j], 0, 0))
    return pl.pallas_call(
        functools.partial(_moe_expert_kernel, te=te),
        out_shape=jax.ShapeDtypeStruct((n_rows, D_PACKED), jnp.uint32),
        grid_spec=pltpu.PrefetchScalarGridSpec(
            num_scalar_prefetch=3,
            grid=(n_rows // te,),
            in_specs=[pl.BlockSpec(memory_space=pltpu.HBM), w_in_spec, w_in_spec, w_out_spec],
            out_specs=pl.BlockSpec((te, D_PACKED), lambda j, *_: (j, 0)),
            scratch_shapes=[pltpu.VMEM((2, te, D_PACKED), h.dtype), pltpu.SemaphoreType.DMA((2,))],
        ),
        compiler_params=_cparams(("arbitrary",)),
        name="moe_experts",
    )(tile_expert, n_tiles, src, h, wg, wu, wd)


TM_COMBINE = 256


def _moe_combine_kernel(dest_ref, ys_hbm, x_ref, rec_t_ref, g_ref, gain_ref, bias_ref, o_ref, y_buf, sems,
                        *, n_tokens, token_offset):
    i = pl.program_id(0)
    tm = x_ref.shape[0]
    slot = i % 2

    def start_gather(tile, s):
        def body(r, carry):
            for k in range(2):
                row = dest_ref[k * n_tokens + token_offset + tile * tm + r]
                _row_copy(ys_hbm, row, y_buf.at[s, k], r, sems.at[s]).start()
            return carry
        lax.fori_loop(0, tm, body, 0, unroll=ROW_DMA_UNROLL)

    def wait_gather(s):
        def body(r, carry):
            for k in range(2):
                _row_copy(ys_hbm, 0, y_buf.at[s, k], r, sems.at[s]).wait()
            return carry
        lax.fori_loop(0, tm, body, 0, unroll=ROW_DMA_UNROLL)

    @pl.when(i == 0)
    def _():
        start_gather(0, 0)

    @pl.when(i + 1 < pl.num_programs(0))
    def _():
        start_gather(i + 1, 1 - slot)

    wait_gather(slot)
    w1, w2 = rec_t_ref[:, 4:5], rec_t_ref[:, 5:6]
    y1_lo, y1_hi = _unpack_bf16_pairs(y_buf[slot, 0])
    y2_lo, y2_hi = _unpack_bf16_pairs(y_buf[slot, 1])
    for half, (y1, y2) in enumerate(((y1_lo, y2_lo), (y1_hi, y2_hi))):
        cols = slice(half * D_PACKED, (half + 1) * D_PACKED)
        z = w1 * y1.astype(F32) + w2 * y2.astype(F32)
        o_ref[:, cols] = ALPHA * x_ref[:, cols] + g_ref[:, cols] * z
    o_ref[...] = _layer_norm_rows(o_ref[...], gain_ref[...], bias_ref[...])


def _moe_combine(dest, ys, x, rec_t, mod_l, ln_gain_l, ln_bias_l, *, row_of_tile, token_offset):
    t = x.shape[0]
    tm = TM_COMBINE
    first_tile = token_offset // tm
    vec = pl.BlockSpec((1, D_MODEL), lambda i, *_: (0, 0))
    return pl.pallas_call(
        functools.partial(_moe_combine_kernel, n_tokens=rec_t.shape[0], token_offset=token_offset),
        out_shape=jax.ShapeDtypeStruct((t, D_MODEL), F32),
        grid_spec=pltpu.PrefetchScalarGridSpec(
            num_scalar_prefetch=1,
            grid=(t // tm,),
            in_specs=[
                pl.BlockSpec(memory_space=pltpu.HBM),
                pl.BlockSpec((tm, D_MODEL), lambda i, *_: (i, 0)),
                pl.BlockSpec((tm, LANES), lambda i, *_: (first_tile + i, 0)),
                _mod_spec(5, row_of_tile),
                vec, vec,
            ],
            out_specs=pl.BlockSpec((tm, D_MODEL), lambda i, *_: (i, 0)),
            scratch_shapes=[pltpu.VMEM((2, 2, tm, D_PACKED), jnp.uint32), pltpu.SemaphoreType.DMA((2,))],
        ),
        compiler_params=_cparams(("arbitrary",)),
        name="moe_combine",
    )(dest, ys, x, rec_t, mod_l, ln_gain_l, ln_bias_l)


def _dispatch_tables(rec, cnt, *, te):
    t = rec.shape[1]
    n_rows = 2 * t + N_EXPERTS * te
    e12 = rec[0:2].astype(jnp.int32)
    r12 = rec[2:4].astype(jnp.int32)
    counts = cnt[:, 0].astype(jnp.int32)
    padded = (counts + te - 1) // te * te
    ends = jnp.cumsum(padded)
    offsets = ends - padded
    expert_ids = jnp.arange(N_EXPERTS, dtype=jnp.int32)[:, None, None]
    dest = (jnp.sum(jnp.where(e12[None] == expert_ids, offsets[:, None, None], 0), axis=0) + r12).reshape(2 * t)
    tokens = jnp.tile(jnp.arange(t, dtype=jnp.int32), 2)
    filler = jnp.arange(n_rows, dtype=jnp.int32) % t
    src = filler.at[dest].set(tokens, unique_indices=True)
    tile_start = jnp.arange(n_rows // te, dtype=jnp.int32) * te
    tile_expert = jnp.minimum(jnp.searchsorted(ends, tile_start, side="right"), N_EXPERTS - 1).astype(jnp.int32)
    n_tiles = (ends[-1:] // te).astype(jnp.int32)
    return dest, src, tile_expert, n_tiles


def _rope_tables(n_tokens):
    rows = n_tokens // GRID_W
    row = jnp.repeat(jnp.arange(rows), GRID_W).astype(F32)
    col = jnp.tile(jnp.arange(GRID_W), rows).astype(F32)
    quarter = HEAD_DIM // 4
    inv_freq = ROPE_THETA ** (-jnp.arange(quarter, dtype=F32) / quarter)
    ang_r, ang_c = row[:, None] * inv_freq, col[:, None] * inv_freq
    cos = jnp.concatenate([jnp.cos(ang_r), jnp.cos(ang_r), jnp.cos(ang_c), jnp.cos(ang_c)], axis=-1)
    zero = jnp.zeros_like(ang_r)
    sin = jnp.concatenate([-jnp.sin(ang_r), zero, -jnp.sin(ang_c), zero,
                           zero, jnp.sin(ang_r), zero, jnp.sin(ang_c)], axis=-1)
    return cos, sin


def _row_of_tile(first_row, tokens_per_row):
    def for_tile(tile):
        return lambda i: first_row + (i * tile) // tokens_per_row
    return for_tile


def _mixer_output(x, attn_arrays, attn_blocks, gates, mod_l, rows, l, w_branch, w_o, ln_gain, ln_bias):
    y = _merge_branches(attn_arrays, attn_blocks, gates, w_branch, l)
    return _out_projection(y, w_o, l, x, mod_l, ln_gain[l, 0:1], ln_bias[l, 0:1], row_of_tile=rows(TM))


EXPERT_TILE = 512


def _moe_ffn(x_a, x_b, mod_l, rows_a, rows_b, l, ln_gain, ln_bias, w_router_t, w_e_gate, w_e_up, w_e_down):
    h2, rec, rec_t, cnt = _moe_route(x_a, x_b, mod_l, w_router_t, rows_a=rows_a, rows_b=rows_b)
    dest, src, tile_expert, n_tiles = _dispatch_tables(rec, cnt, te=EXPERT_TILE)
    ys = _moe_experts(h2, tile_expert, n_tiles, src, w_e_gate, w_e_up, w_e_down, l, te=EXPERT_TILE)
    outs = []
    for x, rows, offset in ((x_a, rows_a, 0), (x_b, rows_b, x_a.shape[0])):
        outs.append(_moe_combine(dest, ys, x, rec_t, mod_l, ln_gain[l, 1:2], ln_bias[l, 1:2],
                                 row_of_tile=rows(TM_COMBINE), token_offset=offset))
    return outs


def kernel(x_prompt, x_sample, cache_kv_a, cache_kv_b, cache_kv_c, c, c_ctx, w_in, w_gate, w_branch, w_o,
           w_mod, b_mod, ln_gain, ln_bias, diff_lam, diff_subln, qk_gain, sink, w_router, w_e_gate, w_e_up,
           w_e_down):
    batch, seq, _ = x_prompt.shape
    dec_batch, dec_seq, _ = x_sample.shape
    past = cache_kv_a.shape[3]
    t_p, t_s = batch * seq, dec_batch * dec_seq

    cond = jnp.concatenate([c_ctx[None], c, jnp.zeros((MOD_ROWS - 1 - dec_batch, D_MODEL), F32)], axis=0)
    mod = _modulation(cond.T, w_mod, b_mod)
    cos_t, sin_t = _rope_tables(dec_seq)
    w_router_t = w_router.T
    w_in, w_gate, w_branch, w_o = (w.astype(BF16) for w in (w_in, w_gate, w_branch, w_o))
    cache_a = cache_kv_a.reshape(dec_batch, DEPTH, 2, past, 4 * 256)
    cache_b = cache_kv_b.reshape(dec_batch, DEPTH, 2, past, 2 * HEAD_DIM)
    cache_c = cache_kv_c.reshape(dec_batch, DEPTH, 2, past, 2 * HEAD_DIM)
    rows_p = _row_of_tile(0, t_p)
    rows_s = _row_of_tile(1, dec_seq)

    y_p = x_prompt.reshape(t_p, D_MODEL)
    y_s = x_sample.reshape(t_s, D_MODEL)
    projections = []
    for l in range(DEPTH):
        lam_init = 0.8 - 0.6 * math.exp(-0.3 * l)
        mod_l = mod[l].reshape(MOD_ROWS, 1, N_MOD * D_MODEL)
        subln_l = diff_subln[l].reshape(1, 256)
        shared = (w_branch, w_o, ln_gain, ln_bias)

        p = _in_projection(y_p, mod_l, w_in, l, qk_gain[l], cos_t, sin_t, row_of_tile=rows_p(TM),
                           rope=False, q_scale=1.0, out_dtype=F32)
        gates = _branch_gates(y_p, mod_l, w_gate, l, row_of_tile=rows_p(TM))
        attn = _attention_prompt(p, diff_lam[l], subln_l, sink[l], lam_init=lam_init, seq=seq)
        y_p = _mixer_output(y_p, (attn, attn, attn), (0, 1, 2), gates, mod_l, rows_p, l, *shared)
        projections.append(p.reshape(batch, seq, IN_COLS))

        qkv = _in_projection(y_s, mod_l, w_in, l, qk_gain[l], cos_t, sin_t, row_of_tile=rows_s(TM),
                             rope=True, q_scale=SCALE * LOG2E, out_dtype=BF16)
        gates = _branch_gates(y_s, mod_l, w_gate, l, row_of_tile=rows_s(TM))
        vec = lambda shape: pl.BlockSpec(shape, lambda b, qi, kk: (0, 0))
        a_o = _flash_sample(functools.partial(_diff_sample_kernel, lam_init=lam_init), qkv, cache_a, l,
                            q_col=COL_AQ, k_col=COL_AK, v_col=COL_AV, kv_width=1024, n_state=8,
                            row_sum_scratch=True, extra=(diff_lam[l], subln_l),
                            extra_specs=[vec((4, HEAD_DIM)), vec((1, 256))], name="diff_attention",
                            dec_seq=dec_seq)
        b_o = _flash_sample(_gqa_sample_kernel, qkv, cache_b, l, q_col=COL_BQ, k_col=COL_BK, v_col=COL_BV,
                            kv_width=256, n_state=8, row_sum_scratch=False, extra=(), extra_specs=[],
                            name="gqa_attention", dec_seq=dec_seq)
        c_o = _window_sample(qkv, cache_c, sink[l], l, dec_seq=dec_seq)
        y_s = _mixer_output(y_s, (a_o, b_o, c_o), (0, 0, 0), gates, mod_l, rows_s, l, *shared)

        y_p, y_s = _moe_ffn(y_p, y_s, mod_l, rows_p, rows_s, l, ln_gain, ln_bias, w_router_t, w_e_gate, w_e_up,
                            w_e_down)

    def new_cache(col_k, col_v, col_end, heads):
        parts = [p[..., c0:c1] for p in projections for c0, c1 in ((col_k, col_v), (col_v, col_end))]
        return jnp.stack(parts, axis=1).reshape(batch, DEPTH, 2, seq, heads, (col_v - col_k) // heads)

    new_kv_a = new_cache(COL_AK, COL_AV, COL_BQ, 4)
    new_kv_b = new_cache(COL_BK, COL_BV, COL_CQ, 2)
    new_kv_c = new_cache(COL_CK, COL_CV, IN_COLS, 2)
    return (y_p.reshape(batch, seq, D_MODEL), y_s.reshape(dec_batch, dec_seq, D_MODEL),
            new_kv_a, new_kv_b, new_kv_c)
```

```python
import functools
import math

import jax
import jax.numpy as jnp
from jax import lax
from jax.experimental import pallas as pl
from jax.experimental.pallas import tpu as pltpu

F32 = jnp.float32
BF16 = jnp.bfloat16

D_MODEL = 2048
HEAD_DIM = 128
GRID_W = 64
ROPE_THETA = 10000.0
WINDOW = 128
N_EXPERTS = 16
N_GROUPS = 4
EXPERTS_PER_GROUP = N_EXPERTS // N_GROUPS
D_EXPERT = 512
N_MOD = 6
DEPTH = 2
ALPHA = (2 * DEPTH) ** 0.25
EPS = 1e-6
IN_COLS = 6144
BRANCH_WIDTH = 1024
SCALE = HEAD_DIM ** -0.5
LOG2E = math.log2(math.e)
NEG_BIG = -1e30

COL_AQ, COL_AK, COL_AV = 0, 1024, 2048
COL_BQ, COL_BK, COL_BV = 3072, 4096, 4352
COL_CQ, COL_CK, COL_CV = 4608, 5632, 5888

LANES = 128
VMEM_LIMIT = 56 * 1024 * 1024

TM = 1024
TN_PROJ = 1024
TN_PROJ_F32 = 1024
TN_GATE = 1024
PROJ_DOT_SPLIT = 2
TM_SMALL = 512


def _cparams(sem):
    return pltpu.CompilerParams(dimension_semantics=sem, vmem_limit_bytes=VMEM_LIMIT)


def _dot(a, b):
    return jnp.dot(a, b, preferred_element_type=F32)


def _dot_nt(a, b):
    return lax.dot_general(a, b, (((1,), (1,)), ((), ())), preferred_element_type=F32)


def _norm_rows(x):
    mu = jnp.mean(x, axis=-1, keepdims=True)
    xc = x - mu
    var = jnp.mean(xc * xc, axis=-1, keepdims=True)
    return xc * lax.rsqrt(var + EPS)


def _rms(v, gain):
    ms = jnp.mean(v * v, axis=-1, keepdims=True)
    return v * lax.rsqrt(ms + EPS) * gain


def _rope(v, cos, sin_from_upper, sin_from_lower):
    return v * cos + pltpu.roll(v, 96, 1) * sin_from_upper + pltpu.roll(v, 32, 1) * sin_from_lower


N_COND = 3
MOD_ROWS = 8
TN_MOD = 1024


def _mod_kernel(cond_ref, w_ref, b_ref, o_ref):
    w = w_ref[...]
    row_idx = lax.broadcasted_iota(jnp.int32, (MOD_ROWS, TN_MOD), 0)
    out = jnp.zeros((MOD_ROWS, TN_MOD), F32)
    for r in range(N_COND):
        c = cond_ref[:, r:r + 1]
        s = c / (1.0 + jnp.exp(-c))
        m = jnp.sum(w * s, axis=0, keepdims=True) + b_ref[...]
        out = jnp.where(row_idx == r, m, out)
    o_ref[...] = out


def _modulation(cond_t, w_mod, b_mod):
    n = N_MOD * D_MODEL
    return pl.pallas_call(
        _mod_kernel,
        out_shape=jax.ShapeDtypeStruct((DEPTH, MOD_ROWS, n), F32),
        grid=(DEPTH, n // TN_MOD),
        in_specs=[
            pl.BlockSpec((D_MODEL, MOD_ROWS), lambda l, j: (0, 0)),
            pl.BlockSpec((None, D_MODEL, TN_MOD), lambda l, j: (l, 0, j)),
            pl.BlockSpec((None, 1, TN_MOD), lambda l, j: (l, 0, j)),
        ],
        out_specs=pl.BlockSpec((None, MOD_ROWS, TN_MOD), lambda l, j: (l, 0, j)),
        compiler_params=_cparams(("parallel", "parallel")),
        name="modulation",
    )(cond_t, w_mod, b_mod.reshape(DEPTH, 1, n))


def _modulate_to_scratch(x_ref, sh_ref, sc_ref, h_scr):
    h = _norm_rows(x_ref[...]) * (1.0 + sc_ref[...]) + sh_ref[...]
    h_scr[...] = h.astype(BF16)


def _head_chunk_kinds():
    kinds = []
    for n_chunks, kind in ((8, (None, True, True)), (8, (None, True, False)), (8, (None, False, False)),
                           (8, ("q", True, True)), (2, ("k", True, False)), (2, (None, False, False)),
                           (8, (None, True, True)), (2, (None, True, False)), (2, (None, False, False))):
        kinds.extend([kind] * n_chunks)
    return kinds


_HEAD_CHUNK_KINDS = _head_chunk_kinds()


def _inproj_kernel(x_ref, sh_ref, sc_ref, w_ref, qk_gain_ref, cos_ref, sin_ref, o_ref, h_scr, acc_scr, *,
                   rope, q_scale):
    j = pl.program_id(1)
    gain_q = qk_gain_ref[0:1, :]
    gain_k = qk_gain_ref[1:2, :]

    @pl.when(j == 0)
    def _():
        _modulate_to_scratch(x_ref, sh_ref, sc_ref, h_scr)

    tn = o_ref.shape[1]
    n_chunks = tn // LANES
    chunks_per_dot = n_chunks // PROJ_DOT_SPLIT

    def store(c, v):
        o_ref[:, c * LANES:(c + 1) * LANES] = v.astype(o_ref.dtype)

    def rp(v):
        return _rope(v, cos_ref[...], sin_ref[:, :HEAD_DIM], sin_ref[:, HEAD_DIM:]) if rope else v

    def qs(v):
        return v * q_scale if q_scale != 1.0 else v

    def finish(c, v, kind):
        norm, rotate, is_query = kind
        if norm is not None:
            v = _rms(v, gain_q if norm == "q" else gain_k)
        if rotate:
            v = rp(v)
        store(c, qs(v) if is_query else v)

    tile_kinds = [tuple(_HEAD_CHUNK_KINDS[t * n_chunks:(t + 1) * n_chunks]) for t in range(IN_COLS // tn)]
    for kinds in dict.fromkeys(tile_kinds):
        tiles = [t for t, k in enumerate(tile_kinds) if k == kinds]
        cond = j == tiles[0]
        for t in tiles[1:]:
            cond = cond | (j == t)

        @pl.when(cond)
        def _(kinds=kinds):
            if all(kind[0] is not None for kind in kinds):
                acc_scr[...] = _dot(h_scr[...], w_ref[...])
                for c, kind in enumerate(kinds):
                    finish(c, acc_scr[:, c * LANES:(c + 1) * LANES], kind)
                return
            for s in range(PROJ_DOT_SPLIT):
                lo = s * chunks_per_dot * LANES
                acc = _dot(h_scr[...], w_ref[:, lo:lo + chunks_per_dot * LANES])
                for k in range(chunks_per_dot):
                    c = s * chunks_per_dot + k
                    finish(c, acc[:, k * LANES:(k + 1) * LANES], kinds[c])


def _gate_kernel(x_ref, sh_ref, sc_ref, w_ref, o_ref, h_scr):
    @pl.when(pl.program_id(1) == 0)
    def _():
        _modulate_to_scratch(x_ref, sh_ref, sc_ref, h_scr)

    slab = o_ref.shape[1] // PROJ_DOT_SPLIT
    for s in range(PROJ_DOT_SPLIT):
        cols = slice(s * slab, (s + 1) * slab)
        acc = _dot(h_scr[...], w_ref[:, cols])
        o_ref[:, cols] = (1.0 / (1.0 + jnp.exp(-acc))).astype(o_ref.dtype)


def _mod_spec(which, row_of_tile):
    return pl.BlockSpec((None, 1, D_MODEL), lambda i, *_: (row_of_tile(i), 0, which))


def _in_projection(x, mod_l, w_in, l, qk_gain_l, cos_t, sin_t, *, row_of_tile, rope, q_scale, out_dtype):
    t = x.shape[0]
    tiles_per_seq = cos_t.shape[0] // TM
    tn = TN_PROJ_F32 if out_dtype == F32 else TN_PROJ
    return pl.pallas_call(
        functools.partial(_inproj_kernel, rope=rope, q_scale=q_scale),
        out_shape=jax.ShapeDtypeStruct((t, IN_COLS), out_dtype),
        grid=(t // TM, IN_COLS // tn),
        in_specs=[
            pl.BlockSpec((TM, D_MODEL), lambda i, j: (i, 0)),
            _mod_spec(0, row_of_tile),
            _mod_spec(1, row_of_tile),
            pl.BlockSpec((None, D_MODEL, tn), lambda i, j: (l, 0, j)),
            pl.BlockSpec((2, HEAD_DIM), lambda i, j: (0, 0)),
            pl.BlockSpec((TM, HEAD_DIM), lambda i, j: (i % tiles_per_seq, 0)),
            pl.BlockSpec((TM, 2 * HEAD_DIM), lambda i, j: (i % tiles_per_seq, 0)),
        ],
        out_specs=pl.BlockSpec((TM, tn), lambda i, j: (i, j)),
        scratch_shapes=[pltpu.VMEM((TM, D_MODEL), BF16), pltpu.VMEM((TM, tn), F32)],
        compiler_params=_cparams(("parallel", "arbitrary")),
        name="in_projection",
    )(x, mod_l, mod_l, w_in, qk_gain_l, cos_t, sin_t)


def _branch_gates(x, mod_l, w_gate, l, *, row_of_tile):
    t = x.shape[0]
    n = w_gate.shape[2]
    return pl.pallas_call(
        _gate_kernel,
        out_shape=jax.ShapeDtypeStruct((t, n), BF16),
        grid=(t // TM, n // TN_GATE),
        in_specs=[
            pl.BlockSpec((TM, D_MODEL), lambda i, j: (i, 0)),
            _mod_spec(0, row_of_tile),
            _mod_spec(1, row_of_tile),
            pl.BlockSpec((None, D_MODEL, TN_GATE), lambda i, j: (l, 0, j)),
        ],
        out_specs=pl.BlockSpec((TM, TN_GATE), lambda i, j: (i, j)),
        scratch_shapes=[pltpu.VMEM((TM, D_MODEL), BF16)],
        compiler_params=_cparams(("parallel", "arbitrary")),
        name="branch_gates",
    )(x, mod_l, mod_l, w_gate)


def _diff_lambda(lam_ref, lam_init):
    lp = lam_ref[...]
    t1 = jnp.sum(lp[0:1] * lp[1:2], axis=-1, keepdims=True)
    t2 = jnp.sum(lp[2:3] * lp[3:4], axis=-1, keepdims=True)
    return jnp.exp(t1) - jnp.exp(t2) + lam_init


def _softmax_rows(s, sink=None):
    m = jnp.max(s, axis=-1, keepdims=True)
    if sink is not None:
        m = jnp.maximum(m, sink)
    e = jnp.exp(s - m)
    den = jnp.sum(e, axis=-1, keepdims=True)
    if sink is not None:
        den = den + jnp.exp(sink - m)
    return e * (1.0 / den)


def _attn_prompt_kernel(p_ref, lam_ref, subln_ref, sink_ref, o_ref, *, lam_init):
    lam = _diff_lambda(lam_ref, lam_init)

    def blk(c0, w):
        return p_ref[:, c0:c0 + w].astype(BF16)

    for h in range(4):
        probs = []
        for m in range(2):
            q = blk(COL_AQ + h * 256 + m * HEAD_DIM, HEAD_DIM)
            k = blk(COL_AK + h * 256 + m * HEAD_DIM, HEAD_DIM)
            probs.append(_softmax_rows(_dot_nt(q, k) * SCALE))
        w = probs[0] - lam * probs[1]
        o = _dot(w.astype(BF16), blk(COL_AV + h * 256, 256))
        o = _rms(o, subln_ref[...]) * (1.0 - lam_init)
        o_ref[:, h * 256:(h + 1) * 256] = o.astype(o_ref.dtype)

    for mixer, (cq, ck, cv) in enumerate(((COL_BQ, COL_BK, COL_BV), (COL_CQ, COL_CK, COL_CV))):
        for kvh in range(2):
            k = blk(ck + kvh * HEAD_DIM, HEAD_DIM)
            v = blk(cv + kvh * HEAD_DIM, HEAD_DIM)
            for g in range(4):
                hq = kvh * 4 + g
                q = blk(cq + hq * HEAD_DIM, HEAD_DIM)
                sink = sink_ref[hq] if mixer == 1 else None
                p = _softmax_rows(_dot_nt(q, k) * SCALE, sink)
                o = _dot(p.astype(BF16), v)
                c0 = BRANCH_WIDTH * (1 + mixer) + hq * HEAD_DIM
                o_ref[:, c0:c0 + HEAD_DIM] = o.astype(o_ref.dtype)


def _attention_prompt(p, diff_lam_l, subln_l, sink_l, *, lam_init, seq):
    t = p.shape[0]
    return pl.pallas_call(
        functools.partial(_attn_prompt_kernel, lam_init=lam_init),
        out_shape=jax.ShapeDtypeStruct((t, 3 * BRANCH_WIDTH), BF16),
        grid=(t // seq,),
        in_specs=[
            pl.BlockSpec((seq, IN_COLS), lambda b: (b, 0)),
            pl.BlockSpec((4, HEAD_DIM), lambda b: (0, 0)),
            pl.BlockSpec((1, 256), lambda b: (0, 0)),
            pl.BlockSpec(memory_space=pltpu.SMEM),
        ],
        out_specs=pl.BlockSpec((seq, 3 * BRANCH_WIDTH), lambda b: (b, 0)),
        compiler_params=_cparams(("parallel",)),
        name="attention_prompt",
    )(p, diff_lam_l, subln_l, sink_l)


TQ = 512
TQ_FLASH = 1024
TK = 1024


def _tile_lanes(v, n):
    return jnp.concatenate([v] * n, axis=-1) if n > 1 else v


def _online_softmax(idx, s, m_scr):
    m_prev = m_scr[idx]
    m_new = jnp.maximum(m_prev, jnp.max(s, axis=-1, keepdims=True))
    m_scr[idx] = m_new
    alpha = jnp.exp2(m_prev - m_new)
    p = jnp.exp2(s - _tile_lanes(m_new, s.shape[1] // LANES))
    return p, alpha


def _diff_sample_kernel(q_ref, k_ref, v_ref, kc_ref, vc_ref, lam_ref, subln_ref, o_ref,
                        m_scr, l_scr, acc_scr, *, lam_init):
    kk = pl.program_id(2)

    @pl.when(kk == 0)
    def _():
        m_scr[...] = jnp.full(m_scr.shape, NEG_BIG, F32)
        l_scr[...] = jnp.zeros(l_scr.shape, F32)
        acc_scr[...] = jnp.zeros(acc_scr.shape, F32)

    def process(kb_ref, vb_ref):
        for h in range(4):
            v = vb_ref[:, h * 256:(h + 1) * 256].astype(BF16)
            for m in range(2):
                c0 = h * 256 + m * HEAD_DIM
                idx = h * 2 + m
                k = kb_ref[:, c0:c0 + HEAD_DIM].astype(BF16)
                p, alpha = _online_softmax(idx, _dot_nt(q_ref[:, c0:c0 + HEAD_DIM], k), m_scr)
                part = p[:, 0:LANES]
                for c in range(1, p.shape[1] // LANES):
                    part = part + p[:, c * LANES:(c + 1) * LANES]
                l_scr[idx] = alpha * l_scr[idx] + part
                acc_scr[idx] = _tile_lanes(alpha, 2) * acc_scr[idx] + _dot(p.astype(BF16), v)

    @pl.when(kk == 0)
    def _():
        process(kc_ref, vc_ref)

    @pl.when(kk > 0)
    def _():
        process(k_ref, v_ref)

    @pl.when(kk == pl.num_programs(2) - 1)
    def _():
        lam = _diff_lambda(lam_ref, lam_init)
        for h in range(4):
            l1 = jnp.sum(l_scr[2 * h], axis=-1, keepdims=True)
            l2 = jnp.sum(l_scr[2 * h + 1], axis=-1, keepdims=True)
            o1 = acc_scr[2 * h] * (1.0 / l1)
            o2 = acc_scr[2 * h + 1] * (1.0 / l2)
            o = _rms(o1 - lam * o2, subln_ref[...]) * (1.0 - lam_init)
            o_ref[:, h * 256:(h + 1) * 256] = o.astype(o_ref.dtype)


def _gqa_sample_kernel(q_ref, k_ref, v_ref, kc_ref, vc_ref, o_ref, m_scr, acc_scr):
    kk = pl.program_id(2)

    @pl.when(kk == 0)
    def _():
        m_scr[...] = jnp.full(m_scr.shape, NEG_BIG, F32)
        acc_scr[...] = jnp.zeros(acc_scr.shape, F32)

    def process(kb_ref, vb_ref):
        for kvh in range(2):
            k = kb_ref[:, kvh * HEAD_DIM:(kvh + 1) * HEAD_DIM].astype(BF16)
            v = vb_ref[:, kvh * HEAD_DIM:(kvh + 1) * HEAD_DIM].astype(BF16)
            v_ones = jnp.concatenate([v, jnp.ones_like(v)], axis=-1)
            for g in range(4):
                hq = kvh * 4 + g
                s = _dot_nt(q_ref[:, hq * HEAD_DIM:(hq + 1) * HEAD_DIM], k)
                p, alpha = _online_softmax(hq, s, m_scr)
                acc_scr[hq] = _tile_lanes(alpha, 2) * acc_scr[hq] + _dot(p.astype(BF16), v_ones)

    @pl.when(kk == 0)
    def _():
        process(kc_ref, vc_ref)

    @pl.when(kk > 0)
    def _():
        process(k_ref, v_ref)

    @pl.when(kk == pl.num_programs(2) - 1)
    def _():
        for hq in range(8):
            o = acc_scr[hq, :, 0:HEAD_DIM] / acc_scr[hq, :, HEAD_DIM:2 * HEAD_DIM]
            o_ref[:, hq * HEAD_DIM:(hq + 1) * HEAD_DIM] = o.astype(o_ref.dtype)


def _flash_sample(kernel, qkv, cache, l, *, q_col, k_col, v_col, kv_width, n_state, row_sum_scratch, extra,
                  extra_specs, name, dec_seq):
    t = qkv.shape[0]
    nb = t // dec_seq
    tq = TQ_FLASH
    nq = dec_seq // tq
    nk = dec_seq // TK
    past = cache.shape[3]
    q_blk, k_blk, v_blk = q_col // BRANCH_WIDTH, k_col // kv_width, v_col // kv_width
    kv_row = lambda b, qi, kk: b * nk + jnp.maximum(kk - 1, 0)
    return pl.pallas_call(
        kernel,
        out_shape=jax.ShapeDtypeStruct((t, BRANCH_WIDTH), BF16),
        grid=(nb, nq, nk + 1),
        in_specs=[
            pl.BlockSpec((tq, BRANCH_WIDTH), lambda b, qi, kk: (b * nq + qi, q_blk)),
            pl.BlockSpec((TK, kv_width), lambda b, qi, kk: (kv_row(b, qi, kk), k_blk)),
            pl.BlockSpec((TK, kv_width), lambda b, qi, kk: (kv_row(b, qi, kk), v_blk)),
            pl.BlockSpec((None, None, None, past, kv_width), lambda b, qi, kk: (b, l, 0, 0, 0)),
            pl.BlockSpec((None, None, None, past, kv_width), lambda b, qi, kk: (b, l, 1, 0, 0)),
        ] + extra_specs,
        out_specs=pl.BlockSpec((tq, BRANCH_WIDTH), lambda b, qi, kk: (b * nq + qi, 0)),
        scratch_shapes=[pltpu.VMEM((n_state, tq, LANES), F32)] * (2 if row_sum_scratch else 1)
        + [pltpu.VMEM((n_state, tq, 2 * LANES), F32)],
        compiler_params=_cparams(("parallel", "parallel", "arbitrary")),
        name=name,
    )(qkv, qkv, qkv, cache, cache, *extra)


def _window_sample_kernel(q0_ref, q1_ref, kp_ref, kc_ref, kn_ref, vp_ref, vc_ref, vn_ref, kctx_ref, vctx_ref,
                          sink_ref, o_ref, *, dec_seq):
    qi = pl.program_id(1)
    q_start = qi * TQ
    half = TQ // 2
    q_pos = q_start + lax.broadcasted_iota(jnp.int32, (TQ, 1), 0)
    segs = ((kp_ref, vp_ref, q_start - half, half), (kc_ref, vc_ref, q_start, TQ),
            (kn_ref, vn_ref, q_start + TQ, half))
    valid = []
    for _, _, start, n in segs:
        k_pos = start + lax.broadcasted_iota(jnp.int32, (TQ, n), 1)
        valid.append((jnp.abs(q_pos - k_pos) <= WINDOW) & (k_pos >= 0) & (k_pos < dec_seq))
    for kvh in range(2):
        q_ref = q0_ref if kvh == 0 else q1_ref
        lo, hi = kvh * HEAD_DIM, (kvh + 1) * HEAD_DIM
        kctx = kctx_ref[:, lo:hi].astype(BF16)
        vctx = vctx_ref[:, lo:hi].astype(BF16)
        for g in range(4):
            hq = kvh * 4 + g
            q = q_ref[:, g * HEAD_DIM:(g + 1) * HEAD_DIM]
            sink = sink_ref[hq] * LOG2E
            scores = [_dot_nt(q, kctx)]
            for (k_ref, _, _, _), ok in zip(segs, valid):
                scores.append(jnp.where(ok, _dot_nt(q, k_ref[:, lo:hi]), NEG_BIG))
            m = jnp.maximum(scores[0].max(axis=-1, keepdims=True), sink)
            for s in scores[1:]:
                m = jnp.maximum(m, s.max(axis=-1, keepdims=True))
            es = [jnp.exp2(s - m) for s in scores]
            den = jnp.exp2(sink - m)
            for e in es:
                den = den + jnp.sum(e, axis=-1, keepdims=True)
            o = _dot(es[0].astype(BF16), vctx)
            for e, (_, v_ref, _, _) in zip(es[1:], segs):
                o = o + _dot(e.astype(BF16), v_ref[:, lo:hi])
            o = o * (1.0 / den)
            o_ref[:, hq * HEAD_DIM:(hq + 1) * HEAD_DIM] = o.astype(o_ref.dtype)


def _window_sample(qkv, cache, sink_l, l, *, dec_seq):
    t = qkv.shape[0]
    nb = t // dec_seq
    nq = dec_seq // TQ
    half = TQ // 2
    n_half = dec_seq // half
    past = cache.shape[3]
    kvw = 2 * HEAD_DIM
    q_blk = COL_CQ // 512
    k_blk, v_blk = COL_CK // kvw, COL_CV // kvw
    prev_row = lambda b, qi: b * n_half + jnp.maximum(2 * qi - 1, 0)
    next_row = lambda b, qi: b * n_half + jnp.minimum(2 * qi + 2, n_half - 1)
    return pl.pallas_call(
        functools.partial(_window_sample_kernel, dec_seq=dec_seq),
        out_shape=jax.ShapeDtypeStruct((t, BRANCH_WIDTH), BF16),
        grid=(nb, nq),
        in_specs=[
            pl.BlockSpec((TQ, 512), lambda b, qi: (b * nq + qi, q_blk)),
            pl.BlockSpec((TQ, 512), lambda b, qi: (b * nq + qi, q_blk + 1)),
            pl.BlockSpec((half, kvw), lambda b, qi: (prev_row(b, qi), k_blk)),
            pl.BlockSpec((TQ, kvw), lambda b, qi: (b * nq + qi, k_blk)),
            pl.BlockSpec((half, kvw), lambda b, qi: (next_row(b, qi), k_blk)),
            pl.BlockSpec((half, kvw), lambda b, qi: (prev_row(b, qi), v_blk)),
            pl.BlockSpec((TQ, kvw), lambda b, qi: (b * nq + qi, v_blk)),
            pl.BlockSpec((half, kvw), lambda b, qi: (next_row(b, qi), v_blk)),
            pl.BlockSpec((None, None, None, past, kvw), lambda b, qi: (b, l, 0, 0, 0)),
            pl.BlockSpec((None, None, None, past, kvw), lambda b, qi: (b, l, 1, 0, 0)),
            pl.BlockSpec(memory_space=pltpu.SMEM),
        ],
        out_specs=pl.BlockSpec((TQ, BRANCH_WIDTH), lambda b, qi: (b * nq + qi, 0)),
        compiler_params=_cparams(("parallel", "parallel")),
        name="window_attention",
    )(qkv, qkv, qkv, qkv, qkv, qkv, qkv, qkv, cache, cache, sink_l)


def _merge_kernel(oa_ref, ob_ref, oc_ref, g_ref, w_ref, y_ref, acc_scr):
    r = pl.program_id(2)

    def contrib(o_ref):
        return g_ref[...].astype(F32) * _dot(o_ref[...], w_ref[...])

    @pl.when(r == 0)
    def _():
        acc_scr[...] = contrib(oa_ref)

    @pl.when(r == 1)
    def _():
        acc_scr[...] += contrib(ob_ref)

    @pl.when(r == 2)
    def _():
        y_ref[...] = (acc_scr[...] + contrib(oc_ref)).astype(y_ref.dtype)


def _merge_branches(o_arrays, o_blocks, gates, w_branch, l):
    t = gates.shape[0]
    tn = D_MODEL
    nn = D_MODEL // tn
    o_specs = [pl.BlockSpec((TM, BRANCH_WIDTH), functools.partial(lambda i, n, r, blk: (i, blk), blk=blk))
               for blk in o_blocks]
    return pl.pallas_call(
        _merge_kernel,
        out_shape=jax.ShapeDtypeStruct((t, D_MODEL), BF16),
        grid=(t // TM, nn, 3),
        in_specs=o_specs + [
            pl.BlockSpec((TM, tn), lambda i, n, r: (i, r * nn + n)),
            pl.BlockSpec((None, None, BRANCH_WIDTH, tn), lambda i, n, r: (l, r, 0, n)),
        ],
        out_specs=pl.BlockSpec((TM, tn), lambda i, n, r: (i, n)),
        scratch_shapes=[pltpu.VMEM((TM, tn), F32)],
        compiler_params=_cparams(("parallel", "parallel", "arbitrary")),
        name="merge_branches",
    )(*o_arrays, gates, w_branch)


def _layer_norm_rows(v, gain, bias):
    return _norm_rows(v) * gain + bias


def _outproj_kernel(y_ref, w_ref, x_ref, g_ref, gain_ref, bias_ref, o_ref):
    n = pl.program_id(1)
    tn = w_ref.shape[1]
    n_blocks = o_ref.shape[1] // tn
    z = _dot(y_ref[...], w_ref[...])
    for b in range(n_blocks):
        @pl.when(n == b)
        def _(b=b):
            cols = slice(b * tn, (b + 1) * tn)
            o_ref[:, cols] = ALPHA * x_ref[:, cols] + g_ref[:, cols] * z

    @pl.when(n == n_blocks - 1)
    def _():
        o_ref[...] = _layer_norm_rows(o_ref[...], gain_ref[...], bias_ref[...])


def _out_projection(y, w_o, l, x, mod_l, ln_gain_l, ln_bias_l, *, row_of_tile):
    t = x.shape[0]
    tn = 512
    return pl.pallas_call(
        _outproj_kernel,
        out_shape=jax.ShapeDtypeStruct((t, D_MODEL), F32),
        grid=(t // TM, D_MODEL // tn),
        in_specs=[
            pl.BlockSpec((TM, D_MODEL), lambda i, n: (i, 0)),
            pl.BlockSpec((None, D_MODEL, tn), lambda i, n: (l, 0, n)),
            pl.BlockSpec((TM, D_MODEL), lambda i, n: (i, 0)),
            _mod_spec(2, row_of_tile),
            pl.BlockSpec((1, D_MODEL), lambda i, n: (0, 0)),
            pl.BlockSpec((1, D_MODEL), lambda i, n: (0, 0)),
        ],
        out_specs=pl.BlockSpec((TM, D_MODEL), lambda i, n: (i, 0)),
        compiler_params=_cparams(("parallel", "arbitrary")),
        name="out_projection",
    )(y, w_o, x, mod_l, ln_gain_l, ln_bias_l)


def _route(p):
    rows = [p[e:e + 1, :] for e in range(N_EXPERTS)]
    best_score, best_group = None, None
    for g in range(N_GROUPS):
        members = rows[g * EXPERTS_PER_GROUP:(g + 1) * EXPERTS_PER_GROUP]
        score = None
        for a in range(EXPERTS_PER_GROUP):
            for b in range(a + 1, EXPERTS_PER_GROUP):
                pair = members[a] + members[b]
                score = pair if score is None else jnp.maximum(score, pair)
        if g == 0:
            best_score, best_group = score, jnp.zeros(score.shape, F32)
        else:
            better = score > best_score
            best_group = jnp.where(better, float(g), best_group)
            best_score = jnp.where(better, score, best_score)
    e_idx = lax.broadcasted_iota(jnp.int32, p.shape, 0).astype(F32)
    g_idx = jnp.floor(e_idx * (1.0 / EXPERTS_PER_GROUP))
    masked = jnp.where(g_idx == best_group, p, -1.0)
    w1 = jnp.max(masked, axis=0, keepdims=True)
    i1 = jnp.min(jnp.where(masked == w1, e_idx, float(N_EXPERTS)), axis=0, keepdims=True)
    masked2 = jnp.where(e_idx == i1, -2.0, masked)
    w2 = jnp.max(masked2, axis=0, keepdims=True)
    i2 = jnp.min(jnp.where(masked2 == w2, e_idx, float(N_EXPERTS)), axis=0, keepdims=True)
    tot = w1 + w2
    return e_idx, i1, i2, w1 / tot, w2 / tot


ROUTE_ROWS = 8
D_PACKED = D_MODEL // 2


def _pack_bf16_pairs(xb):
    half = xb.shape[1] // 2
    lo = lax.bitcast_convert_type(xb[:, :half].astype(F32), jnp.uint32)
    hi = lax.bitcast_convert_type(xb[:, half:].astype(F32), jnp.uint32)
    return (hi & jnp.uint32(0xFFFF0000)) | (lo >> 16)


def _unpack_bf16_pairs(words):
    lo = lax.bitcast_convert_type(words << 16, F32).astype(BF16)
    hi = lax.bitcast_convert_type(words & jnp.uint32(0xFFFF0000), F32).astype(BF16)
    return lo, hi


def _moe_route_kernel(xa_ref, xb_ref, sh_ref, sc_ref, wr_ref, h_ref, rec_ref, rec_t_ref, cnt_ref, carry_scr,
                      *, tiles_a):
    @pl.when(pl.program_id(0) == 0)
    def _():
        carry_scr[...] = jnp.zeros(carry_scr.shape, F32)

    x = jnp.where(pl.program_id(0) < tiles_a, xa_ref[...], xb_ref[...])
    h = _norm_rows(x) * (1.0 + sc_ref[...]) + sh_ref[...]
    hb = h.astype(BF16)
    h_ref[...] = _pack_bf16_pairs(hb)
    logits = _dot_nt(wr_ref[...].astype(BF16), hb)
    m = jnp.max(logits, axis=0, keepdims=True)
    e = jnp.exp(logits - m)
    probs = e / jnp.sum(e, axis=0, keepdims=True)
    e_idx, i1, i2, w1, w2 = _route(probs)
    tm = probs.shape[1]
    oh1 = (e_idx == i1).astype(F32)
    oh2 = (e_idx == i2).astype(F32)
    oh = oh1 + oh2
    earlier = (lax.broadcasted_iota(jnp.int32, (tm, tm), 0) < lax.broadcasted_iota(jnp.int32, (tm, tm), 1))
    rank = carry_scr[:, 0:1] + _dot(oh.astype(BF16), earlier.astype(BF16))
    r1 = jnp.sum(oh1 * rank, axis=0, keepdims=True)
    r2 = jnp.sum(oh2 * rank, axis=0, keepdims=True)
    carry_scr[...] = carry_scr[...] + jnp.sum(oh, axis=1, keepdims=True)
    cnt_ref[...] = carry_scr[...]
    row = lax.broadcasted_iota(jnp.int32, (ROUTE_ROWS, tm), 0)
    rec = jnp.zeros((ROUTE_ROWS, tm), F32)
    for k, v in enumerate((i1, i2, r1, r2, w1, w2)):
        rec = jnp.where(row == k, v, rec)
    rec_ref[...] = rec
    pad = jnp.zeros((LANES - ROUTE_ROWS, tm), F32)
    rec_t_ref[...] = jnp.concatenate([rec, pad], axis=0).T


def _moe_route(x_a, x_b, mod_l, w_router_t, *, rows_a, rows_b):
    tm = TM_SMALL
    tiles_a = x_a.shape[0] // tm
    t = x_a.shape[0] + x_b.shape[0]
    row_a, row_b = rows_a(tm), rows_b(tm)
    row_of_tile = lambda i: jnp.where(i < tiles_a, row_a(i), row_b(i - tiles_a))
    return pl.pallas_call(
        functools.partial(_moe_route_kernel, tiles_a=tiles_a),
        out_shape=(jax.ShapeDtypeStruct((t, D_PACKED), jnp.uint32), jax.ShapeDtypeStruct((ROUTE_ROWS, t), F32),
                   jax.ShapeDtypeStruct((t, LANES), F32), jax.ShapeDtypeStruct((N_EXPERTS, LANES), F32)),
        grid=(t // tm,),
        in_specs=[
            pl.BlockSpec((tm, D_MODEL), lambda i: (jnp.minimum(i, tiles_a - 1), 0)),
            pl.BlockSpec((tm, D_MODEL), lambda i: (jnp.maximum(i - tiles_a, 0), 0)),
            _mod_spec(3, row_of_tile),
            _mod_spec(4, row_of_tile),
            pl.BlockSpec((N_EXPERTS, D_MODEL), lambda i: (0, 0)),
        ],
        out_specs=(pl.BlockSpec((tm, D_PACKED), lambda i: (i, 0)),
                   pl.BlockSpec((ROUTE_ROWS, tm), lambda i: (0, i)),
                   pl.BlockSpec((tm, LANES), lambda i: (i, 0)),
                   pl.BlockSpec((N_EXPERTS, LANES), lambda i: (0, 0))),
        scratch_shapes=[pltpu.VMEM((N_EXPERTS, LANES), F32)],
        compiler_params=_cparams(("arbitrary",)),
        name="moe_route",
    )(x_a, x_b, mod_l, mod_l, w_router_t)


ROW_DMA_UNROLL = 8
GATHER_ORDER_STRIDE = 37


def _row_copy(src_hbm, src_row, dst_buf, dst_row, sem):
    return pltpu.make_async_copy(src_hbm.at[pl.ds(src_row, 1)], dst_buf.at[pl.ds(dst_row, 1)], sem)


def _moe_expert_kernel(tile_expert_ref, n_tiles_ref, src_ref, h_hbm, wg_ref, wu_ref, wd_ref, ys_ref,
                       x_buf, sems, *, te):
    del tile_expert_ref
    j = pl.program_id(0)
    n_valid = n_tiles_ref[0]
    slot = j % 2

    def start_gather(tile, s):
        def body(r, carry):
            row = (r * GATHER_ORDER_STRIDE) % te
            _row_copy(h_hbm, src_ref[tile * te + row], x_buf.at[s], row, sems.at[s]).start()
            return carry
        lax.fori_loop(0, te, body, 0, unroll=ROW_DMA_UNROLL)

    def wait_gather(s):
        def body(r, carry):
            _row_copy(h_hbm, 0, x_buf.at[s], r, sems.at[s]).wait()
            return carry
        lax.fori_loop(0, te, body, 0, unroll=ROW_DMA_UNROLL)

    @pl.when((j == 0) & (n_valid > 0))
    def _():
        start_gather(0, 0)

    @pl.when(j + 1 < n_valid)
    def _():
        start_gather(j + 1, 1 - slot)

    @pl.when(j < n_valid)
    def _():
        wait_gather(slot)
        x_lo, x_hi = _unpack_bf16_pairs(x_buf[slot])

        def project(w_ref):
            return (_dot(x_lo, w_ref[:D_PACKED, :].astype(BF16)) + _dot(x_hi, w_ref[D_PACKED:, :].astype(BF16)))

        gate = project(wg_ref)
        up = project(wu_ref)
        hid = gate / (1.0 + jnp.exp(-gate)) * up
        ys_ref[...] = _pack_bf16_pairs(_dot(hid.astype(BF16), wd_ref[...].astype(BF16)).astype(BF16))

    @pl.when(j >= n_valid)
    def _():
        ys_ref[...] = jnp.zeros(ys_ref.shape, ys_ref.dtype)


def _moe_experts(h, tile_expert, n_tiles, src, wg, wu, wd, l, *, te):
    n_rows = src.shape[0]
    w_in_spec = pl.BlockSpec((None, None, D_MODEL, D_EXPERT), lambda j, te_ref, *_: (l, te_ref[j], 0, 0))
    w_out_spec = pl.BlockSpec((None, None, D_EXPERT, D_MODEL), lambda j, te_ref, *_: (l, te_ref[j], 0, 0))
    return pl.pallas_call(
        functools.partial(_moe_expert_kernel, te=te),
        out_shape=jax.ShapeDtypeStruct((n_rows, D_PACKED), jnp.uint32),
        grid_spec=pltpu.PrefetchScalarGridSpec(
            num_scalar_prefetch=3,
            grid=(n_rows // te,),
            in_specs=[pl.BlockSpec(memory_space=pltpu.HBM), w_in_spec, w_in_spec, w_out_spec],
            out_specs=pl.BlockSpec((te, D_PACKED), lambda j, *_: (j, 0)),
            scratch_shapes=[pltpu.VMEM((2, te, D_PACKED), h.dtype), pltpu.SemaphoreType.DMA((2,))],
        ),
        compiler_params=_cparams(("arbitrary",)),
        name="moe_experts",
    )(tile_expert, n_tiles, src, h, wg, wu, wd)


TM_COMBINE = 256


def _moe_combine_kernel(dest_ref, ys_hbm, x_ref, rec_t_ref, g_ref, gain_ref, bias_ref, o_ref, y_buf, sems,
                        *, n_tokens, token_offset):
    i = pl.program_id(0)
    tm = x_ref.shape[0]
    slot = i % 2

    def start_gather(tile, s):
        def body(r, carry):
            for k in range(2):
                row = dest_ref[k * n_tokens + token_offset + tile * tm + r]
                _row_copy(ys_hbm, row, y_buf.at[s, k], r, sems.at[s]).start()
            return carry
        lax.fori_loop(0, tm, body, 0, unroll=ROW_DMA_UNROLL)

    def wait_gather(s):
        def body(r, carry):
            for k in range(2):
                _row_copy(ys_hbm, 0, y_buf.at[s, k], r, sems.at[s]).wait()
            return carry
        lax.fori_loop(0, tm, body, 0, unroll=ROW_DMA_UNROLL)

    @pl.when(i == 0)
    def _():
        start_gather(0, 0)

    @pl.when(i + 1 < pl.num_programs(0))
    def _():
        start_gather(i + 1, 1 - slot)

    wait_gather(slot)
    w1, w2 = rec_t_ref[:, 4:5], rec_t_ref[:, 5:6]
    y1_lo, y1_hi = _unpack_bf16_pairs(y_buf[slot, 0])
    y2_lo, y2_hi = _unpack_bf16_pairs(y_buf[slot, 1])
    for half, (y1, y2) in enumerate(((y1_lo, y2_lo), (y1_hi, y2_hi))):
        cols = slice(half * D_PACKED, (half + 1) * D_PACKED)
        z = w1 * y1.astype(F32) + w2 * y2.astype(F32)
        o_ref[:, cols] = ALPHA * x_ref[:, cols] + g_ref[:, cols] * z
    o_ref[...] = _layer_norm_rows(o_ref[...], gain_ref[...], bias_ref[...])


def _moe_combine(dest, ys, x, rec_t, mod_l, ln_gain_l, ln_bias_l, *, row_of_tile, token_offset):
    t = x.shape[0]
    tm = TM_COMBINE
    first_tile = token_offset // tm
    vec = pl.BlockSpec((1, D_MODEL), lambda i, *_: (0, 0))
    return pl.pallas_call(
        functools.partial(_moe_combine_kernel, n_tokens=rec_t.shape[0], token_offset=token_offset),
        out_shape=jax.ShapeDtypeStruct((t, D_MODEL), F32),
        grid_spec=pltpu.PrefetchScalarGridSpec(
            num_scalar_prefetch=1,
            grid=(t // tm,),
            in_specs=[
                pl.BlockSpec(memory_space=pltpu.HBM),
                pl.BlockSpec((tm, D_MODEL), lambda i, *_: (i, 0)),
                pl.BlockSpec((tm, LANES), lambda i, *_: (first_tile + i, 0)),
                _mod_spec(5, row_of_tile),
                vec, vec,
            ],
            out_specs=pl.BlockSpec((tm, D_MODEL), lambda i, *_: (i, 0)),
            scratch_shapes=[pltpu.VMEM((2, 2, tm, D_PACKED), jnp.uint32), pltpu.SemaphoreType.DMA((2,))],
        ),
        compiler_params=_cparams(("arbitrary",)),
        name="moe_combine",
    )(dest, ys, x, rec_t, mod_l, ln_gain_l, ln_bias_l)


def _dispatch_tables(rec, cnt, *, te):
    t = rec.shape[1]
    n_rows = 2 * t + N_EXPERTS * te
    e12 = rec[0:2].astype(jnp.int32)
    r12 = rec[2:4].astype(jnp.int32)
    counts = cnt[:, 0].astype(jnp.int32)
    padded = (counts + te - 1) // te * te
    ends = jnp.cumsum(padded)
    offsets = ends - padded
    expert_ids = jnp.arange(N_EXPERTS, dtype=jnp.int32)[:, None, None]
    dest = (jnp.sum(jnp.where(e12[None] == expert_ids, offsets[:, None, None], 0), axis=0) + r12).reshape(2 * t)
    tokens = jnp.tile(jnp.arange(t, dtype=jnp.int32), 2)
    filler = jnp.arange(n_rows, dtype=jnp.int32) % t
    src = filler.at[dest].set(tokens, unique_indices=True)
    tile_start = jnp.arange(n_rows // te, dtype=jnp.int32) * te
    tile_expert = jnp.minimum(jnp.searchsorted(ends, tile_start, side="right"), N_EXPERTS - 1).astype(jnp.int32)
    n_tiles = (ends[-1:] // te).astype(jnp.int32)
    return dest, src, tile_expert, n_tiles


def _rope_tables(n_tokens):
    rows = n_tokens // GRID_W
    row = jnp.repeat(jnp.arange(rows), GRID_W).astype(F32)
    col = jnp.tile(jnp.arange(GRID_W), rows).astype(F32)
    quarter = HEAD_DIM // 4
    inv_freq = ROPE_THETA ** (-jnp.arange(quarter, dtype=F32) / quarter)
    ang_r, ang_c = row[:, None] * inv_freq, col[:, None] * inv_freq
    cos = jnp.concatenate([jnp.cos(ang_r), jnp.cos(ang_r), jnp.cos(ang_c), jnp.cos(ang_c)], axis=-1)
    zero = jnp.zeros_like(ang_r)
    sin = jnp.concatenate([-jnp.sin(ang_r), zero, -jnp.sin(ang_c), zero,
                           zero, jnp.sin(ang_r), zero, jnp.sin(ang_c)], axis=-1)
    return cos, sin


def _row_of_tile(first_row, tokens_per_row):
    def for_tile(tile):
        return lambda i: first_row + (i * tile) // tokens_per_row
    return for_tile


def _mixer_output(x, attn_arrays, attn_blocks, gates, mod_l, rows, l, w_branch, w_o, ln_gain, ln_bias):
    y = _merge_branches(attn_arrays, attn_blocks, gates, w_branch, l)
    return _out_projection(y, w_o, l, x, mod_l, ln_gain[l, 0:1], ln_bias[l, 0:1], row_of_tile=rows(TM))


EXPERT_TILE = 512


def _moe_ffn(x_a, x_b, mod_l, rows_a, rows_b, l, ln_gain, ln_bias, w_router_t, w_e_gate, w_e_up, w_e_down):
    h2, rec, rec_t, cnt = _moe_route(x_a, x_b, mod_l, w_router_t, rows_a=rows_a, rows_b=rows_b)
    dest, src, tile_expert, n_tiles = _dispatch_tables(rec, cnt, te=EXPERT_TILE)
    ys = _moe_experts(h2, tile_expert, n_tiles, src, w_e_gate, w_e_up, w_e_down, l, te=EXPERT_TILE)
    outs = []
    for x, rows, offset in ((x_a, rows_a, 0), (x_b, rows_b, x_a.shape[0])):
        outs.append(_moe_combine(dest, ys, x, rec_t, mod_l, ln_gain[l, 1:2], ln_bias[l, 1:2],
                                 row_of_tile=rows(TM_COMBINE), token_offset=offset))
    return outs


def kernel(x_prompt, x_sample, cache_kv_a, cache_kv_b, cache_kv_c, c, c_ctx, w_in, w_gate, w_branch, w_o,
           w_mod, b_mod, ln_gain, ln_bias, diff_lam, diff_subln, qk_gain, sink, w_router, w_e_gate, w_e_up,
           w_e_down):
    batch, seq, _ = x_prompt.shape
    dec_batch, dec_seq, _ = x_sample.shape
    past = cache_kv_a.shape[3]
    t_p, t_s = batch * seq, dec_batch * dec_seq

    cond = jnp.concatenate([c_ctx[None], c, jnp.zeros((MOD_ROWS - 1 - dec_batch, D_MODEL), F32)], axis=0)
    mod = _modulation(cond.T, w_mod, b_mod)
    cos_t, sin_t = _rope_tables(dec_seq)
    w_router_t = w_router.T
    w_in, w_gate, w_branch, w_o = (w.astype(BF16) for w in (w_in, w_gate, w_branch, w_o))
    cache_a = cache_kv_a.reshape(dec_batch, DEPTH, 2, past, 4 * 256)
    cache_b = cache_kv_b.reshape(dec_batch, DEPTH, 2, past, 2 * HEAD_DIM)
    cache_c = cache_kv_c.reshape(dec_batch, DEPTH, 2, past, 2 * HEAD_DIM)
    rows_p = _row_of_tile(0, t_p)
    rows_s = _row_of_tile(1, dec_seq)

    y_p = x_prompt.reshape(t_p, D_MODEL)
    y_s = x_sample.reshape(t_s, D_MODEL)
    projections = []
    for l in range(DEPTH):
        lam_init = 0.8 - 0.6 * math.exp(-0.3 * l)
        mod_l = mod[l].reshape(MOD_ROWS, 1, N_MOD * D_MODEL)
        subln_l = diff_subln[l].reshape(1, 256)
        shared = (w_branch, w_o, ln_gain, ln_bias)

        p = _in_projection(y_p, mod_l, w_in, l, qk_gain[l], cos_t, sin_t, row_of_tile=rows_p(TM),
                           rope=False, q_scale=1.0, out_dtype=F32)
        gates = _branch_gates(y_p, mod_l, w_gate, l, row_of_tile=rows_p(TM))
        attn = _attention_prompt(p, diff_lam[l], subln_l, sink[l], lam_init=lam_init, seq=seq)
        y_p = _mixer_output(y_p, (attn, attn, attn), (0, 1, 2), gates, mod_l, rows_p, l, *shared)
        projections.append(p.reshape(batch, seq, IN_COLS))

        qkv = _in_projection(y_s, mod_l, w_in, l, qk_gain[l], cos_t, sin_t, row_of_tile=rows_s(TM),
                             rope=True, q_scale=SCALE * LOG2E, out_dtype=BF16)
        gates = _branch_gates(y_s, mod_l, w_gate, l, row_of_tile=rows_s(TM))
        vec = lambda shape: pl.BlockSpec(shape, lambda b, qi, kk: (0, 0))
        a_o = _flash_sample(functools.partial(_diff_sample_kernel, lam_init=lam_init), qkv, cache_a, l,
                            q_col=COL_AQ, k_col=COL_AK, v_col=COL_AV, kv_width=1024, n_state=8,
                            row_sum_scratch=True, extra=(diff_lam[l], subln_l),
                            extra_specs=[vec((4, HEAD_DIM)), vec((1, 256))], name="diff_attention",
                            dec_seq=dec_seq)
        b_o = _flash_sample(_gqa_sample_kernel, qkv, cache_b, l, q_col=COL_BQ, k_col=COL_BK, v_col=COL_BV,
                            kv_width=256, n_state=8, row_sum_scratch=False, extra=(), extra_specs=[],
                            name="gqa_attention", dec_seq=dec_seq)
        c_o = _window_sample(qkv, cache_c, sink[l], l, dec_seq=dec_seq)
        y_s = _mixer_output(y_s, (a_o, b_o, c_o), (0, 0, 0), gates, mod_l, rows_s, l, *shared)

        y_p, y_s = _moe_ffn(y_p, y_s, mod_l, rows_p, rows_s, l, ln_gain, ln_bias, w_router_t, w_e_gate, w_e_up,
                            w_e_down)

    def new_cache(col_k, col_v, col_end, heads):
        parts = [p[..., c0:c1] for p in projections for c0, c1 in ((col_k, col_v), (col_v, col_end))]
        return jnp.stack(parts, axis=1).reshape(batch, DEPTH, 2, seq, heads, (col_v - col_k) // heads)

    new_kv_a = new_cache(COL_AK, COL_AV, COL_BQ, 4)
    new_kv_b = new_cache(COL_BK, COL_BV, COL_CQ, 2)
    new_kv_c = new_cache(COL_CK, COL_CV, IN_COLS, 2)
    return (y_p.reshape(batch, seq, D_MODEL), y_s.reshape(dec_batch, dec_seq, D_MODEL),
            new_kv_a, new_kv_b, new_kv_c)
```

```python
import functools
import math

import jax
import jax.numpy as jnp
from jax import lax
from jax.experimental import pallas as pl
from jax.experimental.pallas import tpu as pltpu

F32 = jnp.float32
BF16 = jnp.bfloat16

D_MODEL = 2048
HEAD_DIM = 128
GRID_W = 64
ROPE_THETA = 10000.0
WINDOW = 128
N_EXPERTS = 16
N_GROUPS = 4
EXPERTS_PER_GROUP = N_EXPERTS // N_GROUPS
D_EXPERT = 512
N_MOD = 6
DEPTH = 2
ALPHA = (2 * DEPTH) ** 0.25
EPS = 1e-6
IN_COLS = 6144
BRANCH_WIDTH = 1024
SCALE = HEAD_DIM ** -0.5
LOG2E = math.log2(math.e)
NEG_BIG = -1e30

COL_AQ, COL_AK, COL_AV = 0, 1024, 2048
COL_BQ, COL_BK, COL_BV = 3072, 4096, 4352
COL_CQ, COL_CK, COL_CV = 4608, 5632, 5888

LANES = 128
VMEM_LIMIT = 56 * 1024 * 1024

TM = 1024
TN_PROJ = 1024
TN_PROJ_F32 = 1024
TN_GATE = 1024
TM_SMALL = 512


def _cparams(sem):
    return pltpu.CompilerParams(dimension_semantics=sem, vmem_limit_bytes=VMEM_LIMIT)


def _dot(a, b):
    return jnp.dot(a, b, preferred_element_type=F32)


def _dot_nt(a, b):
    return lax.dot_general(a, b, (((1,), (1,)), ((), ())), preferred_element_type=F32)


def _norm_rows(x):
    mu = jnp.mean(x, axis=-1, keepdims=True)
    xc = x - mu
    var = jnp.mean(xc * xc, axis=-1, keepdims=True)
    return xc * lax.rsqrt(var + EPS)


def _rms(v, gain):
    ms = jnp.mean(v * v, axis=-1, keepdims=True)
    return v * lax.rsqrt(ms + EPS) * gain


def _rope(v, cos, sin_from_upper, sin_from_lower):
    return v * cos + pltpu.roll(v, 96, 1) * sin_from_upper + pltpu.roll(v, 32, 1) * sin_from_lower


N_COND = 3
MOD_ROWS = 8
TN_MOD = 1024


def _mod_kernel(cond_ref, w_ref, b_ref, o_ref):
    w = w_ref[...]
    row_idx = lax.broadcasted_iota(jnp.int32, (MOD_ROWS, TN_MOD), 0)
    out = jnp.zeros((MOD_ROWS, TN_MOD), F32)
    for r in range(N_COND):
        c = cond_ref[:, r:r + 1]
        s = c / (1.0 + jnp.exp(-c))
        m = jnp.sum(w * s, axis=0, keepdims=True) + b_ref[...]
        out = jnp.where(row_idx == r, m, out)
    o_ref[...] = out


def _modulation(cond_t, w_mod, b_mod):
    n = N_MOD * D_MODEL
    return pl.pallas_call(
        _mod_kernel,
        out_shape=jax.ShapeDtypeStruct((DEPTH, MOD_ROWS, n), F32),
        grid=(DEPTH, n // TN_MOD),
        in_specs=[
            pl.BlockSpec((D_MODEL, MOD_ROWS), lambda l, j: (0, 0)),
            pl.BlockSpec((None, D_MODEL, TN_MOD), lambda l, j: (l, 0, j)),
            pl.BlockSpec((None, 1, TN_MOD), lambda l, j: (l, 0, j)),
        ],
        out_specs=pl.BlockSpec((None, MOD_ROWS, TN_MOD), lambda l, j: (l, 0, j)),
        compiler_params=_cparams(("parallel", "parallel")),
        name="modulation",
    )(cond_t, w_mod, b_mod.reshape(DEPTH, 1, n))


def _modulate_to_scratch(x_ref, sh_ref, sc_ref, h_scr):
    h = _norm_rows(x_ref[...]) * (1.0 + sc_ref[...]) + sh_ref[...]
    h_scr[...] = h.astype(BF16)


def _head_chunk_kinds():
    kinds = []
    for n_chunks, kind in ((8, (None, True, True)), (8, (None, True, False)), (8, (None, False, False)),
                           (8, ("q", True, True)), (2, ("k", True, False)), (2, (None, False, False)),
                           (8, (None, True, True)), (2, (None, True, False)), (2, (None, False, False))):
        kinds.extend([kind] * n_chunks)
    return kinds


_HEAD_CHUNK_KINDS = _head_chunk_kinds()


def _inproj_kernel(x_ref, sh_ref, sc_ref, w_ref, qk_gain_ref, cos_ref, sin_ref, o_ref, h_scr, *, rope,
                   q_scale):
    j = pl.program_id(1)
    gain_q = qk_gain_ref[0:1, :]
    gain_k = qk_gain_ref[1:2, :]

    @pl.when(j == 0)
    def _():
        _modulate_to_scratch(x_ref, sh_ref, sc_ref, h_scr)

    acc = _dot(h_scr[...], w_ref[...])
    tn = o_ref.shape[1]
    n_chunks = tn // LANES

    def chunk(c):
        return acc[:, c * LANES:(c + 1) * LANES]

    def store(c, v):
        o_ref[:, c * LANES:(c + 1) * LANES] = v.astype(o_ref.dtype)

    def rp(v):
        return _rope(v, cos_ref[...], sin_ref[:, :HEAD_DIM], sin_ref[:, HEAD_DIM:]) if rope else v

    def qs(v):
        return v * q_scale if q_scale != 1.0 else v

    def finish(c, kind):
        norm, rotate, is_query = kind
        v = chunk(c)
        if norm is not None:
            v = _rms(v, gain_q if norm == "q" else gain_k)
        if rotate:
            v = rp(v)
        store(c, qs(v) if is_query else v)

    tile_kinds = [tuple(_HEAD_CHUNK_KINDS[t * n_chunks:(t + 1) * n_chunks]) for t in range(IN_COLS // tn)]
    for kinds in dict.fromkeys(tile_kinds):
        tiles = [t for t, k in enumerate(tile_kinds) if k == kinds]
        cond = j == tiles[0]
        for t in tiles[1:]:
            cond = cond | (j == t)

        @pl.when(cond)
        def _(kinds=kinds):
            for c, kind in enumerate(kinds):
                finish(c, kind)


def _gate_kernel(x_ref, sh_ref, sc_ref, w_ref, o_ref, h_scr):
    @pl.when(pl.program_id(1) == 0)
    def _():
        _modulate_to_scratch(x_ref, sh_ref, sc_ref, h_scr)

    acc = _dot(h_scr[...], w_ref[...])
    o_ref[...] = (1.0 / (1.0 + jnp.exp(-acc))).astype(o_ref.dtype)


def _mod_spec(which, row_of_tile):
    return pl.BlockSpec((None, 1, D_MODEL), lambda i, *_: (row_of_tile(i), 0, which))


def _in_projection(x, mod_l, w_in, l, qk_gain_l, cos_t, sin_t, *, row_of_tile, rope, q_scale, out_dtype):
    t = x.shape[0]
    tiles_per_seq = cos_t.shape[0] // TM
    tn = TN_PROJ_F32 if out_dtype == F32 else TN_PROJ
    return pl.pallas_call(
        functools.partial(_inproj_kernel, rope=rope, q_scale=q_scale),
        out_shape=jax.ShapeDtypeStruct((t, IN_COLS), out_dtype),
        grid=(t // TM, IN_COLS // tn),
        in_specs=[
            pl.BlockSpec((TM, D_MODEL), lambda i, j: (i, 0)),
            _mod_spec(0, row_of_tile),
            _mod_spec(1, row_of_tile),
            pl.BlockSpec((None, D_MODEL, tn), lambda i, j: (l, 0, j)),
            pl.BlockSpec((2, HEAD_DIM), lambda i, j: (0, 0)),
            pl.BlockSpec((TM, HEAD_DIM), lambda i, j: (i % tiles_per_seq, 0)),
            pl.BlockSpec((TM, 2 * HEAD_DIM), lambda i, j: (i % tiles_per_seq, 0)),
        ],
        out_specs=pl.BlockSpec((TM, tn), lambda i, j: (i, j)),
        scratch_shapes=[pltpu.VMEM((TM, D_MODEL), BF16)],
        compiler_params=_cparams(("parallel", "arbitrary")),
        name="in_projection",
    )(x, mod_l, mod_l, w_in, qk_gain_l, cos_t, sin_t)


def _branch_gates(x, mod_l, w_gate, l, *, row_of_tile):
    t = x.shape[0]
    n = w_gate.shape[2]
    return pl.pallas_call(
        _gate_kernel,
        out_shape=jax.ShapeDtypeStruct((t, n), BF16),
        grid=(t // TM, n // TN_GATE),
        in_specs=[
            pl.BlockSpec((TM, D_MODEL), lambda i, j: (i, 0)),
            _mod_spec(0, row_of_tile),
            _mod_spec(1, row_of_tile),
            pl.BlockSpec((None, D_MODEL, TN_GATE), lambda i, j: (l, 0, j)),
        ],
        out_specs=pl.BlockSpec((TM, TN_GATE), lambda i, j: (i, j)),
        scratch_shapes=[pltpu.VMEM((TM, D_MODEL), BF16)],
        compiler_params=_cparams(("parallel", "arbitrary")),
        name="branch_gates",
    )(x, mod_l, mod_l, w_gate)


def _diff_lambda(lam_ref, lam_init):
    lp = lam_ref[...]
    t1 = jnp.sum(lp[0:1] * lp[1:2], axis=-1, keepdims=True)
    t2 = jnp.sum(lp[2:3] * lp[3:4], axis=-1, keepdims=True)
    return jnp.exp(t1) - jnp.exp(t2) + lam_init


def _softmax_rows(s, sink=None):
    m = jnp.max(s, axis=-1, keepdims=True)
    if sink is not None:
        m = jnp.maximum(m, sink)
    e = jnp.exp(s - m)
    den = jnp.sum(e, axis=-1, keepdims=True)
    if sink is not None:
        den = den + jnp.exp(sink - m)
    return e * (1.0 / den)


def _attn_prompt_kernel(p_ref, lam_ref, subln_ref, sink_ref, o_ref, *, lam_init):
    lam = _diff_lambda(lam_ref, lam_init)

    def blk(c0, w):
        return p_ref[:, c0:c0 + w].astype(BF16)

    for h in range(4):
        probs = []
        for m in range(2):
            q = blk(COL_AQ + h * 256 + m * HEAD_DIM, HEAD_DIM)
            k = blk(COL_AK + h * 256 + m * HEAD_DIM, HEAD_DIM)
            probs.append(_softmax_rows(_dot_nt(q, k) * SCALE))
        w = probs[0] - lam * probs[1]
        o = _dot(w.astype(BF16), blk(COL_AV + h * 256, 256))
        o = _rms(o, subln_ref[...]) * (1.0 - lam_init)
        o_ref[:, h * 256:(h + 1) * 256] = o.astype(o_ref.dtype)

    for mixer, (cq, ck, cv) in enumerate(((COL_BQ, COL_BK, COL_BV), (COL_CQ, COL_CK, COL_CV))):
        for kvh in range(2):
            k = blk(ck + kvh * HEAD_DIM, HEAD_DIM)
            v = blk(cv + kvh * HEAD_DIM, HEAD_DIM)
            for g in range(4):
                hq = kvh * 4 + g
                q = blk(cq + hq * HEAD_DIM, HEAD_DIM)
                sink = sink_ref[hq] if mixer == 1 else None
                p = _softmax_rows(_dot_nt(q, k) * SCALE, sink)
                o = _dot(p.astype(BF16), v)
                c0 = BRANCH_WIDTH * (1 + mixer) + hq * HEAD_DIM
                o_ref[:, c0:c0 + HEAD_DIM] = o.astype(o_ref.dtype)


def _attention_prompt(p, diff_lam_l, subln_l, sink_l, *, lam_init, seq):
    t = p.shape[0]
    return pl.pallas_call(
        functools.partial(_attn_prompt_kernel, lam_init=lam_init),
        out_shape=jax.ShapeDtypeStruct((t, 3 * BRANCH_WIDTH), BF16),
        grid=(t // seq,),
        in_specs=[
            pl.BlockSpec((seq, IN_COLS), lambda b: (b, 0)),
            pl.BlockSpec((4, HEAD_DIM), lambda b: (0, 0)),
            pl.BlockSpec((1, 256), lambda b: (0, 0)),
            pl.BlockSpec(memory_space=pltpu.SMEM),
        ],
        out_specs=pl.BlockSpec((seq, 3 * BRANCH_WIDTH), lambda b: (b, 0)),
        compiler_params=_cparams(("parallel",)),
        name="attention_prompt",
    )(p, diff_lam_l, subln_l, sink_l)


TQ = 512
TQ_FLASH = 1024
TK = 1024


def _tile_lanes(v, n):
    return jnp.concatenate([v] * n, axis=-1) if n > 1 else v


def _online_softmax(idx, s, m_scr):
    m_prev = m_scr[idx]
    m_new = jnp.maximum(m_prev, jnp.max(s, axis=-1, keepdims=True))
    m_scr[idx] = m_new
    alpha = jnp.exp2(m_prev - m_new)
    p = jnp.exp2(s - _tile_lanes(m_new, s.shape[1] // LANES))
    return p, alpha


def _diff_sample_kernel(q_ref, k_ref, v_ref, kc_ref, vc_ref, lam_ref, subln_ref, o_ref,
                        m_scr, l_scr, acc_scr, *, lam_init):
    kk = pl.program_id(2)

    @pl.when(kk == 0)
    def _():
        m_scr[...] = jnp.full(m_scr.shape, NEG_BIG, F32)
        l_scr[...] = jnp.zeros(l_scr.shape, F32)
        acc_scr[...] = jnp.zeros(acc_scr.shape, F32)

    def process(kb_ref, vb_ref):
        for h in range(4):
            v = vb_ref[:, h * 256:(h + 1) * 256].astype(BF16)
            for m in range(2):
                c0 = h * 256 + m * HEAD_DIM
                idx = h * 2 + m
                k = kb_ref[:, c0:c0 + HEAD_DIM].astype(BF16)
                p, alpha = _online_softmax(idx, _dot_nt(q_ref[:, c0:c0 + HEAD_DIM], k), m_scr)
                part = p[:, 0:LANES]
                for c in range(1, p.shape[1] // LANES):
                    part = part + p[:, c * LANES:(c + 1) * LANES]
                l_scr[idx] = alpha * l_scr[idx] + part
                acc_scr[idx] = _tile_lanes(alpha, 2) * acc_scr[idx] + _dot(p.astype(BF16), v)

    @pl.when(kk == 0)
    def _():
        process(kc_ref, vc_ref)

    @pl.when(kk > 0)
    def _():
        process(k_ref, v_ref)

    @pl.when(kk == pl.num_programs(2) - 1)
    def _():
        lam = _diff_lambda(lam_ref, lam_init)
        for h in range(4):
            l1 = jnp.sum(l_scr[2 * h], axis=-1, keepdims=True)
            l2 = jnp.sum(l_scr[2 * h + 1], axis=-1, keepdims=True)
            o1 = acc_scr[2 * h] * (1.0 / l1)
            o2 = acc_scr[2 * h + 1] * (1.0 / l2)
            o = _rms(o1 - lam * o2, subln_ref[...]) * (1.0 - lam_init)
            o_ref[:, h * 256:(h + 1) * 256] = o.astype(o_ref.dtype)


def _gqa_sample_kernel(q_ref, k_ref, v_ref, kc_ref, vc_ref, o_ref, m_scr, acc_scr):
    kk = pl.program_id(2)

    @pl.when(kk == 0)
    def _():
        m_scr[...] = jnp.full(m_scr.shape, NEG_BIG, F32)
        acc_scr[...] = jnp.zeros(acc_scr.shape, F32)

    def process(kb_ref, vb_ref):
        for kvh in range(2):
            k = kb_ref[:, kvh * HEAD_DIM:(kvh + 1) * HEAD_DIM].astype(BF16)
            v = vb_ref[:, kvh * HEAD_DIM:(kvh + 1) * HEAD_DIM].astype(BF16)
            v_ones = jnp.concatenate([v, jnp.ones_like(v)], axis=-1)
            for g in range(4):
                hq = kvh * 4 + g
                s = _dot_nt(q_ref[:, hq * HEAD_DIM:(hq + 1) * HEAD_DIM], k)
                p, alpha = _online_softmax(hq, s, m_scr)
                acc_scr[hq] = _tile_lanes(alpha, 2) * acc_scr[hq] + _dot(p.astype(BF16), v_ones)

    @pl.when(kk == 0)
    def _():
        process(kc_ref, vc_ref)

    @pl.when(kk > 0)
    def _():
        process(k_ref, v_ref)

    @pl.when(kk == pl.num_programs(2) - 1)
    def _():
        for hq in range(8):
            o = acc_scr[hq, :, 0:HEAD_DIM] / acc_scr[hq, :, HEAD_DIM:2 * HEAD_DIM]
            o_ref[:, hq * HEAD_DIM:(hq + 1) * HEAD_DIM] = o.astype(o_ref.dtype)


def _flash_sample(kernel, qkv, cache, l, *, q_col, k_col, v_col, kv_width, n_state, row_sum_scratch, extra,
                  extra_specs, name, dec_seq):
    t = qkv.shape[0]
    nb = t // dec_seq
    tq = TQ_FLASH
    nq = dec_seq // tq
    nk = dec_seq // TK
    past = cache.shape[3]
    q_blk, k_blk, v_blk = q_col // BRANCH_WIDTH, k_col // kv_width, v_col // kv_width
    kv_row = lambda b, qi, kk: b * nk + jnp.maximum(kk - 1, 0)
    return pl.pallas_call(
        kernel,
        out_shape=jax.ShapeDtypeStruct((t, BRANCH_WIDTH), BF16),
        grid=(nb, nq, nk + 1),
        in_specs=[
            pl.BlockSpec((tq, BRANCH_WIDTH), lambda b, qi, kk: (b * nq + qi, q_blk)),
            pl.BlockSpec((TK, kv_width), lambda b, qi, kk: (kv_row(b, qi, kk), k_blk)),
            pl.BlockSpec((TK, kv_width), lambda b, qi, kk: (kv_row(b, qi, kk), v_blk)),
            pl.BlockSpec((None, None, None, past, kv_width), lambda b, qi, kk: (b, l, 0, 0, 0)),
            pl.BlockSpec((None, None, None, past, kv_width), lambda b, qi, kk: (b, l, 1, 0, 0)),
        ] + extra_specs,
        out_specs=pl.BlockSpec((tq, BRANCH_WIDTH), lambda b, qi, kk: (b * nq + qi, 0)),
        scratch_shapes=[pltpu.VMEM((n_state, tq, LANES), F32)] * (2 if row_sum_scratch else 1)
        + [pltpu.VMEM((n_state, tq, 2 * LANES), F32)],
        compiler_params=_cparams(("parallel", "parallel", "arbitrary")),
        name=name,
    )(qkv, qkv, qkv, cache, cache, *extra)


def _window_sample_kernel(q0_ref, q1_ref, kp_ref, kc_ref, kn_ref, vp_ref, vc_ref, vn_ref, kctx_ref, vctx_ref,
                          sink_ref, o_ref, *, dec_seq):
    qi = pl.program_id(1)
    q_start = qi * TQ
    side = WINDOW
    q_pos = q_start + lax.broadcasted_iota(jnp.int32, (TQ, 1), 0)
    segs = ((kp_ref, vp_ref, q_start - side, side), (kc_ref, vc_ref, q_start, TQ),
            (kn_ref, vn_ref, q_start + TQ, side))
    valid = []
    for _, _, start, n in segs:
        k_pos = start + lax.broadcasted_iota(jnp.int32, (TQ, n), 1)
        valid.append((jnp.abs(q_pos - k_pos) <= WINDOW) & (k_pos >= 0) & (k_pos < dec_seq))
    for kvh in range(2):
        q_ref = q0_ref if kvh == 0 else q1_ref
        lo, hi = kvh * HEAD_DIM, (kvh + 1) * HEAD_DIM
        kctx = kctx_ref[:, lo:hi].astype(BF16)
        vctx = vctx_ref[:, lo:hi].astype(BF16)
        for g in range(4):
            hq = kvh * 4 + g
            q = q_ref[:, g * HEAD_DIM:(g + 1) * HEAD_DIM]
            sink = sink_ref[hq] * LOG2E
            scores = [_dot_nt(q, kctx)]
            for (k_ref, _, _, _), ok in zip(segs, valid):
                scores.append(jnp.where(ok, _dot_nt(q, k_ref[:, lo:hi]), NEG_BIG))
            m = jnp.maximum(scores[0].max(axis=-1, keepdims=True), sink)
            for s in scores[1:]:
                m = jnp.maximum(m, s.max(axis=-1, keepdims=True))
            es = [jnp.exp2(s - m) for s in scores]
            den = jnp.exp2(sink - m)
            for e in es:
                den = den + jnp.sum(e, axis=-1, keepdims=True)
            o = _dot(es[0].astype(BF16), vctx)
            for e, (_, v_ref, _, _) in zip(es[1:], segs):
                o = o + _dot(e.astype(BF16), v_ref[:, lo:hi])
            o = o * (1.0 / den)
            o_ref[:, hq * HEAD_DIM:(hq + 1) * HEAD_DIM] = o.astype(o_ref.dtype)


def _window_sample(qkv, cache, sink_l, l, *, dec_seq):
    t = qkv.shape[0]
    nb = t // dec_seq
    nq = dec_seq // TQ
    half = WINDOW
    n_half = dec_seq // half
    per_tile = TQ // half
    past = cache.shape[3]
    kvw = 2 * HEAD_DIM
    q_blk = COL_CQ // 512
    k_blk, v_blk = COL_CK // kvw, COL_CV // kvw
    prev_row = lambda b, qi: b * n_half + jnp.maximum(per_tile * qi - 1, 0)
    next_row = lambda b, qi: b * n_half + jnp.minimum(per_tile * (qi + 1), n_half - 1)
    return pl.pallas_call(
        functools.partial(_window_sample_kernel, dec_seq=dec_seq),
        out_shape=jax.ShapeDtypeStruct((t, BRANCH_WIDTH), BF16),
        grid=(nb, nq),
        in_specs=[
            pl.BlockSpec((TQ, 512), lambda b, qi: (b * nq + qi, q_blk)),
            pl.BlockSpec((TQ, 512), lambda b, qi: (b * nq + qi, q_blk + 1)),
            pl.BlockSpec((half, kvw), lambda b, qi: (prev_row(b, qi), k_blk)),
            pl.BlockSpec((TQ, kvw), lambda b, qi: (b * nq + qi, k_blk)),
            pl.BlockSpec((half, kvw), lambda b, qi: (next_row(b, qi), k_blk)),
            pl.BlockSpec((half, kvw), lambda b, qi: (prev_row(b, qi), v_blk)),
            pl.BlockSpec((TQ, kvw), lambda b, qi: (b * nq + qi, v_blk)),
            pl.BlockSpec((half, kvw), lambda b, qi: (next_row(b, qi), v_blk)),
            pl.BlockSpec((None, None, None, past, kvw), lambda b, qi: (b, l, 0, 0, 0)),
            pl.BlockSpec((None, None, None, past, kvw), lambda b, qi: (b, l, 1, 0, 0)),
            pl.BlockSpec(memory_space=pltpu.SMEM),
        ],
        out_specs=pl.BlockSpec((TQ, BRANCH_WIDTH), lambda b, qi: (b * nq + qi, 0)),
        compiler_params=_cparams(("parallel", "parallel")),
        name="window_attention",
    )(qkv, qkv, qkv, qkv, qkv, qkv, qkv, qkv, cache, cache, sink_l)


def _merge_kernel(oa_ref, ob_ref, oc_ref, g_ref, w_ref, y_ref, acc_scr):
    r = pl.program_id(2)

    def contrib(o_ref):
        return g_ref[...].astype(F32) * _dot(o_ref[...], w_ref[...])

    @pl.when(r == 0)
    def _():
        acc_scr[...] = contrib(oa_ref)

    @pl.when(r == 1)
    def _():
        acc_scr[...] += contrib(ob_ref)

    @pl.when(r == 2)
    def _():
        y_ref[...] = (acc_scr[...] + contrib(oc_ref)).astype(y_ref.dtype)


def _merge_branches(o_arrays, o_blocks, gates, w_branch, l):
    t = gates.shape[0]
    tn = D_MODEL
    nn = D_MODEL // tn
    o_specs = [pl.BlockSpec((TM, BRANCH_WIDTH), functools.partial(lambda i, n, r, blk: (i, blk), blk=blk))
               for blk in o_blocks]
    return pl.pallas_call(
        _merge_kernel,
        out_shape=jax.ShapeDtypeStruct((t, D_MODEL), BF16),
        grid=(t // TM, nn, 3),
        in_specs=o_specs + [
            pl.BlockSpec((TM, tn), lambda i, n, r: (i, r * nn + n)),
            pl.BlockSpec((None, None, BRANCH_WIDTH, tn), lambda i, n, r: (l, r, 0, n)),
        ],
        out_specs=pl.BlockSpec((TM, tn), lambda i, n, r: (i, n)),
        scratch_shapes=[pltpu.VMEM((TM, tn), F32)],
        compiler_params=_cparams(("parallel", "parallel", "arbitrary")),
        name="merge_branches",
    )(*o_arrays, gates, w_branch)


def _layer_norm_rows(v, gain, bias):
    return _norm_rows(v) * gain + bias


def _outproj_kernel(y_ref, w_ref, x_ref, g_ref, gain_ref, bias_ref, o_ref):
    n = pl.program_id(1)
    tn = w_ref.shape[1]
    n_blocks = o_ref.shape[1] // tn
    z = _dot(y_ref[...], w_ref[...])
    for b in range(n_blocks):
        @pl.when(n == b)
        def _(b=b):
            cols = slice(b * tn, (b + 1) * tn)
            o_ref[:, cols] = ALPHA * x_ref[:, cols] + g_ref[:, cols] * z

    @pl.when(n == n_blocks - 1)
    def _():
        o_ref[...] = _layer_norm_rows(o_ref[...], gain_ref[...], bias_ref[...])


def _out_projection(y, w_o, l, x, mod_l, ln_gain_l, ln_bias_l, *, row_of_tile):
    t = x.shape[0]
    tn = 512
    return pl.pallas_call(
        _outproj_kernel,
        out_shape=jax.ShapeDtypeStruct((t, D_MODEL), F32),
        grid=(t // TM, D_MODEL // tn),
        in_specs=[
            pl.BlockSpec((TM, D_MODEL), lambda i, n: (i, 0)),
            pl.BlockSpec((None, D_MODEL, tn), lambda i, n: (l, 0, n)),
            pl.BlockSpec((TM, D_MODEL), lambda i, n: (i, 0)),
            _mod_spec(2, row_of_tile),
            pl.BlockSpec((1, D_MODEL), lambda i, n: (0, 0)),
            pl.BlockSpec((1, D_MODEL), lambda i, n: (0, 0)),
        ],
        out_specs=pl.BlockSpec((TM, D_MODEL), lambda i, n: (i, 0)),
        compiler_params=_cparams(("parallel", "arbitrary")),
        name="out_projection",
    )(y, w_o, x, mod_l, ln_gain_l, ln_bias_l)


def _route(p):
    rows = [p[e:e + 1, :] for e in range(N_EXPERTS)]
    best_score, best_group = None, None
    for g in range(N_GROUPS):
        members = rows[g * EXPERTS_PER_GROUP:(g + 1) * EXPERTS_PER_GROUP]
        score = None
        for a in range(EXPERTS_PER_GROUP):
            for b in range(a + 1, EXPERTS_PER_GROUP):
                pair = members[a] + members[b]
                score = pair if score is None else jnp.maximum(score, pair)
        if g == 0:
            best_score, best_group = score, jnp.zeros(score.shape, F32)
        else:
            better = score > best_score
            best_group = jnp.where(better, float(g), best_group)
            best_score = jnp.where(better, score, best_score)
    e_idx = lax.broadcasted_iota(jnp.int32, p.shape, 0).astype(F32)
    g_idx = jnp.floor(e_idx * (1.0 / EXPERTS_PER_GROUP))
    masked = jnp.where(g_idx == best_group, p, -1.0)
    w1 = jnp.max(masked, axis=0, keepdims=True)
    i1 = jnp.min(jnp.where(masked == w1, e_idx, float(N_EXPERTS)), axis=0, keepdims=True)
    masked2 = jnp.where(e_idx == i1, -2.0, masked)
    w2 = jnp.max(masked2, axis=0, keepdims=True)
    i2 = jnp.min(jnp.where(masked2 == w2, e_idx, float(N_EXPERTS)), axis=0, keepdims=True)
    tot = w1 + w2
    return e_idx, i1, i2, w1 / tot, w2 / tot


ROUTE_ROWS = 8
D_PACKED = D_MODEL // 2


def _pack_bf16_pairs(xb):
    half = xb.shape[1] // 2
    lo = lax.bitcast_convert_type(xb[:, :half].astype(F32), jnp.uint32)
    hi = lax.bitcast_convert_type(xb[:, half:].astype(F32), jnp.uint32)
    return (hi & jnp.uint32(0xFFFF0000)) | (lo >> 16)


def _unpack_bf16_pairs(words):
    lo = lax.bitcast_convert_type(words << 16, F32).astype(BF16)
    hi = lax.bitcast_convert_type(words & jnp.uint32(0xFFFF0000), F32).astype(BF16)
    return lo, hi


def _moe_route_kernel(xa_ref, xb_ref, sh_ref, sc_ref, wr_ref, h_ref, rec_ref, rec_t_ref, cnt_ref, carry_scr,
                      *, tiles_a):
    @pl.when(pl.program_id(0) == 0)
    def _():
        carry_scr[...] = jnp.zeros(carry_scr.shape, F32)

    x = jnp.where(pl.program_id(0) < tiles_a, xa_ref[...], xb_ref[...])
    h = _norm_rows(x) * (1.0 + sc_ref[...]) + sh_ref[...]
    hb = h.astype(BF16)
    h_ref[...] = _pack_bf16_pairs(hb)
    logits = _dot_nt(wr_ref[...].astype(BF16), hb)
    m = jnp.max(logits, axis=0, keepdims=True)
    e = jnp.exp(logits - m)
    probs = e / jnp.sum(e, axis=0, keepdims=True)
    e_idx, i1, i2, w1, w2 = _route(probs)
    tm = probs.shape[1]
    oh1 = (e_idx == i1).astype(F32)
    oh2 = (e_idx == i2).astype(F32)
    oh = oh1 + oh2
    earlier = (lax.broadcasted_iota(jnp.int32, (tm, tm), 0) < lax.broadcasted_iota(jnp.int32, (tm, tm), 1))
    rank = carry_scr[:, 0:1] + _dot(oh.astype(BF16), earlier.astype(BF16))
    r1 = jnp.sum(oh1 * rank, axis=0, keepdims=True)
    r2 = jnp.sum(oh2 * rank, axis=0, keepdims=True)
    carry_scr[...] = carry_scr[...] + jnp.sum(oh, axis=1, keepdims=True)
    cnt_ref[...] = carry_scr[...]
    row = lax.broadcasted_iota(jnp.int32, (ROUTE_ROWS, tm), 0)
    rec = jnp.zeros((ROUTE_ROWS, tm), F32)
    for k, v in enumerate((i1, i2, r1, r2, w1, w2)):
        rec = jnp.where(row == k, v, rec)
    rec_ref[...] = rec
    pad = jnp.zeros((LANES - ROUTE_ROWS, tm), F32)
    rec_t_ref[...] = jnp.concatenate([rec, pad], axis=0).T


def _moe_route(x_a, x_b, mod_l, w_router_t, *, rows_a, rows_b):
    tm = TM_SMALL
    tiles_a = x_a.shape[0] // tm
    t = x_a.shape[0] + x_b.shape[0]
    row_a, row_b = rows_a(tm), rows_b(tm)
    row_of_tile = lambda i: jnp.where(i < tiles_a, row_a(i), row_b(i - tiles_a))
    return pl.pallas_call(
        functools.partial(_moe_route_kernel, tiles_a=tiles_a),
        out_shape=(jax.ShapeDtypeStruct((t, D_PACKED), jnp.uint32), jax.ShapeDtypeStruct((ROUTE_ROWS, t), F32),
                   jax.ShapeDtypeStruct((t, LANES), F32), jax.ShapeDtypeStruct((N_EXPERTS, LANES), F32)),
        grid=(t // tm,),
        in_specs=[
            pl.BlockSpec((tm, D_MODEL), lambda i: (jnp.minimum(i, tiles_a - 1), 0)),
            pl.BlockSpec((tm, D_MODEL), lambda i: (jnp.maximum(i - tiles_a, 0), 0)),
            _mod_spec(3, row_of_tile),
            _mod_spec(4, row_of_tile),
            pl.BlockSpec((N_EXPERTS, D_MODEL), lambda i: (0, 0)),
        ],
        out_specs=(pl.BlockSpec((tm, D_PACKED), lambda i: (i, 0)),
                   pl.BlockSpec((ROUTE_ROWS, tm), lambda i: (0, i)),
                   pl.BlockSpec((tm, LANES), lambda i: (i, 0)),
                   pl.BlockSpec((N_EXPERTS, LANES), lambda i: (0, 0))),
        scratch_shapes=[pltpu.VMEM((N_EXPERTS, LANES), F32)],
        compiler_params=_cparams(("arbitrary",)),
        name="moe_route",
    )(x_a, x_b, mod_l, mod_l, w_router_t)


ROW_DMA_UNROLL = 8
GATHER_ORDER_STRIDE = 37


def _row_copy(src_hbm, src_row, dst_buf, dst_row, sem):
    return pltpu.make_async_copy(src_hbm.at[pl.ds(src_row, 1)], dst_buf.at[pl.ds(dst_row, 1)], sem)


def _moe_expert_kernel(tile_expert_ref, n_tiles_ref, src_ref, h_hbm, wg_ref, wu_ref, wd_ref, ys_ref,
                       x_buf, sems, *, te):
    del tile_expert_ref
    j = pl.program_id(0)
    n_valid = n_tiles_ref[0]
    slot = j % 2

    def start_gather(tile, s):
        def body(r, carry):
            row = (r * GATHER_ORDER_STRIDE) % te
            _row_copy(h_hbm, src_ref[tile * te + row], x_buf.at[s], row, sems.at[s]).start()
            return carry
        lax.fori_loop(0, te, body, 0, unroll=ROW_DMA_UNROLL)

    def wait_gather(s):
        def body(r, carry):
            _row_copy(h_hbm, 0, x_buf.at[s], r, sems.at[s]).wait()
            return carry
        lax.fori_loop(0, te, body, 0, unroll=ROW_DMA_UNROLL)

    @pl.when((j == 0) & (n_valid > 0))
    def _():
        start_gather(0, 0)

    @pl.when(j + 1 < n_valid)
    def _():
        start_gather(j + 1, 1 - slot)

    @pl.when(j < n_valid)
    def _():
        wait_gather(slot)
        x_lo, x_hi = _unpack_bf16_pairs(x_buf[slot])

        def project(w_ref):
            return (_dot(x_lo, w_ref[:D_PACKED, :].astype(BF16)) + _dot(x_hi, w_ref[D_PACKED:, :].astype(BF16)))

        gate = project(wg_ref)
        up = project(wu_ref)
        hid = gate / (1.0 + jnp.exp(-gate)) * up
        ys_ref[...] = _pack_bf16_pairs(_dot(hid.astype(BF16), wd_ref[...].astype(BF16)).astype(BF16))

    @pl.when(j >= n_valid)
    def _():
        ys_ref[...] = jnp.zeros(ys_ref.shape, ys_ref.dtype)


def _moe_experts(h, tile_expert, n_tiles, src, wg, wu, wd, l, *, te):
    n_rows = src.shape[0]
    w_in_spec = pl.BlockSpec((None, None, D_MODEL, D_EXPERT), lambda j, te_ref, *_: (l, te_ref[j], 0, 0))
    w_out_spec = pl.BlockSpec((None, None, D_EXPERT, D_MODEL), lambda j, te_ref, *_: (l, te_ref[j], 0, 0))
    return pl.pallas_call(
        functools.partial(_moe_expert_kernel, te=te),
        out_shape=jax.ShapeDtypeStruct((n_rows, D_PACKED), jnp.uint32),
        grid_spec=pltpu.PrefetchScalarGridSpec(
            num_scalar_prefetch=3,
            grid=(n_rows // te,),
            in_specs=[pl.BlockSpec(memory_space=pltpu.HBM), w_in_spec, w_in_spec, w_out_spec],
            out_specs=pl.BlockSpec((te, D_PACKED), lambda j, *_: (j, 0)),
            scratch_shapes=[pltpu.VMEM((2, te, D_PACKED), h.dtype), pltpu.SemaphoreType.DMA((2,))],
        ),
        compiler_params=_cparams(("arbitrary",)),
        name="moe_experts",
    )(tile_expert, n_tiles, src, h, wg, wu, wd)


TM_COMBINE = 256


def _moe_combine_kernel(dest_ref, ys_hbm, x_ref, rec_t_ref, g_ref, gain_ref, bias_ref, o_ref, y_buf, sems,
                        *, n_tokens, token_offset):
    i = pl.program_id(0)
    tm = x_ref.shape[0]
    slot = i % 2

    def start_gather(tile, s):
        def body(r, carry):
            for k in range(2):
                row = dest_ref[k * n_tokens + token_offset + tile * tm + r]
                _row_copy(ys_hbm, row, y_buf.at[s, k], r, sems.at[s]).start()
            return carry
        lax.fori_loop(0, tm, body, 0, unroll=ROW_DMA_UNROLL)

    def wait_gather(s):
        def body(r, carry):
            for k in range(2):
                _row_copy(ys_hbm, 0, y_buf.at[s, k], r, sems.at[s]).wait()
            return carry
        lax.fori_loop(0, tm, body, 0, unroll=ROW_DMA_UNROLL)

    @pl.when(i == 0)
    def _():
        start_gather(0, 0)

    @pl.when(i + 1 < pl.num_programs(0))
    def _():
        start_gather(i + 1, 1 - slot)

    wait_gather(slot)
    w1, w2 = rec_t_ref[:, 4:5], rec_t_ref[:, 5:6]
    y1_lo, y1_hi = _unpack_bf16_pairs(y_buf[slot, 0])
    y2_lo, y2_hi = _unpack_bf16_pairs(y_buf[slot, 1])
    for half, (y1, y2) in enumerate(((y1_lo, y2_lo), (y1_hi, y2_hi))):
        cols = slice(half * D_PACKED, (half + 1) * D_PACKED)
        z = w1 * y1.astype(F32) + w2 * y2.astype(F32)
        o_ref[:, cols] = ALPHA * x_ref[:, cols] + g_ref[:, cols] * z
    o_ref[...] = _layer_norm_rows(o_ref[...], gain_ref[...], bias_ref[...])


def _moe_combine(dest, ys, x, rec_t, mod_l, ln_gain_l, ln_bias_l, *, row_of_tile, token_offset):
    t = x.shape[0]
    tm = TM_COMBINE
    first_tile = token_offset // tm
    vec = pl.BlockSpec((1, D_MODEL), lambda i, *_: (0, 0))
    return pl.pallas_call(
        functools.partial(_moe_combine_kernel, n_tokens=rec_t.shape[0], token_offset=token_offset),
        out_shape=jax.ShapeDtypeStruct((t, D_MODEL), F32),
        grid_spec=pltpu.PrefetchScalarGridSpec(
            num_scalar_prefetch=1,
            grid=(t // tm,),
            in_specs=[
                pl.BlockSpec(memory_space=pltpu.HBM),
                pl.BlockSpec((tm, D_MODEL), lambda i, *_: (i, 0)),
                pl.BlockSpec((tm, LANES), lambda i, *_: (first_tile + i, 0)),
                _mod_spec(5, row_of_tile),
                vec, vec,
            ],
            out_specs=pl.BlockSpec((tm, D_MODEL), lambda i, *_: (i, 0)),
            scratch_shapes=[pltpu.VMEM((2, 2, tm, D_PACKED), jnp.uint32), pltpu.SemaphoreType.DMA((2,))],
        ),
        compiler_params=_cparams(("arbitrary",)),
        name="moe_combine",
    )(dest, ys, x, rec_t, mod_l, ln_gain_l, ln_bias_l)


def _dispatch_tables(rec, cnt, *, te):
    t = rec.shape[1]
    n_rows = 2 * t + N_EXPERTS * te
    e12 = rec[0:2].astype(jnp.int32)
    r12 = rec[2:4].astype(jnp.int32)
    counts = cnt[:, 0].astype(jnp.int32)
    padded = (counts + te - 1) // te * te
    ends = jnp.cumsum(padded)
    offsets = ends - padded
    expert_ids = jnp.arange(N_EXPERTS, dtype=jnp.int32)[:, None, None]
    dest = (jnp.sum(jnp.where(e12[None] == expert_ids, offsets[:, None, None], 0), axis=0) + r12).reshape(2 * t)
    tokens = jnp.tile(jnp.arange(t, dtype=jnp.int32), 2)
    filler = jnp.arange(n_rows, dtype=jnp.int32) % t
    src = filler.at[dest].set(tokens, unique_indices=True)
    tile_start = jnp.arange(n_rows // te, dtype=jnp.int32) * te
    tile_expert = jnp.minimum(jnp.searchsorted(ends, tile_start, side="right"), N_EXPERTS - 1).astype(jnp.int32)
    n_tiles = (ends[-1:] // te).astype(jnp.int32)
    return dest, src, tile_expert, n_tiles


def _rope_tables(n_tokens):
    rows = n_tokens // GRID_W
    row = jnp.repeat(jnp.arange(rows), GRID_W).astype(F32)
    col = jnp.tile(jnp.arange(GRID_W), rows).astype(F32)
    quarter = HEAD_DIM // 4
    inv_freq = ROPE_THETA ** (-jnp.arange(quarter, dtype=F32) / quarter)
    ang_r, ang_c = row[:, None] * inv_freq, col[:, None] * inv_freq
    cos = jnp.concatenate([jnp.cos(ang_r), jnp.cos(ang_r), jnp.cos(ang_c), jnp.cos(ang_c)], axis=-1)
    zero = jnp.zeros_like(ang_r)
    sin = jnp.concatenate([-jnp.sin(ang_r), zero, -jnp.sin(ang_c), zero,
                           zero, jnp.sin(ang_r), zero, jnp.sin(ang_c)], axis=-1)
    return cos, sin


def _row_of_tile(first_row, tokens_per_row):
    def for_tile(tile):
        return lambda i: first_row + (i * tile) // tokens_per_row
    return for_tile


def _mixer_output(x, attn_arrays, attn_blocks, gates, mod_l, rows, l, w_branch, w_o, ln_gain, ln_bias):
    y = _merge_branches(attn_arrays, attn_blocks, gates, w_branch, l)
    return _out_projection(y, w_o, l, x, mod_l, ln_gain[l, 0:1], ln_bias[l, 0:1], row_of_tile=rows(TM))


EXPERT_TILE = 512


def _moe_ffn(x_a, x_b, mod_l, rows_a, rows_b, l, ln_gain, ln_bias, w_router_t, w_e_gate, w_e_up, w_e_down):
    h2, rec, rec_t, cnt = _moe_route(x_a, x_b, mod_l, w_router_t, rows_a=rows_a, rows_b=rows_b)
    dest, src, tile_expert, n_tiles = _dispatch_tables(rec, cnt, te=EXPERT_TILE)
    ys = _moe_experts(h2, tile_expert, n_tiles, src, w_e_gate, w_e_up, w_e_down, l, te=EXPERT_TILE)
    outs = []
    for x, rows, offset in ((x_a, rows_a, 0), (x_b, rows_b, x_a.shape[0])):
        outs.append(_moe_combine(dest, ys, x, rec_t, mod_l, ln_gain[l, 1:2], ln_bias[l, 1:2],
                                 row_of_tile=rows(TM_COMBINE), token_offset=offset))
    return outs


def kernel(x_prompt, x_sample, cache_kv_a, cache_kv_b, cache_kv_c, c, c_ctx, w_in, w_gate, w_branch, w_o,
           w_mod, b_mod, ln_gain, ln_bias, diff_lam, diff_subln, qk_gain, sink, w_router, w_e_gate, w_e_up,
           w_e_down):
    batch, seq, _ = x_prompt.shape
    dec_batch, dec_seq, _ = x_sample.shape
    past = cache_kv_a.shape[3]
    t_p, t_s = batch * seq, dec_batch * dec_seq

    cond = jnp.concatenate([c_ctx[None], c, jnp.zeros((MOD_ROWS - 1 - dec_batch, D_MODEL), F32)], axis=0)
    mod = _modulation(cond.T, w_mod, b_mod)
    cos_t, sin_t = _rope_tables(dec_seq)
    w_router_t = w_router.T
    w_in, w_gate, w_branch, w_o = (w.astype(BF16) for w in (w_in, w_gate, w_branch, w_o))
    cache_a = cache_kv_a.reshape(dec_batch, DEPTH, 2, past, 4 * 256)
    cache_b = cache_kv_b.reshape(dec_batch, DEPTH, 2, past, 2 * HEAD_DIM)
    cache_c = cache_kv_c.reshape(dec_batch, DEPTH, 2, past, 2 * HEAD_DIM)
    rows_p = _row_of_tile(0, t_p)
    rows_s = _row_of_tile(1, dec_seq)

    y_p = x_prompt.reshape(t_p, D_MODEL)
    y_s = x_sample.reshape(t_s, D_MODEL)
    projections = []
    for l in range(DEPTH):
        lam_init = 0.8 - 0.6 * math.exp(-0.3 * l)
        mod_l = mod[l].reshape(MOD_ROWS, 1, N_MOD * D_MODEL)
        subln_l = diff_subln[l].reshape(1, 256)
        shared = (w_branch, w_o, ln_gain, ln_bias)

        p = _in_projection(y_p, mod_l, w_in, l, qk_gain[l], cos_t, sin_t, row_of_tile=rows_p(TM),
                           rope=False, q_scale=1.0, out_dtype=F32)
        gates = _branch_gates(y_p, mod_l, w_gate, l, row_of_tile=rows_p(TM))
        attn = _attention_prompt(p, diff_lam[l], subln_l, sink[l], lam_init=lam_init, seq=seq)
        y_p = _mixer_output(y_p, (attn, attn, attn), (0, 1, 2), gates, mod_l, rows_p, l, *shared)
        projections.append(p.reshape(batch, seq, IN_COLS))

        qkv = _in_projection(y_s, mod_l, w_in, l, qk_gain[l], cos_t, sin_t, row_of_tile=rows_s(TM),
                             rope=True, q_scale=SCALE * LOG2E, out_dtype=BF16)
        gates = _branch_gates(y_s, mod_l, w_gate, l, row_of_tile=rows_s(TM))
        vec = lambda shape: pl.BlockSpec(shape, lambda b, qi, kk: (0, 0))
        a_o = _flash_sample(functools.partial(_diff_sample_kernel, lam_init=lam_init), qkv, cache_a, l,
                            q_col=COL_AQ, k_col=COL_AK, v_col=COL_AV, kv_width=1024, n_state=8,
                            row_sum_scratch=True, extra=(diff_lam[l], subln_l),
                            extra_specs=[vec((4, HEAD_DIM)), vec((1, 256))], name="diff_attention",
                            dec_seq=dec_seq)
        b_o = _flash_sample(_gqa_sample_kernel, qkv, cache_b, l, q_col=COL_BQ, k_col=COL_BK, v_col=COL_BV,
                            kv_width=256, n_state=8, row_sum_scratch=False, extra=(), extra_specs=[],
                            name="gqa_attention", dec_seq=dec_seq)
        c_o = _window_sample(qkv, cache_c, sink[l], l, dec_seq=dec_seq)
        y_s = _mixer_output(y_s, (a_o, b_o, c_o), (0, 0, 0), gates, mod_l, rows_s, l, *shared)

        y_p, y_s = _moe_ffn(y_p, y_s, mod_l, rows_p, rows_s, l, ln_gain, ln_bias, w_router_t, w_e_gate, w_e_up,
                            w_e_down)

    def new_cache(col_k, col_v, col_end, heads):
        parts = [p[..., c0:c1] for p in projections for c0, c1 in ((col_k, col_v), (col_v, col_end))]
        return jnp.stack(parts, axis=1).reshape(batch, DEPTH, 2, seq, heads, (col_v - col_k) // heads)

    new_kv_a = new_cache(COL_AK, COL_AV, COL_BQ, 4)
    new_kv_b = new_cache(COL_BK, COL_BV, COL_CQ, 2)
    new_kv_c = new_cache(COL_CK, COL_CV, IN_COLS, 2)
    return (y_p.reshape(batch, seq, D_MODEL), y_s.reshape(dec_batch, dec_seq, D_MODEL),
            new_kv_a, new_kv_b, new_kv_c)
```

```python
import functools
import math

import jax
import jax.numpy as jnp
from jax import lax
from jax.experimental import pallas as pl
from jax.experimental.pallas import tpu as pltpu

F32 = jnp.float32
BF16 = jnp.bfloat16

D_MODEL = 2048
HEAD_DIM = 128
GRID_W = 64
ROPE_THETA = 10000.0
WINDOW = 128
N_EXPERTS = 16
N_GROUPS = 4
EXPERTS_PER_GROUP = N_EXPERTS // N_GROUPS
D_EXPERT = 512
N_MOD = 6
DEPTH = 2
ALPHA = (2 * DEPTH) ** 0.25
EPS = 1e-6
IN_COLS = 6144
BRANCH_WIDTH = 1024
SCALE = HEAD_DIM ** -0.5
LOG2E = math.log2(math.e)
NEG_BIG = -1e30

COL_AQ, COL_AK, COL_AV = 0, 1024, 2048
COL_BQ, COL_BK, COL_BV = 3072, 4096, 4352
COL_CQ, COL_CK, COL_CV = 4608, 5632, 5888

LANES = 128
VMEM_LIMIT = 56 * 1024 * 1024

TM = 1024
TN_PROJ = 1024
TN_PROJ_F32 = 1024
TN_GATE = 1024
TM_SMALL = 512


def _cparams(sem):
    return pltpu.CompilerParams(dimension_semantics=sem, vmem_limit_bytes=VMEM_LIMIT)


def _dot(a, b):
    return jnp.dot(a, b, preferred_element_type=F32)


def _dot_nt(a, b):
    return lax.dot_general(a, b, (((1,), (1,)), ((), ())), preferred_element_type=F32)


def _norm_rows(x):
    mu = jnp.mean(x, axis=-1, keepdims=True)
    xc = x - mu
    var = jnp.mean(xc * xc, axis=-1, keepdims=True)
    return xc * lax.rsqrt(var + EPS)


def _rms(v, gain):
    ms = jnp.mean(v * v, axis=-1, keepdims=True)
    return v * lax.rsqrt(ms + EPS) * gain


def _rope(v, cos, sin_from_upper, sin_from_lower):
    return v * cos + pltpu.roll(v, 96, 1) * sin_from_upper + pltpu.roll(v, 32, 1) * sin_from_lower


N_COND = 3
MOD_ROWS = 8
TN_MOD = 1024


def _mod_kernel(cond_ref, w_ref, b_ref, o_ref):
    w = w_ref[...]
    row_idx = lax.broadcasted_iota(jnp.int32, (MOD_ROWS, TN_MOD), 0)
    out = jnp.zeros((MOD_ROWS, TN_MOD), F32)
    for r in range(N_COND):
        c = cond_ref[:, r:r + 1]
        s = c / (1.0 + jnp.exp(-c))
        m = jnp.sum(w * s, axis=0, keepdims=True) + b_ref[...]
        out = jnp.where(row_idx == r, m, out)
    o_ref[...] = out


def _modulation(cond_t, w_mod, b_mod):
    n = N_MOD * D_MODEL
    return pl.pallas_call(
        _mod_kernel,
        out_shape=jax.ShapeDtypeStruct((DEPTH, MOD_ROWS, n), F32),
        grid=(DEPTH, n // TN_MOD),
        in_specs=[
            pl.BlockSpec((D_MODEL, MOD_ROWS), lambda l, j: (0, 0)),
            pl.BlockSpec((None, D_MODEL, TN_MOD), lambda l, j: (l, 0, j)),
            pl.BlockSpec((None, 1, TN_MOD), lambda l, j: (l, 0, j)),
        ],
        out_specs=pl.BlockSpec((None, MOD_ROWS, TN_MOD), lambda l, j: (l, 0, j)),
        compiler_params=_cparams(("parallel", "parallel")),
        name="modulation",
    )(cond_t, w_mod, b_mod.reshape(DEPTH, 1, n))


def _modulate_to_scratch(x_ref, sh_ref, sc_ref, h_scr):
    h = _norm_rows(x_ref[...]) * (1.0 + sc_ref[...]) + sh_ref[...]
    h_scr[...] = h.astype(BF16)


def _head_chunk_kinds():
    kinds = []
    for n_chunks, kind in ((8, (None, True, True)), (8, (None, True, False)), (8, (None, False, False)),
                           (8, ("q", True, True)), (2, ("k", True, False)), (2, (None, False, False)),
                           (8, (None, True, True)), (2, (None, True, False)), (2, (None, False, False))):
        kinds.extend([kind] * n_chunks)
    return kinds


_HEAD_CHUNK_KINDS = _head_chunk_kinds()


def _inproj_kernel(x_ref, sh_ref, sc_ref, w_ref, qk_gain_ref, cos_ref, sin_ref, o_ref, h_scr, *, rope,
                   q_scale):
    j = pl.program_id(1)
    gain_q = qk_gain_ref[0:1, :]
    gain_k = qk_gain_ref[1:2, :]

    @pl.when(j == 0)
    def _():
        _modulate_to_scratch(x_ref, sh_ref, sc_ref, h_scr)

    acc = _dot(h_scr[...], w_ref[...])
    tn = o_ref.shape[1]
    n_chunks = tn // LANES

    def chunk(c):
        return acc[:, c * LANES:(c + 1) * LANES]

    def store(c, v):
        o_ref[:, c * LANES:(c + 1) * LANES] = v.astype(o_ref.dtype)

    def rp(v):
        return _rope(v, cos_ref[...], sin_ref[:, :HEAD_DIM], sin_ref[:, HEAD_DIM:]) if rope else v

    def qs(v):
        return v * q_scale if q_scale != 1.0 else v

    def finish(c, kind):
        norm, rotate, is_query = kind
        v = chunk(c)
        if norm is not None:
            v = _rms(v, gain_q if norm == "q" else gain_k)
        if rotate:
            v = rp(v)
        store(c, qs(v) if is_query else v)

    tile_kinds = [tuple(_HEAD_CHUNK_KINDS[t * n_chunks:(t + 1) * n_chunks]) for t in range(IN_COLS // tn)]
    for kinds in dict.fromkeys(tile_kinds):
        tiles = [t for t, k in enumerate(tile_kinds) if k == kinds]
        cond = j == tiles[0]
        for t in tiles[1:]:
            cond = cond | (j == t)

        @pl.when(cond)
        def _(kinds=kinds):
            for c, kind in enumerate(kinds):
                finish(c, kind)


def _gate_kernel(x_ref, sh_ref, sc_ref, w_ref, o_ref, h_scr):
    @pl.when(pl.program_id(1) == 0)
    def _():
        _modulate_to_scratch(x_ref, sh_ref, sc_ref, h_scr)

    acc = _dot(h_scr[...], w_ref[...])
    o_ref[...] = (1.0 / (1.0 + jnp.exp(-acc))).astype(o_ref.dtype)


def _mod_spec(which, row_of_tile):
    return pl.BlockSpec((None, 1, D_MODEL), lambda i, *_: (row_of_tile(i), 0, which))


def _in_projection(x, mod_l, w_in, l, qk_gain_l, cos_t, sin_t, *, row_of_tile, rope, q_scale, out_dtype):
    t = x.shape[0]
    tiles_per_seq = cos_t.shape[0] // TM
    tn = TN_PROJ_F32 if out_dtype == F32 else TN_PROJ
    return pl.pallas_call(
        functools.partial(_inproj_kernel, rope=rope, q_scale=q_scale),
        out_shape=jax.ShapeDtypeStruct((t, IN_COLS), out_dtype),
        grid=(t // TM, IN_COLS // tn),
        in_specs=[
            pl.BlockSpec((TM, D_MODEL), lambda i, j: (i, 0)),
            _mod_spec(0, row_of_tile),
            _mod_spec(1, row_of_tile),
            pl.BlockSpec((None, D_MODEL, tn), lambda i, j: (l, 0, j)),
            pl.BlockSpec((2, HEAD_DIM), lambda i, j: (0, 0)),
            pl.BlockSpec((TM, HEAD_DIM), lambda i, j: (i % tiles_per_seq, 0)),
            pl.BlockSpec((TM, 2 * HEAD_DIM), lambda i, j: (i % tiles_per_seq, 0)),
        ],
        out_specs=pl.BlockSpec((TM, tn), lambda i, j: (i, j)),
        scratch_shapes=[pltpu.VMEM((TM, D_MODEL), BF16)],
        compiler_params=_cparams(("parallel", "arbitrary")),
        name="in_projection",
    )(x, mod_l, mod_l, w_in, qk_gain_l, cos_t, sin_t)


def _branch_gates(x, mod_l, w_gate, l, *, row_of_tile):
    t = x.shape[0]
    n = w_gate.shape[2]
    return pl.pallas_call(
        _gate_kernel,
        out_shape=jax.ShapeDtypeStruct((t, n), BF16),
        grid=(t // TM, n // TN_GATE),
        in_specs=[
            pl.BlockSpec((TM, D_MODEL), lambda i, j: (i, 0)),
            _mod_spec(0, row_of_tile),
            _mod_spec(1, row_of_tile),
            pl.BlockSpec((None, D_MODEL, TN_GATE), lambda i, j: (l, 0, j)),
        ],
        out_specs=pl.BlockSpec((TM, TN_GATE), lambda i, j: (i, j)),
        scratch_shapes=[pltpu.VMEM((TM, D_MODEL), BF16)],
        compiler_params=_cparams(("parallel", "arbitrary")),
        name="branch_gates",
    )(x, mod_l, mod_l, w_gate)


def _diff_lambda(lam_ref, lam_init):
    lp = lam_ref[...]
    t1 = jnp.sum(lp[0:1] * lp[1:2], axis=-1, keepdims=True)
    t2 = jnp.sum(lp[2:3] * lp[3:4], axis=-1, keepdims=True)
    return jnp.exp(t1) - jnp.exp(t2) + lam_init


def _softmax_rows(s, sink=None):
    m = jnp.max(s, axis=-1, keepdims=True)
    if sink is not None:
        m = jnp.maximum(m, sink)
    e = jnp.exp(s - m)
    den = jnp.sum(e, axis=-1, keepdims=True)
    if sink is not None:
        den = den + jnp.exp(sink - m)
    return e * (1.0 / den)


def _attn_prompt_kernel(p_ref, lam_ref, subln_ref, sink_ref, o_ref, *, lam_init):
    lam = _diff_lambda(lam_ref, lam_init)

    def blk(c0, w):
        return p_ref[:, c0:c0 + w].astype(BF16)

    for h in range(4):
        probs = []
        for m in range(2):
            q = blk(COL_AQ + h * 256 + m * HEAD_DIM, HEAD_DIM)
            k = blk(COL_AK + h * 256 + m * HEAD_DIM, HEAD_DIM)
            probs.append(_softmax_rows(_dot_nt(q, k) * SCALE))
        w = probs[0] - lam * probs[1]
        o = _dot(w.astype(BF16), blk(COL_AV + h * 256, 256))
        o = _rms(o, subln_ref[...]) * (1.0 - lam_init)
        o_ref[:, h * 256:(h + 1) * 256] = o.astype(o_ref.dtype)

    for mixer, (cq, ck, cv) in enumerate(((COL_BQ, COL_BK, COL_BV), (COL_CQ, COL_CK, COL_CV))):
        for kvh in range(2):
            k = blk(ck + kvh * HEAD_DIM, HEAD_DIM)
            v = blk(cv + kvh * HEAD_DIM, HEAD_DIM)
            for g in range(4):
                hq = kvh * 4 + g
                q = blk(cq + hq * HEAD_DIM, HEAD_DIM)
                sink = sink_ref[hq] if mixer == 1 else None
                p = _softmax_rows(_dot_nt(q, k) * SCALE, sink)
                o = _dot(p.astype(BF16), v)
                c0 = BRANCH_WIDTH * (1 + mixer) + hq * HEAD_DIM
                o_ref[:, c0:c0 + HEAD_DIM] = o.astype(o_ref.dtype)


def _attention_prompt(p, diff_lam_l, subln_l, sink_l, *, lam_init, seq):
    t = p.shape[0]
    return pl.pallas_call(
        functools.partial(_attn_prompt_kernel, lam_init=lam_init),
        out_shape=jax.ShapeDtypeStruct((t, 3 * BRANCH_WIDTH), BF16),
        grid=(t // seq,),
        in_specs=[
            pl.BlockSpec((seq, IN_COLS), lambda b: (b, 0)),
            pl.BlockSpec((4, HEAD_DIM), lambda b: (0, 0)),
            pl.BlockSpec((1, 256), lambda b: (0, 0)),
            pl.BlockSpec(memory_space=pltpu.SMEM),
        ],
        out_specs=pl.BlockSpec((seq, 3 * BRANCH_WIDTH), lambda b: (b, 0)),
        compiler_params=_cparams(("parallel",)),
        name="attention_prompt",
    )(p, diff_lam_l, subln_l, sink_l)


TQ = 512
TQ_FLASH = 1024
TK = 1024


def _tile_lanes(v, n):
    return jnp.concatenate([v] * n, axis=-1) if n > 1 else v


def _online_softmax(idx, s, m_scr):
    m_prev = m_scr[idx]
    m_new = jnp.maximum(m_prev, jnp.max(s, axis=-1, keepdims=True))
    m_scr[idx] = m_new
    alpha = jnp.exp2(m_prev - m_new)
    p = jnp.exp2(s - _tile_lanes(m_new, s.shape[1] // LANES))
    return p, alpha


def _diff_sample_kernel(q_ref, k_ref, v_ref, kc_ref, vc_ref, lam_ref, subln_ref, o_ref,
                        m_scr, l_scr, acc_scr, *, lam_init):
    kk = pl.program_id(2)

    @pl.when(kk == 0)
    def _():
        m_scr[...] = jnp.full(m_scr.shape, NEG_BIG, F32)
        l_scr[...] = jnp.zeros(l_scr.shape, F32)
        acc_scr[...] = jnp.zeros(acc_scr.shape, F32)

    def process(kb_ref, vb_ref):
        for h in range(4):
            v = vb_ref[:, h * 256:(h + 1) * 256].astype(BF16)
            for m in range(2):
                c0 = h * 256 + m * HEAD_DIM
                idx = h * 2 + m
                k = kb_ref[:, c0:c0 + HEAD_DIM].astype(BF16)
                p, alpha = _online_softmax(idx, _dot_nt(q_ref[:, c0:c0 + HEAD_DIM], k), m_scr)
                part = p[:, 0:LANES]
                for c in range(1, p.shape[1] // LANES):
                    part = part + p[:, c * LANES:(c + 1) * LANES]
                l_scr[idx] = alpha * l_scr[idx] + part
                acc_scr[idx] = _tile_lanes(alpha, 2) * acc_scr[idx] + _dot(p.astype(BF16), v)

    @pl.when(kk == 0)
    def _():
        process(kc_ref, vc_ref)

    @pl.when(kk > 0)
    def _():
        process(k_ref, v_ref)

    @pl.when(kk == pl.num_programs(2) - 1)
    def _():
        lam = _diff_lambda(lam_ref, lam_init)
        for h in range(4):
            l1 = jnp.sum(l_scr[2 * h], axis=-1, keepdims=True)
            l2 = jnp.sum(l_scr[2 * h + 1], axis=-1, keepdims=True)
            o1 = acc_scr[2 * h] * (1.0 / l1)
            o2 = acc_scr[2 * h + 1] * (1.0 / l2)
            o = _rms(o1 - lam * o2, subln_ref[...]) * (1.0 - lam_init)
            o_ref[:, h * 256:(h + 1) * 256] = o.astype(o_ref.dtype)


def _gqa_sample_kernel(q_ref, k_ref, v_ref, kc_ref, vc_ref, o_ref, m_scr, acc_scr):
    kk = pl.program_id(2)

    @pl.when(kk == 0)
    def _():
        m_scr[...] = jnp.full(m_scr.shape, NEG_BIG, F32)
        acc_scr[...] = jnp.zeros(acc_scr.shape, F32)

    def process(kb_ref, vb_ref):
        for kvh in range(2):
            k = kb_ref[:, kvh * HEAD_DIM:(kvh + 1) * HEAD_DIM].astype(BF16)
            v = vb_ref[:, kvh * HEAD_DIM:(kvh + 1) * HEAD_DIM].astype(BF16)
            v_ones = jnp.concatenate([v, jnp.ones_like(v)], axis=-1)
            for g in range(4):
                hq = kvh * 4 + g
                s = _dot_nt(q_ref[:, hq * HEAD_DIM:(hq + 1) * HEAD_DIM], k)
                p, alpha = _online_softmax(hq, s, m_scr)
                acc_scr[hq] = _tile_lanes(alpha, 2) * acc_scr[hq] + _dot(p.astype(BF16), v_ones)

    @pl.when(kk == 0)
    def _():
        process(kc_ref, vc_ref)

    @pl.when(kk > 0)
    def _():
        process(k_ref, v_ref)

    @pl.when(kk == pl.num_programs(2) - 1)
    def _():
        for hq in range(8):
            o = acc_scr[hq, :, 0:HEAD_DIM] / acc_scr[hq, :, HEAD_DIM:2 * HEAD_DIM]
            o_ref[:, hq * HEAD_DIM:(hq + 1) * HEAD_DIM] = o.astype(o_ref.dtype)


def _flash_sample(kernel, qkv, cache, l, *, q_col, k_col, v_col, kv_width, n_state, row_sum_scratch, extra,
                  extra_specs, name, dec_seq):
    t = qkv.shape[0]
    nb = t // dec_seq
    tq = TQ_FLASH
    nq = dec_seq // tq
    nk = dec_seq // TK
    past = cache.shape[3]
    q_blk, k_blk, v_blk = q_col // BRANCH_WIDTH, k_col // kv_width, v_col // kv_width
    kv_row = lambda b, qi, kk: b * nk + jnp.maximum(kk - 1, 0)
    return pl.pallas_call(
        kernel,
        out_shape=jax.ShapeDtypeStruct((t, BRANCH_WIDTH), BF16),
        grid=(nb, nq, nk + 1),
        in_specs=[
            pl.BlockSpec((tq, BRANCH_WIDTH), lambda b, qi, kk: (b * nq + qi, q_blk)),
            pl.BlockSpec((TK, kv_width), lambda b, qi, kk: (kv_row(b, qi, kk), k_blk)),
            pl.BlockSpec((TK, kv_width), lambda b, qi, kk: (kv_row(b, qi, kk), v_blk)),
            pl.BlockSpec((None, None, None, past, kv_width), lambda b, qi, kk: (b, l, 0, 0, 0)),
            pl.BlockSpec((None, None, None, past, kv_width), lambda b, qi, kk: (b, l, 1, 0, 0)),
        ] + extra_specs,
        out_specs=pl.BlockSpec((tq, BRANCH_WIDTH), lambda b, qi, kk: (b * nq + qi, 0)),
        scratch_shapes=[pltpu.VMEM((n_state, tq, LANES), F32)] * (2 if row_sum_scratch else 1)
        + [pltpu.VMEM((n_state, tq, 2 * LANES), F32)],
        compiler_params=_cparams(("parallel", "parallel", "arbitrary")),
        name=name,
    )(qkv, qkv, qkv, cache, cache, *extra)


def _window_sample_kernel(q0_ref, q1_ref, kp_ref, kc_ref, kn_ref, vp_ref, vc_ref, vn_ref, kctx_ref, vctx_ref,
                          sink_ref, o_ref, *, dec_seq):
    qi = pl.program_id(1)
    q_start = qi * TQ
    side = WINDOW
    q_pos = q_start + lax.broadcasted_iota(jnp.int32, (TQ, 1), 0)
    segs = ((kp_ref, vp_ref, q_start - side, side), (kc_ref, vc_ref, q_start, TQ),
            (kn_ref, vn_ref, q_start + TQ, side))
    valid = []
    for _, _, start, n in segs:
        k_pos = start + lax.broadcasted_iota(jnp.int32, (TQ, n), 1)
        valid.append((jnp.abs(q_pos - k_pos) <= WINDOW) & (k_pos >= 0) & (k_pos < dec_seq))
    for kvh in range(2):
        q_ref = q0_ref if kvh == 0 else q1_ref
        lo, hi = kvh * HEAD_DIM, (kvh + 1) * HEAD_DIM
        kctx = kctx_ref[:, lo:hi].astype(BF16)
        vctx = vctx_ref[:, lo:hi].astype(BF16)
        for g in range(4):
            hq = kvh * 4 + g
            q = q_ref[:, g * HEAD_DIM:(g + 1) * HEAD_DIM]
            sink = sink_ref[hq] * LOG2E
            scores = [_dot_nt(q, kctx)]
            for (k_ref, _, _, _), ok in zip(segs, valid):
                scores.append(jnp.where(ok, _dot_nt(q, k_ref[:, lo:hi]), NEG_BIG))
            m = jnp.maximum(scores[0].max(axis=-1, keepdims=True), sink)
            for s in scores[1:]:
                m = jnp.maximum(m, s.max(axis=-1, keepdims=True))
            es = [jnp.exp2(s - m) for s in scores]
            den = jnp.exp2(sink - m)
            for e in es:
                den = den + jnp.sum(e, axis=-1, keepdims=True)
            o = _dot(es[0].astype(BF16), vctx)
            for e, (_, v_ref, _, _) in zip(es[1:], segs):
                o = o + _dot(e.astype(BF16), v_ref[:, lo:hi])
            o = o * (1.0 / den)
            o_ref[:, hq * HEAD_DIM:(hq + 1) * HEAD_DIM] = o.astype(o_ref.dtype)


def _window_sample(qkv, cache, sink_l, l, *, dec_seq):
    t = qkv.shape[0]
    nb = t // dec_seq
    nq = dec_seq // TQ
    half = WINDOW
    n_half = dec_seq // half
    per_tile = TQ // half
    past = cache.shape[3]
    kvw = 2 * HEAD_DIM
    q_blk = COL_CQ // 512
    k_blk, v_blk = COL_CK // kvw, COL_CV // kvw
    prev_row = lambda b, qi: b * n_half + jnp.maximum(per_tile * qi - 1, 0)
    next_row = lambda b, qi: b * n_half + jnp.minimum(per_tile * (qi + 1), n_half - 1)
    return pl.pallas_call(
        functools.partial(_window_sample_kernel, dec_seq=dec_seq),
        out_shape=jax.ShapeDtypeStruct((t, BRANCH_WIDTH), BF16),
        grid=(nb, nq),
        in_specs=[
            pl.BlockSpec((TQ, 512), lambda b, qi: (b * nq + qi, q_blk)),
            pl.BlockSpec((TQ, 512), lambda b, qi: (b * nq + qi, q_blk + 1)),
            pl.BlockSpec((half, kvw), lambda b, qi: (prev_row(b, qi), k_blk)),
            pl.BlockSpec((TQ, kvw), lambda b, qi: (b * nq + qi, k_blk)),
            pl.BlockSpec((half, kvw), lambda b, qi: (next_row(b, qi), k_blk)),
            pl.BlockSpec((half, kvw), lambda b, qi: (prev_row(b, qi), v_blk)),
            pl.BlockSpec((TQ, kvw), lambda b, qi: (b * nq + qi, v_blk)),
            pl.BlockSpec((half, kvw), lambda b, qi: (next_row(b, qi), v_blk)),
            pl.BlockSpec((None, None, None, past, kvw), lambda b, qi: (b, l, 0, 0, 0)),
            pl.BlockSpec((None, None, None, past, kvw), lambda b, qi: (b, l, 1, 0, 0)),
            pl.BlockSpec(memory_space=pltpu.SMEM),
        ],
        out_specs=pl.BlockSpec((TQ, BRANCH_WIDTH), lambda b, qi: (b * nq + qi, 0)),
        compiler_params=_cparams(("parallel", "parallel")),
        name="window_attention",
    )(qkv, qkv, qkv, qkv, qkv, qkv, qkv, qkv, cache, cache, sink_l)


def _merge_kernel(oa_ref, ob_ref, oc_ref, g_ref, w_ref, y_ref, acc_scr):
    r = pl.program_id(2)

    def contrib(o_ref):
        return g_ref[...].astype(F32) * _dot(o_ref[...], w_ref[...])

    @pl.when(r == 0)
    def _():
        acc_scr[...] = contrib(oa_ref)

    @pl.when(r == 1)
    def _():
        acc_scr[...] += contrib(ob_ref)

    @pl.when(r == 2)
    def _():
        y_ref[...] = (acc_scr[...] + contrib(oc_ref)).astype(y_ref.dtype)


def _merge_branches(o_arrays, o_blocks, gates, w_branch, l):
    t = gates.shape[0]
    tn = D_MODEL
    nn = D_MODEL // tn
    o_specs = [pl.BlockSpec((TM, BRANCH_WIDTH), functools.partial(lambda i, n, r, blk: (i, blk), blk=blk))
               for blk in o_blocks]
    return pl.pallas_call(
        _merge_kernel,
        out_shape=jax.ShapeDtypeStruct((t, D_MODEL), BF16),
        grid=(t // TM, nn, 3),
        in_specs=o_specs + [
            pl.BlockSpec((TM, tn), lambda i, n, r: (i, r * nn + n)),
            pl.BlockSpec((None, None, BRANCH_WIDTH, tn), lambda i, n, r: (l, r, 0, n)),
        ],
        out_specs=pl.BlockSpec((TM, tn), lambda i, n, r: (i, n)),
        scratch_shapes=[pltpu.VMEM((TM, tn), F32)],
        compiler_params=_cparams(("parallel", "parallel", "arbitrary")),
        name="merge_branches",
    )(*o_arrays, gates, w_branch)


def _layer_norm_rows(v, gain, bias):
    return _norm_rows(v) * gain + bias


def _outproj_kernel(y_ref, w_ref, x_ref, g_ref, gain_ref, bias_ref, o_ref):
    n = pl.program_id(1)
    tn = w_ref.shape[1]
    n_blocks = o_ref.shape[1] // tn
    z = _dot(y_ref[...], w_ref[...])
    for b in range(n_blocks):
        @pl.when(n == b)
        def _(b=b):
            cols = slice(b * tn, (b + 1) * tn)
            o_ref[:, cols] = ALPHA * x_ref[:, cols] + g_ref[:, cols] * z

    @pl.when(n == n_blocks - 1)
    def _():
        o_ref[...] = _layer_norm_rows(o_ref[...], gain_ref[...], bias_ref[...])


def _out_projection(y, w_o, l, x, mod_l, ln_gain_l, ln_bias_l, *, row_of_tile):
    t = x.shape[0]
    tn = 512
    return pl.pallas_call(
        _outproj_kernel,
        out_shape=jax.ShapeDtypeStruct((t, D_MODEL), F32),
        grid=(t // TM, D_MODEL // tn),
        in_specs=[
            pl.BlockSpec((TM, D_MODEL), lambda i, n: (i, 0)),
            pl.BlockSpec((None, D_MODEL, tn), lambda i, n: (l, 0, n)),
            pl.BlockSpec((TM, D_MODEL), lambda i, n: (i, 0)),
            _mod_spec(2, row_of_tile),
            pl.BlockSpec((1, D_MODEL), lambda i, n: (0, 0)),
            pl.BlockSpec((1, D_MODEL), lambda i, n: (0, 0)),
        ],
        out_specs=pl.BlockSpec((TM, D_MODEL), lambda i, n: (i, 0)),
        compiler_params=_cparams(("parallel", "arbitrary")),
        name="out_projection",
    )(y, w_o, x, mod_l, ln_gain_l, ln_bias_l)


def _route(p):
    rows = [p[e:e + 1, :] for e in range(N_EXPERTS)]
    best_score, best_group = None, None
    for g in range(N_GROUPS):
        members = rows[g * EXPERTS_PER_GROUP:(g + 1) * EXPERTS_PER_GROUP]
        score = None
        for a in range(EXPERTS_PER_GROUP):
            for b in range(a + 1, EXPERTS_PER_GROUP):
                pair = members[a] + members[b]
                score = pair if score is None else jnp.maximum(score, pair)
        if g == 0:
            best_score, best_group = score, jnp.zeros(score.shape, F32)
        else:
            better = score > best_score
            best_group = jnp.where(better, float(g), best_group)
            best_score = jnp.where(better, score, best_score)
    e_idx = lax.broadcasted_iota(jnp.int32, p.shape, 0).astype(F32)
    g_idx = jnp.floor(e_idx * (1.0 / EXPERTS_PER_GROUP))
    masked = jnp.where(g_idx == best_group, p, -1.0)
    w1 = jnp.max(masked, axis=0, keepdims=True)
    i1 = jnp.min(jnp.where(masked == w1, e_idx, float(N_EXPERTS)), axis=0, keepdims=True)
    masked2 = jnp.where(e_idx == i1, -2.0, masked)
    w2 = jnp.max(masked2, axis=0, keepdims=True)
    i2 = jnp.min(jnp.where(masked2 == w2, e_idx, float(N_EXPERTS)), axis=0, keepdims=True)
    tot = w1 + w2
    return e_idx, i1, i2, w1 / tot, w2 / tot


ROUTE_ROWS = 8
D_PACKED = D_MODEL // 2


def _pack_bf16_pairs(xb):
    half = xb.shape[1] // 2
    lo = lax.bitcast_convert_type(xb[:, :half].astype(F32), jnp.uint32)
    hi = lax.bitcast_convert_type(xb[:, half:].astype(F32), jnp.uint32)
    return (hi & jnp.uint32(0xFFFF0000)) | (lo >> 16)


def _unpack_bf16_pairs(words):
    lo = lax.bitcast_convert_type(words << 16, F32).astype(BF16)
    hi = lax.bitcast_convert_type(words & jnp.uint32(0xFFFF0000), F32).astype(BF16)
    return lo, hi


def _moe_route_kernel(xa_ref, xb_ref, sh_ref, sc_ref, wr_ref, h_ref, rec_ref, rec_t_ref, cnt_ref, carry_scr,
                      *, tiles_a):
    @pl.when(pl.program_id(0) == 0)
    def _():
        carry_scr[...] = jnp.zeros(carry_scr.shape, F32)

    x = jnp.where(pl.program_id(0) < tiles_a, xa_ref[...], xb_ref[...])
    h = _norm_rows(x) * (1.0 + sc_ref[...]) + sh_ref[...]
    hb = h.astype(BF16)
    h_ref[...] = _pack_bf16_pairs(hb)
    logits = _dot_nt(wr_ref[...].astype(BF16), hb)
    m = jnp.max(logits, axis=0, keepdims=True)
    e = jnp.exp(logits - m)
    probs = e / jnp.sum(e, axis=0, keepdims=True)
    e_idx, i1, i2, w1, w2 = _route(probs)
    tm = probs.shape[1]
    oh1 = (e_idx == i1).astype(F32)
    oh2 = (e_idx == i2).astype(F32)
    oh = oh1 + oh2
    earlier = (lax.broadcasted_iota(jnp.int32, (tm, tm), 0) < lax.broadcasted_iota(jnp.int32, (tm, tm), 1))
    rank = carry_scr[:, 0:1] + _dot(oh.astype(BF16), earlier.astype(BF16))
    r1 = jnp.sum(oh1 * rank, axis=0, keepdims=True)
    r2 = jnp.sum(oh2 * rank, axis=0, keepdims=True)
    carry_scr[...] = carry_scr[...] + jnp.sum(oh, axis=1, keepdims=True)
    cnt_ref[...] = carry_scr[...]
    row = lax.broadcasted_iota(jnp.int32, (ROUTE_ROWS, tm), 0)
    rec = jnp.zeros((ROUTE_ROWS, tm), F32)
    for k, v in enumerate((i1, i2, r1, r2, w1, w2)):
        rec = jnp.where(row == k, v, rec)
    rec_ref[...] = rec
    pad = jnp.zeros((LANES - ROUTE_ROWS, tm), F32)
    rec_t_ref[...] = jnp.concatenate([rec, pad], axis=0).T


def _moe_route(x_a, x_b, mod_l, w_router_t, *, rows_a, rows_b):
    tm = TM_SMALL
    tiles_a = x_a.shape[0] // tm
    t = x_a.shape[0] + x_b.shape[0]
    row_a, row_b = rows_a(tm), rows_b(tm)
    row_of_tile = lambda i: jnp.where(i < tiles_a, row_a(i), row_b(i - tiles_a))
    return pl.pallas_call(
        functools.partial(_moe_route_kernel, tiles_a=tiles_a),
        out_shape=(jax.ShapeDtypeStruct((t, D_PACKED), jnp.uint32), jax.ShapeDtypeStruct((ROUTE_ROWS, t), F32),
                   jax.ShapeDtypeStruct((t, LANES), F32), jax.ShapeDtypeStruct((N_EXPERTS, LANES), F32)),
        grid=(t // tm,),
        in_specs=[
            pl.BlockSpec((tm, D_MODEL), lambda i: (jnp.minimum(i, tiles_a - 1), 0)),
            pl.BlockSpec((tm, D_MODEL), lambda i: (jnp.maximum(i - tiles_a, 0), 0)),
            _mod_spec(3, row_of_tile),
            _mod_spec(4, row_of_tile),
            pl.BlockSpec((N_EXPERTS, D_MODEL), lambda i: (0, 0)),
        ],
        out_specs=(pl.BlockSpec((tm, D_PACKED), lambda i: (i, 0)),
                   pl.BlockSpec((ROUTE_ROWS, tm), lambda i: (0, i)),
                   pl.BlockSpec((tm, LANES), lambda i: (i, 0)),
                   pl.BlockSpec((N_EXPERTS, LANES), lambda i: (0, 0))),
        scratch_shapes=[pltpu.VMEM((N_EXPERTS, LANES), F32)],
        compiler_params=_cparams(("arbitrary",)),
        name="moe_route",
    )(x_a, x_b, mod_l, mod_l, w_router_t)


ROW_DMA_UNROLL = 8
GATHER_ORDER_STRIDE = 37


def _row_copy(src_hbm, src_row, dst_buf, dst_row, sem):
    return pltpu.make_async_copy(src_hbm.at[pl.ds(src_row, 1)], dst_buf.at[pl.ds(dst_row, 1)], sem)


def _moe_expert_kernel(tile_expert_ref, n_tiles_ref, src_ref, h_hbm, wg_ref, wu_ref, wd_ref, ys_ref,
                       x_buf, sems, *, te):
    del tile_expert_ref
    j = pl.program_id(0)
    n_valid = n_tiles_ref[0]
    slot = j % 2

    def start_gather(tile, s):
        def body(r, carry):
            row = (r * GATHER_ORDER_STRIDE) % te
            _row_copy(h_hbm, src_ref[tile * te + row], x_buf.at[s], row, sems.at[s]).start()
            return carry
        lax.fori_loop(0, te, body, 0, unroll=ROW_DMA_UNROLL)

    def wait_gather(s):
        def body(r, carry):
            _row_copy(h_hbm, 0, x_buf.at[s], r, sems.at[s]).wait()
            return carry
        lax.fori_loop(0, te, body, 0, unroll=ROW_DMA_UNROLL)

    @pl.when((j == 0) & (n_valid > 0))
    def _():
        start_gather(0, 0)

    @pl.when(j + 1 < n_valid)
    def _():
        start_gather(j + 1, 1 - slot)

    @pl.when(j < n_valid)
    def _():
        wait_gather(slot)
        x_lo, x_hi = _unpack_bf16_pairs(x_buf[slot])

        def project(w_ref):
            return (_dot(x_lo, w_ref[:D_PACKED, :].astype(BF16)) + _dot(x_hi, w_ref[D_PACKED:, :].astype(BF16)))

        gate = project(wg_ref)
        up = project(wu_ref)
        hid = gate / (1.0 + jnp.exp(-gate)) * up
        ys_ref[...] = _pack_bf16_pairs(_dot(hid.astype(BF16), wd_ref[...].astype(BF16)).astype(BF16))

    @pl.when(j >= n_valid)
    def _():
        ys_ref[...] = jnp.zeros(ys_ref.shape, ys_ref.dtype)


def _moe_experts(h, tile_expert, n_tiles, src, wg, wu, wd, l, *, te):
    n_rows = src.shape[0]
    w_in_spec = pl.BlockSpec((None, None, D_MODEL, D_EXPERT), lambda j, te_ref, *_: (l, te_ref[j], 0, 0))
    w_out_spec = pl.BlockSpec((None, None, D_EXPERT, D_MODEL), lambda j, te_ref, *_: (l, te_ref[j], 0, 0))
    return pl.pallas_call(
        functools.partial(_moe_expert_kernel, te=te),
        out_shape=jax.ShapeDtypeStruct((n_rows, D_PACKED), jnp.uint32),
        grid_spec=pltpu.PrefetchScalarGridSpec(
            num_scalar_prefetch=3,
            grid=(n_rows // te,),
            in_specs=[pl.BlockSpec(memory_space=pltpu.HBM), w_in_spec, w_in_spec, w_out_spec],
            out_specs=pl.BlockSpec((te, D_PACKED), lambda j, *_: (j, 0)),
            scratch_shapes=[pltpu.VMEM((2, te, D_PACKED), h.dtype), pltpu.SemaphoreType.DMA((2,))],
        ),
        compiler_params=_cparams(("arbitrary",)),
        name="moe_experts",
    )(tile_expert, n_tiles, src, h, wg, wu, wd)


TM_COMBINE = 256


def _moe_combine_kernel(dest_ref, ys_hbm, x_ref, rec_t_ref, g_ref, gain_ref, bias_ref, o_ref, y_buf, sems,
                        *, n_tokens, token_offset):
    i = pl.program_id(0)
    tm = x_ref.shape[0]
    slot = i % 2

    def start_gather(tile, s):
        def body(r, carry):
            for k in range(2):
                row = dest_ref[k * n_tokens + token_offset + tile * tm + r]
                _row_copy(ys_hbm, row, y_buf.at[s, k], r, sems.at[s]).start()
            return carry
        lax.fori_loop(0, tm, body, 0, unroll=ROW_DMA_UNROLL)

    def wait_gather(s):
        def body(r, carry):
            for k in range(2):
                _row_copy(ys_hbm, 0, y_buf.at[s, k], r, sems.at[s]).wait()
            return carry
        lax.fori_loop(0, tm, body, 0, unroll=ROW_DMA_UNROLL)

    @pl.when(i == 0)
    def _():
        start_gather(0, 0)

    @pl.when(i + 1 < pl.num_programs(0))
    def _():
        start_gather(i + 1, 1 - slot)

    wait_gather(slot)
    w1, w2 = rec_t_ref[:, 4:5], rec_t_ref[:, 5:6]
    y1_lo, y1_hi = _unpack_bf16_pairs(y_buf[slot, 0])
    y2_lo, y2_hi = _unpack_bf16_pairs(y_buf[slot, 1])
    for half, (y1, y2) in enumerate(((y1_lo, y2_lo), (y1_hi, y2_hi))):
        cols = slice(half * D_PACKED, (half + 1) * D_PACKED)
        z = w1 * y1.astype(F32) + w2 * y2.astype(F32)
        o_ref[:, cols] = ALPHA * x_ref[:, cols] + g_ref[:, cols] * z
    o_ref[...] = _layer_norm_rows(o_ref[...], gain_ref[...], bias_ref[...])


def _moe_combine(dest, ys, x, rec_t, mod_l, ln_gain_l, ln_bias_l, *, row_of_tile, token_offset):
    t = x.shape[0]
    tm = TM_COMBINE
    first_tile = token_offset // tm
    vec = pl.BlockSpec((1, D_MODEL), lambda i, *_: (0, 0))
    return pl.pallas_call(
        functools.partial(_moe_combine_kernel, n_tokens=rec_t.shape[0], token_offset=token_offset),
        out_shape=jax.ShapeDtypeStruct((t, D_MODEL), F32),
        grid_spec=pltpu.PrefetchScalarGridSpec(
            num_scalar_prefetch=1,
            grid=(t // tm,),
            in_specs=[
                pl.BlockSpec(memory_space=pltpu.HBM),
                pl.BlockSpec((tm, D_MODEL), lambda i, *_: (i, 0)),
                pl.BlockSpec((tm, LANES), lambda i, *_: (first_tile + i, 0)),
                _mod_spec(5, row_of_tile),
                vec, vec,
            ],
            out_specs=pl.BlockSpec((tm, D_MODEL), lambda i, *_: (i, 0)),
            scratch_shapes=[pltpu.VMEM((2, 2, tm, D_PACKED), jnp.uint32), pltpu.SemaphoreType.DMA((2,))],
        ),
        compiler_params=_cparams(("arbitrary",)),
        name="moe_combine",
    )(dest, ys, x, rec_t, mod_l, ln_gain_l, ln_bias_l)


def _dispatch_tables(rec, cnt, *, te):
    t = rec.shape[1]
    n_rows = 2 * t + N_EXPERTS * te
    e12 = rec[0:2].astype(jnp.int32)
    r12 = rec[2:4].astype(jnp.int32)
    counts = cnt[:, 0].astype(jnp.int32)
    padded = (counts + te - 1) // te * te
    ends = jnp.cumsum(padded)
    offsets = ends - padded
    expert_ids = jnp.arange(N_EXPERTS, dtype=jnp.int32)[:, None, None]
    dest = (jnp.sum(jnp.where(e12[None] == expert_ids, offsets[:, None, None], 0), axis=0) + r12).reshape(2 * t)
    tokens = jnp.tile(jnp.arange(t, dtype=jnp.int32), 2)
    filler = jnp.arange(n_rows, dtype=jnp.int32) % t
    src = filler.at[dest].set(tokens, unique_indices=True)
    tile_start = jnp.arange(n_rows // te, dtype=jnp.int32) * te
    tile_expert = jnp.minimum(jnp.searchsorted(ends, tile_start, side="right"), N_EXPERTS - 1).astype(jnp.int32)
    n_tiles = (ends[-1:] // te).astype(jnp.int32)
    return dest, src, tile_expert, n_tiles


def _rope_tables(n_tokens):
    rows = n_tokens // GRID_W
    row = jnp.repeat(jnp.arange(rows), GRID_W).astype(F32)
    col = jnp.tile(jnp.arange(GRID_W), rows).astype(F32)
    quarter = HEAD_DIM // 4
    inv_freq = ROPE_THETA ** (-jnp.arange(quarter, dtype=F32) / quarter)
    ang_r, ang_c = row[:, None] * inv_freq, col[:, None] * inv_freq
    cos = jnp.concatenate([jnp.cos(ang_r), jnp.cos(ang_r), jnp.cos(ang_c), jnp.cos(ang_c)], axis=-1)
    zero = jnp.zeros_like(ang_r)
    sin = jnp.concatenate([-jnp.sin(ang_r), zero, -jnp.sin(ang_c), zero,
                           zero, jnp.sin(ang_r), zero, jnp.sin(ang_c)], axis=-1)
    return cos, sin


def _row_of_tile(first_row, tokens_per_row):
    def for_tile(tile):
        return lambda i: first_row + (i * tile) // tokens_per_row
    return for_tile


def _mixer_output(x, attn_arrays, attn_blocks, gates, mod_l, rows, l, w_branch, w_o, ln_gain, ln_bias):
    y = _merge_branches(attn_arrays, attn_blocks, gates, w_branch, l)
    return _out_projection(y, w_o, l, x, mod_l, ln_gain[l, 0:1], ln_bias[l, 0:1], row_of_tile=rows(TM))


EXPERT_TILE = 256


def _moe_ffn(x_a, x_b, mod_l, rows_a, rows_b, l, ln_gain, ln_bias, w_router_t, w_e_gate, w_e_up, w_e_down):
    h2, rec, rec_t, cnt = _moe_route(x_a, x_b, mod_l, w_router_t, rows_a=rows_a, rows_b=rows_b)
    dest, src, tile_expert, n_tiles = _dispatch_tables(rec, cnt, te=EXPERT_TILE)
    ys = _moe_experts(h2, tile_expert, n_tiles, src, w_e_gate, w_e_up, w_e_down, l, te=EXPERT_TILE)
    outs = []
    for x, rows, offset in ((x_a, rows_a, 0), (x_b, rows_b, x_a.shape[0])):
        outs.append(_moe_combine(dest, ys, x, rec_t, mod_l, ln_gain[l, 1:2], ln_bias[l, 1:2],
                                 row_of_tile=rows(TM_COMBINE), token_offset=offset))
    return outs


def kernel(x_prompt, x_sample, cache_kv_a, cache_kv_b, cache_kv_c, c, c_ctx, w_in, w_gate, w_branch, w_o,
           w_mod, b_mod, ln_gain, ln_bias, diff_lam, diff_subln, qk_gain, sink, w_router, w_e_gate, w_e_up,
           w_e_down):
    batch, seq, _ = x_prompt.shape
    dec_batch, dec_seq, _ = x_sample.shape
    past = cache_kv_a.shape[3]
    t_p, t_s = batch * seq, dec_batch * dec_seq

    cond = jnp.concatenate([c_ctx[None], c, jnp.zeros((MOD_ROWS - 1 - dec_batch, D_MODEL), F32)], axis=0)
    mod = _modulation(cond.T, w_mod, b_mod)
    cos_t, sin_t = _rope_tables(dec_seq)
    w_router_t = w_router.T
    w_in, w_gate, w_branch, w_o = (w.astype(BF16) for w in (w_in, w_gate, w_branch, w_o))
    cache_a = cache_kv_a.reshape(dec_batch, DEPTH, 2, past, 4 * 256)
    cache_b = cache_kv_b.reshape(dec_batch, DEPTH, 2, past, 2 * HEAD_DIM)
    cache_c = cache_kv_c.reshape(dec_batch, DEPTH, 2, past, 2 * HEAD_DIM)
    rows_p = _row_of_tile(0, t_p)
    rows_s = _row_of_tile(1, dec_seq)

    y_p = x_prompt.reshape(t_p, D_MODEL)
    y_s = x_sample.reshape(t_s, D_MODEL)
    projections = []
    for l in range(DEPTH):
        lam_init = 0.8 - 0.6 * math.exp(-0.3 * l)
        mod_l = mod[l].reshape(MOD_ROWS, 1, N_MOD * D_MODEL)
        subln_l = diff_subln[l].reshape(1, 256)
        shared = (w_branch, w_o, ln_gain, ln_bias)

        p = _in_projection(y_p, mod_l, w_in, l, qk_gain[l], cos_t, sin_t, row_of_tile=rows_p(TM),
                           rope=False, q_scale=1.0, out_dtype=F32)
        gates = _branch_gates(y_p, mod_l, w_gate, l, row_of_tile=rows_p(TM))
        attn = _attention_prompt(p, diff_lam[l], subln_l, sink[l], lam_init=lam_init, seq=seq)
        y_p = _mixer_output(y_p, (attn, attn, attn), (0, 1, 2), gates, mod_l, rows_p, l, *shared)
        projections.append(p.reshape(batch, seq, IN_COLS))

        qkv = _in_projection(y_s, mod_l, w_in, l, qk_gain[l], cos_t, sin_t, row_of_tile=rows_s(TM),
                             rope=True, q_scale=SCALE * LOG2E, out_dtype=BF16)
        gates = _branch_gates(y_s, mod_l, w_gate, l, row_of_tile=rows_s(TM))
        vec = lambda shape: pl.BlockSpec(shape, lambda b, qi, kk: (0, 0))
        a_o = _flash_sample(functools.partial(_diff_sample_kernel, lam_init=lam_init), qkv, cache_a, l,
                            q_col=COL_AQ, k_col=COL_AK, v_col=COL_AV, kv_width=1024, n_state=8,
                            row_sum_scratch=True, extra=(diff_lam[l], subln_l),
                            extra_specs=[vec((4, HEAD_DIM)), vec((1, 256))], name="diff_attention",
                            dec_seq=dec_seq)
        b_o = _flash_sample(_gqa_sample_kernel, qkv, cache_b, l, q_col=COL_BQ, k_col=COL_BK, v_col=COL_BV,
                            kv_width=256, n_state=8, row_sum_scratch=False, extra=(), extra_specs=[],
                            name="gqa_attention", dec_seq=dec_seq)
        c_o = _window_sample(qkv, cache_c, sink[l], l, dec_seq=dec_seq)
        y_s = _mixer_output(y_s, (a_o, b_o, c_o), (0, 0, 0), gates, mod_l, rows_s, l, *shared)

        y_p, y_s = _moe_ffn(y_p, y_s, mod_l, rows_p, rows_s, l, ln_gain, ln_bias, w_router_t, w_e_gate, w_e_up,
                            w_e_down)

    def new_cache(col_k, col_v, col_end, heads):
        parts = [p[..., c0:c1] for p in projections for c0, c1 in ((col_k, col_v), (col_v, col_end))]
        return jnp.stack(parts, axis=1).reshape(batch, DEPTH, 2, seq, heads, (col_v - col_k) // heads)

    new_kv_a = new_cache(COL_AK, COL_AV, COL_BQ, 4)
    new_kv_b = new_cache(COL_BK, COL_BV, COL_CQ, 2)
    new_kv_c = new_cache(COL_CK, COL_CV, IN_COLS, 2)
    return (y_p.reshape(batch, seq, D_MODEL), y_s.reshape(dec_batch, dec_seq, D_MODEL),
            new_kv_a, new_kv_b, new_kv_c)
```

```python
import functools
import math

import jax
import jax.numpy as jnp
from jax import lax
from jax.experimental import pallas as pl
from jax.experimental.pallas import tpu as pltpu

F32 = jnp.float32
BF16 = jnp.bfloat16

D_MODEL = 2048
HEAD_DIM = 128
GRID_W = 64
ROPE_THETA = 10000.0
WINDOW = 128
N_EXPERTS = 16
N_GROUPS = 4
EXPERTS_PER_GROUP = N_EXPERTS // N_GROUPS
D_EXPERT = 512
N_MOD = 6
DEPTH = 2
ALPHA = (2 * DEPTH) ** 0.25
EPS = 1e-6
IN_COLS = 6144
BRANCH_WIDTH = 1024
SCALE = HEAD_DIM ** -0.5
LOG2E = math.log2(math.e)
NEG_BIG = -1e30

COL_AQ, COL_AK, COL_AV = 0, 1024, 2048
COL_BQ, COL_BK, COL_BV = 3072, 4096, 4352
COL_CQ, COL_CK, COL_CV = 4608, 5632, 5888

LANES = 128
VMEM_LIMIT = 56 * 1024 * 1024

TM = 1024
TN_PROJ = 1024
TN_PROJ_F32 = 1024
TN_GATE = 1024
TM_SMALL = 512


def _cparams(sem):
    return pltpu.CompilerParams(dimension_semantics=sem, vmem_limit_bytes=VMEM_LIMIT)


def _dot(a, b):
    return jnp.dot(a, b, preferred_element_type=F32)


def _dot_nt(a, b):
    return lax.dot_general(a, b, (((1,), (1,)), ((), ())), preferred_element_type=F32)


def _norm_rows(x):
    mu = jnp.mean(x, axis=-1, keepdims=True)
    xc = x - mu
    var = jnp.mean(xc * xc, axis=-1, keepdims=True)
    return xc * lax.rsqrt(var + EPS)


def _rms(v, gain):
    ms = jnp.mean(v * v, axis=-1, keepdims=True)
    return v * lax.rsqrt(ms + EPS) * gain


def _rope(v, cos, sin_from_upper, sin_from_lower):
    return v * cos + pltpu.roll(v, 96, 1) * sin_from_upper + pltpu.roll(v, 32, 1) * sin_from_lower


N_COND = 3
MOD_ROWS = 8
TN_MOD = 1024


def _mod_kernel(cond_ref, w_ref, b_ref, o_ref):
    w = w_ref[...]
    row_idx = lax.broadcasted_iota(jnp.int32, (MOD_ROWS, TN_MOD), 0)
    out = jnp.zeros((MOD_ROWS, TN_MOD), F32)
    for r in range(N_COND):
        c = cond_ref[:, r:r + 1]
        s = c / (1.0 + jnp.exp(-c))
        m = jnp.sum(w * s, axis=0, keepdims=True) + b_ref[...]
        out = jnp.where(row_idx == r, m, out)
    o_ref[...] = out


def _modulation(cond_t, w_mod, b_mod):
    n = N_MOD * D_MODEL
    return pl.pallas_call(
        _mod_kernel,
        out_shape=jax.ShapeDtypeStruct((DEPTH, MOD_ROWS, n), F32),
        grid=(DEPTH, n // TN_MOD),
        in_specs=[
            pl.BlockSpec((D_MODEL, MOD_ROWS), lambda l, j: (0, 0)),
            pl.BlockSpec((None, D_MODEL, TN_MOD), lambda l, j: (l, 0, j)),
            pl.BlockSpec((None, 1, TN_MOD), lambda l, j: (l, 0, j)),
        ],
        out_specs=pl.BlockSpec((None, MOD_ROWS, TN_MOD), lambda l, j: (l, 0, j)),
        compiler_params=_cparams(("parallel", "parallel")),
        name="modulation",
    )(cond_t, w_mod, b_mod.reshape(DEPTH, 1, n))


def _modulate_to_scratch(x_ref, sh_ref, sc_ref, h_scr):
    h = _norm_rows(x_ref[...]) * (1.0 + sc_ref[...]) + sh_ref[...]
    h_scr[...] = h.astype(BF16)


def _head_chunk_kinds():
    kinds = []
    for n_chunks, kind in ((8, (None, True, True)), (8, (None, True, False)), (8, (None, False, False)),
                           (8, ("q", True, True)), (2, ("k", True, False)), (2, (None, False, False)),
                           (8, (None, True, True)), (2, (None, True, False)), (2, (None, False, False))):
        kinds.extend([kind] * n_chunks)
    return kinds


_HEAD_CHUNK_KINDS = _head_chunk_kinds()


def _inproj_kernel(x_ref, sh_ref, sc_ref, w_ref, qk_gain_ref, cos_ref, sin_ref, o_ref, h_scr, *, rope,
                   q_scale):
    j = pl.program_id(1)
    gain_q = qk_gain_ref[0:1, :]
    gain_k = qk_gain_ref[1:2, :]

    @pl.when(j == 0)
    def _():
        _modulate_to_scratch(x_ref, sh_ref, sc_ref, h_scr)

    acc = _dot(h_scr[...], w_ref[...])
    tn = o_ref.shape[1]
    n_chunks = tn // LANES

    def chunk(c):
        return acc[:, c * LANES:(c + 1) * LANES]

    def store(c, v):
        o_ref[:, c * LANES:(c + 1) * LANES] = v.astype(o_ref.dtype)

    def rp(v):
        return _rope(v, cos_ref[...], sin_ref[:, :HEAD_DIM], sin_ref[:, HEAD_DIM:]) if rope else v

    def qs(v):
        return v * q_scale if q_scale != 1.0 else v

    def finish(c, kind):
        norm, rotate, is_query = kind
        v = chunk(c)
        if norm is not None:
            v = _rms(v, gain_q if norm == "q" else gain_k)
        if rotate:
            v = rp(v)
        store(c, qs(v) if is_query else v)

    tile_kinds = [tuple(_HEAD_CHUNK_KINDS[t * n_chunks:(t + 1) * n_chunks]) for t in range(IN_COLS // tn)]
    for kinds in dict.fromkeys(tile_kinds):
        tiles = [t for t, k in enumerate(tile_kinds) if k == kinds]
        cond = j == tiles[0]
        for t in tiles[1:]:
            cond = cond | (j == t)

        @pl.when(cond)
        def _(kinds=kinds):
            for c, kind in enumerate(kinds):
                finish(c, kind)


def _gate_kernel(x_ref, sh_ref, sc_ref, w_ref, o_ref, h_scr):
    @pl.when(pl.program_id(1) == 0)
    def _():
        _modulate_to_scratch(x_ref, sh_ref, sc_ref, h_scr)

    acc = _dot(h_scr[...], w_ref[...])
    o_ref[...] = (1.0 / (1.0 + jnp.exp(-acc))).astype(o_ref.dtype)


def _mod_spec(which, row_of_tile):
    return pl.BlockSpec((None, 1, D_MODEL), lambda i, *_: (row_of_tile(i), 0, which))


def _in_projection(x, mod_l, w_in, l, qk_gain_l, cos_t, sin_t, *, row_of_tile, rope, q_scale, out_dtype):
    t = x.shape[0]
    tiles_per_seq = cos_t.shape[0] // TM
    tn = TN_PROJ_F32 if out_dtype == F32 else TN_PROJ
    return pl.pallas_call(
        functools.partial(_inproj_kernel, rope=rope, q_scale=q_scale),
        out_shape=jax.ShapeDtypeStruct((t, IN_COLS), out_dtype),
        grid=(t // TM, IN_COLS // tn),
        in_specs=[
            pl.BlockSpec((TM, D_MODEL), lambda i, j: (i, 0)),
            _mod_spec(0, row_of_tile),
            _mod_spec(1, row_of_tile),
            pl.BlockSpec((None, D_MODEL, tn), lambda i, j: (l, 0, j)),
            pl.BlockSpec((2, HEAD_DIM), lambda i, j: (0, 0)),
            pl.BlockSpec((TM, HEAD_DIM), lambda i, j: (i % tiles_per_seq, 0)),
            pl.BlockSpec((TM, 2 * HEAD_DIM), lambda i, j: (i % tiles_per_seq, 0)),
        ],
        out_specs=pl.BlockSpec((TM, tn), lambda i, j: (i, j)),
        scratch_shapes=[pltpu.VMEM((TM, D_MODEL), BF16)],
        compiler_params=_cparams(("parallel", "arbitrary")),
        name="in_projection",
    )(x, mod_l, mod_l, w_in, qk_gain_l, cos_t, sin_t)


def _branch_gates(x, mod_l, w_gate, l, *, row_of_tile):
    t = x.shape[0]
    n = w_gate.shape[2]
    return pl.pallas_call(
        _gate_kernel,
        out_shape=jax.ShapeDtypeStruct((t, n), BF16),
        grid=(t // TM, n // TN_GATE),
        in_specs=[
            pl.BlockSpec((TM, D_MODEL), lambda i, j: (i, 0)),
            _mod_spec(0, row_of_tile),
            _mod_spec(1, row_of_tile),
            pl.BlockSpec((None, D_MODEL, TN_GATE), lambda i, j: (l, 0, j)),
        ],
        out_specs=pl.BlockSpec((TM, TN_GATE), lambda i, j: (i, j)),
        scratch_shapes=[pltpu.VMEM((TM, D_MODEL), BF16)],
        compiler_params=_cparams(("parallel", "arbitrary")),
        name="branch_gates",
    )(x, mod_l, mod_l, w_gate)


def _diff_lambda(lam_ref, lam_init):
    lp = lam_ref[...]
    t1 = jnp.sum(lp[0:1] * lp[1:2], axis=-1, keepdims=True)
    t2 = jnp.sum(lp[2:3] * lp[3:4], axis=-1, keepdims=True)
    return jnp.exp(t1) - jnp.exp(t2) + lam_init


def _softmax_rows(s, sink=None):
    m = jnp.max(s, axis=-1, keepdims=True)
    if sink is not None:
        m = jnp.maximum(m, sink)
    e = jnp.exp(s - m)
    den = jnp.sum(e, axis=-1, keepdims=True)
    if sink is not None:
        den = den + jnp.exp(sink - m)
    return e * (1.0 / den)


def _attn_prompt_kernel(p_ref, lam_ref, subln_ref, sink_ref, o_ref, *, lam_init):
    lam = _diff_lambda(lam_ref, lam_init)

    def blk(c0, w):
        return p_ref[:, c0:c0 + w].astype(BF16)

    for h in range(4):
        probs = []
        for m in range(2):
            q = blk(COL_AQ + h * 256 + m * HEAD_DIM, HEAD_DIM)
            k = blk(COL_AK + h * 256 + m * HEAD_DIM, HEAD_DIM)
            probs.append(_softmax_rows(_dot_nt(q, k) * SCALE))
        w = probs[0] - lam * probs[1]
        o = _dot(w.astype(BF16), blk(COL_AV + h * 256, 256))
        o = _rms(o, subln_ref[...]) * (1.0 - lam_init)
        o_ref[:, h * 256:(h + 1) * 256] = o.astype(o_ref.dtype)

    for mixer, (cq, ck, cv) in enumerate(((COL_BQ, COL_BK, COL_BV), (COL_CQ, COL_CK, COL_CV))):
        for kvh in range(2):
            k = blk(ck + kvh * HEAD_DIM, HEAD_DIM)
            v = blk(cv + kvh * HEAD_DIM, HEAD_DIM)
            for g in range(4):
                hq = kvh * 4 + g
                q = blk(cq + hq * HEAD_DIM, HEAD_DIM)
                sink = sink_ref[hq] if mixer == 1 else None
                p = _softmax_rows(_dot_nt(q, k) * SCALE, sink)
                o = _dot(p.astype(BF16), v)
                c0 = BRANCH_WIDTH * (1 + mixer) + hq * HEAD_DIM
                o_ref[:, c0:c0 + HEAD_DIM] = o.astype(o_ref.dtype)


def _attention_prompt(p, diff_lam_l, subln_l, sink_l, *, lam_init, seq):
    t = p.shape[0]
    return pl.pallas_call(
        functools.partial(_attn_prompt_kernel, lam_init=lam_init),
        out_shape=jax.ShapeDtypeStruct((t, 3 * BRANCH_WIDTH), BF16),
        grid=(t // seq,),
        in_specs=[
            pl.BlockSpec((seq, IN_COLS), lambda b: (b, 0)),
            pl.BlockSpec((4, HEAD_DIM), lambda b: (0, 0)),
            pl.BlockSpec((1, 256), lambda b: (0, 0)),
            pl.BlockSpec(memory_space=pltpu.SMEM),
        ],
        out_specs=pl.BlockSpec((seq, 3 * BRANCH_WIDTH), lambda b: (b, 0)),
        compiler_params=_cparams(("parallel",)),
        name="attention_prompt",
    )(p, diff_lam_l, subln_l, sink_l)


TQ = 512
TQ_FLASH = 1024
TK = 1024


def _tile_lanes(v, n):
    return jnp.concatenate([v] * n, axis=-1) if n > 1 else v


def _online_softmax(idx, s, m_scr):
    m_prev = m_scr[idx]
    m_new = jnp.maximum(m_prev, jnp.max(s, axis=-1, keepdims=True))
    m_scr[idx] = m_new
    alpha = jnp.exp2(m_prev - m_new)
    p = jnp.exp2(s - _tile_lanes(m_new, s.shape[1] // LANES))
    return p, alpha


def _diff_sample_kernel(q_ref, k_ref, v_ref, kc_ref, vc_ref, lam_ref, subln_ref, o_ref,
                        m_scr, l_scr, acc_scr, *, lam_init):
    kk = pl.program_id(2)

    @pl.when(kk == 0)
    def _():
        m_scr[...] = jnp.full(m_scr.shape, NEG_BIG, F32)
        l_scr[...] = jnp.zeros(l_scr.shape, F32)
        acc_scr[...] = jnp.zeros(acc_scr.shape, F32)

    def process(kb_ref, vb_ref):
        for h in range(4):
            v = vb_ref[:, h * 256:(h + 1) * 256].astype(BF16)
            for m in range(2):
                c0 = h * 256 + m * HEAD_DIM
                idx = h * 2 + m
                k = kb_ref[:, c0:c0 + HEAD_DIM].astype(BF16)
                p, alpha = _online_softmax(idx, _dot_nt(q_ref[:, c0:c0 + HEAD_DIM], k), m_scr)
                part = p[:, 0:LANES]
                for c in range(1, p.shape[1] // LANES):
                    part = part + p[:, c * LANES:(c + 1) * LANES]
                l_scr[idx] = alpha * l_scr[idx] + part
                acc_scr[idx] = _tile_lanes(alpha, 2) * acc_scr[idx] + _dot(p.astype(BF16), v)

    @pl.when(kk == 0)
    def _():
        process(kc_ref, vc_ref)

    @pl.when(kk > 0)
    def _():
        process(k_ref, v_ref)

    @pl.when(kk == pl.num_programs(2) - 1)
    def _():
        lam = _diff_lambda(lam_ref, lam_init)
        for h in range(4):
            l1 = jnp.sum(l_scr[2 * h], axis=-1, keepdims=True)
            l2 = jnp.sum(l_scr[2 * h + 1], axis=-1, keepdims=True)
            o1 = acc_scr[2 * h] * (1.0 / l1)
            o2 = acc_scr[2 * h + 1] * (1.0 / l2)
            o = _rms(o1 - lam * o2, subln_ref[...]) * (1.0 - lam_init)
            o_ref[:, h * 256:(h + 1) * 256] = o.astype(o_ref.dtype)


def _gqa_sample_kernel(q_ref, k_ref, v_ref, kc_ref, vc_ref, o_ref, m_scr, acc_scr):
    kk = pl.program_id(2)

    @pl.when(kk == 0)
    def _():
        m_scr[...] = jnp.full(m_scr.shape, NEG_BIG, F32)
        acc_scr[...] = jnp.zeros(acc_scr.shape, F32)

    def process(kb_ref, vb_ref):
        for kvh in range(2):
            k = kb_ref[:, kvh * HEAD_DIM:(kvh + 1) * HEAD_DIM].astype(BF16)
            v = vb_ref[:, kvh * HEAD_DIM:(kvh + 1) * HEAD_DIM].astype(BF16)
            v_ones = jnp.concatenate([v, jnp.ones_like(v)], axis=-1)
            for g in range(4):
                hq = kvh * 4 + g
                s = _dot_nt(q_ref[:, hq * HEAD_DIM:(hq + 1) * HEAD_DIM], k)
                p, alpha = _online_softmax(hq, s, m_scr)
                acc_scr[hq] = _tile_lanes(alpha, 2) * acc_scr[hq] + _dot(p.astype(BF16), v_ones)

    @pl.when(kk == 0)
    def _():
        process(kc_ref, vc_ref)

    @pl.when(kk > 0)
    def _():
        process(k_ref, v_ref)

    @pl.when(kk == pl.num_programs(2) - 1)
    def _():
        for hq in range(8):
            o = acc_scr[hq, :, 0:HEAD_DIM] / acc_scr[hq, :, HEAD_DIM:2 * HEAD_DIM]
            o_ref[:, hq * HEAD_DIM:(hq + 1) * HEAD_DIM] = o.astype(o_ref.dtype)


def _flash_sample(kernel, qkv, cache, l, *, q_col, k_col, v_col, kv_width, n_state, row_sum_scratch, extra,
                  extra_specs, name, dec_seq):
    t = qkv.shape[0]
    nb = t // dec_seq
    tq = TQ_FLASH
    nq = dec_seq // tq
    nk = dec_seq // TK
    past = cache.shape[3]
    q_blk, k_blk, v_blk = q_col // BRANCH_WIDTH, k_col // kv_width, v_col // kv_width
    kv_row = lambda b, qi, kk: b * nk + jnp.maximum(kk - 1, 0)
    return pl.pallas_call(
        kernel,
        out_shape=jax.ShapeDtypeStruct((t, BRANCH_WIDTH), BF16),
        grid=(nb, nq, nk + 1),
        in_specs=[
            pl.BlockSpec((tq, BRANCH_WIDTH), lambda b, qi, kk: (b * nq + qi, q_blk)),
            pl.BlockSpec((TK, kv_width), lambda b, qi, kk: (kv_row(b, qi, kk), k_blk)),
            pl.BlockSpec((TK, kv_width), lambda b, qi, kk: (kv_row(b, qi, kk), v_blk)),
            pl.BlockSpec((None, None, None, past, kv_width), lambda b, qi, kk: (b, l, 0, 0, 0)),
            pl.BlockSpec((None, None, None, past, kv_width), lambda b, qi, kk: (b, l, 1, 0, 0)),
        ] + extra_specs,
        out_specs=pl.BlockSpec((tq, BRANCH_WIDTH), lambda b, qi, kk: (b * nq + qi, 0)),
        scratch_shapes=[pltpu.VMEM((n_state, tq, LANES), F32)] * (2 if row_sum_scratch else 1)
        + [pltpu.VMEM((n_state, tq, 2 * LANES), F32)],
        compiler_params=_cparams(("parallel", "parallel", "arbitrary")),
        name=name,
    )(qkv, qkv, qkv, cache, cache, *extra)


def _window_sample_kernel(q0_ref, q1_ref, kp_ref, kc_ref, kn_ref, vp_ref, vc_ref, vn_ref, kctx_ref, vctx_ref,
                          sink_ref, o_ref, *, dec_seq):
    qi = pl.program_id(1)
    q_start = qi * TQ
    side = WINDOW
    q_pos = q_start + lax.broadcasted_iota(jnp.int32, (TQ, 1), 0)
    segs = ((kp_ref, vp_ref, q_start - side, side), (kc_ref, vc_ref, q_start, TQ),
            (kn_ref, vn_ref, q_start + TQ, side))
    valid = []
    for _, _, start, n in segs:
        k_pos = start + lax.broadcasted_iota(jnp.int32, (TQ, n), 1)
        valid.append((jnp.abs(q_pos - k_pos) <= WINDOW) & (k_pos >= 0) & (k_pos < dec_seq))
    for kvh in range(2):
        q_ref = q0_ref if kvh == 0 else q1_ref
        lo, hi = kvh * HEAD_DIM, (kvh + 1) * HEAD_DIM
        kctx = kctx_ref[:, lo:hi].astype(BF16)
        vctx = vctx_ref[:, lo:hi].astype(BF16)
        for g in range(4):
            hq = kvh * 4 + g
            q = q_ref[:, g * HEAD_DIM:(g + 1) * HEAD_DIM]
            sink = sink_ref[hq] * LOG2E
            scores = [_dot_nt(q, kctx)]
            for (k_ref, _, _, _), ok in zip(segs, valid):
                scores.append(jnp.where(ok, _dot_nt(q, k_ref[:, lo:hi]), NEG_BIG))
            m = jnp.maximum(scores[0].max(axis=-1, keepdims=True), sink)
            for s in scores[1:]:
                m = jnp.maximum(m, s.max(axis=-1, keepdims=True))
            es = [jnp.exp2(s - m) for s in scores]
            den = jnp.exp2(sink - m)
            for e in es:
                den = den + jnp.sum(e, axis=-1, keepdims=True)
            o = _dot(es[0].astype(BF16), vctx)
            for e, (_, v_ref, _, _) in zip(es[1:], segs):
                o = o + _dot(e.astype(BF16), v_ref[:, lo:hi])
            o = o * (1.0 / den)
            o_ref[:, hq * HEAD_DIM:(hq + 1) * HEAD_DIM] = o.astype(o_ref.dtype)


def _window_sample(qkv, cache, sink_l, l, *, dec_seq):
    t = qkv.shape[0]
    nb = t // dec_seq
    nq = dec_seq // TQ
    half = WINDOW
    n_half = dec_seq // half
    per_tile = TQ // half
    past = cache.shape[3]
    kvw = 2 * HEAD_DIM
    q_blk = COL_CQ // 512
    k_blk, v_blk = COL_CK // kvw, COL_CV // kvw
    prev_row = lambda b, qi: b * n_half + jnp.maximum(per_tile * qi - 1, 0)
    next_row = lambda b, qi: b * n_half + jnp.minimum(per_tile * (qi + 1), n_half - 1)
    return pl.pallas_call(
        functools.partial(_window_sample_kernel, dec_seq=dec_seq),
        out_shape=jax.ShapeDtypeStruct((t, BRANCH_WIDTH), BF16),
        grid=(nb, nq),
        in_specs=[
            pl.BlockSpec((TQ, 512), lambda b, qi: (b * nq + qi, q_blk)),
            pl.BlockSpec((TQ, 512), lambda b, qi: (b * nq + qi, q_blk + 1)),
            pl.BlockSpec((half, kvw), lambda b, qi: (prev_row(b, qi), k_blk)),
            pl.BlockSpec((TQ, kvw), lambda b, qi: (b * nq + qi, k_blk)),
            pl.BlockSpec((half, kvw), lambda b, qi: (next_row(b, qi), k_blk)),
            pl.BlockSpec((half, kvw), lambda b, qi: (prev_row(b, qi), v_blk)),
            pl.BlockSpec((TQ, kvw), lambda b, qi: (b * nq + qi, v_blk)),
            pl.BlockSpec((half, kvw), lambda b, qi: (next_row(b, qi), v_blk)),
            pl.BlockSpec((None, None, None, past, kvw), lambda b, qi: (b, l, 0, 0, 0)),
            pl.BlockSpec((None, None, None, past, kvw), lambda b, qi: (b, l, 1, 0, 0)),
            pl.BlockSpec(memory_space=pltpu.SMEM),
        ],
        out_specs=pl.BlockSpec((TQ, BRANCH_WIDTH), lambda b, qi: (b * nq + qi, 0)),
        compiler_params=_cparams(("parallel", "parallel")),
        name="window_attention",
    )(qkv, qkv, qkv, qkv, qkv, qkv, qkv, qkv, cache, cache, sink_l)


def _merge_kernel(oa_ref, ob_ref, oc_ref, g_ref, w_ref, y_ref, acc_scr):
    r = pl.program_id(2)

    def contrib(o_ref):
        return g_ref[...].astype(F32) * _dot(o_ref[...], w_ref[...])

    @pl.when(r == 0)
    def _():
        acc_scr[...] = contrib(oa_ref)

    @pl.when(r == 1)
    def _():
        acc_scr[...] += contrib(ob_ref)

    @pl.when(r == 2)
    def _():
        y_ref[...] = (acc_scr[...] + contrib(oc_ref)).astype(y_ref.dtype)


def _merge_branches(o_arrays, o_blocks, gates, w_branch, l):
    t = gates.shape[0]
    tn = D_MODEL
    nn = D_MODEL // tn
    o_specs = [pl.BlockSpec((TM, BRANCH_WIDTH), functools.partial(lambda i, n, r, blk: (i, blk), blk=blk))
               for blk in o_blocks]
    return pl.pallas_call(
        _merge_kernel,
        out_shape=jax.ShapeDtypeStruct((t, D_MODEL), BF16),
        grid=(t // TM, nn, 3),
        in_specs=o_specs + [
            pl.BlockSpec((TM, tn), lambda i, n, r: (i, r * nn + n)),
            pl.BlockSpec((None, None, BRANCH_WIDTH, tn), lambda i, n, r: (l, r, 0, n)),
        ],
        out_specs=pl.BlockSpec((TM, tn), lambda i, n, r: (i, n)),
        scratch_shapes=[pltpu.VMEM((TM, tn), F32)],
        compiler_params=_cparams(("parallel", "parallel", "arbitrary")),
        name="merge_branches",
    )(*o_arrays, gates, w_branch)


def _layer_norm_rows(v, gain, bias):
    return _norm_rows(v) * gain + bias


def _outproj_kernel(y_ref, w_ref, x_ref, g_ref, gain_ref, bias_ref, o_ref):
    n = pl.program_id(1)
    tn = w_ref.shape[1]
    n_blocks = o_ref.shape[1] // tn
    z = _dot(y_ref[...], w_ref[...])
    for b in range(n_blocks):
        @pl.when(n == b)
        def _(b=b):
            cols = slice(b * tn, (b + 1) * tn)
            o_ref[:, cols] = ALPHA * x_ref[:, cols] + g_ref[:, cols] * z

    @pl.when(n == n_blocks - 1)
    def _():
        o_ref[...] = _layer_norm_rows(o_ref[...], gain_ref[...], bias_ref[...])


def _out_projection(y, w_o, l, x, mod_l, ln_gain_l, ln_bias_l, *, row_of_tile):
    t = x.shape[0]
    tn = 512
    return pl.pallas_call(
        _outproj_kernel,
        out_shape=jax.ShapeDtypeStruct((t, D_MODEL), F32),
        grid=(t // TM, D_MODEL // tn),
        in_specs=[
            pl.BlockSpec((TM, D_MODEL), lambda i, n: (i, 0)),
            pl.BlockSpec((None, D_MODEL, tn), lambda i, n: (l, 0, n)),
            pl.BlockSpec((TM, D_MODEL), lambda i, n: (i, 0)),
            _mod_spec(2, row_of_tile),
            pl.BlockSpec((1, D_MODEL), lambda i, n: (0, 0)),
            pl.BlockSpec((1, D_MODEL), lambda i, n: (0, 0)),
        ],
        out_specs=pl.BlockSpec((TM, D_MODEL), lambda i, n: (i, 0)),
        compiler_params=_cparams(("parallel", "arbitrary")),
        name="out_projection",
    )(y, w_o, x, mod_l, ln_gain_l, ln_bias_l)


def _route(p):
    rows = [p[e:e + 1, :] for e in range(N_EXPERTS)]
    best_score, best_group = None, None
    for g in range(N_GROUPS):
        members = rows[g * EXPERTS_PER_GROUP:(g + 1) * EXPERTS_PER_GROUP]
        score = None
        for a in range(EXPERTS_PER_GROUP):
            for b in range(a + 1, EXPERTS_PER_GROUP):
                pair = members[a] + members[b]
                score = pair if score is None else jnp.maximum(score, pair)
        if g == 0:
            best_score, best_group = score, jnp.zeros(score.shape, F32)
        else:
            better = score > best_score
            best_group = jnp.where(better, float(g), best_group)
            best_score = jnp.where(better, score, best_score)
    e_idx = lax.broadcasted_iota(jnp.int32, p.shape, 0).astype(F32)
    g_idx = jnp.floor(e_idx * (1.0 / EXPERTS_PER_GROUP))
    masked = jnp.where(g_idx == best_group, p, -1.0)
    w1 = jnp.max(masked, axis=0, keepdims=True)
    i1 = jnp.min(jnp.where(masked == w1, e_idx, float(N_EXPERTS)), axis=0, keepdims=True)
    masked2 = jnp.where(e_idx == i1, -2.0, masked)
    w2 = jnp.max(masked2, axis=0, keepdims=True)
    i2 = jnp.min(jnp.where(masked2 == w2, e_idx, float(N_EXPERTS)), axis=0, keepdims=True)
    tot = w1 + w2
    return e_idx, i1, i2, w1 / tot, w2 / tot


ROUTE_ROWS = 8
D_PACKED = D_MODEL // 2


def _pack_bf16_pairs(xb):
    half = xb.shape[1] // 2
    lo = lax.bitcast_convert_type(xb[:, :half].astype(F32), jnp.uint32)
    hi = lax.bitcast_convert_type(xb[:, half:].astype(F32), jnp.uint32)
    return (hi & jnp.uint32(0xFFFF0000)) | (lo >> 16)


def _unpack_bf16_pairs(words):
    lo = lax.bitcast_convert_type(words << 16, F32).astype(BF16)
    hi = lax.bitcast_convert_type(words & jnp.uint32(0xFFFF0000), F32).astype(BF16)
    return lo, hi


def _moe_route_kernel(xa_ref, xb_ref, sh_ref, sc_ref, wr_ref, h_ref, rec_ref, rec_t_ref, cnt_ref, carry_scr,
                      *, tiles_a):
    @pl.when(pl.program_id(0) == 0)
    def _():
        carry_scr[...] = jnp.zeros(carry_scr.shape, F32)

    x = jnp.where(pl.program_id(0) < tiles_a, xa_ref[...], xb_ref[...])
    h = _norm_rows(x) * (1.0 + sc_ref[...]) + sh_ref[...]
    hb = h.astype(BF16)
    h_ref[...] = _pack_bf16_pairs(hb)
    logits = _dot_nt(wr_ref[...].astype(BF16), hb)
    m = jnp.max(logits, axis=0, keepdims=True)
    e = jnp.exp(logits - m)
    probs = e / jnp.sum(e, axis=0, keepdims=True)
    e_idx, i1, i2, w1, w2 = _route(probs)
    tm = probs.shape[1]
    oh1 = (e_idx == i1).astype(F32)
    oh2 = (e_idx == i2).astype(F32)
    oh = oh1 + oh2
    earlier = (lax.broadcasted_iota(jnp.int32, (tm, tm), 0) < lax.broadcasted_iota(jnp.int32, (tm, tm), 1))
    rank = carry_scr[:, 0:1] + _dot(oh.astype(BF16), earlier.astype(BF16))
    r1 = jnp.sum(oh1 * rank, axis=0, keepdims=True)
    r2 = jnp.sum(oh2 * rank, axis=0, keepdims=True)
    carry_scr[...] = carry_scr[...] + jnp.sum(oh, axis=1, keepdims=True)
    cnt_ref[...] = carry_scr[...]
    row = lax.broadcasted_iota(jnp.int32, (ROUTE_ROWS, tm), 0)
    rec = jnp.zeros((ROUTE_ROWS, tm), F32)
    for k, v in enumerate((i1, i2, r1, r2, w1, w2)):
        rec = jnp.where(row == k, v, rec)
    rec_ref[...] = rec
    pad = jnp.zeros((LANES - ROUTE_ROWS, tm), F32)
    rec_t_ref[...] = jnp.concatenate([rec, pad], axis=0).T


def _moe_route(x_a, x_b, mod_l, w_router_t, *, rows_a, rows_b):
    tm = TM_SMALL
    tiles_a = x_a.shape[0] // tm
    t = x_a.shape[0] + x_b.shape[0]
    row_a, row_b = rows_a(tm), rows_b(tm)
    row_of_tile = lambda i: jnp.where(i < tiles_a, row_a(i), row_b(i - tiles_a))
    return pl.pallas_call(
        functools.partial(_moe_route_kernel, tiles_a=tiles_a),
        out_shape=(jax.ShapeDtypeStruct((t, D_PACKED), jnp.uint32), jax.ShapeDtypeStruct((ROUTE_ROWS, t), F32),
                   jax.ShapeDtypeStruct((t, LANES), F32), jax.ShapeDtypeStruct((N_EXPERTS, LANES), F32)),
        grid=(t // tm,),
        in_specs=[
            pl.BlockSpec((tm, D_MODEL), lambda i: (jnp.minimum(i, tiles_a - 1), 0)),
            pl.BlockSpec((tm, D_MODEL), lambda i: (jnp.maximum(i - tiles_a, 0), 0)),
            _mod_spec(3, row_of_tile),
            _mod_spec(4, row_of_tile),
            pl.BlockSpec((N_EXPERTS, D_MODEL), lambda i: (0, 0)),
        ],
        out_specs=(pl.BlockSpec((tm, D_PACKED), lambda i: (i, 0)),
                   pl.BlockSpec((ROUTE_ROWS, tm), lambda i: (0, i)),
                   pl.BlockSpec((tm, LANES), lambda i: (i, 0)),
                   pl.BlockSpec((N_EXPERTS, LANES), lambda i: (0, 0))),
        scratch_shapes=[pltpu.VMEM((N_EXPERTS, LANES), F32)],
        compiler_params=_cparams(("arbitrary",)),
        name="moe_route",
    )(x_a, x_b, mod_l, mod_l, w_router_t)


ROW_DMA_UNROLL = 8
GATHER_ORDER_STRIDE = 37


def _row_copy(src_hbm, src_row, dst_buf, dst_row, sem):
    return pltpu.make_async_copy(src_hbm.at[pl.ds(src_row, 1)], dst_buf.at[pl.ds(dst_row, 1)], sem)


def _moe_expert_kernel(tile_expert_ref, n_tiles_ref, src_ref, h_hbm, wg_ref, wu_ref, wd_ref, ys_ref,
                       x_buf, sems, *, te):
    del tile_expert_ref
    j = pl.program_id(0)
    n_valid = n_tiles_ref[0]
    slot = j % 2

    def start_gather(tile, s):
        def body(r, carry):
            row = (r * GATHER_ORDER_STRIDE) % te
            _row_copy(h_hbm, src_ref[tile * te + row], x_buf.at[s], row, sems.at[s]).start()
            return carry
        lax.fori_loop(0, te, body, 0, unroll=ROW_DMA_UNROLL)

    def wait_gather(s):
        def body(r, carry):
            _row_copy(h_hbm, 0, x_buf.at[s], r, sems.at[s]).wait()
            return carry
        lax.fori_loop(0, te, body, 0, unroll=ROW_DMA_UNROLL)

    @pl.when((j == 0) & (n_valid > 0))
    def _():
        start_gather(0, 0)

    @pl.when(j + 1 < n_valid)
    def _():
        start_gather(j + 1, 1 - slot)

    @pl.when(j < n_valid)
    def _():
        wait_gather(slot)
        x_lo, x_hi = _unpack_bf16_pairs(x_buf[slot])

        def project(w_ref):
            return (_dot(x_lo, w_ref[:D_PACKED, :].astype(BF16)) + _dot(x_hi, w_ref[D_PACKED:, :].astype(BF16)))

        gate = project(wg_ref)
        up = project(wu_ref)
        hid = gate / (1.0 + jnp.exp(-gate)) * up
        ys_ref[...] = _pack_bf16_pairs(_dot(hid.astype(BF16), wd_ref[...].astype(BF16)).astype(BF16))

    @pl.when(j >= n_valid)
    def _():
        ys_ref[...] = jnp.zeros(ys_ref.shape, ys_ref.dtype)


def _moe_experts(h, tile_expert, n_tiles, src, wg, wu, wd, l, *, te):
    n_rows = src.shape[0]
    w_in_spec = pl.BlockSpec((None, None, D_MODEL, D_EXPERT), lambda j, te_ref, *_: (l, te_ref[j], 0, 0))
    w_out_spec = pl.BlockSpec((None, None, D_EXPERT, D_MODEL), lambda j, te_ref, *_: (l, te_ref[j], 0, 0))
    return pl.pallas_call(
        functools.partial(_moe_expert_kernel, te=te),
        out_shape=jax.ShapeDtypeStruct((n_rows, D_PACKED), jnp.uint32),
        grid_spec=pltpu.PrefetchScalarGridSpec(
            num_scalar_prefetch=3,
            grid=(n_rows // te,),
            in_specs=[pl.BlockSpec(memory_space=pltpu.HBM), w_in_spec, w_in_spec, w_out_spec],
            out_specs=pl.BlockSpec((te, D_PACKED), lambda j, *_: (j, 0)),
            scratch_shapes=[pltpu.VMEM((2, te, D_PACKED), h.dtype), pltpu.SemaphoreType.DMA((2,))],
        ),
        compiler_params=_cparams(("arbitrary",)),
        name="moe_experts",
    )(tile_expert, n_tiles, src, h, wg, wu, wd)


TM_COMBINE = 256


def _moe_combine_kernel(dest_ref, ys_hbm, x_ref, rec_t_ref, g_ref, gain_ref, bias_ref, o_ref, y_buf, sems,
                        *, n_tokens, token_offset):
    i = pl.program_id(0)
    tm = x_ref.shape[0]
    slot = i % 2

    def start_gather(tile, s):
        def body(r, carry):
            for k in range(2):
                row = dest_ref[k * n_tokens + token_offset + tile * tm + r]
                _row_copy(ys_hbm, row, y_buf.at[s, k], r, sems.at[s]).start()
            return carry
        lax.fori_loop(0, tm, body, 0, unroll=ROW_DMA_UNROLL)

    def wait_gather(s):
        def body(r, carry):
            for k in range(2):
                _row_copy(ys_hbm, 0, y_buf.at[s, k], r, sems.at[s]).wait()
            return carry
        lax.fori_loop(0, tm, body, 0, unroll=ROW_DMA_UNROLL)

    @pl.when(i == 0)
    def _():
        start_gather(0, 0)

    @pl.when(i + 1 < pl.num_programs(0))
    def _():
        start_gather(i + 1, 1 - slot)

    wait_gather(slot)
    w1, w2 = rec_t_ref[:, 4:5], rec_t_ref[:, 5:6]
    y1_lo, y1_hi = _unpack_bf16_pairs(y_buf[slot, 0])
    y2_lo, y2_hi = _unpack_bf16_pairs(y_buf[slot, 1])
    for half, (y1, y2) in enumerate(((y1_lo, y2_lo), (y1_hi, y2_hi))):
        cols = slice(half * D_PACKED, (half + 1) * D_PACKED)
        z = w1 * y1.astype(F32) + w2 * y2.astype(F32)
        o_ref[:, cols] = ALPHA * x_ref[:, cols] + g_ref[:, cols] * z
    o_ref[...] = _layer_norm_rows(o_ref[...], gain_ref[...], bias_ref[...])


def _moe_combine(dest, ys, x, rec_t, mod_l, ln_gain_l, ln_bias_l, *, row_of_tile, token_offset):
    t = x.shape[0]
    tm = TM_COMBINE
    first_tile = token_offset // tm
    vec = pl.BlockSpec((1, D_MODEL), lambda i, *_: (0, 0))
    return pl.pallas_call(
        functools.partial(_moe_combine_kernel, n_tokens=rec_t.shape[0], token_offset=token_offset),
        out_shape=jax.ShapeDtypeStruct((t, D_MODEL), F32),
        grid_spec=pltpu.PrefetchScalarGridSpec(
            num_scalar_prefetch=1,
            grid=(t // tm,),
            in_specs=[
                pl.BlockSpec(memory_space=pltpu.HBM),
                pl.BlockSpec((tm, D_MODEL), lambda i, *_: (i, 0)),
                pl.BlockSpec((tm, LANES), lambda i, *_: (first_tile + i, 0)),
                _mod_spec(5, row_of_tile),
                vec, vec,
            ],
            out_specs=pl.BlockSpec((tm, D_MODEL), lambda i, *_: (i, 0)),
            scratch_shapes=[pltpu.VMEM((2, 2, tm, D_PACKED), jnp.uint32), pltpu.SemaphoreType.DMA((2,))],
        ),
        compiler_params=_cparams(("arbitrary",)),
        name="moe_combine",
    )(dest, ys, x, rec_t, mod_l, ln_gain_l, ln_bias_l)


def _dispatch_tables(rec, cnt, *, te):
    t = rec.shape[1]
    n_rows = 2 * t + N_EXPERTS * te
    e12 = rec[0:2].astype(jnp.int32)
    r12 = rec[2:4].astype(jnp.int32)
    counts = cnt[:, 0].astype(jnp.int32)
    padded = (counts + te - 1) // te * te
    ends = jnp.cumsum(padded)
    offsets = ends - padded
    expert_ids = jnp.arange(N_EXPERTS, dtype=jnp.int32)[:, None, None]
    dest = (jnp.sum(jnp.where(e12[None] == expert_ids, offsets[:, None, None], 0), axis=0) + r12).reshape(2 * t)
    tokens = jnp.tile(jnp.arange(t, dtype=jnp.int32), 2)
    filler = jnp.arange(n_rows, dtype=jnp.int32) % t
    src = filler.at[dest].set(tokens, unique_indices=True)
    tile_start = jnp.arange(n_rows // te, dtype=jnp.int32) * te
    tile_expert = jnp.minimum(jnp.searchsorted(ends, tile_start, side="right"), N_EXPERTS - 1).astype(jnp.int32)
    n_tiles = (ends[-1:] // te).astype(jnp.int32)
    return dest, src, tile_expert, n_tiles


def _rope_tables(n_tokens):
    rows = n_tokens // GRID_W
    row = jnp.repeat(jnp.arange(rows), GRID_W).astype(F32)
    col = jnp.tile(jnp.arange(GRID_W), rows).astype(F32)
    quarter = HEAD_DIM // 4
    inv_freq = ROPE_THETA ** (-jnp.arange(quarter, dtype=F32) / quarter)
    ang_r, ang_c = row[:, None] * inv_freq, col[:, None] * inv_freq
    cos = jnp.concatenate([jnp.cos(ang_r), jnp.cos(ang_r), jnp.cos(ang_c), jnp.cos(ang_c)], axis=-1)
    zero = jnp.zeros_like(ang_r)
    sin = jnp.concatenate([-jnp.sin(ang_r), zero, -jnp.sin(ang_c), zero,
                           zero, jnp.sin(ang_r), zero, jnp.sin(ang_c)], axis=-1)
    return cos, sin


def _row_of_tile(first_row, tokens_per_row):
    def for_tile(tile):
        return lambda i: first_row + (i * tile) // tokens_per_row
    return for_tile


def _mixer_output(x, attn_arrays, attn_blocks, gates, mod_l, rows, l, w_branch, w_o, ln_gain, ln_bias):
    y = _merge_branches(attn_arrays, attn_blocks, gates, w_branch, l)
    return _out_projection(y, w_o, l, x, mod_l, ln_gain[l, 0:1], ln_bias[l, 0:1], row_of_tile=rows(TM))


EXPERT_TILE = 512


def _moe_ffn(x_a, x_b, mod_l, rows_a, rows_b, l, ln_gain, ln_bias, w_router_t, w_e_gate, w_e_up, w_e_down):
    h2, rec, rec_t, cnt = _moe_route(x_a, x_b, mod_l, w_router_t, rows_a=rows_a, rows_b=rows_b)
    dest, src, tile_expert, n_tiles = _dispatch_tables(rec, cnt, te=EXPERT_TILE)
    ys = _moe_experts(h2, tile_expert, n_tiles, src, w_e_gate, w_e_up, w_e_down, l, te=EXPERT_TILE)
    outs = []
    for x, rows, offset in ((x_a, rows_a, 0), (x_b, rows_b, x_a.shape[0])):
        outs.append(_moe_combine(dest, ys, x, rec_t, mod_l, ln_gain[l, 1:2], ln_bias[l, 1:2],
                                 row_of_tile=rows(TM_COMBINE), token_offset=offset))
    return outs


def kernel(x_prompt, x_sample, cache_kv_a, cache_kv_b, cache_kv_c, c, c_ctx, w_in, w_gate, w_branch, w_o,
           w_mod, b_mod, ln_gain, ln_bias, diff_lam, diff_subln, qk_gain, sink, w_router, w_e_gate, w_e_up,
           w_e_down):
    batch, seq, _ = x_prompt.shape
    dec_batch, dec_seq, _ = x_sample.shape
    past = cache_kv_a.shape[3]
    t_p, t_s = batch * seq, dec_batch * dec_seq

    cond = jnp.concatenate([c_ctx[None], c, jnp.zeros((MOD_ROWS - 1 - dec_batch, D_MODEL), F32)], axis=0)
    mod = _modulation(cond.T, w_mod, b_mod)
    cos_t, sin_t = _rope_tables(dec_seq)
    w_router_t = w_router.T
    w_in, w_gate, w_branch, w_o = (w.astype(BF16) for w in (w_in, w_gate, w_branch, w_o))
    cache_a = cache_kv_a.reshape(dec_batch, DEPTH, 2, past, 4 * 256)
    cache_b = cache_kv_b.reshape(dec_batch, DEPTH, 2, past, 2 * HEAD_DIM)
    cache_c = cache_kv_c.reshape(dec_batch, DEPTH, 2, past, 2 * HEAD_DIM)
    rows_p = _row_of_tile(0, t_p)
    rows_s = _row_of_tile(1, dec_seq)

    y_p = x_prompt.reshape(t_p, D_MODEL)
    y_s = x_sample.reshape(t_s, D_MODEL)
    projections = []
    for l in range(DEPTH):
        lam_init = 0.8 - 0.6 * math.exp(-0.3 * l)
        mod_l = mod[l].reshape(MOD_ROWS, 1, N_MOD * D_MODEL)
        subln_l = diff_subln[l].reshape(1, 256)
        shared = (w_branch, w_o, ln_gain, ln_bias)

        p = _in_projection(y_p, mod_l, w_in, l, qk_gain[l], cos_t, sin_t, row_of_tile=rows_p(TM),
                           rope=False, q_scale=1.0, out_dtype=F32)
        gates = _branch_gates(y_p, mod_l, w_gate, l, row_of_tile=rows_p(TM))
        attn = _attention_prompt(p, diff_lam[l], subln_l, sink[l], lam_init=lam_init, seq=seq)
        y_p = _mixer_output(y_p, (attn, attn, attn), (0, 1, 2), gates, mod_l, rows_p, l, *shared)
        projections.append(p.reshape(batch, seq, IN_COLS))

        qkv = _in_projection(y_s, mod_l, w_in, l, qk_gain[l], cos_t, sin_t, row_of_tile=rows_s(TM),
                             rope=True, q_scale=SCALE * LOG2E, out_dtype=BF16)
        gates = _branch_gates(y_s, mod_l, w_gate, l, row_of_tile=rows_s(TM))
        vec = lambda shape: pl.BlockSpec(shape, lambda b, qi, kk: (0, 0))
        a_o = _flash_sample(functools.partial(_diff_sample_kernel, lam_init=lam_init), qkv, cache_a, l,
                            q_col=COL_AQ, k_col=COL_AK, v_col=COL_AV, kv_width=1024, n_state=8,
                            row_sum_scratch=True, extra=(diff_lam[l], subln_l),
                            extra_specs=[vec((4, HEAD_DIM)), vec((1, 256))], name="diff_attention",
                            dec_seq=dec_seq)
        b_o = _flash_sample(_gqa_sample_kernel, qkv, cache_b, l, q_col=COL_BQ, k_col=COL_BK, v_col=COL_BV,
                            kv_width=256, n_state=8, row_sum_scratch=False, extra=(), extra_specs=[],
                            name="gqa_attention", dec_seq=dec_seq)
        c_o = _window_sample(qkv, cache_c, sink[l], l, dec_seq=dec_seq)
        y_s = _mixer_output(y_s, (a_o, b_o, c_o), (0, 0, 0), gates, mod_l, rows_s, l, *shared)

        y_p, y_s = _moe_ffn(y_p, y_s, mod_l, rows_p, rows_s, l, ln_gain, ln_bias, w_router_t, w_e_gate, w_e_up,
                            w_e_down)

    def new_cache(col_k, col_v, col_end, heads):
        parts = [p[..., c0:c1] for p in projections for c0, c1 in ((col_k, col_v), (col_v, col_end))]
        return jnp.stack(parts, axis=1).reshape(batch, DEPTH, 2, seq, heads, (col_v - col_k) // heads)

    new_kv_a = new_cache(COL_AK, COL_AV, COL_BQ, 4)
    new_kv_b = new_cache(COL_BK, COL_BV, COL_CQ, 2)
    new_kv_c = new_cache(COL_CK, COL_CV, IN_COLS, 2)
    return (y_p.reshape(batch, seq, D_MODEL), y_s.reshape(dec_batch, dec_seq, D_MODEL),
            new_kv_a, new_kv_b, new_kv_c)
```

```python
import functools
import math

import jax
import jax.numpy as jnp
from jax import lax
from jax.experimental import pallas as pl
from jax.experimental.pallas import tpu as pltpu

F32 = jnp.float32
BF16 = jnp.bfloat16

D_MODEL = 2048
HEAD_DIM = 128
GRID_W = 64
ROPE_THETA = 10000.0
WINDOW = 128
N_EXPERTS = 16
N_GROUPS = 4
EXPERTS_PER_GROUP = N_EXPERTS // N_GROUPS
D_EXPERT = 512
N_MOD = 6
DEPTH = 2
ALPHA = (2 * DEPTH) ** 0.25
EPS = 1e-6
IN_COLS = 6144
BRANCH_WIDTH = 1024
SCALE = HEAD_DIM ** -0.5
LOG2E = math.log2(math.e)
NEG_BIG = -1e30

COL_AQ, COL_AK, COL_AV = 0, 1024, 2048
COL_BQ, COL_BK, COL_BV = 3072, 4096, 4352
COL_CQ, COL_CK, COL_CV = 4608, 5632, 5888

LANES = 128
VMEM_LIMIT = 56 * 1024 * 1024

TM = 1024
TN_PROJ = 1024
TN_PROJ_F32 = 1024
TN_GATE = 1024
TM_SMALL = 512


def _cparams(sem):
    return pltpu.CompilerParams(dimension_semantics=sem, vmem_limit_bytes=VMEM_LIMIT)


def _dot(a, b):
    return jnp.dot(a, b, preferred_element_type=F32)


def _dot_nt(a, b):
    return lax.dot_general(a, b, (((1,), (1,)), ((), ())), preferred_element_type=F32)


def _norm_rows(x):
    mu = jnp.mean(x, axis=-1, keepdims=True)
    xc = x - mu
    var = jnp.mean(xc * xc, axis=-1, keepdims=True)
    return xc * lax.rsqrt(var + EPS)


def _rms(v, gain):
    ms = jnp.mean(v * v, axis=-1, keepdims=True)
    return v * lax.rsqrt(ms + EPS) * gain


def _rope(v, cos, sin_from_upper, sin_from_lower):
    return v * cos + pltpu.roll(v, 96, 1) * sin_from_upper + pltpu.roll(v, 32, 1) * sin_from_lower


N_COND = 3
MOD_ROWS = 8
TN_MOD = 1024


def _mod_kernel(cond_ref, w_ref, b_ref, o_ref):
    w = w_ref[...]
    row_idx = lax.broadcasted_iota(jnp.int32, (MOD_ROWS, TN_MOD), 0)
    out = jnp.zeros((MOD_ROWS, TN_MOD), F32)
    for r in range(N_COND):
        c = cond_ref[:, r:r + 1]
        s = c / (1.0 + jnp.exp(-c))
        m = jnp.sum(w * s, axis=0, keepdims=True) + b_ref[...]
        out = jnp.where(row_idx == r, m, out)
    o_ref[...] = out


def _modulation(cond_t, w_mod, b_mod):
    n = N_MOD * D_MODEL
    return pl.pallas_call(
        _mod_kernel,
        out_shape=jax.ShapeDtypeStruct((DEPTH, MOD_ROWS, n), F32),
        grid=(DEPTH, n // TN_MOD),
        in_specs=[
            pl.BlockSpec((D_MODEL, MOD_ROWS), lambda l, j: (0, 0)),
            pl.BlockSpec((None, D_MODEL, TN_MOD), lambda l, j: (l, 0, j)),
            pl.BlockSpec((None, 1, TN_MOD), lambda l, j: (l, 0, j)),
        ],
        out_specs=pl.BlockSpec((None, MOD_ROWS, TN_MOD), lambda l, j: (l, 0, j)),
        compiler_params=_cparams(("parallel", "parallel")),
        name="modulation",
    )(cond_t, w_mod, b_mod.reshape(DEPTH, 1, n))


def _modulate_to_scratch(x_ref, sh_ref, sc_ref, h_scr):
    h = _norm_rows(x_ref[...]) * (1.0 + sc_ref[...]) + sh_ref[...]
    h_scr[...] = h.astype(BF16)


def _head_chunk_kinds():
    kinds = []
    for n_chunks, kind in ((8, (None, True, True)), (8, (None, True, False)), (8, (None, False, False)),
                           (8, ("q", True, True)), (2, ("k", True, False)), (2, (None, False, False)),
                           (8, (None, True, True)), (2, (None, True, False)), (2, (None, False, False))):
        kinds.extend([kind] * n_chunks)
    return kinds


_HEAD_CHUNK_KINDS = _head_chunk_kinds()


def _inproj_kernel(x_ref, sh_ref, sc_ref, w_ref, qk_gain_ref, cos_ref, sin_ref, o_ref, h_scr, *, rope,
                   q_scale):
    j = pl.program_id(1)
    gain_q = qk_gain_ref[0:1, :]
    gain_k = qk_gain_ref[1:2, :]

    @pl.when(j == 0)
    def _():
        _modulate_to_scratch(x_ref, sh_ref, sc_ref, h_scr)

    acc = _dot(h_scr[...], w_ref[...])
    tn = o_ref.shape[1]
    n_chunks = tn // LANES

    def chunk(c):
        return acc[:, c * LANES:(c + 1) * LANES]

    def store(c, v):
        o_ref[:, c * LANES:(c + 1) * LANES] = v.astype(o_ref.dtype)

    def rp(v):
        return _rope(v, cos_ref[...], sin_ref[:, :HEAD_DIM], sin_ref[:, HEAD_DIM:]) if rope else v

    def qs(v):
        return v * q_scale if q_scale != 1.0 else v

    def finish(c, kind):
        norm, rotate, is_query = kind
        v = chunk(c)
        if norm is not None:
            v = _rms(v, gain_q if norm == "q" else gain_k)
        if rotate:
            v = rp(v)
        store(c, qs(v) if is_query else v)

    tile_kinds = [tuple(_HEAD_CHUNK_KINDS[t * n_chunks:(t + 1) * n_chunks]) for t in range(IN_COLS // tn)]
    for kinds in dict.fromkeys(tile_kinds):
        tiles = [t for t, k in enumerate(tile_kinds) if k == kinds]
        cond = j == tiles[0]
        for t in tiles[1:]:
            cond = cond | (j == t)

        @pl.when(cond)
        def _(kinds=kinds):
            for c, kind in enumerate(kinds):
                finish(c, kind)


def _gate_kernel(x_ref, sh_ref, sc_ref, w_ref, o_ref, h_scr):
    @pl.when(pl.program_id(1) == 0)
    def _():
        _modulate_to_scratch(x_ref, sh_ref, sc_ref, h_scr)

    acc = _dot(h_scr[...], w_ref[...])
    o_ref[...] = (1.0 / (1.0 + jnp.exp(-acc))).astype(o_ref.dtype)


def _mod_spec(which, row_of_tile):
    return pl.BlockSpec((None, 1, D_MODEL), lambda i, *_: (row_of_tile(i), 0, which))


def _in_projection(x, mod_l, w_in, l, qk_gain_l, cos_t, sin_t, *, row_of_tile, rope, q_scale, out_dtype):
    t = x.shape[0]
    tiles_per_seq = cos_t.shape[0] // TM
    tn = TN_PROJ_F32 if out_dtype == F32 else TN_PROJ
    return pl.pallas_call(
        functools.partial(_inproj_kernel, rope=rope, q_scale=q_scale),
        out_shape=jax.ShapeDtypeStruct((t, IN_COLS), out_dtype),
        grid=(t // TM, IN_COLS // tn),
        in_specs=[
            pl.BlockSpec((TM, D_MODEL), lambda i, j: (i, 0)),
            _mod_spec(0, row_of_tile),
            _mod_spec(1, row_of_tile),
            pl.BlockSpec((None, D_MODEL, tn), lambda i, j: (l, 0, j)),
            pl.BlockSpec((2, HEAD_DIM), lambda i, j: (0, 0)),
            pl.BlockSpec((TM, HEAD_DIM), lambda i, j: (i % tiles_per_seq, 0)),
            pl.BlockSpec((TM, 2 * HEAD_DIM), lambda i, j: (i % tiles_per_seq, 0)),
        ],
        out_specs=pl.BlockSpec((TM, tn), lambda i, j: (i, j)),
        scratch_shapes=[pltpu.VMEM((TM, D_MODEL), BF16)],
        compiler_params=_cparams(("parallel", "arbitrary")),
        name="in_projection",
    )(x, mod_l, mod_l, w_in, qk_gain_l, cos_t, sin_t)


def _branch_gates(x, mod_l, w_gate, l, *, row_of_tile):
    t = x.shape[0]
    n = w_gate.shape[2]
    return pl.pallas_call(
        _gate_kernel,
        out_shape=jax.ShapeDtypeStruct((t, n), BF16),
        grid=(t // TM, n // TN_GATE),
        in_specs=[
            pl.BlockSpec((TM, D_MODEL), lambda i, j: (i, 0)),
            _mod_spec(0, row_of_tile),
            _mod_spec(1, row_of_tile),
            pl.BlockSpec((None, D_MODEL, TN_GATE), lambda i, j: (l, 0, j)),
        ],
        out_specs=pl.BlockSpec((TM, TN_GATE), lambda i, j: (i, j)),
        scratch_shapes=[pltpu.VMEM((TM, D_MODEL), BF16)],
        compiler_params=_cparams(("parallel", "arbitrary")),
        name="branch_gates",
    )(x, mod_l, mod_l, w_gate)


def _diff_lambda(lam_ref, lam_init):
    lp = lam_ref[...]
    t1 = jnp.sum(lp[0:1] * lp[1:2], axis=-1, keepdims=True)
    t2 = jnp.sum(lp[2:3] * lp[3:4], axis=-1, keepdims=True)
    return jnp.exp(t1) - jnp.exp(t2) + lam_init


def _softmax_rows(s, sink=None):
    m = jnp.max(s, axis=-1, keepdims=True)
    if sink is not None:
        m = jnp.maximum(m, sink)
    e = jnp.exp(s - m)
    den = jnp.sum(e, axis=-1, keepdims=True)
    if sink is not None:
        den = den + jnp.exp(sink - m)
    return e * (1.0 / den)


def _attn_prompt_kernel(p_ref, lam_ref, subln_ref, sink_ref, o_ref, *, lam_init):
    lam = _diff_lambda(lam_ref, lam_init)

    def blk(c0, w):
        return p_ref[:, c0:c0 + w].astype(BF16)

    for h in range(4):
        probs = []
        for m in range(2):
            q = blk(COL_AQ + h * 256 + m * HEAD_DIM, HEAD_DIM)
            k = blk(COL_AK + h * 256 + m * HEAD_DIM, HEAD_DIM)
            probs.append(_softmax_rows(_dot_nt(q, k) * SCALE))
        w = probs[0] - lam * probs[1]
        o = _dot(w.astype(BF16), blk(COL_AV + h * 256, 256))
        o = _rms(o, subln_ref[...]) * (1.0 - lam_init)
        o_ref[:, h * 256:(h + 1) * 256] = o.astype(o_ref.dtype)

    for mixer, (cq, ck, cv) in enumerate(((COL_BQ, COL_BK, COL_BV), (COL_CQ, COL_CK, COL_CV))):
        for kvh in range(2):
            k = blk(ck + kvh * HEAD_DIM, HEAD_DIM)
            v = blk(cv + kvh * HEAD_DIM, HEAD_DIM)
            for g in range(4):
                hq = kvh * 4 + g
                q = blk(cq + hq * HEAD_DIM, HEAD_DIM)
                sink = sink_ref[hq] if mixer == 1 else None
                p = _softmax_rows(_dot_nt(q, k) * SCALE, sink)
                o = _dot(p.astype(BF16), v)
                c0 = BRANCH_WIDTH * (1 + mixer) + hq * HEAD_DIM
                o_ref[:, c0:c0 + HEAD_DIM] = o.astype(o_ref.dtype)


def _attention_prompt(p, diff_lam_l, subln_l, sink_l, *, lam_init, seq):
    t = p.shape[0]
    return pl.pallas_call(
        functools.partial(_attn_prompt_kernel, lam_init=lam_init),
        out_shape=jax.ShapeDtypeStruct((t, 3 * BRANCH_WIDTH), BF16),
        grid=(t // seq,),
        in_specs=[
            pl.BlockSpec((seq, IN_COLS), lambda b: (b, 0)),
            pl.BlockSpec((4, HEAD_DIM), lambda b: (0, 0)),
            pl.BlockSpec((1, 256), lambda b: (0, 0)),
            pl.BlockSpec(memory_space=pltpu.SMEM),
        ],
        out_specs=pl.BlockSpec((seq, 3 * BRANCH_WIDTH), lambda b: (b, 0)),
        compiler_params=_cparams(("parallel",)),
        name="attention_prompt",
    )(p, diff_lam_l, subln_l, sink_l)


TQ = 512
TQ_FLASH = 1024
TK = 1024


def _tile_lanes(v, n):
    return jnp.concatenate([v] * n, axis=-1) if n > 1 else v


def _online_softmax(idx, s, m_scr):
    m_prev = m_scr[idx]
    m_new = jnp.maximum(m_prev, jnp.max(s, axis=-1, keepdims=True))
    m_scr[idx] = m_new
    alpha = jnp.exp2(m_prev - m_new)
    p = jnp.exp2(s - _tile_lanes(m_new, s.shape[1] // LANES))
    return p, alpha


def _diff_sample_kernel(q_ref, k_ref, v_ref, kc_ref, vc_ref, lam_ref, subln_ref, o_ref,
                        m_scr, l_scr, acc_scr, *, lam_init):
    kk = pl.program_id(2)

    @pl.when(kk == 0)
    def _():
        m_scr[...] = jnp.full(m_scr.shape, NEG_BIG, F32)
        l_scr[...] = jnp.zeros(l_scr.shape, F32)
        acc_scr[...] = jnp.zeros(acc_scr.shape, F32)

    def process(kb_ref, vb_ref):
        for h in range(4):
            v = vb_ref[:, h * 256:(h + 1) * 256].astype(BF16)
            for m in range(2):
                c0 = h * 256 + m * HEAD_DIM
                idx = h * 2 + m
                k = kb_ref[:, c0:c0 + HEAD_DIM].astype(BF16)
                p, alpha = _online_softmax(idx, _dot_nt(q_ref[:, c0:c0 + HEAD_DIM], k), m_scr)
                part = p[:, 0:LANES]
                for c in range(1, p.shape[1] // LANES):
                    part = part + p[:, c * LANES:(c + 1) * LANES]
                l_scr[idx] = alpha * l_scr[idx] + part
                acc_scr[idx] = _tile_lanes(alpha, 2) * acc_scr[idx] + _dot(p.astype(BF16), v)

    @pl.when(kk == 0)
    def _():
        process(kc_ref, vc_ref)

    @pl.when(kk > 0)
    def _():
        process(k_ref, v_ref)

    @pl.when(kk == pl.num_programs(2) - 1)
    def _():
        lam = _diff_lambda(lam_ref, lam_init)
        for h in range(4):
            l1 = jnp.sum(l_scr[2 * h], axis=-1, keepdims=True)
            l2 = jnp.sum(l_scr[2 * h + 1], axis=-1, keepdims=True)
            o1 = acc_scr[2 * h] * (1.0 / l1)
            o2 = acc_scr[2 * h + 1] * (1.0 / l2)
            o = _rms(o1 - lam * o2, subln_ref[...]) * (1.0 - lam_init)
            o_ref[:, h * 256:(h + 1) * 256] = o.astype(o_ref.dtype)


def _gqa_sample_kernel(q_ref, k_ref, v_ref, kc_ref, vc_ref, o_ref, m_scr, acc_scr):
    kk = pl.program_id(2)

    @pl.when(kk == 0)
    def _():
        m_scr[...] = jnp.full(m_scr.shape, NEG_BIG, F32)
        acc_scr[...] = jnp.zeros(acc_scr.shape, F32)

    def process(kb_ref, vb_ref):
        for kvh in range(2):
            k = kb_ref[:, kvh * HEAD_DIM:(kvh + 1) * HEAD_DIM].astype(BF16)
            v = vb_ref[:, kvh * HEAD_DIM:(kvh + 1) * HEAD_DIM].astype(BF16)
            v_ones = jnp.concatenate([v, jnp.ones_like(v)], axis=-1)
            for g in range(4):
                hq = kvh * 4 + g
                s = _dot_nt(q_ref[:, hq * HEAD_DIM:(hq + 1) * HEAD_DIM], k)
                p, alpha = _online_softmax(hq, s, m_scr)
                acc_scr[hq] = _tile_lanes(alpha, 2) * acc_scr[hq] + _dot(p.astype(BF16), v_ones)

    @pl.when(kk == 0)
    def _():
        process(kc_ref, vc_ref)

    @pl.when(kk > 0)
    def _():
        process(k_ref, v_ref)

    @pl.when(kk == pl.num_programs(2) - 1)
    def _():
        for hq in range(8):
            o = acc_scr[hq, :, 0:HEAD_DIM] / acc_scr[hq, :, HEAD_DIM:2 * HEAD_DIM]
            o_ref[:, hq * HEAD_DIM:(hq + 1) * HEAD_DIM] = o.astype(o_ref.dtype)


def _flash_sample(kernel, qkv, cache, l, *, q_col, k_col, v_col, kv_width, n_state, row_sum_scratch, extra,
                  extra_specs, name, dec_seq):
    t = qkv.shape[0]
    nb = t // dec_seq
    tq = TQ_FLASH
    nq = dec_seq // tq
    nk = dec_seq // TK
    past = cache.shape[3]
    q_blk, k_blk, v_blk = q_col // BRANCH_WIDTH, k_col // kv_width, v_col // kv_width
    kv_row = lambda b, qi, kk: b * nk + jnp.maximum(kk - 1, 0)
    return pl.pallas_call(
        kernel,
        out_shape=jax.ShapeDtypeStruct((t, BRANCH_WIDTH), BF16),
        grid=(nb, nq, nk + 1),
        in_specs=[
            pl.BlockSpec((tq, BRANCH_WIDTH), lambda b, qi, kk: (b * nq + qi, q_blk)),
            pl.BlockSpec((TK, kv_width), lambda b, qi, kk: (kv_row(b, qi, kk), k_blk)),
            pl.BlockSpec((TK, kv_width), lambda b, qi, kk: (kv_row(b, qi, kk), v_blk)),
            pl.BlockSpec((None, None, None, past, kv_width), lambda b, qi, kk: (b, l, 0, 0, 0)),
            pl.BlockSpec((None, None, None, past, kv_width), lambda b, qi, kk: (b, l, 1, 0, 0)),
        ] + extra_specs,
        out_specs=pl.BlockSpec((tq, BRANCH_WIDTH), lambda b, qi, kk: (b * nq + qi, 0)),
        scratch_shapes=[pltpu.VMEM((n_state, tq, LANES), F32)] * (2 if row_sum_scratch else 1)
        + [pltpu.VMEM((n_state, tq, 2 * LANES), F32)],
        compiler_params=_cparams(("parallel", "parallel", "arbitrary")),
        name=name,
    )(qkv, qkv, qkv, cache, cache, *extra)


def _window_sample_kernel(q0_ref, q1_ref, kp_ref, kc_ref, kn_ref, vp_ref, vc_ref, vn_ref, kctx_ref, vctx_ref,
                          sink_ref, o_ref, *, dec_seq):
    qi = pl.program_id(1)
    q_start = qi * TQ
    side = WINDOW
    q_pos = q_start + lax.broadcasted_iota(jnp.int32, (TQ, 1), 0)
    segs = ((kp_ref, vp_ref, q_start - side, side), (kc_ref, vc_ref, q_start, TQ),
            (kn_ref, vn_ref, q_start + TQ, side))
    valid = []
    for _, _, start, n in segs:
        k_pos = start + lax.broadcasted_iota(jnp.int32, (TQ, n), 1)
        valid.append((jnp.abs(q_pos - k_pos) <= WINDOW) & (k_pos >= 0) & (k_pos < dec_seq))
    for kvh in range(2):
        q_ref = q0_ref if kvh == 0 else q1_ref
        lo, hi = kvh * HEAD_DIM, (kvh + 1) * HEAD_DIM
        kctx = kctx_ref[:, lo:hi].astype(BF16)
        vctx = vctx_ref[:, lo:hi].astype(BF16)
        for g in range(4):
            hq = kvh * 4 + g
            q = q_ref[:, g * HEAD_DIM:(g + 1) * HEAD_DIM]
            sink = sink_ref[hq] * LOG2E
            scores = [_dot_nt(q, kctx)]
            for (k_ref, _, _, _), ok in zip(segs, valid):
                scores.append(jnp.where(ok, _dot_nt(q, k_ref[:, lo:hi]), NEG_BIG))
            m = jnp.maximum(scores[0].max(axis=-1, keepdims=True), sink)
            for s in scores[1:]:
                m = jnp.maximum(m, s.max(axis=-1, keepdims=True))
            es = [jnp.exp2(s - m) for s in scores]
            den = jnp.exp2(sink - m)
            for e in es:
                den = den + jnp.sum(e, axis=-1, keepdims=True)
            o = _dot(es[0].astype(BF16), vctx)
            for e, (_, v_ref, _, _) in zip(es[1:], segs):
                o = o + _dot(e.astype(BF16), v_ref[:, lo:hi])
            o = o * (1.0 / den)
            o_ref[:, hq * HEAD_DIM:(hq + 1) * HEAD_DIM] = o.astype(o_ref.dtype)


def _window_sample(qkv, cache, sink_l, l, *, dec_seq):
    t = qkv.shape[0]
    nb = t // dec_seq
    nq = dec_seq // TQ
    half = WINDOW
    n_half = dec_seq // half
    per_tile = TQ // half
    past = cache.shape[3]
    kvw = 2 * HEAD_DIM
    q_blk = COL_CQ // 512
    k_blk, v_blk = COL_CK // kvw, COL_CV // kvw
    prev_row = lambda b, qi: b * n_half + jnp.maximum(per_tile * qi - 1, 0)
    next_row = lambda b, qi: b * n_half + jnp.minimum(per_tile * (qi + 1), n_half - 1)
    return pl.pallas_call(
        functools.partial(_window_sample_kernel, dec_seq=dec_seq),
        out_shape=jax.ShapeDtypeStruct((t, BRANCH_WIDTH), BF16),
        grid=(nb, nq),
        in_specs=[
            pl.BlockSpec((TQ, 512), lambda b, qi: (b * nq + qi, q_blk)),
            pl.BlockSpec((TQ, 512), lambda b, qi: (b * nq + qi, q_blk + 1)),
            pl.BlockSpec((half, kvw), lambda b, qi: (prev_row(b, qi), k_blk)),
            pl.BlockSpec((TQ, kvw), lambda b, qi: (b * nq + qi, k_blk)),
            pl.BlockSpec((half, kvw), lambda b, qi: (next_row(b, qi), k_blk)),
            pl.BlockSpec((half, kvw), lambda b, qi: (prev_row(b, qi), v_blk)),
            pl.BlockSpec((TQ, kvw), lambda b, qi: (b * nq + qi, v_blk)),
            pl.BlockSpec((half, kvw), lambda b, qi: (next_row(b, qi), v_blk)),
            pl.BlockSpec((None, None, None, past, kvw), lambda b, qi: (b, l, 0, 0, 0)),
            pl.BlockSpec((None, None, None, past, kvw), lambda b, qi: (b, l, 1, 0, 0)),
            pl.BlockSpec(memory_space=pltpu.SMEM),
        ],
        out_specs=pl.BlockSpec((TQ, BRANCH_WIDTH), lambda b, qi: (b * nq + qi, 0)),
        compiler_params=_cparams(("parallel", "parallel")),
        name="window_attention",
    )(qkv, qkv, qkv, qkv, qkv, qkv, qkv, qkv, cache, cache, sink_l)


def _merge_kernel(oa_ref, ob_ref, oc_ref, g_ref, w_ref, y_ref, acc_scr):
    r = pl.program_id(2)

    def contrib(o_ref):
        return g_ref[...].astype(F32) * _dot(o_ref[...], w_ref[...])

    @pl.when(r == 0)
    def _():
        acc_scr[...] = contrib(oa_ref)

    @pl.when(r == 1)
    def _():
        acc_scr[...] += contrib(ob_ref)

    @pl.when(r == 2)
    def _():
        y_ref[...] = (acc_scr[...] + contrib(oc_ref)).astype(y_ref.dtype)


def _merge_branches(o_arrays, o_blocks, gates, w_branch, l):
    t = gates.shape[0]
    tn = D_MODEL
    nn = D_MODEL // tn
    o_specs = [pl.BlockSpec((TM, BRANCH_WIDTH), functools.partial(lambda i, n, r, blk: (i, blk), blk=blk))
               for blk in o_blocks]
    return pl.pallas_call(
        _merge_kernel,
        out_shape=jax.ShapeDtypeStruct((t, D_MODEL), BF16),
        grid=(t // TM, nn, 3),
        in_specs=o_specs + [
            pl.BlockSpec((TM, tn), lambda i, n, r: (i, r * nn + n)),
            pl.BlockSpec((None, None, BRANCH_WIDTH, tn), lambda i, n, r: (l, r, 0, n)),
        ],
        out_specs=pl.BlockSpec((TM, tn), lambda i, n, r: (i, n)),
        scratch_shapes=[pltpu.VMEM((TM, tn), F32)],
        compiler_params=_cparams(("parallel", "parallel", "arbitrary")),
        name="merge_branches",
    )(*o_arrays, gates, w_branch)


def _layer_norm_rows(v, gain, bias):
    return _norm_rows(v) * gain + bias


def _outproj_kernel(y_ref, w_ref, x_ref, g_ref, gain_ref, bias_ref, o_ref):
    n = pl.program_id(1)
    tn = w_ref.shape[1]
    n_blocks = o_ref.shape[1] // tn
    z = _dot(y_ref[...], w_ref[...])
    for b in range(n_blocks):
        @pl.when(n == b)
        def _(b=b):
            cols = slice(b * tn, (b + 1) * tn)
            o_ref[:, cols] = ALPHA * x_ref[:, cols] + g_ref[:, cols] * z

    @pl.when(n == n_blocks - 1)
    def _():
        o_ref[...] = _layer_norm_rows(o_ref[...], gain_ref[...], bias_ref[...])


def _out_projection(y, w_o, l, x, mod_l, ln_gain_l, ln_bias_l, *, row_of_tile):
    t = x.shape[0]
    tn = 512
    return pl.pallas_call(
        _outproj_kernel,
        out_shape=jax.ShapeDtypeStruct((t, D_MODEL), F32),
        grid=(t // TM, D_MODEL // tn),
        in_specs=[
            pl.BlockSpec((TM, D_MODEL), lambda i, n: (i, 0)),
            pl.BlockSpec((None, D_MODEL, tn), lambda i, n: (l, 0, n)),
            pl.BlockSpec((TM, D_MODEL), lambda i, n: (i, 0)),
            _mod_spec(2, row_of_tile),
            pl.BlockSpec((1, D_MODEL), lambda i, n: (0, 0)),
            pl.BlockSpec((1, D_MODEL), lambda i, n: (0, 0)),
        ],
        out_specs=pl.BlockSpec((TM, D_MODEL), lambda i, n: (i, 0)),
        compiler_params=_cparams(("parallel", "arbitrary")),
        name="out_projection",
    )(y, w_o, x, mod_l, ln_gain_l, ln_bias_l)


def _route(p):
    rows = [p[e:e + 1, :] for e in range(N_EXPERTS)]
    best_score, best_group = None, None
    for g in range(N_GROUPS):
        members = rows[g * EXPERTS_PER_GROUP:(g + 1) * EXPERTS_PER_GROUP]
        score = None
        for a in range(EXPERTS_PER_GROUP):
            for b in range(a + 1, EXPERTS_PER_GROUP):
                pair = members[a] + members[b]
                score = pair if score is None else jnp.maximum(score, pair)
        if g == 0:
            best_score, best_group = score, jnp.zeros(score.shape, F32)
        else:
            better = score > best_score
            best_group = jnp.where(better, float(g), best_group)
            best_score = jnp.where(better, score, best_score)
    e_idx = lax.broadcasted_iota(jnp.int32, p.shape, 0).astype(F32)
    g_idx = jnp.floor(e_idx * (1.0 / EXPERTS_PER_GROUP))
    masked = jnp.where(g_idx == best_group, p, -1.0)
    w1 = jnp.max(masked, axis=0, keepdims=True)
    i1 = jnp.min(jnp.where(masked == w1, e_idx, float(N_EXPERTS)), axis=0, keepdims=True)
    masked2 = jnp.where(e_idx == i1, -2.0, masked)
    w2 = jnp.max(masked2, axis=0, keepdims=True)
    i2 = jnp.min(jnp.where(masked2 == w2, e_idx, float(N_EXPERTS)), axis=0, keepdims=True)
    tot = w1 + w2
    return e_idx, i1, i2, w1 / tot, w2 / tot


ROUTE_ROWS = 8
D_PACKED = D_MODEL // 2


def _pack_bf16_pairs(xb):
    half = xb.shape[1] // 2
    lo = lax.bitcast_convert_type(xb[:, :half].astype(F32), jnp.uint32)
    hi = lax.bitcast_convert_type(xb[:, half:].astype(F32), jnp.uint32)
    return (hi & jnp.uint32(0xFFFF0000)) | (lo >> 16)


def _unpack_bf16_pairs(words):
    lo = lax.bitcast_convert_type(words << 16, F32).astype(BF16)
    hi = lax.bitcast_convert_type(words & jnp.uint32(0xFFFF0000), F32).astype(BF16)
    return lo, hi


def _moe_route_kernel(xa_ref, xb_ref, sh_ref, sc_ref, wr_ref, h_ref, rec_ref, rec_t_ref, cnt_ref, carry_scr,
                      *, tiles_a):
    @pl.when(pl.program_id(0) == 0)
    def _():
        carry_scr[...] = jnp.zeros(carry_scr.shape, F32)

    x = jnp.where(pl.program_id(0) < tiles_a, xa_ref[...], xb_ref[...])
    h = _norm_rows(x) * (1.0 + sc_ref[...]) + sh_ref[...]
    hb = h.astype(BF16)
    h_ref[...] = _pack_bf16_pairs(hb)
    logits = _dot_nt(wr_ref[...].astype(BF16), hb)
    m = jnp.max(logits, axis=0, keepdims=True)
    e = jnp.exp(logits - m)
    probs = e / jnp.sum(e, axis=0, keepdims=True)
    e_idx, i1, i2, w1, w2 = _route(probs)
    tm = probs.shape[1]
    oh1 = (e_idx == i1).astype(F32)
    oh2 = (e_idx == i2).astype(F32)
    oh = oh1 + oh2
    earlier = (lax.broadcasted_iota(jnp.int32, (tm, tm), 0) < lax.broadcasted_iota(jnp.int32, (tm, tm), 1))
    rank = carry_scr[:, 0:1] + _dot(oh.astype(BF16), earlier.astype(BF16))
    r1 = jnp.sum(oh1 * rank, axis=0, keepdims=True)
    r2 = jnp.sum(oh2 * rank, axis=0, keepdims=True)
    carry_scr[...] = carry_scr[...] + jnp.sum(oh, axis=1, keepdims=True)
    cnt_ref[...] = carry_scr[...]
    row = lax.broadcasted_iota(jnp.int32, (ROUTE_ROWS, tm), 0)
    rec = jnp.zeros((ROUTE_ROWS, tm), F32)
    for k, v in enumerate((i1, i2, r1, r2, w1, w2)):
        rec = jnp.where(row == k, v, rec)
    rec_ref[...] = rec
    pad = jnp.zeros((LANES - ROUTE_ROWS, tm), F32)
    rec_t_ref[...] = jnp.concatenate([rec, pad], axis=0).T


def _moe_route(x_a, x_b, mod_l, w_router_t, *, rows_a, rows_b):
    tm = TM_SMALL
    tiles_a = x_a.shape[0] // tm
    t = x_a.shape[0] + x_b.shape[0]
    row_a, row_b = rows_a(tm), rows_b(tm)
    row_of_tile = lambda i: jnp.where(i < tiles_a, row_a(i), row_b(i - tiles_a))
    return pl.pallas_call(
        functools.partial(_moe_route_kernel, tiles_a=tiles_a),
        out_shape=(jax.ShapeDtypeStruct((t, D_PACKED), jnp.uint32), jax.ShapeDtypeStruct((ROUTE_ROWS, t), F32),
                   jax.ShapeDtypeStruct((t, LANES), F32), jax.ShapeDtypeStruct((N_EXPERTS, LANES), F32)),
        grid=(t // tm,),
        in_specs=[
            pl.BlockSpec((tm, D_MODEL), lambda i: (jnp.minimum(i, tiles_a - 1), 0)),
            pl.BlockSpec((tm, D_MODEL), lambda i: (jnp.maximum(i - tiles_a, 0), 0)),
            _mod_spec(3, row_of_tile),
            _mod_spec(4, row_of_tile),
            pl.BlockSpec((N_EXPERTS, D_MODEL), lambda i: (0, 0)),
        ],
        out_specs=(pl.BlockSpec((tm, D_PACKED), lambda i: (i, 0)),
                   pl.BlockSpec((ROUTE_ROWS, tm), lambda i: (0, i)),
                   pl.BlockSpec((tm, LANES), lambda i: (i, 0)),
                   pl.BlockSpec((N_EXPERTS, LANES), lambda i: (0, 0))),
        scratch_shapes=[pltpu.VMEM((N_EXPERTS, LANES), F32)],
        compiler_params=_cparams(("arbitrary",)),
        name="moe_route",
    )(x_a, x_b, mod_l, mod_l, w_router_t)


ROW_DMA_UNROLL = 16
GATHER_ORDER_STRIDE = 37


def _row_copy(src_hbm, src_row, dst_buf, dst_row, sem):
    return pltpu.make_async_copy(src_hbm.at[pl.ds(src_row, 1)], dst_buf.at[pl.ds(dst_row, 1)], sem)


def _moe_expert_kernel(tile_expert_ref, n_tiles_ref, src_ref, h_hbm, wg_ref, wu_ref, wd_ref, ys_ref,
                       x_buf, sems, *, te):
    del tile_expert_ref
    j = pl.program_id(0)
    n_valid = n_tiles_ref[0]
    slot = j % 2

    def start_gather(tile, s):
        def body(r, carry):
            row = (r * GATHER_ORDER_STRIDE) % te
            _row_copy(h_hbm, src_ref[tile * te + row], x_buf.at[s], row, sems.at[s]).start()
            return carry
        lax.fori_loop(0, te, body, 0, unroll=ROW_DMA_UNROLL)

    def wait_gather(s):
        def body(r, carry):
            _row_copy(h_hbm, 0, x_buf.at[s], r, sems.at[s]).wait()
            return carry
        lax.fori_loop(0, te, body, 0, unroll=ROW_DMA_UNROLL)

    @pl.when((j == 0) & (n_valid > 0))
    def _():
        start_gather(0, 0)

    @pl.when(j + 1 < n_valid)
    def _():
        start_gather(j + 1, 1 - slot)

    @pl.when(j < n_valid)
    def _():
        wait_gather(slot)
        x_lo, x_hi = _unpack_bf16_pairs(x_buf[slot])

        def project(w_ref):
            return (_dot(x_lo, w_ref[:D_PACKED, :].astype(BF16)) + _dot(x_hi, w_ref[D_PACKED:, :].astype(BF16)))

        gate = project(wg_ref)
        up = project(wu_ref)
        hid = gate / (1.0 + jnp.exp(-gate)) * up
        ys_ref[...] = _pack_bf16_pairs(_dot(hid.astype(BF16), wd_ref[...].astype(BF16)).astype(BF16))

    @pl.when(j >= n_valid)
    def _():
        ys_ref[...] = jnp.zeros(ys_ref.shape, ys_ref.dtype)


def _moe_experts(h, tile_expert, n_tiles, src, wg, wu, wd, l, *, te):
    n_rows = src.shape[0]
    w_in_spec = pl.BlockSpec((None, None, D_MODEL, D_EXPERT), lambda j, te_ref, *_: (l, te_ref[j], 0, 0))
    w_out_spec = pl.BlockSpec((None, None, D_EXPERT, D_MODEL), lambda j, te_ref, *_: (l, te_ref[j], 0, 0))
    return pl.pallas_call(
        functools.partial(_moe_expert_kernel, te=te),
        out_shape=jax.ShapeDtypeStruct((n_rows, D_PACKED), jnp.uint32),
        grid_spec=pltpu.PrefetchScalarGridSpec(
            num_scalar_prefetch=3,
            grid=(n_rows // te,),
            in_specs=[pl.BlockSpec(memory_space=pltpu.HBM), w_in_spec, w_in_spec, w_out_spec],
            out_specs=pl.BlockSpec((te, D_PACKED), lambda j, *_: (j, 0)),
            scratch_shapes=[pltpu.VMEM((2, te, D_PACKED), h.dtype), pltpu.SemaphoreType.DMA((2,))],
        ),
        compiler_params=_cparams(("arbitrary",)),
        name="moe_experts",
    )(tile_expert, n_tiles, src, h, wg, wu, wd)


TM_COMBINE = 256


def _moe_combine_kernel(dest_ref, ys_hbm, x_ref, rec_t_ref, g_ref, gain_ref, bias_ref, o_ref, y_buf, sems,
                        *, n_tokens, token_offset):
    i = pl.program_id(0)
    tm = x_ref.shape[0]
    slot = i % 2

    def start_gather(tile, s):
        def body(r, carry):
            for k in range(2):
                row = dest_ref[k * n_tokens + token_offset + tile * tm + r]
                _row_copy(ys_hbm, row, y_buf.at[s, k], r, sems.at[s]).start()
            return carry
        lax.fori_loop(0, tm, body, 0, unroll=ROW_DMA_UNROLL)

    def wait_gather(s):
        def body(r, carry):
            for k in range(2):
                _row_copy(ys_hbm, 0, y_buf.at[s, k], r, sems.at[s]).wait()
            return carry
        lax.fori_loop(0, tm, body, 0, unroll=ROW_DMA_UNROLL)

    @pl.when(i == 0)
    def _():
        start_gather(0, 0)

    @pl.when(i + 1 < pl.num_programs(0))
    def _():
        start_gather(i + 1, 1 - slot)

    wait_gather(slot)
    w1, w2 = rec_t_ref[:, 4:5], rec_t_ref[:, 5:6]
    y1_lo, y1_hi = _unpack_bf16_pairs(y_buf[slot, 0])
    y2_lo, y2_hi = _unpack_bf16_pairs(y_buf[slot, 1])
    for half, (y1, y2) in enumerate(((y1_lo, y2_lo), (y1_hi, y2_hi))):
        cols = slice(half * D_PACKED, (half + 1) * D_PACKED)
        z = w1 * y1.astype(F32) + w2 * y2.astype(F32)
        o_ref[:, cols] = ALPHA * x_ref[:, cols] + g_ref[:, cols] * z
    o_ref[...] = _layer_norm_rows(o_ref[...], gain_ref[...], bias_ref[...])


def _moe_combine(dest, ys, x, rec_t, mod_l, ln_gain_l, ln_bias_l, *, row_of_tile, token_offset):
    t = x.shape[0]
    tm = TM_COMBINE
    first_tile = token_offset // tm
    vec = pl.BlockSpec((1, D_MODEL), lambda i, *_: (0, 0))
    return pl.pallas_call(
        functools.partial(_moe_combine_kernel, n_tokens=rec_t.shape[0], token_offset=token_offset),
        out_shape=jax.ShapeDtypeStruct((t, D_MODEL), F32),
        grid_spec=pltpu.PrefetchScalarGridSpec(
            num_scalar_prefetch=1,
            grid=(t // tm,),
            in_specs=[
                pl.BlockSpec(memory_space=pltpu.HBM),
                pl.BlockSpec((tm, D_MODEL), lambda i, *_: (i, 0)),
                pl.BlockSpec((tm, LANES), lambda i, *_: (first_tile + i, 0)),
                _mod_spec(5, row_of_tile),
                vec, vec,
            ],
            out_specs=pl.BlockSpec((tm, D_MODEL), lambda i, *_: (i, 0)),
            scratch_shapes=[pltpu.VMEM((2, 2, tm, D_PACKED), jnp.uint32), pltpu.SemaphoreType.DMA((2,))],
        ),
        compiler_params=_cparams(("arbitrary",)),
        name="moe_combine",
    )(dest, ys, x, rec_t, mod_l, ln_gain_l, ln_bias_l)


def _dispatch_tables(rec, cnt, *, te):
    t = rec.shape[1]
    n_rows = 2 * t + N_EXPERTS * te
    e12 = rec[0:2].astype(jnp.int32)
    r12 = rec[2:4].astype(jnp.int32)
    counts = cnt[:, 0].astype(jnp.int32)
    padded = (counts + te - 1) // te * te
    ends = jnp.cumsum(padded)
    offsets = ends - padded
    expert_ids = jnp.arange(N_EXPERTS, dtype=jnp.int32)[:, None, None]
    dest = (jnp.sum(jnp.where(e12[None] == expert_ids, offsets[:, None, None], 0), axis=0) + r12).reshape(2 * t)
    tokens = jnp.tile(jnp.arange(t, dtype=jnp.int32), 2)
    filler = jnp.arange(n_rows, dtype=jnp.int32) % t
    src = filler.at[dest].set(tokens, unique_indices=True)
    tile_start = jnp.arange(n_rows // te, dtype=jnp.int32) * te
    tile_expert = jnp.minimum(jnp.searchsorted(ends, tile_start, side="right"), N_EXPERTS - 1).astype(jnp.int32)
    n_tiles = (ends[-1:] // te).astype(jnp.int32)
    return dest, src, tile_expert, n_tiles


def _rope_tables(n_tokens):
    rows = n_tokens // GRID_W
    row = jnp.repeat(jnp.arange(rows), GRID_W).astype(F32)
    col = jnp.tile(jnp.arange(GRID_W), rows).astype(F32)
    quarter = HEAD_DIM // 4
    inv_freq = ROPE_THETA ** (-jnp.arange(quarter, dtype=F32) / quarter)
    ang_r, ang_c = row[:, None] * inv_freq, col[:, None] * inv_freq
    cos = jnp.concatenate([jnp.cos(ang_r), jnp.cos(ang_r), jnp.cos(ang_c), jnp.cos(ang_c)], axis=-1)
    zero = jnp.zeros_like(ang_r)
    sin = jnp.concatenate([-jnp.sin(ang_r), zero, -jnp.sin(ang_c), zero,
                           zero, jnp.sin(ang_r), zero, jnp.sin(ang_c)], axis=-1)
    return cos, sin


def _row_of_tile(first_row, tokens_per_row):
    def for_tile(tile):
        return lambda i: first_row + (i * tile) // tokens_per_row
    return for_tile


def _mixer_output(x, attn_arrays, attn_blocks, gates, mod_l, rows, l, w_branch, w_o, ln_gain, ln_bias):
    y = _merge_branches(attn_arrays, attn_blocks, gates, w_branch, l)
    return _out_projection(y, w_o, l, x, mod_l, ln_gain[l, 0:1], ln_bias[l, 0:1], row_of_tile=rows(TM))


EXPERT_TILE = 512


def _moe_ffn(x_a, x_b, mod_l, rows_a, rows_b, l, ln_gain, ln_bias, w_router_t, w_e_gate, w_e_up, w_e_down):
    h2, rec, rec_t, cnt = _moe_route(x_a, x_b, mod_l, w_router_t, rows_a=rows_a, rows_b=rows_b)
    dest, src, tile_expert, n_tiles = _dispatch_tables(rec, cnt, te=EXPERT_TILE)
    ys = _moe_experts(h2, tile_expert, n_tiles, src, w_e_gate, w_e_up, w_e_down, l, te=EXPERT_TILE)
    outs = []
    for x, rows, offset in ((x_a, rows_a, 0), (x_b, rows_b, x_a.shape[0])):
        outs.append(_moe_combine(dest, ys, x, rec_t, mod_l, ln_gain[l, 1:2], ln_bias[l, 1:2],
                                 row_of_tile=rows(TM_COMBINE), token_offset=offset))
    return outs


def kernel(x_prompt, x_sample, cache_kv_a, cache_kv_b, cache_kv_c, c, c_ctx, w_in, w_gate, w_branch, w_o,
           w_mod, b_mod, ln_gain, ln_bias, diff_lam, diff_subln, qk_gain, sink, w_router, w_e_gate, w_e_up,
           w_e_down):
    batch, seq, _ = x_prompt.shape
    dec_batch, dec_seq, _ = x_sample.shape
    past = cache_kv_a.shape[3]
    t_p, t_s = batch * seq, dec_batch * dec_seq

    cond = jnp.concatenate([c_ctx[None], c, jnp.zeros((MOD_ROWS - 1 - dec_batch, D_MODEL), F32)], axis=0)
    mod = _modulation(cond.T, w_mod, b_mod)
    cos_t, sin_t = _rope_tables(dec_seq)
    w_router_t = w_router.T
    w_in, w_gate, w_branch, w_o = (w.astype(BF16) for w in (w_in, w_gate, w_branch, w_o))
    cache_a = cache_kv_a.reshape(dec_batch, DEPTH, 2, past, 4 * 256)
    cache_b = cache_kv_b.reshape(dec_batch, DEPTH, 2, past, 2 * HEAD_DIM)
    cache_c = cache_kv_c.reshape(dec_batch, DEPTH, 2, past, 2 * HEAD_DIM)
    rows_p = _row_of_tile(0, t_p)
    rows_s = _row_of_tile(1, dec_seq)

    y_p = x_prompt.reshape(t_p, D_MODEL)
    y_s = x_sample.reshape(t_s, D_MODEL)
    projections = []
    for l in range(DEPTH):
        lam_init = 0.8 - 0.6 * math.exp(-0.3 * l)
        mod_l = mod[l].reshape(MOD_ROWS, 1, N_MOD * D_MODEL)
        subln_l = diff_subln[l].reshape(1, 256)
        shared = (w_branch, w_o, ln_gain, ln_bias)

        p = _in_projection(y_p, mod_l, w_in, l, qk_gain[l], cos_t, sin_t, row_of_tile=rows_p(TM),
                           rope=False, q_scale=1.0, out_dtype=F32)
        gates = _branch_gates(y_p, mod_l, w_gate, l, row_of_tile=rows_p(TM))
        attn = _attention_prompt(p, diff_lam[l], subln_l, sink[l], lam_init=lam_init, seq=seq)
        y_p = _mixer_output(y_p, (attn, attn, attn), (0, 1, 2), gates, mod_l, rows_p, l, *shared)
        projections.append(p.reshape(batch, seq, IN_COLS))

        qkv = _in_projection(y_s, mod_l, w_in, l, qk_gain[l], cos_t, sin_t, row_of_tile=rows_s(TM),
                             rope=True, q_scale=SCALE * LOG2E, out_dtype=BF16)
        gates = _branch_gates(y_s, mod_l, w_gate, l, row_of_tile=rows_s(TM))
        vec = lambda shape: pl.BlockSpec(shape, lambda b, qi, kk: (0, 0))
        a_o = _flash_sample(functools.partial(_diff_sample_kernel, lam_init=lam_init), qkv, cache_a, l,
                            q_col=COL_AQ, k_col=COL_AK, v_col=COL_AV, kv_width=1024, n_state=8,
                            row_sum_scratch=True, extra=(diff_lam[l], subln_l),
                            extra_specs=[vec((4, HEAD_DIM)), vec((1, 256))], name="diff_attention",
                            dec_seq=dec_seq)
        b_o = _flash_sample(_gqa_sample_kernel, qkv, cache_b, l, q_col=COL_BQ, k_col=COL_BK, v_col=COL_BV,
                            kv_width=256, n_state=8, row_sum_scratch=False, extra=(), extra_specs=[],
                            name="gqa_attention", dec_seq=dec_seq)
        c_o = _window_sample(qkv, cache_c, sink[l], l, dec_seq=dec_seq)
        y_s = _mixer_output(y_s, (a_o, b_o, c_o), (0, 0, 0), gates, mod_l, rows_s, l, *shared)

        y_p, y_s = _moe_ffn(y_p, y_s, mod_l, rows_p, rows_s, l, ln_gain, ln_bias, w_router_t, w_e_gate, w_e_up,
                            w_e_down)

    def new_cache(col_k, col_v, col_end, heads):
        parts = [p[..., c0:c1] for p in projections for c0, c1 in ((col_k, col_v), (col_v, col_end))]
        return jnp.stack(parts, axis=1).reshape(batch, DEPTH, 2, seq, heads, (col_v - col_k) // heads)

    new_kv_a = new_cache(COL_AK, COL_AV, COL_BQ, 4)
    new_kv_b = new_cache(COL_BK, COL_BV, COL_CQ, 2)
    new_kv_c = new_cache(COL_CK, COL_CV, IN_COLS, 2)
    return (y_p.reshape(batch, seq, D_MODEL), y_s.reshape(dec_batch, dec_seq, D_MODEL),
            new_kv_a, new_kv_b, new_kv_c)
```

```python
import functools
import math

import jax
import jax.numpy as jnp
from jax import lax
from jax.experimental import pallas as pl
from jax.experimental.pallas import tpu as pltpu

F32 = jnp.float32
BF16 = jnp.bfloat16

D_MODEL = 2048
HEAD_DIM = 128
GRID_W = 64
ROPE_THETA = 10000.0
WINDOW = 128
N_EXPERTS = 16
N_GROUPS = 4
EXPERTS_PER_GROUP = N_EXPERTS // N_GROUPS
D_EXPERT = 512
N_MOD = 6
DEPTH = 2
ALPHA = (2 * DEPTH) ** 0.25
EPS = 1e-6
IN_COLS = 6144
BRANCH_WIDTH = 1024
SCALE = HEAD_DIM ** -0.5
LOG2E = math.log2(math.e)
NEG_BIG = -1e30

COL_AQ, COL_AK, COL_AV = 0, 1024, 2048
COL_BQ, COL_BK, COL_BV = 3072, 4096, 4352
COL_CQ, COL_CK, COL_CV = 4608, 5632, 5888

LANES = 128
VMEM_LIMIT = 56 * 1024 * 1024

TM = 1024
TN_PROJ = 1024
TN_PROJ_F32 = 1024
TN_GATE = 1024
TM_SMALL = 512


def _cparams(sem):
    return pltpu.CompilerParams(dimension_semantics=sem, vmem_limit_bytes=VMEM_LIMIT)


def _dot(a, b):
    return jnp.dot(a, b, preferred_element_type=F32)


def _dot_nt(a, b):
    return lax.dot_general(a, b, (((1,), (1,)), ((), ())), preferred_element_type=F32)


def _norm_rows(x):
    mu = jnp.mean(x, axis=-1, keepdims=True)
    xc = x - mu
    var = jnp.mean(xc * xc, axis=-1, keepdims=True)
    return xc * lax.rsqrt(var + EPS)


def _rms(v, gain):
    ms = jnp.mean(v * v, axis=-1, keepdims=True)
    return v * lax.rsqrt(ms + EPS) * gain


def _rope(v, cos, sin_from_upper, sin_from_lower):
    return v * cos + pltpu.roll(v, 96, 1) * sin_from_upper + pltpu.roll(v, 32, 1) * sin_from_lower


N_COND = 3
MOD_ROWS = 8
TN_MOD = 1024


def _mod_kernel(cond_ref, w_ref, b_ref, o_ref):
    w = w_ref[...]
    row_idx = lax.broadcasted_iota(jnp.int32, (MOD_ROWS, TN_MOD), 0)
    out = jnp.zeros((MOD_ROWS, TN_MOD), F32)
    for r in range(N_COND):
        c = cond_ref[:, r:r + 1]
        s = c / (1.0 + jnp.exp(-c))
        m = jnp.sum(w * s, axis=0, keepdims=True) + b_ref[...]
        out = jnp.where(row_idx == r, m, out)
    o_ref[...] = out


def _modulation(cond_t, w_mod, b_mod):
    n = N_MOD * D_MODEL
    return pl.pallas_call(
        _mod_kernel,
        out_shape=jax.ShapeDtypeStruct((DEPTH, MOD_ROWS, n), F32),
        grid=(DEPTH, n // TN_MOD),
        in_specs=[
            pl.BlockSpec((D_MODEL, MOD_ROWS), lambda l, j: (0, 0)),
            pl.BlockSpec((None, D_MODEL, TN_MOD), lambda l, j: (l, 0, j)),
            pl.BlockSpec((None, 1, TN_MOD), lambda l, j: (l, 0, j)),
        ],
        out_specs=pl.BlockSpec((None, MOD_ROWS, TN_MOD), lambda l, j: (l, 0, j)),
        compiler_params=_cparams(("parallel", "parallel")),
        name="modulation",
    )(cond_t, w_mod, b_mod.reshape(DEPTH, 1, n))


def _modulate_to_scratch(x_ref, sh_ref, sc_ref, h_scr):
    h = _norm_rows(x_ref[...]) * (1.0 + sc_ref[...]) + sh_ref[...]
    h_scr[...] = h.astype(BF16)


def _head_chunk_kinds():
    kinds = []
    for n_chunks, kind in ((8, (None, True, True)), (8, (None, True, False)), (8, (None, False, False)),
                           (8, ("q", True, True)), (2, ("k", True, False)), (2, (None, False, False)),
                           (8, (None, True, True)), (2, (None, True, False)), (2, (None, False, False))):
        kinds.extend([kind] * n_chunks)
    return kinds


_HEAD_CHUNK_KINDS = _head_chunk_kinds()


def _inproj_kernel(x_ref, sh_ref, sc_ref, w_ref, qk_gain_ref, cos_ref, sin_ref, o_ref, h_scr, *, rope,
                   q_scale):
    j = pl.program_id(1)
    gain_q = qk_gain_ref[0:1, :]
    gain_k = qk_gain_ref[1:2, :]

    @pl.when(j == 0)
    def _():
        _modulate_to_scratch(x_ref, sh_ref, sc_ref, h_scr)

    acc = _dot(h_scr[...], w_ref[...])
    tn = o_ref.shape[1]
    n_chunks = tn // LANES

    def chunk(c):
        return acc[:, c * LANES:(c + 1) * LANES]

    def store(c, v):
        o_ref[:, c * LANES:(c + 1) * LANES] = v.astype(o_ref.dtype)

    def rp(v):
        return _rope(v, cos_ref[...], sin_ref[:, :HEAD_DIM], sin_ref[:, HEAD_DIM:]) if rope else v

    def qs(v):
        return v * q_scale if q_scale != 1.0 else v

    def finish(c, kind):
        norm, rotate, is_query = kind
        v = chunk(c)
        if norm is not None:
            v = _rms(v, gain_q if norm == "q" else gain_k)
        if rotate:
            v = rp(v)
        store(c, qs(v) if is_query else v)

    tile_kinds = [tuple(_HEAD_CHUNK_KINDS[t * n_chunks:(t + 1) * n_chunks]) for t in range(IN_COLS // tn)]
    for kinds in dict.fromkeys(tile_kinds):
        tiles = [t for t, k in enumerate(tile_kinds) if k == kinds]
        cond = j == tiles[0]
        for t in tiles[1:]:
            cond = cond | (j == t)

        @pl.when(cond)
        def _(kinds=kinds):
            for c, kind in enumerate(kinds):
                finish(c, kind)


def _gate_kernel(x_ref, sh_ref, sc_ref, w_ref, o_ref, h_scr):
    @pl.when(pl.program_id(1) == 0)
    def _():
        _modulate_to_scratch(x_ref, sh_ref, sc_ref, h_scr)

    acc = _dot(h_scr[...], w_ref[...])
    o_ref[...] = (1.0 / (1.0 + jnp.exp(-acc))).astype(o_ref.dtype)


def _mod_spec(which, row_of_tile):
    return pl.BlockSpec((None, 1, D_MODEL), lambda i, *_: (row_of_tile(i), 0, which))


def _in_projection(x, mod_l, w_in, l, qk_gain_l, cos_t, sin_t, *, row_of_tile, rope, q_scale, out_dtype):
    t = x.shape[0]
    tiles_per_seq = cos_t.shape[0] // TM
    tn = TN_PROJ_F32 if out_dtype == F32 else TN_PROJ
    return pl.pallas_call(
        functools.partial(_inproj_kernel, rope=rope, q_scale=q_scale),
        out_shape=jax.ShapeDtypeStruct((t, IN_COLS), out_dtype),
        grid=(t // TM, IN_COLS // tn),
        in_specs=[
            pl.BlockSpec((TM, D_MODEL), lambda i, j: (i, 0)),
            _mod_spec(0, row_of_tile),
            _mod_spec(1, row_of_tile),
            pl.BlockSpec((None, D_MODEL, tn), lambda i, j: (l, 0, j)),
            pl.BlockSpec((2, HEAD_DIM), lambda i, j: (0, 0)),
            pl.BlockSpec((TM, HEAD_DIM), lambda i, j: (i % tiles_per_seq, 0)),
            pl.BlockSpec((TM, 2 * HEAD_DIM), lambda i, j: (i % tiles_per_seq, 0)),
        ],
        out_specs=pl.BlockSpec((TM, tn), lambda i, j: (i, j)),
        scratch_shapes=[pltpu.VMEM((TM, D_MODEL), BF16)],
        compiler_params=_cparams(("parallel", "arbitrary")),
        name="in_projection",
    )(x, mod_l, mod_l, w_in, qk_gain_l, cos_t, sin_t)


def _branch_gates(x, mod_l, w_gate, l, *, row_of_tile):
    t = x.shape[0]
    n = w_gate.shape[2]
    return pl.pallas_call(
        _gate_kernel,
        out_shape=jax.ShapeDtypeStruct((t, n), BF16),
        grid=(t // TM, n // TN_GATE),
        in_specs=[
            pl.BlockSpec((TM, D_MODEL), lambda i, j: (i, 0)),
            _mod_spec(0, row_of_tile),
            _mod_spec(1, row_of_tile),
            pl.BlockSpec((None, D_MODEL, TN_GATE), lambda i, j: (l, 0, j)),
        ],
        out_specs=pl.BlockSpec((TM, TN_GATE), lambda i, j: (i, j)),
        scratch_shapes=[pltpu.VMEM((TM, D_MODEL), BF16)],
        compiler_params=_cparams(("parallel", "arbitrary")),
        name="branch_gates",
    )(x, mod_l, mod_l, w_gate)


def _diff_lambda(lam_ref, lam_init):
    lp = lam_ref[...]
    t1 = jnp.sum(lp[0:1] * lp[1:2], axis=-1, keepdims=True)
    t2 = jnp.sum(lp[2:3] * lp[3:4], axis=-1, keepdims=True)
    return jnp.exp(t1) - jnp.exp(t2) + lam_init


def _softmax_rows(s, sink=None):
    m = jnp.max(s, axis=-1, keepdims=True)
    if sink is not None:
        m = jnp.maximum(m, sink)
    e = jnp.exp(s - m)
    den = jnp.sum(e, axis=-1, keepdims=True)
    if sink is not None:
        den = den + jnp.exp(sink - m)
    return e * (1.0 / den)


def _attn_prompt_kernel(p_ref, lam_ref, subln_ref, sink_ref, o_ref, *, lam_init):
    lam = _diff_lambda(lam_ref, lam_init)

    def blk(c0, w):
        return p_ref[:, c0:c0 + w].astype(BF16)

    for h in range(4):
        probs = []
        for m in range(2):
            q = blk(COL_AQ + h * 256 + m * HEAD_DIM, HEAD_DIM)
            k = blk(COL_AK + h * 256 + m * HEAD_DIM, HEAD_DIM)
            probs.append(_softmax_rows(_dot_nt(q, k) * SCALE))
        w = probs[0] - lam * probs[1]
        o = _dot(w.astype(BF16), blk(COL_AV + h * 256, 256))
        o = _rms(o, subln_ref[...]) * (1.0 - lam_init)
        o_ref[:, h * 256:(h + 1) * 256] = o.astype(o_ref.dtype)

    for mixer, (cq, ck, cv) in enumerate(((COL_BQ, COL_BK, COL_BV), (COL_CQ, COL_CK, COL_CV))):
        for kvh in range(2):
            k = blk(ck + kvh * HEAD_DIM, HEAD_DIM)
            v = blk(cv + kvh * HEAD_DIM, HEAD_DIM)
            for g in range(4):
                hq = kvh * 4 + g
                q = blk(cq + hq * HEAD_DIM, HEAD_DIM)
                sink = sink_ref[hq] if mixer == 1 else None
                p = _softmax_rows(_dot_nt(q, k) * SCALE, sink)
                o = _dot(p.astype(BF16), v)
                c0 = BRANCH_WIDTH * (1 + mixer) + hq * HEAD_DIM
                o_ref[:, c0:c0 + HEAD_DIM] = o.astype(o_ref.dtype)


def _attention_prompt(p, diff_lam_l, subln_l, sink_l, *, lam_init, seq):
    t = p.shape[0]
    return pl.pallas_call(
        functools.partial(_attn_prompt_kernel, lam_init=lam_init),
        out_shape=jax.ShapeDtypeStruct((t, 3 * BRANCH_WIDTH), BF16),
        grid=(t // seq,),
        in_specs=[
            pl.BlockSpec((seq, IN_COLS), lambda b: (b, 0)),
            pl.BlockSpec((4, HEAD_DIM), lambda b: (0, 0)),
            pl.BlockSpec((1, 256), lambda b: (0, 0)),
            pl.BlockSpec(memory_space=pltpu.SMEM),
        ],
        out_specs=pl.BlockSpec((seq, 3 * BRANCH_WIDTH), lambda b: (b, 0)),
        compiler_params=_cparams(("parallel",)),
        name="attention_prompt",
    )(p, diff_lam_l, subln_l, sink_l)


TQ = 512
TQ_FLASH = 1024
TK = 1024


def _tile_lanes(v, n):
    return jnp.concatenate([v] * n, axis=-1) if n > 1 else v


def _online_softmax(idx, s, m_scr):
    m_prev = m_scr[idx]
    m_new = jnp.maximum(m_prev, jnp.max(s, axis=-1, keepdims=True))
    m_scr[idx] = m_new
    alpha = jnp.exp2(m_prev - m_new)
    p = jnp.exp2(s - _tile_lanes(m_new, s.shape[1] // LANES))
    return p, alpha


def _diff_sample_kernel(q_ref, k_ref, v_ref, kc_ref, vc_ref, lam_ref, subln_ref, o_ref,
                        m_scr, l_scr, acc_scr, *, lam_init):
    kk = pl.program_id(2)

    @pl.when(kk == 0)
    def _():
        m_scr[...] = jnp.full(m_scr.shape, NEG_BIG, F32)
        l_scr[...] = jnp.zeros(l_scr.shape, F32)
        acc_scr[...] = jnp.zeros(acc_scr.shape, F32)

    def process(kb_ref, vb_ref):
        for h in range(4):
            v = vb_ref[:, h * 256:(h + 1) * 256].astype(BF16)
            for m in range(2):
                c0 = h * 256 + m * HEAD_DIM
                idx = h * 2 + m
                k = kb_ref[:, c0:c0 + HEAD_DIM].astype(BF16)
                p, alpha = _online_softmax(idx, _dot_nt(q_ref[:, c0:c0 + HEAD_DIM], k), m_scr)
                part = p[:, 0:LANES]
                for c in range(1, p.shape[1] // LANES):
                    part = part + p[:, c * LANES:(c + 1) * LANES]
                l_scr[idx] = alpha * l_scr[idx] + part
                acc_scr[idx] = _tile_lanes(alpha, 2) * acc_scr[idx] + _dot(p.astype(BF16), v)

    @pl.when(kk == 0)
    def _():
        process(kc_ref, vc_ref)

    @pl.when(kk > 0)
    def _():
        process(k_ref, v_ref)

    @pl.when(kk == pl.num_programs(2) - 1)
    def _():
        lam = _diff_lambda(lam_ref, lam_init)
        for h in range(4):
            l1 = jnp.sum(l_scr[2 * h], axis=-1, keepdims=True)
            l2 = jnp.sum(l_scr[2 * h + 1], axis=-1, keepdims=True)
            o1 = acc_scr[2 * h] * (1.0 / l1)
            o2 = acc_scr[2 * h + 1] * (1.0 / l2)
            o = _rms(o1 - lam * o2, subln_ref[...]) * (1.0 - lam_init)
            o_ref[:, h * 256:(h + 1) * 256] = o.astype(o_ref.dtype)


def _gqa_sample_kernel(q_ref, k_ref, v_ref, kc_ref, vc_ref, o_ref, m_scr, acc_scr):
    kk = pl.program_id(2)

    @pl.when(kk == 0)
    def _():
        m_scr[...] = jnp.full(m_scr.shape, NEG_BIG, F32)
        acc_scr[...] = jnp.zeros(acc_scr.shape, F32)

    def process(kb_ref, vb_ref):
        for kvh in range(2):
            k = kb_ref[:, kvh * HEAD_DIM:(kvh + 1) * HEAD_DIM].astype(BF16)
            v = vb_ref[:, kvh * HEAD_DIM:(kvh + 1) * HEAD_DIM].astype(BF16)
            v_ones = jnp.concatenate([v, jnp.ones_like(v)], axis=-1)
            for g in range(4):
                hq = kvh * 4 + g
                s = _dot_nt(q_ref[:, hq * HEAD_DIM:(hq + 1) * HEAD_DIM], k)
                p, alpha = _online_softmax(hq, s, m_scr)
                acc_scr[hq] = _tile_lanes(alpha, 2) * acc_scr[hq] + _dot(p.astype(BF16), v_ones)

    @pl.when(kk == 0)
    def _():
        process(kc_ref, vc_ref)

    @pl.when(kk > 0)
    def _():
        process(k_ref, v_ref)

    @pl.when(kk == pl.num_programs(2) - 1)
    def _():
        for hq in range(8):
            o = acc_scr[hq, :, 0:HEAD_DIM] / acc_scr[hq, :, HEAD_DIM:2 * HEAD_DIM]
            o_ref[:, hq * HEAD_DIM:(hq + 1) * HEAD_DIM] = o.astype(o_ref.dtype)


def _flash_sample(kernel, qkv, cache, l, *, q_col, k_col, v_col, kv_width, n_state, row_sum_scratch, extra,
                  extra_specs, name, dec_seq):
    t = qkv.shape[0]
    nb = t // dec_seq
    tq = TQ_FLASH
    nq = dec_seq // tq
    nk = dec_seq // TK
    past = cache.shape[3]
    q_blk, k_blk, v_blk = q_col // BRANCH_WIDTH, k_col // kv_width, v_col // kv_width
    kv_row = lambda b, qi, kk: b * nk + jnp.maximum(kk - 1, 0)
    return pl.pallas_call(
        kernel,
        out_shape=jax.ShapeDtypeStruct((t, BRANCH_WIDTH), BF16),
        grid=(nb, nq, nk + 1),
        in_specs=[
            pl.BlockSpec((tq, BRANCH_WIDTH), lambda b, qi, kk: (b * nq + qi, q_blk)),
            pl.BlockSpec((TK, kv_width), lambda b, qi, kk: (kv_row(b, qi, kk), k_blk)),
            pl.BlockSpec((TK, kv_width), lambda b, qi, kk: (kv_row(b, qi, kk), v_blk)),
            pl.BlockSpec((None, None, None, past, kv_width), lambda b, qi, kk: (b, l, 0, 0, 0)),
            pl.BlockSpec((None, None, None, past, kv_width), lambda b, qi, kk: (b, l, 1, 0, 0)),
        ] + extra_specs,
        out_specs=pl.BlockSpec((tq, BRANCH_WIDTH), lambda b, qi, kk: (b * nq + qi, 0)),
        scratch_shapes=[pltpu.VMEM((n_state, tq, LANES), F32)] * (2 if row_sum_scratch else 1)
        + [pltpu.VMEM((n_state, tq, 2 * LANES), F32)],
        compiler_params=_cparams(("parallel", "parallel", "arbitrary")),
        name=name,
    )(qkv, qkv, qkv, cache, cache, *extra)


def _window_sample_kernel(q0_ref, q1_ref, kp_ref, kc_ref, kn_ref, vp_ref, vc_ref, vn_ref, kctx_ref, vctx_ref,
                          sink_ref, o_ref, *, dec_seq):
    qi = pl.program_id(1)
    q_start = qi * TQ
    side = WINDOW
    q_pos = q_start + lax.broadcasted_iota(jnp.int32, (TQ, 1), 0)
    segs = ((kp_ref, vp_ref, q_start - side, side), (kc_ref, vc_ref, q_start, TQ),
            (kn_ref, vn_ref, q_start + TQ, side))
    valid = []
    for _, _, start, n in segs:
        k_pos = start + lax.broadcasted_iota(jnp.int32, (TQ, n), 1)
        valid.append((jnp.abs(q_pos - k_pos) <= WINDOW) & (k_pos >= 0) & (k_pos < dec_seq))
    for kvh in range(2):
        q_ref = q0_ref if kvh == 0 else q1_ref
        lo, hi = kvh * HEAD_DIM, (kvh + 1) * HEAD_DIM
        kctx = kctx_ref[:, lo:hi].astype(BF16)
        vctx = vctx_ref[:, lo:hi].astype(BF16)
        for g in range(4):
            hq = kvh * 4 + g
            q = q_ref[:, g * HEAD_DIM:(g + 1) * HEAD_DIM]
            sink = sink_ref[hq] * LOG2E
            scores = [_dot_nt(q, kctx)]
            for (k_ref, _, _, _), ok in zip(segs, valid):
                scores.append(jnp.where(ok, _dot_nt(q, k_ref[:, lo:hi]), NEG_BIG))
            m = jnp.maximum(scores[0].max(axis=-1, keepdims=True), sink)
            for s in scores[1:]:
                m = jnp.maximum(m, s.max(axis=-1, keepdims=True))
            es = [jnp.exp2(s - m) for s in scores]
            den = jnp.exp2(sink - m)
            for e in es:
                den = den + jnp.sum(e, axis=-1, keepdims=True)
            o = _dot(es[0].astype(BF16), vctx)
            for e, (_, v_ref, _, _) in zip(es[1:], segs):
                o = o + _dot(e.astype(BF16), v_ref[:, lo:hi])
            o = o * (1.0 / den)
            o_ref[:, hq * HEAD_DIM:(hq + 1) * HEAD_DIM] = o.astype(o_ref.dtype)


def _window_sample(qkv, cache, sink_l, l, *, dec_seq):
    t = qkv.shape[0]
    nb = t // dec_seq
    nq = dec_seq // TQ
    half = WINDOW
    n_half = dec_seq // half
    per_tile = TQ // half
    past = cache.shape[3]
    kvw = 2 * HEAD_DIM
    q_blk = COL_CQ // 512
    k_blk, v_blk = COL_CK // kvw, COL_CV // kvw
    prev_row = lambda b, qi: b * n_half + jnp.maximum(per_tile * qi - 1, 0)
    next_row = lambda b, qi: b * n_half + jnp.minimum(per_tile * (qi + 1), n_half - 1)
    return pl.pallas_call(
        functools.partial(_window_sample_kernel, dec_seq=dec_seq),
        out_shape=jax.ShapeDtypeStruct((t, BRANCH_WIDTH), BF16),
        grid=(nb, nq),
        in_specs=[
            pl.BlockSpec((TQ, 512), lambda b, qi: (b * nq + qi, q_blk)),
            pl.BlockSpec((TQ, 512), lambda b, qi: (b * nq + qi, q_blk + 1)),
            pl.BlockSpec((half, kvw), lambda b, qi: (prev_row(b, qi), k_blk)),
            pl.BlockSpec((TQ, kvw), lambda b, qi: (b * nq + qi, k_blk)),
            pl.BlockSpec((half, kvw), lambda b, qi: (next_row(b, qi), k_blk)),
            pl.BlockSpec((half, kvw), lambda b, qi: (prev_row(b, qi), v_blk)),
            pl.BlockSpec((TQ, kvw), lambda b, qi: (b * nq + qi, v_blk)),
            pl.BlockSpec((half, kvw), lambda b, qi: (next_row(b, qi), v_blk)),
            pl.BlockSpec((None, None, None, past, kvw), lambda b, qi: (b, l, 0, 0, 0)),
            pl.BlockSpec((None, None, None, past, kvw), lambda b, qi: (b, l, 1, 0, 0)),
            pl.BlockSpec(memory_space=pltpu.SMEM),
        ],
        out_specs=pl.BlockSpec((TQ, BRANCH_WIDTH), lambda b, qi: (b * nq + qi, 0)),
        compiler_params=_cparams(("parallel", "parallel")),
        name="window_attention",
    )(qkv, qkv, qkv, qkv, qkv, qkv, qkv, qkv, cache, cache, sink_l)


def _merge_kernel(oa_ref, ob_ref, oc_ref, g_ref, w_ref, y_ref, acc_scr):
    r = pl.program_id(2)

    def contrib(o_ref):
        return g_ref[...].astype(F32) * _dot(o_ref[...], w_ref[...])

    @pl.when(r == 0)
    def _():
        acc_scr[...] = contrib(oa_ref)

    @pl.when(r == 1)
    def _():
        acc_scr[...] += contrib(ob_ref)

    @pl.when(r == 2)
    def _():
        y_ref[...] = (acc_scr[...] + contrib(oc_ref)).astype(y_ref.dtype)


def _merge_branches(o_arrays, o_blocks, gates, w_branch, l):
    t = gates.shape[0]
    tn = D_MODEL
    nn = D_MODEL // tn
    o_specs = [pl.BlockSpec((TM, BRANCH_WIDTH), functools.partial(lambda i, n, r, blk: (i, blk), blk=blk))
               for blk in o_blocks]
    return pl.pallas_call(
        _merge_kernel,
        out_shape=jax.ShapeDtypeStruct((t, D_MODEL), BF16),
        grid=(t // TM, nn, 3),
        in_specs=o_specs + [
            pl.BlockSpec((TM, tn), lambda i, n, r: (i, r * nn + n)),
            pl.BlockSpec((None, None, BRANCH_WIDTH, tn), lambda i, n, r: (l, r, 0, n)),
        ],
        out_specs=pl.BlockSpec((TM, tn), lambda i, n, r: (i, n)),
        scratch_shapes=[pltpu.VMEM((TM, tn), F32)],
        compiler_params=_cparams(("parallel", "parallel", "arbitrary")),
        name="merge_branches",
    )(*o_arrays, gates, w_branch)


def _layer_norm_rows(v, gain, bias):
    return _norm_rows(v) * gain + bias


def _outproj_kernel(y_ref, w_ref, x_ref, g_ref, gain_ref, bias_ref, o_ref):
    n = pl.program_id(1)
    tn = w_ref.shape[1]
    n_blocks = o_ref.shape[1] // tn
    z = _dot(y_ref[...], w_ref[...])
    for b in range(n_blocks):
        @pl.when(n == b)
        def _(b=b):
            cols = slice(b * tn, (b + 1) * tn)
            o_ref[:, cols] = ALPHA * x_ref[:, cols] + g_ref[:, cols] * z

    @pl.when(n == n_blocks - 1)
    def _():
        o_ref[...] = _layer_norm_rows(o_ref[...], gain_ref[...], bias_ref[...])


def _out_projection(y, w_o, l, x, mod_l, ln_gain_l, ln_bias_l, *, row_of_tile):
    t = x.shape[0]
    tn = 512
    return pl.pallas_call(
        _outproj_kernel,
        out_shape=jax.ShapeDtypeStruct((t, D_MODEL), F32),
        grid=(t // TM, D_MODEL // tn),
        in_specs=[
            pl.BlockSpec((TM, D_MODEL), lambda i, n: (i, 0)),
            pl.BlockSpec((None, D_MODEL, tn), lambda i, n: (l, 0, n)),
            pl.BlockSpec((TM, D_MODEL), lambda i, n: (i, 0)),
            _mod_spec(2, row_of_tile),
            pl.BlockSpec((1, D_MODEL), lambda i, n: (0, 0)),
            pl.BlockSpec((1, D_MODEL), lambda i, n: (0, 0)),
        ],
        out_specs=pl.BlockSpec((TM, D_MODEL), lambda i, n: (i, 0)),
        compiler_params=_cparams(("parallel", "arbitrary")),
        name="out_projection",
    )(y, w_o, x, mod_l, ln_gain_l, ln_bias_l)


def _route(p):
    rows = [p[e:e + 1, :] for e in range(N_EXPERTS)]
    best_score, best_group = None, None
    for g in range(N_GROUPS):
        members = rows[g * EXPERTS_PER_GROUP:(g + 1) * EXPERTS_PER_GROUP]
        score = None
        for a in range(EXPERTS_PER_GROUP):
            for b in range(a + 1, EXPERTS_PER_GROUP):
                pair = members[a] + members[b]
                score = pair if score is None else jnp.maximum(score, pair)
        if g == 0:
            best_score, best_group = score, jnp.zeros(score.shape, F32)
        else:
            better = score > best_score
            best_group = jnp.where(better, float(g), best_group)
            best_score = jnp.where(better, score, best_score)
    e_idx = lax.broadcasted_iota(jnp.int32, p.shape, 0).astype(F32)
    g_idx = jnp.floor(e_idx * (1.0 / EXPERTS_PER_GROUP))
    masked = jnp.where(g_idx == best_group, p, -1.0)
    w1 = jnp.max(masked, axis=0, keepdims=True)
    i1 = jnp.min(jnp.where(masked == w1, e_idx, float(N_EXPERTS)), axis=0, keepdims=True)
    masked2 = jnp.where(e_idx == i1, -2.0, masked)
    w2 = jnp.max(masked2, axis=0, keepdims=True)
    i2 = jnp.min(jnp.where(masked2 == w2, e_idx, float(N_EXPERTS)), axis=0, keepdims=True)
    tot = w1 + w2
    return e_idx, i1, i2, w1 / tot, w2 / tot


ROUTE_ROWS = 8
D_PACKED = D_MODEL // 2


def _pack_bf16_pairs(xb):
    half = xb.shape[1] // 2
    lo = lax.bitcast_convert_type(xb[:, :half].astype(F32), jnp.uint32)
    hi = lax.bitcast_convert_type(xb[:, half:].astype(F32), jnp.uint32)
    return (hi & jnp.uint32(0xFFFF0000)) | (lo >> 16)


def _unpack_bf16_pairs(words):
    lo = lax.bitcast_convert_type(words << 16, F32).astype(BF16)
    hi = lax.bitcast_convert_type(words & jnp.uint32(0xFFFF0000), F32).astype(BF16)
    return lo, hi


def _moe_route_kernel(xa_ref, xb_ref, sh_ref, sc_ref, wr_ref, h_ref, rec_ref, rec_t_ref, cnt_ref, carry_scr,
                      *, tiles_a):
    @pl.when(pl.program_id(0) == 0)
    def _():
        carry_scr[...] = jnp.zeros(carry_scr.shape, F32)

    x = jnp.where(pl.program_id(0) < tiles_a, xa_ref[...], xb_ref[...])
    h = _norm_rows(x) * (1.0 + sc_ref[...]) + sh_ref[...]
    hb = h.astype(BF16)
    h_ref[...] = _pack_bf16_pairs(hb)
    logits = _dot_nt(wr_ref[...].astype(BF16), hb)
    m = jnp.max(logits, axis=0, keepdims=True)
    e = jnp.exp(logits - m)
    probs = e / jnp.sum(e, axis=0, keepdims=True)
    e_idx, i1, i2, w1, w2 = _route(probs)
    tm = probs.shape[1]
    oh1 = (e_idx == i1).astype(F32)
    oh2 = (e_idx == i2).astype(F32)
    oh = oh1 + oh2
    earlier = (lax.broadcasted_iota(jnp.int32, (tm, tm), 0) < lax.broadcasted_iota(jnp.int32, (tm, tm), 1))
    rank = carry_scr[:, 0:1] + _dot(oh.astype(BF16), earlier.astype(BF16))
    r1 = jnp.sum(oh1 * rank, axis=0, keepdims=True)
    r2 = jnp.sum(oh2 * rank, axis=0, keepdims=True)
    carry_scr[...] = carry_scr[...] + jnp.sum(oh, axis=1, keepdims=True)
    cnt_ref[...] = carry_scr[...]
    row = lax.broadcasted_iota(jnp.int32, (ROUTE_ROWS, tm), 0)
    rec = jnp.zeros((ROUTE_ROWS, tm), F32)
    for k, v in enumerate((i1, i2, r1, r2, w1, w2)):
        rec = jnp.where(row == k, v, rec)
    rec_ref[...] = rec
    pad = jnp.zeros((LANES - ROUTE_ROWS, tm), F32)
    rec_t_ref[...] = jnp.concatenate([rec, pad], axis=0).T


def _moe_route(x_a, x_b, mod_l, w_router_t, *, rows_a, rows_b):
    tm = TM_SMALL
    tiles_a = x_a.shape[0] // tm
    t = x_a.shape[0] + x_b.shape[0]
    row_a, row_b = rows_a(tm), rows_b(tm)
    row_of_tile = lambda i: jnp.where(i < tiles_a, row_a(i), row_b(i - tiles_a))
    return pl.pallas_call(
        functools.partial(_moe_route_kernel, tiles_a=tiles_a),
        out_shape=(jax.ShapeDtypeStruct((t, D_PACKED), jnp.uint32), jax.ShapeDtypeStruct((ROUTE_ROWS, t), F32),
                   jax.ShapeDtypeStruct((t, LANES), F32), jax.ShapeDtypeStruct((N_EXPERTS, LANES), F32)),
        grid=(t // tm,),
        in_specs=[
            pl.BlockSpec((tm, D_MODEL), lambda i: (jnp.minimum(i, tiles_a - 1), 0)),
            pl.BlockSpec((tm, D_MODEL), lambda i: (jnp.maximum(i - tiles_a, 0), 0)),
            _mod_spec(3, row_of_tile),
            _mod_spec(4, row_of_tile),
            pl.BlockSpec((N_EXPERTS, D_MODEL), lambda i: (0, 0)),
        ],
        out_specs=(pl.BlockSpec((tm, D_PACKED), lambda i: (i, 0)),
                   pl.BlockSpec((ROUTE_ROWS, tm), lambda i: (0, i)),
                   pl.BlockSpec((tm, LANES), lambda i: (i, 0)),
                   pl.BlockSpec((N_EXPERTS, LANES), lambda i: (0, 0))),
        scratch_shapes=[pltpu.VMEM((N_EXPERTS, LANES), F32)],
        compiler_params=_cparams(("arbitrary",)),
        name="moe_route",
    )(x_a, x_b, mod_l, mod_l, w_router_t)


ROW_DMA_UNROLL = 32
GATHER_ORDER_STRIDE = 37


def _row_copy(src_hbm, src_row, dst_buf, dst_row, sem):
    return pltpu.make_async_copy(src_hbm.at[pl.ds(src_row, 1)], dst_buf.at[pl.ds(dst_row, 1)], sem)


def _moe_expert_kernel(tile_expert_ref, n_tiles_ref, src_ref, h_hbm, wg_ref, wu_ref, wd_ref, ys_ref,
                       x_buf, sems, *, te):
    del tile_expert_ref
    j = pl.program_id(0)
    n_valid = n_tiles_ref[0]
    slot = j % 2

    def start_gather(tile, s):
        def body(r, carry):
            row = (r * GATHER_ORDER_STRIDE) % te
            _row_copy(h_hbm, src_ref[tile * te + row], x_buf.at[s], row, sems.at[s]).start()
            return carry
        lax.fori_loop(0, te, body, 0, unroll=ROW_DMA_UNROLL)

    def wait_gather(s):
        def body(r, carry):
            _row_copy(h_hbm, 0, x_buf.at[s], r, sems.at[s]).wait()
            return carry
        lax.fori_loop(0, te, body, 0, unroll=ROW_DMA_UNROLL)

    @pl.when((j == 0) & (n_valid > 0))
    def _():
        start_gather(0, 0)

    @pl.when(j + 1 < n_valid)
    def _():
        start_gather(j + 1, 1 - slot)

    @pl.when(j < n_valid)
    def _():
        wait_gather(slot)
        x_lo, x_hi = _unpack_bf16_pairs(x_buf[slot])

        def project(w_ref):
            return (_dot(x_lo, w_ref[:D_PACKED, :].astype(BF16)) + _dot(x_hi, w_ref[D_PACKED:, :].astype(BF16)))

        gate = project(wg_ref)
        up = project(wu_ref)
        hid = gate / (1.0 + jnp.exp(-gate)) * up
        ys_ref[...] = _pack_bf16_pairs(_dot(hid.astype(BF16), wd_ref[...].astype(BF16)).astype(BF16))

    @pl.when(j >= n_valid)
    def _():
        ys_ref[...] = jnp.zeros(ys_ref.shape, ys_ref.dtype)


def _moe_experts(h, tile_expert, n_tiles, src, wg, wu, wd, l, *, te):
    n_rows = src.shape[0]
    w_in_spec = pl.BlockSpec((None, None, D_MODEL, D_EXPERT), lambda j, te_ref, *_: (l, te_ref[j], 0, 0))
    w_out_spec = pl.BlockSpec((None, None, D_EXPERT, D_MODEL), lambda j, te_ref, *_: (l, te_ref[j], 0, 0))
    return pl.pallas_call(
        functools.partial(_moe_expert_kernel, te=te),
        out_shape=jax.ShapeDtypeStruct((n_rows, D_PACKED), jnp.uint32),
        grid_spec=pltpu.PrefetchScalarGridSpec(
            num_scalar_prefetch=3,
            grid=(n_rows // te,),
            in_specs=[pl.BlockSpec(memory_space=pltpu.HBM), w_in_spec, w_in_spec, w_out_spec],
            out_specs=pl.BlockSpec((te, D_PACKED), lambda j, *_: (j, 0)),
            scratch_shapes=[pltpu.VMEM((2, te, D_PACKED), h.dtype), pltpu.SemaphoreType.DMA((2,))],
        ),
        compiler_params=_cparams(("arbitrary",)),
        name="moe_experts",
    )(tile_expert, n_tiles, src, h, wg, wu, wd)


TM_COMBINE = 256


def _moe_combine_kernel(dest_ref, ys_hbm, x_ref, rec_t_ref, g_ref, gain_ref, bias_ref, o_ref, y_buf, sems,
                        *, n_tokens, token_offset):
    i = pl.program_id(0)
    tm = x_ref.shape[0]
    slot = i % 2

    def start_gather(tile, s):
        def body(r, carry):
            for k in range(2):
                row = dest_ref[k * n_tokens + token_offset + tile * tm + r]
                _row_copy(ys_hbm, row, y_buf.at[s, k], r, sems.at[s]).start()
            return carry
        lax.fori_loop(0, tm, body, 0, unroll=ROW_DMA_UNROLL)

    def wait_gather(s):
        def body(r, carry):
            for k in range(2):
                _row_copy(ys_hbm, 0, y_buf.at[s, k], r, sems.at[s]).wait()
            return carry
        lax.fori_loop(0, tm, body, 0, unroll=ROW_DMA_UNROLL)

    @pl.when(i == 0)
    def _():
        start_gather(0, 0)

    @pl.when(i + 1 < pl.num_programs(0))
    def _():
        start_gather(i + 1, 1 - slot)

    wait_gather(slot)
    w1, w2 = rec_t_ref[:, 4:5], rec_t_ref[:, 5:6]
    y1_lo, y1_hi = _unpack_bf16_pairs(y_buf[slot, 0])
    y2_lo, y2_hi = _unpack_bf16_pairs(y_buf[slot, 1])
    for half, (y1, y2) in enumerate(((y1_lo, y2_lo), (y1_hi, y2_hi))):
        cols = slice(half * D_PACKED, (half + 1) * D_PACKED)
        z = w1 * y1.astype(F32) + w2 * y2.astype(F32)
        o_ref[:, cols] = ALPHA * x_ref[:, cols] + g_ref[:, cols] * z
    o_ref[...] = _layer_norm_rows(o_ref[...], gain_ref[...], bias_ref[...])


def _moe_combine(dest, ys, x, rec_t, mod_l, ln_gain_l, ln_bias_l, *, row_of_tile, token_offset):
    t = x.shape[0]
    tm = TM_COMBINE
    first_tile = token_offset // tm
    vec = pl.BlockSpec((1, D_MODEL), lambda i, *_: (0, 0))
    return pl.pallas_call(
        functools.partial(_moe_combine_kernel, n_tokens=rec_t.shape[0], token_offset=token_offset),
        out_shape=jax.ShapeDtypeStruct((t, D_MODEL), F32),
        grid_spec=pltpu.PrefetchScalarGridSpec(
            num_scalar_prefetch=1,
            grid=(t // tm,),
            in_specs=[
                pl.BlockSpec(memory_space=pltpu.HBM),
                pl.BlockSpec((tm, D_MODEL), lambda i, *_: (i, 0)),
                pl.BlockSpec((tm, LANES), lambda i, *_: (first_tile + i, 0)),
                _mod_spec(5, row_of_tile),
                vec, vec,
            ],
            out_specs=pl.BlockSpec((tm, D_MODEL), lambda i, *_: (i, 0)),
            scratch_shapes=[pltpu.VMEM((2, 2, tm, D_PACKED), jnp.uint32), pltpu.SemaphoreType.DMA((2,))],
        ),
        compiler_params=_cparams(("arbitrary",)),
        name="moe_combine",
    )(dest, ys, x, rec_t, mod_l, ln_gain_l, ln_bias_l)


def _dispatch_tables(rec, cnt, *, te):
    t = rec.shape[1]
    n_rows = 2 * t + N_EXPERTS * te
    e12 = rec[0:2].astype(jnp.int32)
    r12 = rec[2:4].astype(jnp.int32)
    counts = cnt[:, 0].astype(jnp.int32)
    padded = (counts + te - 1) // te * te
    ends = jnp.cumsum(padded)
    offsets = ends - padded
    expert_ids = jnp.arange(N_EXPERTS, dtype=jnp.int32)[:, None, None]
    dest = (jnp.sum(jnp.where(e12[None] == expert_ids, offsets[:, None, None], 0), axis=0) + r12).reshape(2 * t)
    tokens = jnp.tile(jnp.arange(t, dtype=jnp.int32), 2)
    filler = jnp.arange(n_rows, dtype=jnp.int32) % t
    src = filler.at[dest].set(tokens, unique_indices=True)
    tile_start = jnp.arange(n_rows // te, dtype=jnp.int32) * te
    tile_expert = jnp.minimum(jnp.searchsorted(ends, tile_start, side="right"), N_EXPERTS - 1).astype(jnp.int32)
    n_tiles = (ends[-1:] // te).astype(jnp.int32)
    return dest, src, tile_expert, n_tiles


def _rope_tables(n_tokens):
    rows = n_tokens // GRID_W
    row = jnp.repeat(jnp.arange(rows), GRID_W).astype(F32)
    col = jnp.tile(jnp.arange(GRID_W), rows).astype(F32)
    quarter = HEAD_DIM // 4
    inv_freq = ROPE_THETA ** (-jnp.arange(quarter, dtype=F32) / quarter)
    ang_r, ang_c = row[:, None] * inv_freq, col[:, None] * inv_freq
    cos = jnp.concatenate([jnp.cos(ang_r), jnp.cos(ang_r), jnp.cos(ang_c), jnp.cos(ang_c)], axis=-1)
    zero = jnp.zeros_like(ang_r)
    sin = jnp.concatenate([-jnp.sin(ang_r), zero, -jnp.sin(ang_c), zero,
                           zero, jnp.sin(ang_r), zero, jnp.sin(ang_c)], axis=-1)
    return cos, sin


def _row_of_tile(first_row, tokens_per_row):
    def for_tile(tile):
        return lambda i: first_row + (i * tile) // tokens_per_row
    return for_tile


def _mixer_output(x, attn_arrays, attn_blocks, gates, mod_l, rows, l, w_branch, w_o, ln_gain, ln_bias):
    y = _merge_branches(attn_arrays, attn_blocks, gates, w_branch, l)
    return _out_projection(y, w_o, l, x, mod_l, ln_gain[l, 0:1], ln_bias[l, 0:1], row_of_tile=rows(TM))


EXPERT_TILE = 512


def _moe_ffn(x_a, x_b, mod_l, rows_a, rows_b, l, ln_gain, ln_bias, w_router_t, w_e_gate, w_e_up, w_e_down):
    h2, rec, rec_t, cnt = _moe_route(x_a, x_b, mod_l, w_router_t, rows_a=rows_a, rows_b=rows_b)
    dest, src, tile_expert, n_tiles = _dispatch_tables(rec, cnt, te=EXPERT_TILE)
    ys = _moe_experts(h2, tile_expert, n_tiles, src, w_e_gate, w_e_up, w_e_down, l, te=EXPERT_TILE)
    outs = []
    for x, rows, offset in ((x_a, rows_a, 0), (x_b, rows_b, x_a.shape[0])):
        outs.append(_moe_combine(dest, ys, x, rec_t, mod_l, ln_gain[l, 1:2], ln_bias[l, 1:2],
                                 row_of_tile=rows(TM_COMBINE), token_offset=offset))
    return outs


def kernel(x_prompt, x_sample, cache_kv_a, cache_kv_b, cache_kv_c, c, c_ctx, w_in, w_gate, w_branch, w_o,
           w_mod, b_mod, ln_gain, ln_bias, diff_lam, diff_subln, qk_gain, sink, w_router, w_e_gate, w_e_up,
           w_e_down):
    batch, seq, _ = x_prompt.shape
    dec_batch, dec_seq, _ = x_sample.shape
    past = cache_kv_a.shape[3]
    t_p, t_s = batch * seq, dec_batch * dec_seq

    cond = jnp.concatenate([c_ctx[None], c, jnp.zeros((MOD_ROWS - 1 - dec_batch, D_MODEL), F32)], axis=0)
    mod = _modulation(cond.T, w_mod, b_mod)
    cos_t, sin_t = _rope_tables(dec_seq)
    w_router_t = w_router.T
    w_in, w_gate, w_branch, w_o = (w.astype(BF16) for w in (w_in, w_gate, w_branch, w_o))
    cache_a = cache_kv_a.reshape(dec_batch, DEPTH, 2, past, 4 * 256)
    cache_b = cache_kv_b.reshape(dec_batch, DEPTH, 2, past, 2 * HEAD_DIM)
    cache_c = cache_kv_c.reshape(dec_batch, DEPTH, 2, past, 2 * HEAD_DIM)
    rows_p = _row_of_tile(0, t_p)
    rows_s = _row_of_tile(1, dec_seq)

    y_p = x_prompt.reshape(t_p, D_MODEL)
    y_s = x_sample.reshape(t_s, D_MODEL)
    projections = []
    for l in range(DEPTH):
        lam_init = 0.8 - 0.6 * math.exp(-0.3 * l)
        mod_l = mod[l].reshape(MOD_ROWS, 1, N_MOD * D_MODEL)
        subln_l = diff_subln[l].reshape(1, 256)
        shared = (w_branch, w_o, ln_gain, ln_bias)

        p = _in_projection(y_p, mod_l, w_in, l, qk_gain[l], cos_t, sin_t, row_of_tile=rows_p(TM),
                           rope=False, q_scale=1.0, out_dtype=F32)
        gates = _branch_gates(y_p, mod_l, w_gate, l, row_of_tile=rows_p(TM))
        attn = _attention_prompt(p, diff_lam[l], subln_l, sink[l], lam_init=lam_init, seq=seq)
        y_p = _mixer_output(y_p, (attn, attn, attn), (0, 1, 2), gates, mod_l, rows_p, l, *shared)
        projections.append(p.reshape(batch, seq, IN_COLS))

        qkv = _in_projection(y_s, mod_l, w_in, l, qk_gain[l], cos_t, sin_t, row_of_tile=rows_s(TM),
                             rope=True, q_scale=SCALE * LOG2E, out_dtype=BF16)
        gates = _branch_gates(y_s, mod_l, w_gate, l, row_of_tile=rows_s(TM))
        vec = lambda shape: pl.BlockSpec(shape, lambda b, qi, kk: (0, 0))
        a_o = _flash_sample(functools.partial(_diff_sample_kernel, lam_init=lam_init), qkv, cache_a, l,
                            q_col=COL_AQ, k_col=COL_AK, v_col=COL_AV, kv_width=1024, n_state=8,
                            row_sum_scratch=True, extra=(diff_lam[l], subln_l),
                            extra_specs=[vec((4, HEAD_DIM)), vec((1, 256))], name="diff_attention",
                            dec_seq=dec_seq)
        b_o = _flash_sample(_gqa_sample_kernel, qkv, cache_b, l, q_col=COL_BQ, k_col=COL_BK, v_col=COL_BV,
                            kv_width=256, n_state=8, row_sum_scratch=False, extra=(), extra_specs=[],
                            name="gqa_attention", dec_seq=dec_seq)
        c_o = _window_sample(qkv, cache_c, sink[l], l, dec_seq=dec_seq)
        y_s = _mixer_output(y_s, (a_o, b_o, c_o), (0, 0, 0), gates, mod_l, rows_s, l, *shared)

        y_p, y_s = _moe_ffn(y_p, y_s, mod_l, rows_p, rows_s, l, ln_gain, ln_bias, w_router_t, w_e_gate, w_e_up,
                            w_e_down)

    def new_cache(col_k, col_v, col_end, heads):
        parts = [p[..., c0:c1] for p in projections for c0, c1 in ((col_k, col_v), (col_v, col_end))]
        return jnp.stack(parts, axis=1).reshape(batch, DEPTH, 2, seq, heads, (col_v - col_k) // heads)

    new_kv_a = new_cache(COL_AK, COL_AV, COL_BQ, 4)
    new_kv_b = new_cache(COL_BK, COL_BV, COL_CQ, 2)
    new_kv_c = new_cache(COL_CK, COL_CV, IN_COLS, 2)
    return (y_p.reshape(batch, seq, D_MODEL), y_s.reshape(dec_batch, dec_seq, D_MODEL),
            new_kv_a, new_kv_b, new_kv_c)
```
